```python
import math
import jax, jax.numpy as jnp
from jax import lax
import numpy as np

D_MODEL = 1024
BATCH = 8
SEQ = 2048
DEPTH = 2

HEAD_DIM = 64
DSA_HEADS = 4
DSA_TOPK = 256
IDX_HEADS = 8
IDX_DIM = 32
FOX_HEADS = 6
NSA_HEADS = 6
NSA_KV_GROUPS = 2
NSA_HPG = NSA_HEADS // NSA_KV_GROUPS
NSA_CMP_LEN = 32
NSA_CMP_STRIDE = 16
NSA_SEL_BLOCK = 64
NSA_SEL_N = 16
NSA_WINDOW = 512
NSA_Q_BLOCK = 64

Q_BLOCK = 128
ROPE_THETA = 10000.0
LN_EPS = 1e-5
ALPHA = (2.0 * DEPTH) ** 0.25
BETA = (8.0 * DEPTH) ** -0.25

DSA_W = DSA_HEADS * HEAD_DIM
FOX_W = FOX_HEADS * HEAD_DIM
NSA_W = NSA_HEADS * HEAD_DIM
NSA_KV_W = NSA_KV_GROUPS * HEAD_DIM
MIX_W = DSA_W + FOX_W + NSA_W

IN_SPLITS = (
    ("dsa_q", DSA_W), ("dsa_k", HEAD_DIM), ("dsa_v", HEAD_DIM),
    ("idx_q", IDX_HEADS * IDX_DIM), ("idx_k", IDX_DIM), ("idx_w", IDX_HEADS),
    ("fox_q", FOX_W), ("fox_k", FOX_W), ("fox_v", FOX_W), ("fox_f", FOX_HEADS),
    ("nsa_q", NSA_W),
    ("nsa_kc", NSA_KV_W), ("nsa_vc", NSA_KV_W),
    ("nsa_ks", NSA_KV_W), ("nsa_vs", NSA_KV_W),
    ("nsa_kw", NSA_KV_W), ("nsa_vw", NSA_KV_W),
    ("nsa_g", 3 * NSA_HEADS),
    ("gate", MIX_W),
)
IN_WIDTH = sum(w for _, w in IN_SPLITS)

kernel_name = "hybrid_dsa_fox_nsa_parallel_heads"


def split_cols(h):
    out, off = {}, 0
    for name, w in IN_SPLITS:
        out[name] = h[..., off:off + w]
        off += w
    return out


def layer_norm(x, g, b):
    xf = x.astype(jnp.float32)
    mu = jnp.mean(xf, -1, keepdims=True)
    var = jnp.mean(jnp.square(xf - mu), -1, keepdims=True)
    return ((xf - mu) * lax.rsqrt(var + LN_EPS) * g + b).astype(x.dtype)


def rope(x, pos):
    half = x.shape[-1] // 2
    inv = ROPE_THETA ** (-jnp.arange(half, dtype=jnp.float32) / half)
    ang = pos.astype(jnp.float32)[:, None] * inv[None, :]
    cos = jnp.cos(ang)[:, None, :]
    sin = jnp.sin(ang)[:, None, :]
    x1, x2 = x[..., :half], x[..., half:]
    return jnp.concatenate([x1 * cos - x2 * sin, x1 * sin + x2 * cos], -1).astype(x.dtype)


def masked_softmax(s, mask):
    s = jnp.where(mask, s.astype(jnp.float32), -jnp.inf)
    m = jnp.max(s, -1, keepdims=True)
    m = jnp.where(jnp.isfinite(m), m, 0.0)
    e = jnp.where(mask, jnp.exp(s - m), 0.0)
    return e / jnp.maximum(jnp.sum(e, -1, keepdims=True), 1e-30)


def stack_blocks(out):
    nb, B, T, H, D = out.shape
    return out.transpose(1, 0, 2, 3, 4).reshape(B, nb * T, H, D)


def dsa_mixer(q, k, v, iq, ik, iw):
    B, S, H, D = q.shape
    topk = min(DSA_TOPK, S // 4)
    kpos = jnp.arange(S)
    scale = D ** -0.5
    gather = jax.vmap(lambda kb, ib: kb[ib])

    def block(i):
        t0 = i * Q_BLOCK
        tpos = t0 + jnp.arange(Q_BLOCK)
        qb = lax.dynamic_slice_in_dim(q, t0, Q_BLOCK, 1)
        iqb = lax.dynamic_slice_in_dim(iq, t0, Q_BLOCK, 1)
        iwb = lax.dynamic_slice_in_dim(iw, t0, Q_BLOCK, 1).astype(jnp.float32)
        isc = jnp.einsum("bthd,bsd->bths", iqb, ik).astype(jnp.float32)
        isc = jnp.einsum("bth,bths->bts", iwb, jax.nn.relu(isc))
        causal = kpos[None, :] <= tpos[:, None]
        isc = jnp.where(causal[None], isc, -jnp.inf)
        _, idx = lax.top_k(isc, topk)
        kg = gather(k, idx)
        vg = gather(v, idx)
        valid = idx <= tpos[None, :, None]
        s = jnp.einsum("bthd,btkd->bthk", qb, kg) * scale
        p = masked_softmax(s, valid[:, :, None, :])
        return jnp.einsum("bthk,btkd->bthd", p.astype(v.dtype), vg)

    return stack_blocks(lax.map(block, jnp.arange(S // Q_BLOCK)))


def fox_mixer(q, k, v, logf):
    B, S, H, D = q.shape
    cum = jnp.cumsum(logf.astype(jnp.float32), axis=1).transpose(0, 2, 1)
    kpos = jnp.arange(S)
    scale = D ** -0.5

    def block(i):
        t0 = i * Q_BLOCK
        tpos = t0 + jnp.arange(Q_BLOCK)
        qb = lax.dynamic_slice_in_dim(q, t0, Q_BLOCK, 1)
        cb = lax.dynamic_slice_in_dim(cum, t0, Q_BLOCK, 2)
        s = jnp.einsum("bthd,bshd->bhts", qb, k).astype(jnp.float32) * scale
        s = s + cb[:, :, :, None] - cum[:, :, None, :]
        causal = kpos[None, :] <= tpos[:, None]
        p = masked_softmax(s, causal)
        return jnp.einsum("bhts,bshd->bthd", p.astype(v.dtype), v)

    return stack_blocks(lax.map(block, jnp.arange(S // Q_BLOCK)))


def nsa_compress(kv, pe, w1, w2):
    B, S, G, D = kv.shape
    n_c = (S - NSA_CMP_LEN) // NSA_CMP_STRIDE + 1
    idx = jnp.arange(n_c)[:, None] * NSA_CMP_STRIDE + jnp.arange(NSA_CMP_LEN)[None, :]
    blocks = kv[:, idx] + pe[None, None, :, None, :]
    blocks = blocks.transpose(0, 1, 3, 2, 4).reshape(B, n_c, G, NSA_CMP_LEN * D)
    return jax.nn.silu(blocks @ w1) @ w2


def nsa_mixer(q, q_rot, kc, vc, ks, vs, kw, vw, gates):
    B, S, G, J, D = q.shape
    n_c = kc.shape[1]
    nsb = S // NSA_SEL_BLOCK
    n_sel = min(NSA_SEL_N, nsb)
    scale = D ** -0.5
    cstart = jnp.arange(n_c) * NSA_CMP_STRIDE
    cend = cstart + NSA_CMP_LEN - 1
    bstart = jnp.arange(nsb) * NSA_SEL_BLOCK
    overlap = ((cstart[:, None] < bstart[None, :] + NSA_SEL_BLOCK)
               & (cstart[:, None] + NSA_CMP_LEN > bstart[None, :])).astype(jnp.float32)
    ks_t = ks.transpose(0, 2, 1, 3)
    vs_t = vs.transpose(0, 2, 1, 3)
    pad = ((0, 0), (NSA_WINDOW, 0), (0, 0), (0, 0))
    kw_pad = jnp.pad(kw, pad)
    vw_pad = jnp.pad(vw, pad)
    gather = jax.vmap(jax.vmap(lambda kk, ii: kk[ii]))
    jb = jnp.arange(nsb)
    tok_off = jnp.arange(NSA_SEL_BLOCK)

    def block(i):
        T = NSA_Q_BLOCK
        t0 = i * T
        tpos = t0 + jnp.arange(T)
        qb = lax.dynamic_slice_in_dim(q, t0, T, 1)
        qrb = lax.dynamic_slice_in_dim(q_rot, t0, T, 1)
        gb = lax.dynamic_slice_in_dim(gates, t0, T, 1)
        s = jnp.einsum("btgjd,bcgd->bgjtc", qb, kc) * scale
        p_cmp = masked_softmax(s, cend[None, :] <= tpos[:, None])
        o_cmp = jnp.einsum("bgjtc,bcgd->btgjd", p_cmp.astype(vc.dtype), vc)
        score = jnp.einsum("bgjtc,cn->bgtn", p_cmp, overlap)
        cur = tpos // NSA_SEL_BLOCK
        forced = (jb[None, :] == 0) | (jb[None, :] == cur[:, None]) | (jb[None, :] == cur[:, None] - 1)
        future = bstart[None, :] > tpos[:, None]
        score = jnp.where(forced, jnp.inf, jnp.where(future, -jnp.inf, score))
        _, blk = lax.top_k(score, n_sel)
        tok = (blk[..., None] * NSA_SEL_BLOCK + tok_off).reshape(B, G, T, n_sel * NSA_SEL_BLOCK)
        kg = gather(ks_t, tok)
        vg = gather(vs_t, tok)
        s = jnp.einsum("btgjd,bgtnd->bgjtn", qrb, kg) * scale
        p = masked_softmax(s, (tok <= tpos[None, None, :, None])[:, :, None])
        o_slc = jnp.einsum("bgjtn,bgtnd->btgjd", p.astype(vs.dtype), vg)
        kwb = lax.dynamic_slice_in_dim(kw_pad, t0, NSA_WINDOW + T, 1)
        vwb = lax.dynamic_slice_in_dim(vw_pad, t0, NSA_WINDOW + T, 1)
        kpos = t0 - NSA_WINDOW + jnp.arange(NSA_WINDOW + T)
        wmask = ((kpos[None, :] <= tpos[:, None]) & (kpos[None, :] > tpos[:, None] - NSA_WINDOW)
                 & (kpos[None, :] >= 0))
        s = jnp.einsum("btgjd,bkgd->bgjtk", qrb, kwb) * scale
        p = masked_softmax(s, wmask)
        o_win = jnp.einsum("bgjtk,bkgd->btgjd", p.astype(vw.dtype), vwb)
        o = (gb[:, :, 0, :, :, None] * o_cmp + gb[:, :, 1, :, :, None] * o_slc
             + gb[:, :, 2, :, :, None] * o_win)
        return o.reshape(B, T, G * J, D)

    return stack_blocks(lax.map(block, jnp.arange(S // NSA_Q_BLOCK)))


def hybrid_layer(x, c, w_ada, b_ada, w_in, b_f, cmp_pe, cmp_w1, cmp_w2, w_out, ln_g, ln_b):
    B, S, _ = x.shape
    pos = jnp.arange(S)
    shift, scale, gate = jnp.split(c @ w_ada + b_ada, 3, axis=-1)
    u = x * (1.0 + scale[:, None, :]) + shift[:, None, :]
    h = split_cols(u @ w_in)

    dq = rope(h["dsa_q"].reshape(B, S, DSA_HEADS, HEAD_DIM), pos)
    dk = rope(h["dsa_k"][:, :, None, :], pos)[:, :, 0]
    iq = rope(h["idx_q"].reshape(B, S, IDX_HEADS, IDX_DIM), pos)
    ik = rope(h["idx_k"][:, :, None, :], pos)[:, :, 0]
    iw = h["idx_w"] * (IDX_HEADS ** -0.5)
    o_dsa = dsa_mixer(dq, dk, h["dsa_v"], iq, ik, iw)

    fq = h["fox_q"].reshape(B, S, FOX_HEADS, HEAD_DIM)
    fk = h["fox_k"].reshape(B, S, FOX_HEADS, HEAD_DIM)
    fv = h["fox_v"].reshape(B, S, FOX_HEADS, HEAD_DIM)
    logf = jax.nn.log_sigmoid(h["fox_f"].astype(jnp.float32) + b_f)
    o_fox = fox_mixer(fq, fk, fv, logf)

    kvs = lambda name: h[name].reshape(B, S, NSA_KV_GROUPS, HEAD_DIM)
    nq = h["nsa_q"].reshape(B, S, NSA_HEADS, HEAD_DIM)
    nq_rot = rope(nq, pos)
    kc = nsa_compress(kvs("nsa_kc"), cmp_pe[0], cmp_w1[0], cmp_w2[0])
    vc = nsa_compress(kvs("nsa_vc"), cmp_pe[1], cmp_w1[1], cmp_w2[1])
    ks = rope(kvs("nsa_ks"), pos)
    kw = rope(kvs("nsa_kw"), pos)
    gates = jax.nn.sigmoid(h["nsa_g"].reshape(B, S, 3, NSA_KV_GROUPS, NSA_HPG))
    grp = lambda t: t.reshape(B, S, NSA_KV_GROUPS, NSA_HPG, HEAD_DIM)
    o_nsa = nsa_mixer(grp(nq), grp(nq_rot), kc, vc, ks, kvs("nsa_vs"), kw, kvs("nsa_vw"), gates)

    mix = jnp.concatenate([o_dsa.reshape(B, S, DSA_W), o_fox.reshape(B, S, FOX_W),
                           o_nsa.reshape(B, S, NSA_W)], axis=-1)
    y = (mix * jax.nn.silu(h["gate"])) @ w_out
    return layer_norm(ALPHA * x + (1.0 + gate[:, None, :]) * y, ln_g, ln_b)


def setup_inputs(seed: int = 0) -> dict:
    key = jax.random.key(seed)
    ks = jax.random.split(key, 12)
    D = D_MODEL
    L = NSA_CMP_LEN
    nrm = lambda k, shape, s: jax.random.normal(k, shape, jnp.float32) * s
    return {
        "x": nrm(ks[0], (BATCH, SEQ, D), 1.0),
        "c": nrm(ks[1], (BATCH, D), 1.0),
        "w_ada": nrm(ks[2], (DEPTH, D, 3 * D), 0.1 * D ** -0.5),
        "b_ada": nrm(ks[3], (DEPTH, 3 * D), 0.01),
        "w_in": nrm(ks[4], (DEPTH, D, IN_WIDTH), D ** -0.5),
        "b_f": jax.random.uniform(ks[5], (DEPTH, FOX_HEADS), jnp.float32, 1.0, 4.0),
        "cmp_pe": nrm(ks[6], (DEPTH, 2, L, HEAD_DIM), 0.1),
        "cmp_w1": nrm(ks[7], (DEPTH, 2, L * HEAD_DIM, HEAD_DIM), (L * HEAD_DIM) ** -0.5),
        "cmp_w2": nrm(ks[8], (DEPTH, 2, HEAD_DIM, HEAD_DIM), HEAD_DIM ** -0.5),
        "w_out": nrm(ks[9], (DEPTH, MIX_W, D), BETA * MIX_W ** -0.5),
        "ln_g": 1.0 + nrm(ks[10], (DEPTH, D), 0.01),
        "ln_b": nrm(ks[11], (DEPTH, D), 0.01),
    }


def reference(x, c, w_ada, b_ada, w_in, b_f, cmp_pe, cmp_w1, cmp_w2, w_out, ln_g, ln_b):
    for l in range(DEPTH):
        x = hybrid_layer(x, c, w_ada[l], b_ada[l], w_in[l], b_f[l], cmp_pe[l], cmp_w1[l],
                         cmp_w2[l], w_out[l], ln_g[l], ln_b[l])
    return x
```

```python
import functools

import jax
import jax.numpy as jnp
from jax import lax
from jax.experimental import pallas as pl
from jax.experimental.pallas import tpu as pltpu

F32 = jnp.float32
BF16 = jnp.bfloat16
I32 = jnp.int32

D_MODEL = 1024
SEQ = 2048
DEPTH = 2
HEAD_DIM = 64
DSA_HEADS = 4
DSA_TOPK = 256
IDX_HEADS = 8
IDX_DIM = 32
FOX_HEADS = 6
NSA_HEADS = 6
NSA_GROUPS = 2
NSA_HPG = NSA_HEADS // NSA_GROUPS
CMP_LEN = 32
CMP_STRIDE = 16
SEL_BLOCK = 64
SEL_N = 16
N_SEL_BLOCKS = SEQ // SEL_BLOCK
WINDOW = 512
ROPE_THETA = 10000.0
LN_EPS = 1e-5
ALPHA = (2.0 * DEPTH) ** 0.25

DSA_W = DSA_HEADS * HEAD_DIM
FOX_W = FOX_HEADS * HEAD_DIM
NSA_W = NSA_HEADS * HEAD_DIM
NSA_KV_W = NSA_GROUPS * HEAD_DIM
MIX_W = DSA_W + FOX_W + NSA_W
IDX_W = IDX_HEADS * IDX_DIM

IN_SPLITS = (
    ("dsa_q", DSA_W), ("dsa_k", HEAD_DIM), ("dsa_v", HEAD_DIM),
    ("idx_q", IDX_W), ("idx_k", IDX_DIM), ("idx_w", IDX_HEADS),
    ("fox_q", FOX_W), ("fox_k", FOX_W), ("fox_v", FOX_W), ("fox_f", FOX_HEADS),
    ("nsa_q", NSA_W),
    ("nsa_kc", NSA_KV_W), ("nsa_vc", NSA_KV_W),
    ("nsa_ks", NSA_KV_W), ("nsa_vs", NSA_KV_W),
    ("nsa_kw", NSA_KV_W), ("nsa_vw", NSA_KV_W),
    ("nsa_g", 3 * NSA_HEADS),
    ("gate", MIX_W),
)

LANES = 128
KEY_CHUNK = 512
PROJ_TOKENS = 512
Q_TILE = 128
FOX_Q_TILE = 512
N_KEY_CHUNKS = SEQ // KEY_CHUNK
N_WIN_CHUNKS = SEQ // LANES
WIN_SPAN = WINDOW + Q_TILE
CUM_CHUNK = 256
VMEM_LIMIT = 56 * 1024 * 1024

NEG = -1e30
INT_MIN = -(2 ** 31)
KEY_NEG_INF = 0x807FFFFF - 2 ** 32

T_DQ, T_IQ, T_FQ, T_NQ = 0, 256, 512, 896
T_DV, T_FV, T_VS, T_VW = 1280, 1408, 1792, 1920
T_GATE, T_SMALL, T_ROWS = 2048, 3072, 3120
S_DK, S_IK, S_FK, S_KS, S_KW, S_KC, S_VC, S_FF, S_COLS = 0, 128, 256, 640, 768, 896, 1024, 1152, 1280


def _cparams(sem):
    return pltpu.CompilerParams(dimension_semantics=sem, vmem_limit_bytes=VMEM_LIMIT)


def _dot(a, b):
    return jnp.dot(a, b, preferred_element_type=F32)


def _dot_nt(a, b):
    return lax.dot_general(a, b, (((1,), (1,)), ((), ())), preferred_element_type=F32)


def _log_sigmoid(x):
    return jnp.minimum(x, 0.0) - jnp.log(1.0 + jnp.exp(-jnp.abs(x)))


def _sigmoid(x):
    return 1.0 / (1.0 + jnp.exp(-x))


def _mod_kernel(c_ref, w_ref, b_ref, o_ref):
    o_ref[0] = _dot(c_ref[...].astype(BF16), w_ref[0].astype(BF16)) + b_ref[0]


def _modulation(c, w_ada, b_ada):
    depth, d, d3 = w_ada.shape
    bsz = c.shape[0]
    return pl.pallas_call(
        _mod_kernel,
        grid=(depth, d3 // d),
        in_specs=[
            pl.BlockSpec((bsz, d), lambda l, j: (0, 0)),
            pl.BlockSpec((1, d, d), lambda l, j: (l, 0, j)),
            pl.BlockSpec((1, 1, d), lambda l, j: (l, 0, j)),
        ],
        out_specs=pl.BlockSpec((1, bsz, d), lambda l, j: (l, 0, j)),
        out_shape=jax.ShapeDtypeStruct((depth, bsz, d3), F32),
        compiler_params=_cparams(("arbitrary", "arbitrary")),
        name="adaln_mod",
    )(c, w_ada, b_ada.reshape(depth, 1, d3))


def _proj_kernel(x_ref, sc_ref, sh_ref, wt_ref, ws_ref,
                 ct64_ref, st64_ref, ct32_ref, st32_ref,
                 ck64_ref, sk64_ref, ck32_ref, sk32_ref, bfc_ref, bfr_ref,
                 dq_ref, iq_ref, fq_ref, nq_ref, nqr_ref,
                 dv_ref, fv_ref, vs_ref, vw_ref, gate_ref,
                 iw_ref, lft_ref, gts_ref,
                 dk_ref, ik_ref, fk_ref, ks_ref, kw_ref, kcin_ref, vcin_ref, lfs_ref):
    tm = x_ref.shape[1]
    u = (x_ref[0] * (1.0 + sc_ref[0]) + sh_ref[0]).astype(BF16)

    def proj_t(r0, r1):
        return _dot_nt(wt_ref[r0:r1, :], u)

    def rope_t(h, n_heads, hd, c, s):
        half = hd // 2
        out = []
        for hh in range(n_heads):
            x1 = h[hh * hd:hh * hd + half]
            x2 = h[hh * hd + half:(hh + 1) * hd]
            out.append((hh * hd, x1 * c - x2 * s))
            out.append((hh * hd + half, x1 * s + x2 * c))
        return out

    c64, s64 = ct64_ref[...], st64_ref[...]
    c32, s32 = ct32_ref[...], st32_ref[...]
    qscale = HEAD_DIM ** -0.5

    h = proj_t(T_DQ, T_DQ + DSA_W)
    for r, v in rope_t(h, DSA_HEADS, HEAD_DIM, c64, s64):
        dq_ref[0, r:r + HEAD_DIM // 2, :] = (v * qscale).astype(BF16)
    h = proj_t(T_IQ, T_IQ + IDX_W)
    for r, v in rope_t(h, IDX_HEADS, IDX_DIM, c32, s32):
        iq_ref[0, r:r + IDX_DIM // 2, :] = v.astype(BF16)
    fq_ref[0] = (proj_t(T_FQ, T_FQ + FOX_W) * qscale).astype(BF16)
    h = proj_t(T_NQ, T_NQ + NSA_W)
    nq_ref[0] = (h * qscale).astype(BF16)
    for r, v in rope_t(h, NSA_HEADS, HEAD_DIM, c64, s64):
        nqr_ref[0, r:r + HEAD_DIM // 2, :] = (v * qscale).astype(BF16)

    dv_ref[0, 0] = proj_t(T_DV, T_DV + HEAD_DIM).astype(BF16)
    h = proj_t(T_FV, T_FV + FOX_W)
    for hh in range(FOX_HEADS):
        fv_ref[0, hh, 0] = h[hh * HEAD_DIM:(hh + 1) * HEAD_DIM].astype(BF16)
    h = proj_t(T_VS, T_VS + NSA_KV_W)
    for g in range(NSA_GROUPS):
        vs_ref[0, g, 0] = h[g * HEAD_DIM:(g + 1) * HEAD_DIM].astype(BF16)
    h = proj_t(T_VW, T_VW + NSA_KV_W)
    for g in range(NSA_GROUPS):
        for j in range(tm // LANES):
            vw_ref[0, g, j] = h[g * HEAD_DIM:(g + 1) * HEAD_DIM, j * LANES:(j + 1) * LANES].astype(BF16)

    for r0 in range(0, MIX_W, 256):
        h = proj_t(T_GATE + r0, T_GATE + r0 + 256)
        gate_ref[0, r0:r0 + 256, :] = h * _sigmoid(h)

    h = proj_t(T_SMALL, T_ROWS)
    iw_ref[0] = h[0:8] * (IDX_HEADS ** -0.5)
    lft_ref[0] = _log_sigmoid(h[8:16] + bfc_ref[...])
    g_all = _sigmoid(h[16:48])
    gts_ref[0, 0] = g_all[0:16]
    gts_ref[0, 1] = g_all[16:32]

    lane = lax.broadcasted_iota(I32, (tm, LANES), 1)

    def rope_s(g, half, c, s_signed):
        first = (lane & (2 * half - 1)) < half
        sw = jnp.where(first, pltpu.roll(g, LANES - half, 1), pltpu.roll(g, half, 1))
        return g * c + sw * s_signed

    def proj_s(c0, c1):
        return _dot(u, ws_ref[:, c0:c1])

    ck64, sk64 = ck64_ref[...], sk64_ref[...]
    g = rope_s(proj_s(S_DK, S_DK + LANES), HEAD_DIM // 2, ck64, sk64)
    dk_ref[0] = g[:, :HEAD_DIM].astype(BF16)
    g = rope_s(proj_s(S_IK, S_IK + LANES), IDX_DIM // 2, ck32_ref[...], sk32_ref[...])
    ik_ref[0] = g[:, :IDX_DIM].astype(BF16)
    g = proj_s(S_FK, S_FK + FOX_W)
    for hh in range(FOX_HEADS):
        fk_ref[0, hh] = g[:, hh * HEAD_DIM:(hh + 1) * HEAD_DIM].astype(BF16)
    g = rope_s(proj_s(S_KS, S_KS + LANES), HEAD_DIM // 2, ck64, sk64)
    for gg in range(NSA_GROUPS):
        ks_ref[0, gg] = g[:, gg * HEAD_DIM:(gg + 1) * HEAD_DIM].astype(BF16)
    g = rope_s(proj_s(S_KW, S_KW + LANES), HEAD_DIM // 2, ck64, sk64)
    for gg in range(NSA_GROUPS):
        kw_ref[0, gg] = g[:, gg * HEAD_DIM:(gg + 1) * HEAD_DIM].astype(BF16)
    kcin_ref[0] = proj_s(S_KC, S_KC + LANES)
    vcin_ref[0] = proj_s(S_VC, S_VC + LANES)
    g = proj_s(S_FF, S_FF + LANES)
    lfs_ref[0] = _log_sigmoid(g[:, :8] + bfr_ref[...])


def _projection(x, scale, shift, wt, ws, tabs, bfc, bfr):
    bsz, seq, d = x.shape
    tm = PROJ_TOKENS
    nt = seq // tm
    ct64, st64, ct32, st32, ck64, sk64, ck32, sk32 = tabs
    const = lambda shape: pl.BlockSpec(shape, lambda b, i: (0,) * len(shape))
    feat = lambda rows: pl.BlockSpec((1, rows, tm), lambda b, i: (b, 0, i))
    in_specs = [
        pl.BlockSpec((1, tm, d), lambda b, i: (b, i, 0)),
        pl.BlockSpec((1, 1, d), lambda b, i: (b, 0, 0)),
        pl.BlockSpec((1, 1, d), lambda b, i: (b, 0, 0)),
        const((T_ROWS, d)),
        const((d, S_COLS)),
        pl.BlockSpec((HEAD_DIM // 2, tm), lambda b, i: (0, i)),
        pl.BlockSpec((HEAD_DIM // 2, tm), lambda b, i: (0, i)),
        pl.BlockSpec((IDX_DIM // 2, tm), lambda b, i: (0, i)),
        pl.BlockSpec((IDX_DIM // 2, tm), lambda b, i: (0, i)),
        pl.BlockSpec((tm, LANES), lambda b, i: (i, 0)),
        pl.BlockSpec((tm, LANES), lambda b, i: (i, 0)),
        pl.BlockSpec((tm, LANES), lambda b, i: (i, 0)),
        pl.BlockSpec((tm, LANES), lambda b, i: (i, 0)),
        const((8, 1)),
        const((1, 8)),
    ]
    sds = jax.ShapeDtypeStruct
    out_shape = [
        sds((bsz, DSA_W, seq), BF16), sds((bsz, IDX_W, seq), BF16), sds((bsz, FOX_W, seq), BF16),
        sds((bsz, NSA_W, seq), BF16), sds((bsz, NSA_W, seq), BF16),
        sds((bsz, N_KEY_CHUNKS, HEAD_DIM, KEY_CHUNK), BF16),
        sds((bsz, FOX_HEADS, N_KEY_CHUNKS, HEAD_DIM, KEY_CHUNK), BF16),
        sds((bsz, NSA_GROUPS, N_KEY_CHUNKS, HEAD_DIM, KEY_CHUNK), BF16),
        sds((bsz, NSA_GROUPS, N_WIN_CHUNKS, HEAD_DIM, LANES), BF16),
        sds((bsz, MIX_W, seq), F32),
        sds((bsz, 8, seq), F32), sds((bsz, 8, seq), F32), sds((bsz, NSA_GROUPS, 16, seq), F32),
        sds((bsz, seq, HEAD_DIM), BF16), sds((bsz, seq, IDX_DIM), BF16),
        sds((bsz, FOX_HEADS, seq, HEAD_DIM), BF16),
        sds((bsz, NSA_GROUPS, seq, HEAD_DIM), BF16), sds((bsz, NSA_GROUPS, seq, HEAD_DIM), BF16),
        sds((bsz, seq, LANES), F32), sds((bsz, seq, LANES), F32), sds((bsz, seq, 8), F32),
    ]
    wpc = tm // LANES
    out_specs = [
        feat(DSA_W), feat(IDX_W), feat(FOX_W), feat(NSA_W), feat(NSA_W),
        pl.BlockSpec((1, 1, HEAD_DIM, KEY_CHUNK), lambda b, i: (b, i, 0, 0)),
        pl.BlockSpec((1, FOX_HEADS, 1, HEAD_DIM, KEY_CHUNK), lambda b, i: (b, 0, i, 0, 0)),
        pl.BlockSpec((1, NSA_GROUPS, 1, HEAD_DIM, KEY_CHUNK), lambda b, i: (b, 0, i, 0, 0)),
        pl.BlockSpec((1, NSA_GROUPS, wpc, HEAD_DIM, LANES), lambda b, i: (b, 0, i, 0, 0)),
        feat(MIX_W),
        feat(8), feat(8),
        pl.BlockSpec((1, NSA_GROUPS, 16, tm), lambda b, i: (b, 0, 0, i)),
        pl.BlockSpec((1, tm, HEAD_DIM), lambda b, i: (b, i, 0)),
        pl.BlockSpec((1, tm, IDX_DIM), lambda b, i: (b, i, 0)),
        pl.BlockSpec((1, FOX_HEADS, tm, HEAD_DIM), lambda b, i: (b, 0, i, 0)),
        pl.BlockSpec((1, NSA_GROUPS, tm, HEAD_DIM), lambda b, i: (b, 0, i, 0)),
        pl.BlockSpec((1, NSA_GROUPS, tm, HEAD_DIM), lambda b, i: (b, 0, i, 0)),
        pl.BlockSpec((1, tm, LANES), lambda b, i: (b, i, 0)),
        pl.BlockSpec((1, tm, LANES), lambda b, i: (b, i, 0)),
        pl.BlockSpec((1, tm, 8), lambda b, i: (b, i, 0)),
    ]
    assert tm == KEY_CHUNK
    return pl.pallas_call(
        _proj_kernel,
        grid=(bsz, nt),
        in_specs=in_specs,
        out_specs=out_specs,
        out_shape=out_shape,
        compiler_params=_cparams(("arbitrary", "arbitrary")),
        name="in_proj",
    )(x, scale, shift, wt, ws, ct64, st64, ct32, st32, ck64, sk64, ck32, sk32, bfc, bfr)


def _cumsum_kernel(lft_ref, lfs_ref, cumt_ref, cums_ref):
    seq = lft_ref.shape[2]
    r = lax.broadcasted_iota(I32, (CUM_CHUNK, CUM_CHUNK), 0)
    c = lax.broadcasted_iota(I32, (CUM_CHUNK, CUM_CHUNK), 1)
    tri_u = (r <= c).astype(F32)
    tri_l = (c <= r).astype(F32)
    carry_t = jnp.zeros((8, 1), F32)
    carry_s = jnp.zeros((1, 8), F32)
    for k in range(seq // CUM_CHUNK):
        sl = slice(k * CUM_CHUNK, (k + 1) * CUM_CHUNK)
        ct = jnp.dot(lft_ref[0, :, sl], tri_u, preferred_element_type=F32,
                     precision=lax.Precision.HIGHEST) + carry_t
        cumt_ref[0, :, sl] = ct
        carry_t = ct[:, CUM_CHUNK - 1:CUM_CHUNK]
        cs = jnp.dot(tri_l, lfs_ref[0, sl, :], preferred_element_type=F32,
                     precision=lax.Precision.HIGHEST) + carry_s
        cums_ref[0, sl, :] = cs
        carry_s = cs[CUM_CHUNK - 1:CUM_CHUNK, :]


def _forget_cumsum(lft, lfs):
    bsz, _, seq = lft.shape
    return pl.pallas_call(
        _cumsum_kernel,
        grid=(bsz,),
        in_specs=[pl.BlockSpec((1, 8, seq), lambda b: (b, 0, 0)),
                  pl.BlockSpec((1, seq, 8), lambda b: (b, 0, 0))],
        out_specs=[pl.BlockSpec((1, 8, seq), lambda b: (b, 0, 0)),
                   pl.BlockSpec((1, seq, 8), lambda b: (b, 0, 0))],
        out_shape=[jax.ShapeDtypeStruct((bsz, 8, seq), F32),
                   jax.ShapeDtypeStruct((bsz, seq, 8), F32)],
        compiler_params=_cparams(("arbitrary",)),
        name="forget_cumsum",
    )(lft, lfs)


def _compress_kernel(kcin_ref, vcin_ref, pelo_ref, pehi_ref, w1lo_ref, w1hi_ref, w2_ref,
                     kc_ref, vct_ref):
    n_blk = kcin_ref.shape[1] // CMP_STRIDE
    for kv, src in enumerate((kcin_ref, vcin_ref)):
        lo, hi = [], []
        for j in range(CMP_STRIDE):
            piece = src[0, pl.ds(j, n_blk, stride=CMP_STRIDE), :]
            lo.append((piece + pelo_ref[kv, j:j + 1, :]).astype(BF16))
            hi.append((piece + pehi_ref[kv, j:j + 1, :]).astype(BF16))
        a = _dot(jnp.concatenate(lo, axis=1), w1lo_ref[kv])
        b = _dot(jnp.concatenate(hi, axis=1), w1hi_ref[kv])
        pre = a + pltpu.roll(b, n_blk - 1, 0)
        act = pre * _sigmoid(pre)
        out = _dot(act.astype(BF16), w2_ref[kv])
        if kv == 0:
            for g in range(NSA_GROUPS):
                kc_ref[0, g] = out[:, g * HEAD_DIM:(g + 1) * HEAD_DIM].astype(BF16)
        else:
            out_t = out.T
            for g in range(NSA_GROUPS):
                vct_ref[0, g] = out_t[g * HEAD_DIM:(g + 1) * HEAD_DIM].astype(BF16)


def _compress(kcin, vcin, pelo, pehi, w1lo, w1hi, w2):
    bsz, seq, _ = kcin.shape
    n_blk = seq // CMP_STRIDE
    full = lambda a: pl.BlockSpec(a.shape, lambda b: (0,) * a.ndim)
    return pl.pallas_call(
        _compress_kernel,
        grid=(bsz,),
        in_specs=[pl.BlockSpec((1, seq, LANES), lambda b: (b, 0, 0)),
                  pl.BlockSpec((1, seq, LANES), lambda b: (b, 0, 0)),
                  full(pelo), full(pehi), full(w1lo), full(w1hi), full(w2)],
        out_specs=[pl.BlockSpec((1, NSA_GROUPS, n_blk, HEAD_DIM), lambda b: (b, 0, 0, 0)),
                   pl.BlockSpec((1, NSA_GROUPS, HEAD_DIM, n_blk), lambda b: (b, 0, 0, 0))],
        out_shape=[jax.ShapeDtypeStruct((bsz, NSA_GROUPS, n_blk, HEAD_DIM), BF16),
                   jax.ShapeDtypeStruct((bsz, NSA_GROUPS, HEAD_DIM, n_blk), BF16)],
        compiler_params=_cparams(("arbitrary",)),
        name="nsa_compress",
    )(kcin, vcin, pelo, pehi, w1lo, w1hi, w2)


def _softmax_step(carry, s, v_t):
    m, l, acc = carry
    m_new = jnp.maximum(m, jnp.max(s, axis=0, keepdims=True))
    alpha = jnp.exp(m - m_new)
    p = jnp.exp(s - m_new)
    l = alpha * l + jnp.sum(p, axis=0, keepdims=True)
    acc = alpha * acc + _dot(v_t, p.astype(BF16))
    return m_new, l, acc


def _softmax_init(n):
    return (jnp.full((1, n), NEG, F32), jnp.zeros((1, n), F32), jnp.zeros((HEAD_DIM, n), F32))


def _tile_lanes(a, n):
    return jnp.concatenate([a] * n, axis=1)


def _dsa_kernel(iq_ref, ik_ref, iw_ref, dq_ref, dk_ref, dv_ref, o_ref, key_ref, jsel_ref):
    i = pl.program_id(1)
    n_chunks = lax.shift_right_logical(i + 4, 2)
    t0 = i * Q_TILE
    tpos = t0 + lax.broadcasted_iota(I32, (1, Q_TILE), 1)
    srow = lax.broadcasted_iota(I32, (KEY_CHUNK, Q_TILE), 0)

    iq_all = jnp.concatenate(
        [iq_ref[0, h * IDX_DIM:(h + 1) * IDX_DIM, :] for h in range(IDX_HEADS)], axis=1)
    iw = iw_ref[0]

    def index_chunk(c, _):
        k0 = pl.multiple_of(c * KEY_CHUNK, KEY_CHUNK)
        x = _dot(ik_ref[0, pl.ds(k0, KEY_CHUNK), :], iq_all)
        acc = jnp.zeros((KEY_CHUNK, Q_TILE), F32)
        for h in range(IDX_HEADS):
            acc = acc + iw[h:h + 1, :] * jnp.maximum(x[:, h * Q_TILE:(h + 1) * Q_TILE], 0.0)
        acc = jnp.where(srow + k0 <= tpos, acc, -jnp.inf)
        bits = lax.bitcast_convert_type(acc, I32)
        key = jnp.where(bits < 0, bits ^ 0x7FFFFFFF, bits)
        key = jnp.where(key == -1, 0, key)
        key_ref[pl.ds(k0, KEY_CHUNK), :] = key
        return 0

    lax.fori_loop(0, n_chunks, index_chunk, 0)

    def count(pred):
        def body(c, acc):
            k0 = pl.multiple_of(c * KEY_CHUNK, KEY_CHUNK)
            hit = pred(key_ref[pl.ds(k0, KEY_CHUNK), :], srow + k0)
            return acc + jnp.sum(jnp.where(hit, 1.0, 0.0), axis=0, keepdims=True)
        return lax.fori_loop(0, n_chunks, body, jnp.zeros((1, Q_TILE), F32))

    def value_bit(it, thr_u):
        cand_u = thr_u | lax.shift_left(jnp.int32(1), 31 - it)
        cand_s = cand_u ^ INT_MIN
        cnt = count(lambda key, spos: key >= cand_s)
        return jnp.where(cnt >= DSA_TOPK, cand_u, thr_u)

    thr = lax.fori_loop(0, 32, value_bit, jnp.zeros((1, Q_TILE), I32)) ^ INT_MIN

    cnt_gt = count(lambda key, spos: key > thr)
    cnt_ge = count(lambda key, spos: key >= thr)
    need = DSA_TOPK - cnt_gt
    has_tie = jnp.logical_and(cnt_ge > DSA_TOPK, thr > KEY_NEG_INF)
    jsel_ref[...] = jnp.full((1, Q_TILE), SEQ, I32)

    @pl.when(jnp.max(jnp.where(has_tie, 1.0, 0.0)) > 0.0)
    def _():
        def index_bit(it, j):
            cand = j | lax.shift_left(jnp.int32(1), 10 - it)
            cnt = count(lambda key, spos: jnp.logical_and(key == thr, spos < cand))
            return jnp.where(cnt < need, cand, j)
        jsel_ref[...] = lax.fori_loop(0, 11, index_bit, jnp.zeros((1, Q_TILE), I32))

    jsel = jsel_ref[...]
    q_all = jnp.concatenate(
        [dq_ref[0, h * HEAD_DIM:(h + 1) * HEAD_DIM, :] for h in range(DSA_HEADS)], axis=1)

    def attend(c, carry):
        k0 = pl.multiple_of(c * KEY_CHUNK, KEY_CHUNK)
        key = key_ref[pl.ds(k0, KEY_CHUNK), :]
        spos = srow + k0
        sel = jnp.logical_or(key > thr, jnp.logical_and(key == thr, spos <= jsel))
        sel = jnp.logical_and(sel, spos <= tpos)
        bias = _tile_lanes(jnp.where(sel, 0.0, NEG), DSA_HEADS)
        s = _dot(dk_ref[0, pl.ds(k0, KEY_CHUNK), :], q_all) + bias
        return _softmax_step(carry, s, dv_ref[0, c])

    _, l, acc = lax.fori_loop(0, n_chunks, attend, _softmax_init(DSA_HEADS * Q_TILE))
    out = acc * (1.0 / l)
    for h in range(DSA_HEADS):
        o_ref[0, h * HEAD_DIM:(h + 1) * HEAD_DIM, :] = out[:, h * Q_TILE:(h + 1) * Q_TILE]


def _dsa(iq_t, ik, iw_t, dq_t, dk, dv_t):
    bsz, _, seq = dq_t.shape
    return pl.pallas_call(
        _dsa_kernel,
        grid=(bsz, seq // Q_TILE),
        in_specs=[
            pl.BlockSpec((1, IDX_W, Q_TILE), lambda b, i: (b, 0, i)),
            pl.BlockSpec((1, seq, IDX_DIM), lambda b, i: (b, 0, 0)),
            pl.BlockSpec((1, 8, Q_TILE), lambda b, i: (b, 0, i)),
            pl.BlockSpec((1, DSA_W, Q_TILE), lambda b, i: (b, 0, i)),
            pl.BlockSpec((1, seq, HEAD_DIM), lambda b, i: (b, 0, 0)),
            pl.BlockSpec((1, N_KEY_CHUNKS, HEAD_DIM, KEY_CHUNK), lambda b, i: (b, 0, 0, 0)),
        ],
        out_specs=pl.BlockSpec((1, DSA_W, Q_TILE), lambda b, i: (b, 0, i)),
        out_shape=jax.ShapeDtypeStruct((bsz, DSA_W, seq), F32),
        scratch_shapes=[pltpu.VMEM((seq, Q_TILE), I32), pltpu.VMEM((1, Q_TILE), I32)],
        compiler_params=_cparams(("arbitrary", "arbitrary")),
        name="dsa_attn",
    )(iq_t, ik, iw_t, dq_t, dk, dv_t)


def _fox_kernel(q_ref, k_ref, v_ref, cumt_ref, cums_ref, o_ref):
    h = pl.program_id(1)
    i = pl.program_id(2)
    q = q_ref[0]
    crow = cumt_ref[0, 0]
    head_lane = lax.broadcasted_iota(I32, (KEY_CHUNK, 8), 1) == h

    def scores(c):
        k0 = pl.multiple_of(c * KEY_CHUNK, KEY_CHUNK)
        ccol = jnp.sum(jnp.where(head_lane, cums_ref[0, pl.ds(k0, KEY_CHUNK), :], 0.0),
                       axis=1, keepdims=True)
        return _dot(k_ref[0, 0, pl.ds(k0, KEY_CHUNK), :], q) + crow - ccol

    def full_chunk(c, carry):
        return _softmax_step(carry, scores(c), v_ref[0, 0, c])

    carry = lax.fori_loop(0, i, full_chunk, _softmax_init(FOX_Q_TILE))
    srow = lax.broadcasted_iota(I32, (KEY_CHUNK, FOX_Q_TILE), 0)
    tcol = lax.broadcasted_iota(I32, (KEY_CHUNK, FOX_Q_TILE), 1)
    s = jnp.where(srow <= tcol, scores(i), NEG)
    _, l, acc = _softmax_step(carry, s, v_ref[0, 0, i])
    o_ref[0] = acc * (1.0 / l)


def _fox(fq_t, fk, fv_t, cum_t, cum_s):
    bsz, _, seq = fq_t.shape
    assert FOX_Q_TILE == KEY_CHUNK
    return pl.pallas_call(
        _fox_kernel,
        grid=(bsz, FOX_HEADS, seq // FOX_Q_TILE),
        in_specs=[
            pl.BlockSpec((1, HEAD_DIM, FOX_Q_TILE), lambda b, h, i: (b, h, i)),
            pl.BlockSpec((1, 1, seq, HEAD_DIM), lambda b, h, i: (b, h, 0, 0)),
            pl.BlockSpec((1, 1, N_KEY_CHUNKS, HEAD_DIM, KEY_CHUNK), lambda b, h, i: (b, h, 0, 0, 0)),
            pl.BlockSpec((1, 1, 1, FOX_Q_TILE), lambda b, h, i: (b, h, 0, i)),
            pl.BlockSpec((1, seq, 8), lambda b, h, i: (b, 0, 0)),
        ],
        out_specs=pl.BlockSpec((1, HEAD_DIM, FOX_Q_TILE), lambda b, h, i: (b, h, i)),
        out_shape=jax.ShapeDtypeStruct((bsz, FOX_W, seq), F32),
        compiler_params=_cparams(("arbitrary", "arbitrary", "arbitrary")),
        name="fox_attn",
    )(fq_t, fk, fv_t, cum_t.reshape(bsz, 8, 1, seq), cum_s)


def _nsa_kernel(q_ref, qr_ref, kc_ref, vct_ref, ks_ref, vs_ref, kw_ref, vw_ref, g_ref,
                ovl_ref, expand_ref, o_ref, sel_ref):
    i = pl.program_id(2)
    nq = NSA_HPG * Q_TILE
    t0 = i * Q_TILE
    tpos = t0 + lax.broadcasted_iota(I32, (1, Q_TILE), 1)
    q3 = jnp.concatenate([q_ref[0, j * HEAD_DIM:(j + 1) * HEAD_DIM, :] for j in range(NSA_HPG)], axis=1)
    qr3 = jnp.concatenate([qr_ref[0, j * HEAD_DIM:(j + 1) * HEAD_DIM, :] for j in range(NSA_HPG)], axis=1)

    n_cmp = kc_ref.shape[2]
    cend = lax.broadcasted_iota(I32, (n_cmp, Q_TILE), 0) * CMP_STRIDE + (CMP_LEN - 1)
    cbias = _tile_lanes(jnp.where(cend <= tpos, 0.0, NEG), NSA_HPG)
    cvalid = _tile_lanes(jnp.where(cend <= tpos, 1.0, 0.0), NSA_HPG)
    s = _dot(kc_ref[0, 0], q3) + cbias
    m = jnp.max(s, axis=0, keepdims=True)
    e = jnp.exp(s - m) * cvalid
    p_cmp = (e / jnp.maximum(jnp.sum(e, axis=0, keepdims=True), 1e-30)).astype(BF16)
    o_cmp = _dot(vct_ref[0, 0], p_cmp)

    p_stack = jnp.concatenate([p_cmp[:, j * Q_TILE:(j + 1) * Q_TILE] for j in range(NSA_HPG)], axis=0)
    score = _dot(ovl_ref[...], p_stack)
    blk = lax.broadcasted_iota(I32, (N_SEL_BLOCKS, Q_TILE), 0)
    cur = lax.shift_right_logical(tpos, 6)
    forced = jnp.logical_or(blk == 0, jnp.logical_or(blk == cur, blk == cur - 1))
    score = jnp.where(forced, jnp.inf, jnp.where(blk > cur, -jnp.inf, score))
    rank = jnp.zeros((N_SEL_BLOCKS, Q_TILE), F32)
    for mth in range(N_SEL_BLOCKS):
        row = score[mth:mth + 1, :]
        ahead = jnp.logical_or(row > score, jnp.logical_and(row == score, blk > mth))
        rank = rank + jnp.where(ahead, 1.0, 0.0)
    sel_ref[...] = jnp.where(rank < SEL_N, 1.0, 0.0)

    n_chunks = lax.shift_right_logical(i + 4, 2)
    srow = lax.broadcasted_iota(I32, (KEY_CHUNK, Q_TILE), 0)
    blocks_per_chunk = KEY_CHUNK // SEL_BLOCK

    def attend_sel(c, carry):
        k0 = pl.multiple_of(c * KEY_CHUNK, KEY_CHUNK)
        b0 = pl.multiple_of(c * blocks_per_chunk, blocks_per_chunk)
        picked = _dot(expand_ref[...], sel_ref[pl.ds(b0, blocks_per_chunk), :])
        ok = jnp.logical_and(picked > 0.5, srow + k0 <= tpos)
        bias = _tile_lanes(jnp.where(ok, 0.0, NEG), NSA_HPG)
        s = _dot(ks_ref[0, 0, pl.ds(k0, KEY_CHUNK), :], qr3) + bias
        return _softmax_step(carry, s, vs_ref[0, 0, c])

    _, l, acc = lax.fori_loop(0, n_chunks, attend_sel, _softmax_init(nq))
    o_slc = acc * (1.0 / l)

    wc = jnp.maximum(i - WINDOW // Q_TILE, 0)
    w0 = pl.multiple_of(wc * Q_TILE, Q_TILE)
    kpos = w0 + lax.broadcasted_iota(I32, (WIN_SPAN, Q_TILE), 0)
    wok = jnp.logical_and(kpos <= tpos, kpos > tpos - WINDOW)
    s = _dot(kw_ref[0, 0, pl.ds(w0, WIN_SPAN), :], qr3) + _tile_lanes(jnp.where(wok, 0.0, NEG), NSA_HPG)
    m = jnp.max(s, axis=0, keepdims=True)
    e = jnp.exp(s - m)
    l = jnp.sum(e, axis=0, keepdims=True)
    e = e.astype(BF16)
    o_win = jnp.zeros((HEAD_DIM, nq), F32)
    for j in range(WIN_SPAN // Q_TILE):
        o_win = o_win + _dot(vw_ref[0, 0, wc + j], e[j * Q_TILE:(j + 1) * Q_TILE, :])
    o_win = o_win * (1.0 / l)

    gts = g_ref[0, 0]
    for j in range(NSA_HPG):
        sl = slice(j * Q_TILE, (j + 1) * Q_TILE)
        o_ref[0, j * HEAD_DIM:(j + 1) * HEAD_DIM, :] = (
            gts[j:j + 1, :] * o_cmp[:, sl]
            + gts[NSA_HPG + j:NSA_HPG + j + 1, :] * o_slc[:, sl]
            + gts[2 * NSA_HPG + j:2 * NSA_HPG + j + 1, :] * o_win[:, sl])


def _nsa(nq_t, nqr_t, kc, vc_t, ks, vs_t, kw, vw_t, gts, ovl, expand):
    bsz, _, seq = nq_t.shape
    n_cmp = kc.shape[2]
    gw = NSA_HPG * HEAD_DIM
    per_group = lambda shape: pl.BlockSpec((1, 1) + shape, lambda b, g, i: (b, g) + (0,) * len(shape))
    return pl.pallas_call(
        _nsa_kernel,
        grid=(bsz, NSA_GROUPS, seq // Q_TILE),
        in_specs=[
            pl.BlockSpec((1, gw, Q_TILE), lambda b, g, i: (b, g, i)),
            pl.BlockSpec((1, gw, Q_TILE), lambda b, g, i: (b, g, i)),
            per_group((n_cmp, HEAD_DIM)),
            per_group((HEAD_DIM, n_cmp)),
            per_group((seq, HEAD_DIM)),
            per_group((N_KEY_CHUNKS, HEAD_DIM, KEY_CHUNK)),
            per_group((seq, HEAD_DIM)),
            per_group((N_WIN_CHUNKS, HEAD_DIM, LANES)),
            pl.BlockSpec((1, 1, 16, Q_TILE), lambda b, g, i: (b, g, 0, i)),
            pl.BlockSpec(ovl.shape, lambda b, g, i: (0, 0)),
            pl.BlockSpec(expand.shape, lambda b, g, i: (0, 0)),
        ],
        out_specs=pl.BlockSpec((1, gw, Q_TILE), lambda b, g, i: (b, g, i)),
        out_shape=jax.ShapeDtypeStruct((bsz, NSA_W, seq), F32),
        scratch_shapes=[pltpu.VMEM((N_SEL_BLOCKS, Q_TILE), F32)],
        compiler_params=_cparams(("arbitrary", "arbitrary", "arbitrary")),
        name="nsa_attn",
    )(nq_t, nqr_t, kc, vc_t, ks, vs_t, kw, vw_t, gts, ovl, expand)


def _out_kernel(od_ref, of_ref, on_ref, gate_ref, w_ref, x_ref, gm_ref, lg_ref, lb_ref, o_ref):
    z = jnp.concatenate([od_ref[0], of_ref[0], on_ref[0]], axis=0) * gate_ref[0]
    y = lax.dot_general(z.astype(BF16), w_ref[...], (((0,), (0,)), ((), ())),
                        preferred_element_type=F32)
    r = ALPHA * x_ref[0] + (1.0 + gm_ref[0]) * y
    mu = jnp.mean(r, axis=-1, keepdims=True)
    rc = r - mu
    var = jnp.mean(rc * rc, axis=-1, keepdims=True)
    o_ref[0] = rc * lax.rsqrt(var + LN_EPS) * lg_ref[...] + lb_ref[...]


def _output(od_t, of_t, on_t, gate_t, w_out, x, gmod, ln_g, ln_b):
    bsz, seq, d = x.shape
    tm = PROJ_TOKENS
    feat = lambda rows: pl.BlockSpec((1, rows, tm), lambda b, i: (b, 0, i))
    return pl.pallas_call(
        _out_kernel,
        grid=(bsz, seq // tm),
        in_specs=[
            feat(DSA_W), feat(FOX_W), feat(NSA_W), feat(MIX_W),
            pl.BlockSpec((MIX_W, d), lambda b, i: (0, 0)),
            pl.BlockSpec((1, tm, d), lambda b, i: (b, i, 0)),
            pl.BlockSpec((1, 1, d), lambda b, i: (b, 0, 0)),
            pl.BlockSpec((1, d), lambda b, i: (0, 0)),
            pl.BlockSpec((1, d), lambda b, i: (0, 0)),
        ],
        out_specs=pl.BlockSpec((1, tm, d), lambda b, i: (b, i, 0)),
        out_shape=jax.ShapeDtypeStruct((bsz, seq, d), F32),
        compiler_params=_cparams(("arbitrary", "arbitrary")),
        name="out_proj_ln",
    )(od_t, of_t, on_t, gate_t, w_out, x, gmod, ln_g, ln_b)


def _split_cols(w):
    out, off = {}, 0
    for name, width in IN_SPLITS:
        out[name] = w[:, off:off + width]
        off += width
    return out


def _prep_w_in(w_in):
    p = _split_cols(w_in)
    d = w_in.shape[0]
    z = lambda n: jnp.zeros((d, n), w_in.dtype)
    gcols = p["nsa_g"].reshape(d, 3, NSA_GROUPS, NSA_HPG)
    gparts = []
    for g in range(NSA_GROUPS):
        gparts += [gcols[:, :, g, :].reshape(d, 3 * NSA_HPG), z(16 - 3 * NSA_HPG)]
    wt = jnp.concatenate(
        [p["dsa_q"], p["idx_q"], p["fox_q"], p["nsa_q"],
         p["dsa_v"], z(HEAD_DIM), p["fox_v"], p["nsa_vs"], p["nsa_vw"], p["gate"],
         p["idx_w"], p["fox_f"], z(8 - FOX_HEADS)] + gparts, axis=1)
    assert wt.shape[1] == T_ROWS
    ws = jnp.concatenate(
        [p["dsa_k"], z(LANES - HEAD_DIM), p["idx_k"], z(LANES - IDX_DIM), p["fox_k"],
         p["nsa_ks"], p["nsa_kw"], p["nsa_kc"], p["nsa_vc"], p["fox_f"], z(LANES - FOX_HEADS)], axis=1)
    assert ws.shape[1] == S_COLS
    return wt.T.astype(BF16), ws.astype(BF16)


def _rope_tables(seq):
    pos = jnp.arange(seq, dtype=F32)
    tabs = []
    for hd in (HEAD_DIM, IDX_DIM):
        half = hd // 2
        inv = ROPE_THETA ** (-jnp.arange(half, dtype=F32) / half)
        ang = inv[:, None] * pos[None, :]
        tabs += [jnp.cos(ang), jnp.sin(ang)]
    lane = jnp.arange(LANES)
    for hd in (HEAD_DIM, IDX_DIM):
        half = hd // 2
        inv = ROPE_THETA ** (-jnp.arange(half, dtype=F32) / half)
        ang = pos[:, None] * inv[lane % half][None, :]
        sign = jnp.where((lane % hd) < half, -1.0, 1.0).astype(F32)
        tabs += [jnp.cos(ang), jnp.sin(ang) * sign[None, :]]
    return tuple(tabs)


def _prep_compress(cmp_pe, cmp_w1, cmp_w2):
    eye = jnp.eye(NSA_GROUPS, dtype=F32)
    w1 = cmp_w1.reshape(2, CMP_LEN, HEAD_DIM, HEAD_DIM)
    big1 = jnp.einsum("kjde,gh->kjgdhe", w1, eye).reshape(2, CMP_LEN, LANES, LANES)
    w1lo = big1[:, :CMP_STRIDE].reshape(2, CMP_STRIDE * LANES, LANES).astype(BF16)
    w1hi = big1[:, CMP_STRIDE:].reshape(2, CMP_STRIDE * LANES, LANES).astype(BF16)
    w2 = jnp.einsum("kde,gh->kgdhe", cmp_w2, eye).reshape(2, LANES, LANES).astype(BF16)
    pe2 = jnp.concatenate([cmp_pe] * NSA_GROUPS, axis=-1)
    return pe2[:, :CMP_STRIDE], pe2[:, CMP_STRIDE:], w1lo, w1hi, w2


def _selection_constants(n_cmp):
    cstart = jnp.arange(n_cmp) * CMP_STRIDE
    bstart = jnp.arange(N_SEL_BLOCKS) * SEL_BLOCK
    real = (jnp.arange(n_cmp) < (SEQ - CMP_LEN) // CMP_STRIDE + 1)[:, None]
    ovl = ((cstart[:, None] < bstart[None, :] + SEL_BLOCK) & (cstart[:, None] + CMP_LEN > bstart[None, :]) & real)
    ovl_t = jnp.concatenate([ovl.T.astype(BF16)] * NSA_HPG, axis=1)
    expand = (jnp.arange(KEY_CHUNK)[:, None] // SEL_BLOCK == jnp.arange(KEY_CHUNK // SEL_BLOCK)[None, :]).astype(F32)
    return ovl_t, expand


def kernel(x, c, w_ada, b_ada, w_in, b_f, cmp_pe, cmp_w1, cmp_w2, w_out, ln_g, ln_b):
    bsz, seq, d = x.shape
    assert (seq, d) == (SEQ, D_MODEL)
    mod = _modulation(c, w_ada, b_ada)
    tabs = _rope_tables(seq)
    ovl, expand = _selection_constants(seq // CMP_STRIDE)
    for l in range(DEPTH):
        shift = mod[l, :, :d].reshape(bsz, 1, d)
        scale = mod[l, :, d:2 * d].reshape(bsz, 1, d)
        gmod = mod[l, :, 2 * d:].reshape(bsz, 1, d)
        wt, ws = _prep_w_in(w_in[l])
        bf8 = jnp.concatenate([b_f[l], jnp.zeros((8 - FOX_HEADS,), F32)])
        (dq, iq, fq, nq, nqr, dv, fv, vs, vw, gate, iw, lft, gts,
         dk, ik, fk, ks, kw, kcin, vcin, lfs) = _projection(
            x, scale, shift, wt, ws, tabs, bf8.reshape(8, 1), bf8.reshape(1, 8))
        cum_t, cum_s = _forget_cumsum(lft, lfs)
        kc, vc_t = _compress(kcin, vcin, *_prep_compress(cmp_pe[l], cmp_w1[l], cmp_w2[l]))
        o_dsa = _dsa(iq, ik, iw, dq, dk, dv)
        o_fox = _fox(fq, fk, fv, cum_t, cum_s)
        o_nsa = _nsa(nq, nqr, kc, vc_t, ks, vs, kw, vw, gts, ovl, expand)
        x = _output(o_dsa, o_fox, o_nsa, gate, w_out[l].astype(BF16), x, gmod,
                    ln_g[l].reshape(1, d), ln_b[l].reshape(1, d))
    return x
```

```python
import functools

import jax
import jax.numpy as jnp
from jax import lax
from jax.experimental import pallas as pl
from jax.experimental.pallas import tpu as pltpu

F32 = jnp.float32
BF16 = jnp.bfloat16
I32 = jnp.int32

D_MODEL = 1024
SEQ = 2048
DEPTH = 2
HEAD_DIM = 64
DSA_HEADS = 4
DSA_TOPK = 256
IDX_HEADS = 8
IDX_DIM = 32
FOX_HEADS = 6
NSA_HEADS = 6
NSA_GROUPS = 2
NSA_HPG = NSA_HEADS // NSA_GROUPS
CMP_LEN = 32
CMP_STRIDE = 16
SEL_BLOCK = 64
SEL_N = 16
N_SEL_BLOCKS = SEQ // SEL_BLOCK
WINDOW = 512
ROPE_THETA = 10000.0
LN_EPS = 1e-5
ALPHA = (2.0 * DEPTH) ** 0.25

DSA_W = DSA_HEADS * HEAD_DIM
FOX_W = FOX_HEADS * HEAD_DIM
NSA_W = NSA_HEADS * HEAD_DIM
NSA_KV_W = NSA_GROUPS * HEAD_DIM
MIX_W = DSA_W + FOX_W + NSA_W
IDX_W = IDX_HEADS * IDX_DIM

IN_SPLITS = (
    ("dsa_q", DSA_W), ("dsa_k", HEAD_DIM), ("dsa_v", HEAD_DIM),
    ("idx_q", IDX_W), ("idx_k", IDX_DIM), ("idx_w", IDX_HEADS),
    ("fox_q", FOX_W), ("fox_k", FOX_W), ("fox_v", FOX_W), ("fox_f", FOX_HEADS),
    ("nsa_q", NSA_W),
    ("nsa_kc", NSA_KV_W), ("nsa_vc", NSA_KV_W),
    ("nsa_ks", NSA_KV_W), ("nsa_vs", NSA_KV_W),
    ("nsa_kw", NSA_KV_W), ("nsa_vw", NSA_KV_W),
    ("nsa_g", 3 * NSA_HEADS),
    ("gate", MIX_W),
)

LANES = 128
KEY_CHUNK = 512
PROJ_TOKENS = 512
Q_TILE = 256
DSA_Q_TILE = 512
FOX_Q_TILE = 512
N_KEY_CHUNKS = SEQ // KEY_CHUNK
WIN_CHUNK = 256
N_WIN_CHUNKS = SEQ // WIN_CHUNK
WIN_SPAN = WINDOW + Q_TILE
CUM_CHUNK = 256
VMEM_LIMIT = 56 * 1024 * 1024

FOLD_ROWS = 32
NEG = -1e30
INT_MIN = -(2 ** 31)
KEY_NEG_INF = 0x807FFFFF - 2 ** 32

T_DQ, T_IQ, T_FQ, T_NQ = 0, 256, 512, 896
T_DV, T_FV, T_VS, T_VW = 1280, 1408, 1792, 1920
T_GATE, T_SMALL, T_ROWS = 2048, 3072, 3120
S_DK, S_IK, S_FK, S_KS, S_KW, S_KC, S_VC, S_FF, S_COLS = 0, 128, 256, 640, 768, 896, 1024, 1152, 1280


def _cparams(sem):
    return pltpu.CompilerParams(dimension_semantics=sem, vmem_limit_bytes=VMEM_LIMIT)


def _dot(a, b):
    return jnp.dot(a, b, preferred_element_type=F32)


def _dot_nt(a, b):
    return lax.dot_general(a, b, (((1,), (1,)), ((), ())), preferred_element_type=F32)


def _log_sigmoid(x):
    return jnp.minimum(x, 0.0) - jnp.log(1.0 + jnp.exp(-jnp.abs(x)))


def _sigmoid(x):
    return 1.0 / (1.0 + jnp.exp(-x))


def _mod_kernel(c_ref, w_ref, b_ref, o_ref):
    o_ref[0] = _dot(c_ref[...].astype(BF16), w_ref[0].astype(BF16)) + b_ref[0]


def _modulation(c, w_ada, b_ada):
    depth, d, d3 = w_ada.shape
    bsz = c.shape[0]
    return pl.pallas_call(
        _mod_kernel,
        grid=(depth, d3 // d),
        in_specs=[
            pl.BlockSpec((bsz, d), lambda l, j: (0, 0)),
            pl.BlockSpec((1, d, d), lambda l, j: (l, 0, j)),
            pl.BlockSpec((1, 1, d), lambda l, j: (l, 0, j)),
        ],
        out_specs=pl.BlockSpec((1, bsz, d), lambda l, j: (l, 0, j)),
        out_shape=jax.ShapeDtypeStruct((depth, bsz, d3), F32),
        compiler_params=_cparams(("arbitrary", "arbitrary")),
        name="adaln_mod",
    )(c, w_ada, b_ada.reshape(depth, 1, d3))


def _proj_kernel(x_ref, sc_ref, sh_ref, wt_ref, ws_ref,
                 ct64_ref, st64_ref, ct32_ref, st32_ref,
                 ck64_ref, sk64_ref, ck32_ref, sk32_ref, bfc_ref, bfr_ref,
                 dq_ref, iq_ref, fq_ref, nq_ref, nqr_ref,
                 dv_ref, fv_ref, vs_ref, vw_ref, gate_ref,
                 iw_ref, lft_ref, gts_ref,
                 dk_ref, ik_ref, fk_ref, ks_ref, kw_ref, kcin_ref, vcin_ref, lfs_ref):
    tm = x_ref.shape[1]
    u = (x_ref[0] * (1.0 + sc_ref[0]) + sh_ref[0]).astype(BF16)

    def proj_t(r0, r1):
        return _dot_nt(wt_ref[r0:r1, :], u)

    def rope_t(h, n_heads, hd, c, s):
        half = hd // 2
        out = []
        for hh in range(n_heads):
            x1 = h[hh * hd:hh * hd + half]
            x2 = h[hh * hd + half:(hh + 1) * hd]
            out.append((hh * hd, x1 * c - x2 * s))
            out.append((hh * hd + half, x1 * s + x2 * c))
        return out

    c64, s64 = ct64_ref[...], st64_ref[...]
    c32, s32 = ct32_ref[...], st32_ref[...]
    qscale = HEAD_DIM ** -0.5

    h = proj_t(T_DQ, T_DQ + DSA_W)
    for r, v in rope_t(h, DSA_HEADS, HEAD_DIM, c64, s64):
        dq_ref[0, r:r + HEAD_DIM // 2, :] = (v * qscale).astype(BF16)
    h = proj_t(T_IQ, T_IQ + IDX_W)
    for r, v in rope_t(h, IDX_HEADS, IDX_DIM, c32, s32):
        iq_ref[0, r:r + IDX_DIM // 2, :] = v.astype(BF16)
    fq_ref[0] = (proj_t(T_FQ, T_FQ + FOX_W) * qscale).astype(BF16)
    h = proj_t(T_NQ, T_NQ + NSA_W)
    nq_ref[0] = (h * qscale).astype(BF16)
    for r, v in rope_t(h, NSA_HEADS, HEAD_DIM, c64, s64):
        nqr_ref[0, r:r + HEAD_DIM // 2, :] = (v * qscale).astype(BF16)

    dv_ref[0, 0] = proj_t(T_DV, T_DV + HEAD_DIM).astype(BF16)
    h = proj_t(T_FV, T_FV + FOX_W)
    for hh in range(FOX_HEADS):
        fv_ref[0, hh, 0] = h[hh * HEAD_DIM:(hh + 1) * HEAD_DIM].astype(BF16)
    h = proj_t(T_VS, T_VS + NSA_KV_W)
    for g in range(NSA_GROUPS):
        vs_ref[0, g, 0] = h[g * HEAD_DIM:(g + 1) * HEAD_DIM].astype(BF16)
    h = proj_t(T_VW, T_VW + NSA_KV_W)
    for g in range(NSA_GROUPS):
        for j in range(tm // WIN_CHUNK):
            vw_ref[0, g, j] = h[g * HEAD_DIM:(g + 1) * HEAD_DIM, j * WIN_CHUNK:(j + 1) * WIN_CHUNK].astype(BF16)

    for r0 in range(0, MIX_W, 256):
        h = proj_t(T_GATE + r0, T_GATE + r0 + 256)
        gate_ref[0, r0:r0 + 256, :] = h * _sigmoid(h)

    h = proj_t(T_SMALL, T_ROWS)
    iw_ref[0] = h[0:8] * (IDX_HEADS ** -0.5)
    lft_ref[0] = _log_sigmoid(h[8:16] + bfc_ref[...])
    g_all = _sigmoid(h[16:48])
    gts_ref[0, 0] = g_all[0:16]
    gts_ref[0, 1] = g_all[16:32]

    lane = lax.broadcasted_iota(I32, (tm, LANES), 1)

    def rope_s(g, half, c, s_signed):
        first = (lane & (2 * half - 1)) < half
        sw = jnp.where(first, pltpu.roll(g, LANES - half, 1), pltpu.roll(g, half, 1))
        return g * c + sw * s_signed

    def proj_s(c0, c1):
        return _dot(u, ws_ref[:, c0:c1])

    ck64, sk64 = ck64_ref[...], sk64_ref[...]
    g = rope_s(proj_s(S_DK, S_DK + LANES), HEAD_DIM // 2, ck64, sk64)
    dk_ref[0] = g[:, :HEAD_DIM].astype(BF16)
    g = rope_s(proj_s(S_IK, S_IK + LANES), IDX_DIM // 2, ck32_ref[...], sk32_ref[...])
    ik_ref[0] = g[:, :IDX_DIM].astype(BF16)
    g = proj_s(S_FK, S_FK + FOX_W)
    for hh in range(FOX_HEADS):
        fk_ref[0, hh] = g[:, hh * HEAD_DIM:(hh + 1) * HEAD_DIM].astype(BF16)
    g = rope_s(proj_s(S_KS, S_KS + LANES), HEAD_DIM // 2, ck64, sk64)
    for gg in range(NSA_GROUPS):
        ks_ref[0, gg] = g[:, gg * HEAD_DIM:(gg + 1) * HEAD_DIM].astype(BF16)
    g = rope_s(proj_s(S_KW, S_KW + LANES), HEAD_DIM // 2, ck64, sk64)
    for gg in range(NSA_GROUPS):
        kw_ref[0, gg] = g[:, gg * HEAD_DIM:(gg + 1) * HEAD_DIM].astype(BF16)
    kcin_ref[0] = proj_s(S_KC, S_KC + LANES)
    vcin_ref[0] = proj_s(S_VC, S_VC + LANES)
    g = proj_s(S_FF, S_FF + LANES)
    lfs_ref[0] = _log_sigmoid(g[:, :8] + bfr_ref[...])


def _projection(x, scale, shift, wt, ws, tabs, bfc, bfr):
    bsz, seq, d = x.shape
    tm = PROJ_TOKENS
    nt = seq // tm
    ct64, st64, ct32, st32, ck64, sk64, ck32, sk32 = tabs
    const = lambda shape: pl.BlockSpec(shape, lambda b, i: (0,) * len(shape))
    feat = lambda rows: pl.BlockSpec((1, rows, tm), lambda b, i: (b, 0, i))
    in_specs = [
        pl.BlockSpec((1, tm, d), lambda b, i: (b, i, 0)),
        pl.BlockSpec((1, 1, d), lambda b, i: (b, 0, 0)),
        pl.BlockSpec((1, 1, d), lambda b, i: (b, 0, 0)),
        const((T_ROWS, d)),
        const((d, S_COLS)),
        pl.BlockSpec((HEAD_DIM // 2, tm), lambda b, i: (0, i)),
        pl.BlockSpec((HEAD_DIM // 2, tm), lambda b, i: (0, i)),
        pl.BlockSpec((IDX_DIM // 2, tm), lambda b, i: (0, i)),
        pl.BlockSpec((IDX_DIM // 2, tm), lambda b, i: (0, i)),
        pl.BlockSpec((tm, LANES), lambda b, i: (i, 0)),
        pl.BlockSpec((tm, LANES), lambda b, i: (i, 0)),
        pl.BlockSpec((tm, LANES), lambda b, i: (i, 0)),
        pl.BlockSpec((tm, LANES), lambda b, i: (i, 0)),
        const((8, 1)),
        const((1, 8)),
    ]
    sds = jax.ShapeDtypeStruct
    out_shape = [
        sds((bsz, DSA_W, seq), BF16), sds((bsz, IDX_W, seq), BF16), sds((bsz, FOX_W, seq), BF16),
        sds((bsz, NSA_W, seq), BF16), sds((bsz, NSA_W, seq), BF16),
        sds((bsz, N_KEY_CHUNKS, HEAD_DIM, KEY_CHUNK), BF16),
        sds((bsz, FOX_HEADS, N_KEY_CHUNKS, HEAD_DIM, KEY_CHUNK), BF16),
        sds((bsz, NSA_GROUPS, N_KEY_CHUNKS, HEAD_DIM, KEY_CHUNK), BF16),
        sds((bsz, NSA_GROUPS, N_WIN_CHUNKS, HEAD_DIM, WIN_CHUNK), BF16),
        sds((bsz, MIX_W, seq), F32),
        sds((bsz, 8, seq), F32), sds((bsz, 8, seq), F32), sds((bsz, NSA_GROUPS, 16, seq), F32),
        sds((bsz, seq, HEAD_DIM), BF16), sds((bsz, seq, IDX_DIM), BF16),
        sds((bsz, FOX_HEADS, seq, HEAD_DIM), BF16),
        sds((bsz, NSA_GROUPS, seq, HEAD_DIM), BF16), sds((bsz, NSA_GROUPS, seq, HEAD_DIM), BF16),
        sds((bsz, seq, LANES), F32), sds((bsz, seq, LANES), F32), sds((bsz, seq, 8), F32),
    ]
    wpc = tm // WIN_CHUNK
    out_specs = [
        feat(DSA_W), feat(IDX_W), feat(FOX_W), feat(NSA_W), feat(NSA_W),
        pl.BlockSpec((1, 1, HEAD_DIM, KEY_CHUNK), lambda b, i: (b, i, 0, 0)),
        pl.BlockSpec((1, FOX_HEADS, 1, HEAD_DIM, KEY_CHUNK), lambda b, i: (b, 0, i, 0, 0)),
        pl.BlockSpec((1, NSA_GROUPS, 1, HEAD_DIM, KEY_CHUNK), lambda b, i: (b, 0, i, 0, 0)),
        pl.BlockSpec((1, NSA_GROUPS, wpc, HEAD_DIM, WIN_CHUNK), lambda b, i: (b, 0, i, 0, 0)),
        feat(MIX_W),
        feat(8), feat(8),
        pl.BlockSpec((1, NSA_GROUPS, 16, tm), lambda b, i: (b, 0, 0, i)),
        pl.BlockSpec((1, tm, HEAD_DIM), lambda b, i: (b, i, 0)),
        pl.BlockSpec((1, tm, IDX_DIM), lambda b, i: (b, i, 0)),
        pl.BlockSpec((1, FOX_HEADS, tm, HEAD_DIM), lambda b, i: (b, 0, i, 0)),
        pl.BlockSpec((1, NSA_GROUPS, tm, HEAD_DIM), lambda b, i: (b, 0, i, 0)),
        pl.BlockSpec((1, NSA_GROUPS, tm, HEAD_DIM), lambda b, i: (b, 0, i, 0)),
        pl.BlockSpec((1, tm, LANES), lambda b, i: (b, i, 0)),
        pl.BlockSpec((1, tm, LANES), lambda b, i: (b, i, 0)),
        pl.BlockSpec((1, tm, 8), lambda b, i: (b, i, 0)),
    ]
    assert tm == KEY_CHUNK
    return pl.pallas_call(
        _proj_kernel,
        grid=(bsz, nt),
        in_specs=in_specs,
        out_specs=out_specs,
        out_shape=out_shape,
        compiler_params=_cparams(("arbitrary", "arbitrary")),
        name="in_proj",
    )(x, scale, shift, wt, ws, ct64, st64, ct32, st32, ck64, sk64, ck32, sk32, bfc, bfr)


def _cumsum_kernel(lft_ref, lfs_ref, cumt_ref, cums_ref):
    seq = lft_ref.shape[2]
    r = lax.broadcasted_iota(I32, (CUM_CHUNK, CUM_CHUNK), 0)
    c = lax.broadcasted_iota(I32, (CUM_CHUNK, CUM_CHUNK), 1)
    tri_u = (r <= c).astype(F32)
    tri_l = (c <= r).astype(F32)
    carry_t = jnp.zeros((8, 1), F32)
    carry_s = jnp.zeros((1, 8), F32)
    for k in range(seq // CUM_CHUNK):
        sl = slice(k * CUM_CHUNK, (k + 1) * CUM_CHUNK)
        ct = jnp.dot(lft_ref[0, :, sl], tri_u, preferred_element_type=F32,
                     precision=lax.Precision.HIGHEST) + carry_t
        cumt_ref[0, :, sl] = ct
        carry_t = ct[:, CUM_CHUNK - 1:CUM_CHUNK]
        cs = jnp.dot(tri_l, lfs_ref[0, sl, :], preferred_element_type=F32,
                     precision=lax.Precision.HIGHEST) + carry_s
        cums_ref[0, sl, :] = cs
        carry_s = cs[CUM_CHUNK - 1:CUM_CHUNK, :]


def _forget_cumsum(lft, lfs):
    bsz, _, seq = lft.shape
    return pl.pallas_call(
        _cumsum_kernel,
        grid=(bsz,),
        in_specs=[pl.BlockSpec((1, 8, seq), lambda b: (b, 0, 0)),
                  pl.BlockSpec((1, seq, 8), lambda b: (b, 0, 0))],
        out_specs=[pl.BlockSpec((1, 8, seq), lambda b: (b, 0, 0)),
                   pl.BlockSpec((1, seq, 8), lambda b: (b, 0, 0))],
        out_shape=[jax.ShapeDtypeStruct((bsz, 8, seq), F32),
                   jax.ShapeDtypeStruct((bsz, seq, 8), F32)],
        compiler_params=_cparams(("arbitrary",)),
        name="forget_cumsum",
    )(lft, lfs)


def _compress_kernel(kcin_ref, vcin_ref, pelo_ref, pehi_ref, w1lo_ref, w1hi_ref, w2_ref,
                     kc_ref, vct_ref):
    n_blk = kcin_ref.shape[1] // CMP_STRIDE
    for kv, src in enumerate((kcin_ref, vcin_ref)):
        lo, hi = [], []
        for j in range(CMP_STRIDE):
            piece = src[0, pl.ds(j, n_blk, stride=CMP_STRIDE), :]
            lo.append((piece + pelo_ref[kv, j:j + 1, :]).astype(BF16))
            hi.append((piece + pehi_ref[kv, j:j + 1, :]).astype(BF16))
        a = _dot(jnp.concatenate(lo, axis=1), w1lo_ref[kv])
        b = _dot(jnp.concatenate(hi, axis=1), w1hi_ref[kv])
        pre = a + pltpu.roll(b, n_blk - 1, 0)
        act = pre * _sigmoid(pre)
        out = _dot(act.astype(BF16), w2_ref[kv])
        if kv == 0:
            for g in range(NSA_GROUPS):
                kc_ref[0, g] = out[:, g * HEAD_DIM:(g + 1) * HEAD_DIM].astype(BF16)
        else:
            out_t = out.T
            for g in range(NSA_GROUPS):
                vct_ref[0, g] = out_t[g * HEAD_DIM:(g + 1) * HEAD_DIM].astype(BF16)


def _compress(kcin, vcin, pelo, pehi, w1lo, w1hi, w2):
    bsz, seq, _ = kcin.shape
    n_blk = seq // CMP_STRIDE
    full = lambda a: pl.BlockSpec(a.shape, lambda b: (0,) * a.ndim)
    return pl.pallas_call(
        _compress_kernel,
        grid=(bsz,),
        in_specs=[pl.BlockSpec((1, seq, LANES), lambda b: (b, 0, 0)),
                  pl.BlockSpec((1, seq, LANES), lambda b: (b, 0, 0)),
                  full(pelo), full(pehi), full(w1lo), full(w1hi), full(w2)],
        out_specs=[pl.BlockSpec((1, NSA_GROUPS, n_blk, HEAD_DIM), lambda b: (b, 0, 0, 0)),
                   pl.BlockSpec((1, NSA_GROUPS, HEAD_DIM, n_blk), lambda b: (b, 0, 0, 0))],
        out_shape=[jax.ShapeDtypeStruct((bsz, NSA_GROUPS, n_blk, HEAD_DIM), BF16),
                   jax.ShapeDtypeStruct((bsz, NSA_GROUPS, HEAD_DIM, n_blk), BF16)],
        compiler_params=_cparams(("arbitrary",)),
        name="nsa_compress",
    )(kcin, vcin, pelo, pehi, w1lo, w1hi, w2)


def _fold_rows(x, op):
    k, n = x.shape
    return op(x.reshape(k // FOLD_ROWS, FOLD_ROWS, n), axis=0)


def _reduce_rows(x, op):
    if x.shape[0] % FOLD_ROWS == 0 and x.shape[0] > FOLD_ROWS:
        x = _fold_rows(x, op)
    return op(x, axis=0, keepdims=True)


def _softmax_step(carry, s, v_t):
    m, l, acc = carry
    m_new = jnp.maximum(m, _reduce_rows(s, jnp.max))
    alpha = jnp.exp(m - m_new)
    p = jnp.exp(s - m_new)
    l = alpha * l + _reduce_rows(p, jnp.sum)
    acc = alpha * acc + _dot(v_t, p.astype(BF16))
    return m_new, l, acc


def _softmax_init(n):
    return (jnp.full((1, n), NEG, F32), jnp.zeros((1, n), F32), jnp.zeros((HEAD_DIM, n), F32))


def _tile_lanes(a, n):
    return jnp.concatenate([a] * n, axis=1)


def _dsa_kernel(iq_ref, ik_ref, iw_ref, dq_ref, dk_ref, dv_ref, o_ref, key_ref, jsel_ref):
    i = pl.program_id(1)
    tq = DSA_Q_TILE
    n_chunks = i + 1
    tpos = i * tq + lax.broadcasted_iota(I32, (1, tq), 1)
    srow = lax.broadcasted_iota(I32, (KEY_CHUNK, tq), 0)
    iw = iw_ref[0]

    def index_chunk(c, _):
        k0 = pl.multiple_of(c * KEY_CHUNK, KEY_CHUNK)
        ikc = ik_ref[0, pl.ds(k0, KEY_CHUNK), :]
        acc = jnp.zeros((KEY_CHUNK, tq), F32)
        for h in range(IDX_HEADS):
            x = _dot(ikc, iq_ref[0, h * IDX_DIM:(h + 1) * IDX_DIM, :])
            acc = acc + iw[h:h + 1, :] * jnp.maximum(x, 0.0)
        acc = jnp.where(srow + k0 <= tpos, acc, -jnp.inf)
        bits = lax.bitcast_convert_type(acc, I32)
        key = jnp.where(bits < 0, bits ^ 0x7FFFFFFF, bits)
        key = jnp.where(key == -1, 0, key)
        key_ref[pl.ds(k0, KEY_CHUNK), :] = key
        return 0

    lax.fori_loop(0, n_chunks, index_chunk, 0)

    def count(pred):
        def body(c, acc):
            k0 = pl.multiple_of(c * KEY_CHUNK, KEY_CHUNK)
            hit = pred(key_ref[pl.ds(k0, KEY_CHUNK), :], srow + k0)
            return acc + _fold_rows(jnp.where(hit, 1, 0), jnp.sum)
        acc = lax.fori_loop(0, n_chunks, body, jnp.zeros((FOLD_ROWS, tq), I32))
        return jnp.sum(acc, axis=0, keepdims=True)

    def value_bit(it, thr_u):
        cand_u = thr_u | lax.shift_left(jnp.int32(1), 31 - it)
        cand_s = cand_u ^ INT_MIN
        cnt = count(lambda key, spos: key >= cand_s)
        return jnp.where(cnt >= DSA_TOPK, cand_u, thr_u)

    thr = lax.fori_loop(0, 32, value_bit, jnp.zeros((1, tq), I32)) ^ INT_MIN

    cnt_gt = count(lambda key, spos: key > thr)
    cnt_ge = count(lambda key, spos: key >= thr)
    need = DSA_TOPK - cnt_gt
    has_tie = jnp.logical_and(cnt_ge > DSA_TOPK, thr > KEY_NEG_INF)
    jsel_ref[...] = jnp.full((1, tq), SEQ, I32)

    @pl.when(jnp.max(jnp.where(has_tie, 1.0, 0.0)) > 0.0)
    def _():
        def index_bit(it, j):
            cand = j | lax.shift_left(jnp.int32(1), 10 - it)
            cnt = count(lambda key, spos: jnp.logical_and(key == thr, spos < cand))
            return jnp.where(cnt < need, cand, j)
        jsel_ref[...] = lax.fori_loop(0, 11, index_bit, jnp.zeros((1, tq), I32))

    jsel = jsel_ref[...]

    def attend(c, carry):
        k0 = pl.multiple_of(c * KEY_CHUNK, KEY_CHUNK)
        key = key_ref[pl.ds(k0, KEY_CHUNK), :]
        spos = srow + k0
        sel = jnp.logical_or(key > thr, jnp.logical_and(key == thr, spos <= jsel))
        sel = jnp.logical_and(sel, spos <= tpos)
        bias = jnp.where(sel, 0.0, NEG)
        kc = dk_ref[0, pl.ds(k0, KEY_CHUNK), :]
        v_t = dv_ref[0, c]
        return tuple(
            _softmax_step(carry[h], _dot(kc, dq_ref[0, h * HEAD_DIM:(h + 1) * HEAD_DIM, :]) + bias, v_t)
            for h in range(DSA_HEADS))

    heads = lax.fori_loop(0, n_chunks, attend, tuple(_softmax_init(tq) for _ in range(DSA_HEADS)))
    for h, (_, l, acc) in enumerate(heads):
        o_ref[0, h * HEAD_DIM:(h + 1) * HEAD_DIM, :] = acc * (1.0 / l)


def _dsa(iq_t, ik, iw_t, dq_t, dk, dv_t):
    bsz, _, seq = dq_t.shape
    tq = DSA_Q_TILE
    assert tq == KEY_CHUNK
    return pl.pallas_call(
        _dsa_kernel,
        grid=(bsz, seq // tq),
        in_specs=[
            pl.BlockSpec((1, IDX_W, tq), lambda b, i: (b, 0, i)),
            pl.BlockSpec((1, seq, IDX_DIM), lambda b, i: (b, 0, 0)),
            pl.BlockSpec((1, 8, tq), lambda b, i: (b, 0, i)),
            pl.BlockSpec((1, DSA_W, tq), lambda b, i: (b, 0, i)),
            pl.BlockSpec((1, seq, HEAD_DIM), lambda b, i: (b, 0, 0)),
            pl.BlockSpec((1, N_KEY_CHUNKS, HEAD_DIM, KEY_CHUNK), lambda b, i: (b, 0, 0, 0)),
        ],
        out_specs=pl.BlockSpec((1, DSA_W, tq), lambda b, i: (b, 0, i)),
        out_shape=jax.ShapeDtypeStruct((bsz, DSA_W, seq), F32),
        scratch_shapes=[pltpu.VMEM((seq, tq), I32), pltpu.VMEM((1, tq), I32)],
        compiler_params=_cparams(("arbitrary", "arbitrary")),
        name="dsa_attn",
    )(iq_t, ik, iw_t, dq_t, dk, dv_t)


def _fox_kernel(q_ref, k_ref, v_ref, cumt_ref, cums_ref, o_ref):
    h = pl.program_id(1)
    tq = FOX_Q_TILE
    n_tiles = q_ref.shape[2] // tq
    head_lane = lax.broadcasted_iota(I32, (KEY_CHUNK, 8), 1) == h
    srow = lax.broadcasted_iota(I32, (KEY_CHUNK, tq), 0)
    tcol = lax.broadcasted_iota(I32, (KEY_CHUNK, tq), 1)
    ccols = []
    for c in range(n_tiles):
        col = jnp.sum(jnp.where(head_lane, cums_ref[0, c * KEY_CHUNK:(c + 1) * KEY_CHUNK, :], 0.0),
                      axis=1, keepdims=True)
        ccols.append(_tile_lanes(jnp.broadcast_to(col, (KEY_CHUNK, LANES)), tq // LANES))
    for i in range(n_tiles):
        q = q_ref[0, :, i * tq:(i + 1) * tq]
        crow = cumt_ref[0, 0, :, i * tq:(i + 1) * tq]
        carry = _softmax_init(tq)
        for c in range(i + 1):
            s = _dot(k_ref[0, 0, c * KEY_CHUNK:(c + 1) * KEY_CHUNK, :], q) + crow - ccols[c]
            if c == i:
                s = jnp.where(srow <= tcol, s, NEG)
            carry = _softmax_step(carry, s, v_ref[0, 0, c])
        _, l, acc = carry
        o_ref[0, :, i * tq:(i + 1) * tq] = acc * (1.0 / l)


def _fox(fq_t, fk, fv_t, cum_t, cum_s):
    bsz, _, seq = fq_t.shape
    assert FOX_Q_TILE == KEY_CHUNK
    return pl.pallas_call(
        _fox_kernel,
        grid=(bsz, FOX_HEADS),
        in_specs=[
            pl.BlockSpec((1, HEAD_DIM, seq), lambda b, h: (b, h, 0)),
            pl.BlockSpec((1, 1, seq, HEAD_DIM), lambda b, h: (b, h, 0, 0)),
            pl.BlockSpec((1, 1, N_KEY_CHUNKS, HEAD_DIM, KEY_CHUNK), lambda b, h: (b, h, 0, 0, 0)),
            pl.BlockSpec((1, 1, 1, seq), lambda b, h: (b, h, 0, 0)),
            pl.BlockSpec((1, seq, 8), lambda b, h: (b, 0, 0)),
        ],
        out_specs=pl.BlockSpec((1, HEAD_DIM, seq), lambda b, h: (b, h, 0)),
        out_shape=jax.ShapeDtypeStruct((bsz, FOX_W, seq), F32),
        compiler_params=_cparams(("arbitrary", "arbitrary")),
        name="fox_attn",
    )(fq_t, fk, fv_t, cum_t.reshape(bsz, 8, 1, seq), cum_s)


def _nsa_kernel(q_ref, qr_ref, kc_ref, vct_ref, ks_ref, vs_ref, kw_ref, vw_ref, g_ref,
                ovl_ref, expand_ref, o_ref, sel_ref):
    i = pl.program_id(2)
    nq = NSA_HPG * Q_TILE
    t0 = i * Q_TILE
    tpos = t0 + lax.broadcasted_iota(I32, (1, Q_TILE), 1)
    q3 = jnp.concatenate([q_ref[0, j * HEAD_DIM:(j + 1) * HEAD_DIM, :] for j in range(NSA_HPG)], axis=1)
    qr3 = jnp.concatenate([qr_ref[0, j * HEAD_DIM:(j + 1) * HEAD_DIM, :] for j in range(NSA_HPG)], axis=1)

    n_cmp = kc_ref.shape[2]
    cend = lax.broadcasted_iota(I32, (n_cmp, Q_TILE), 0) * CMP_STRIDE + (CMP_LEN - 1)
    cbias = _tile_lanes(jnp.where(cend <= tpos, 0.0, NEG), NSA_HPG)
    cvalid = _tile_lanes(jnp.where(cend <= tpos, 1.0, 0.0), NSA_HPG)
    s = _dot(kc_ref[0, 0], q3) + cbias
    m = _reduce_rows(s, jnp.max)
    e = jnp.exp(s - m) * cvalid
    p_cmp = (e / jnp.maximum(_reduce_rows(e, jnp.sum), 1e-30)).astype(BF16)
    o_cmp = _dot(vct_ref[0, 0], p_cmp)

    p_stack = jnp.concatenate([p_cmp[:, j * Q_TILE:(j + 1) * Q_TILE] for j in range(NSA_HPG)], axis=0)
    score = _dot(ovl_ref[...], p_stack)
    blk = lax.broadcasted_iota(I32, (N_SEL_BLOCKS, Q_TILE), 0)
    cur = lax.shift_right_logical(tpos, 6)
    forced = jnp.logical_or(blk == 0, jnp.logical_or(blk == cur, blk == cur - 1))
    score = jnp.where(forced, jnp.inf, jnp.where(blk > cur, -jnp.inf, score))
    rank = jnp.zeros((N_SEL_BLOCKS, Q_TILE), F32)
    for mth in range(N_SEL_BLOCKS):
        row = score[mth:mth + 1, :]
        ahead = jnp.logical_or(row > score, jnp.logical_and(row == score, blk > mth))
        rank = rank + jnp.where(ahead, 1.0, 0.0)
    sel_ref[...] = jnp.where(rank < SEL_N, 1.0, 0.0)

    n_chunks = (i * Q_TILE + Q_TILE + KEY_CHUNK - 1) // KEY_CHUNK
    srow = lax.broadcasted_iota(I32, (KEY_CHUNK, Q_TILE), 0)
    blocks_per_chunk = KEY_CHUNK // SEL_BLOCK

    def attend_sel(c, carry):
        k0 = pl.multiple_of(c * KEY_CHUNK, KEY_CHUNK)
        b0 = pl.multiple_of(c * blocks_per_chunk, blocks_per_chunk)
        picked = _dot(expand_ref[...], sel_ref[pl.ds(b0, blocks_per_chunk), :])
        ok = jnp.logical_and(picked > 0.5, srow + k0 <= tpos)
        bias = _tile_lanes(jnp.where(ok, 0.0, NEG), NSA_HPG)
        s = _dot(ks_ref[0, 0, pl.ds(k0, KEY_CHUNK), :], qr3) + bias
        return _softmax_step(carry, s, vs_ref[0, 0, c])

    _, l, acc = lax.fori_loop(0, n_chunks, attend_sel, _softmax_init(nq))
    o_slc = acc * (1.0 / l)

    wc = jnp.maximum(i - WINDOW // Q_TILE, 0) * (Q_TILE // WIN_CHUNK)
    w0 = pl.multiple_of(wc * WIN_CHUNK, WIN_CHUNK)
    kpos = w0 + lax.broadcasted_iota(I32, (WIN_SPAN, Q_TILE), 0)
    wok = jnp.logical_and(kpos <= tpos, kpos > tpos - WINDOW)
    s = _dot(kw_ref[0, 0, pl.ds(w0, WIN_SPAN), :], qr3) + _tile_lanes(jnp.where(wok, 0.0, NEG), NSA_HPG)
    m = _reduce_rows(s, jnp.max)
    e = jnp.exp(s - m)
    l = _reduce_rows(e, jnp.sum)
    e = e.astype(BF16)
    o_win = jnp.zeros((HEAD_DIM, nq), F32)
    for j in range(WIN_SPAN // WIN_CHUNK):
        o_win = o_win + _dot(vw_ref[0, 0, wc + j], e[j * WIN_CHUNK:(j + 1) * WIN_CHUNK, :])
    o_win = o_win * (1.0 / l)

    gts = g_ref[0, 0]
    for j in range(NSA_HPG):
        sl = slice(j * Q_TILE, (j + 1) * Q_TILE)
        o_ref[0, j * HEAD_DIM:(j + 1) * HEAD_DIM, :] = (
            gts[j:j + 1, :] * o_cmp[:, sl]
            + gts[NSA_HPG + j:NSA_HPG + j + 1, :] * o_slc[:, sl]
            + gts[2 * NSA_HPG + j:2 * NSA_HPG + j + 1, :] * o_win[:, sl])


def _nsa(nq_t, nqr_t, kc, vc_t, ks, vs_t, kw, vw_t, gts, ovl, expand):
    bsz, _, seq = nq_t.shape
    n_cmp = kc.shape[2]
    gw = NSA_HPG * HEAD_DIM
    per_group = lambda shape: pl.BlockSpec((1, 1) + shape, lambda b, g, i: (b, g) + (0,) * len(shape))
    return pl.pallas_call(
        _nsa_kernel,
        grid=(bsz, NSA_GROUPS, seq // Q_TILE),
        in_specs=[
            pl.BlockSpec((1, gw, Q_TILE), lambda b, g, i: (b, g, i)),
            pl.BlockSpec((1, gw, Q_TILE), lambda b, g, i: (b, g, i)),
            per_group((n_cmp, HEAD_DIM)),
            per_group((HEAD_DIM, n_cmp)),
            per_group((seq, HEAD_DIM)),
            per_group((N_KEY_CHUNKS, HEAD_DIM, KEY_CHUNK)),
            per_group((seq, HEAD_DIM)),
            per_group((N_WIN_CHUNKS, HEAD_DIM, WIN_CHUNK)),
            pl.BlockSpec((1, 1, 16, Q_TILE), lambda b, g, i: (b, g, 0, i)),
            pl.BlockSpec(ovl.shape, lambda b, g, i: (0, 0)),
            pl.BlockSpec(expand.shape, lambda b, g, i: (0, 0)),
        ],
        out_specs=pl.BlockSpec((1, gw, Q_TILE), lambda b, g, i: (b, g, i)),
        out_shape=jax.ShapeDtypeStruct((bsz, NSA_W, seq), F32),
        scratch_shapes=[pltpu.VMEM((N_SEL_BLOCKS, Q_TILE), F32)],
        compiler_params=_cparams(("arbitrary", "arbitrary", "arbitrary")),
        name="nsa_attn",
    )(nq_t, nqr_t, kc, vc_t, ks, vs_t, kw, vw_t, gts, ovl, expand)


def _out_kernel(od_ref, of_ref, on_ref, gate_ref, w_ref, x_ref, gm_ref, lg_ref, lb_ref, o_ref):
    z = jnp.concatenate([od_ref[0], of_ref[0], on_ref[0]], axis=0) * gate_ref[0]
    y = lax.dot_general(z.astype(BF16), w_ref[...], (((0,), (0,)), ((), ())),
                        preferred_element_type=F32)
    r = ALPHA * x_ref[0] + (1.0 + gm_ref[0]) * y
    mu = jnp.mean(r, axis=-1, keepdims=True)
    rc = r - mu
    var = jnp.mean(rc * rc, axis=-1, keepdims=True)
    o_ref[0] = rc * lax.rsqrt(var + LN_EPS) * lg_ref[...] + lb_ref[...]


def _output(od_t, of_t, on_t, gate_t, w_out, x, gmod, ln_g, ln_b):
    bsz, seq, d = x.shape
    tm = PROJ_TOKENS
    feat = lambda rows: pl.BlockSpec((1, rows, tm), lambda b, i: (b, 0, i))
    return pl.pallas_call(
        _out_kernel,
        grid=(bsz, seq // tm),
        in_specs=[
            feat(DSA_W), feat(FOX_W), feat(NSA_W), feat(MIX_W),
            pl.BlockSpec((MIX_W, d), lambda b, i: (0, 0)),
            pl.BlockSpec((1, tm, d), lambda b, i: (b, i, 0)),
            pl.BlockSpec((1, 1, d), lambda b, i: (b, 0, 0)),
            pl.BlockSpec((1, d), lambda b, i: (0, 0)),
            pl.BlockSpec((1, d), lambda b, i: (0, 0)),
        ],
        out_specs=pl.BlockSpec((1, tm, d), lambda b, i: (b, i, 0)),
        out_shape=jax.ShapeDtypeStruct((bsz, seq, d), F32),
        compiler_params=_cparams(("arbitrary", "arbitrary")),
        name="out_proj_ln",
    )(od_t, of_t, on_t, gate_t, w_out, x, gmod, ln_g, ln_b)


def _split_cols(w):
    out, off = {}, 0
    for name, width in IN_SPLITS:
        out[name] = w[:, off:off + width]
        off += width
    return out


def _prep_w_in(w_in):
    p = _split_cols(w_in)
    d = w_in.shape[0]
    z = lambda n: jnp.zeros((d, n), w_in.dtype)
    gcols = p["nsa_g"].reshape(d, 3, NSA_GROUPS, NSA_HPG)
    gparts = []
    for g in range(NSA_GROUPS):
        gparts += [gcols[:, :, g, :].reshape(d, 3 * NSA_HPG), z(16 - 3 * NSA_HPG)]
    wt = jnp.concatenate(
        [p["dsa_q"], p["idx_q"], p["fox_q"], p["nsa_q"],
         p["dsa_v"], z(HEAD_DIM), p["fox_v"], p["nsa_vs"], p["nsa_vw"], p["gate"],
         p["idx_w"], p["fox_f"], z(8 - FOX_HEADS)] + gparts, axis=1)
    assert wt.shape[1] == T_ROWS
    ws = jnp.concatenate(
        [p["dsa_k"], z(LANES - HEAD_DIM), p["idx_k"], z(LANES - IDX_DIM), p["fox_k"],
         p["nsa_ks"], p["nsa_kw"], p["nsa_kc"], p["nsa_vc"], p["fox_f"], z(LANES - FOX_HEADS)], axis=1)
    assert ws.shape[1] == S_COLS
    return wt.T.astype(BF16), ws.astype(BF16)


def _rope_tables(seq):
    pos = jnp.arange(seq, dtype=F32)
    tabs = []
    for hd in (HEAD_DIM, IDX_DIM):
        half = hd // 2
        inv = ROPE_THETA ** (-jnp.arange(half, dtype=F32) / half)
        ang = inv[:, None] * pos[None, :]
        tabs += [jnp.cos(ang), jnp.sin(ang)]
    lane = jnp.arange(LANES)
    for hd in (HEAD_DIM, IDX_DIM):
        half = hd // 2
        inv = ROPE_THETA ** (-jnp.arange(half, dtype=F32) / half)
        ang = pos[:, None] * inv[lane % half][None, :]
        sign = jnp.where((lane % hd) < half, -1.0, 1.0).astype(F32)
        tabs += [jnp.cos(ang), jnp.sin(ang) * sign[None, :]]
    return tuple(tabs)


def _prep_compress(cmp_pe, cmp_w1, cmp_w2):
    eye = jnp.eye(NSA_GROUPS, dtype=F32)
    w1 = cmp_w1.reshape(2, CMP_LEN, HEAD_DIM, HEAD_DIM)
    big1 = jnp.einsum("kjde,gh->kjgdhe", w1, eye).reshape(2, CMP_LEN, LANES, LANES)
    w1lo = big1[:, :CMP_STRIDE].reshape(2, CMP_STRIDE * LANES, LANES).astype(BF16)
    w1hi = big1[:, CMP_STRIDE:].reshape(2, CMP_STRIDE * LANES, LANES).astype(BF16)
    w2 = jnp.einsum("kde,gh->kgdhe", cmp_w2, eye).reshape(2, LANES, LANES).astype(BF16)
    pe2 = jnp.concatenate([cmp_pe] * NSA_GROUPS, axis=-1)
    return pe2[:, :CMP_STRIDE], pe2[:, CMP_STRIDE:], w1lo, w1hi, w2


def _selection_constants(n_cmp):
    cstart = jnp.arange(n_cmp) * CMP_STRIDE
    bstart = jnp.arange(N_SEL_BLOCKS) * SEL_BLOCK
    real = (jnp.arange(n_cmp) < (SEQ - CMP_LEN) // CMP_STRIDE + 1)[:, None]
    ovl = ((cstart[:, None] < bstart[None, :] + SEL_BLOCK) & (cstart[:, None] + CMP_LEN > bstart[None, :]) & real)
    ovl_t = jnp.concatenate([ovl.T.astype(BF16)] * NSA_HPG, axis=1)
    expand = (jnp.arange(KEY_CHUNK)[:, None] // SEL_BLOCK == jnp.arange(KEY_CHUNK // SEL_BLOCK)[None, :]).astype(F32)
    return ovl_t, expand


def kernel(x, c, w_ada, b_ada, w_in, b_f, cmp_pe, cmp_w1, cmp_w2, w_out, ln_g, ln_b):
    bsz, seq, d = x.shape
    assert (seq, d) == (SEQ, D_MODEL)
    mod = _modulation(c, w_ada, b_ada)
    tabs = _rope_tables(seq)
    ovl, expand = _selection_constants(seq // CMP_STRIDE)
    for l in range(DEPTH):
        shift = mod[l, :, :d].reshape(bsz, 1, d)
        scale = mod[l, :, d:2 * d].reshape(bsz, 1, d)
        gmod = mod[l, :, 2 * d:].reshape(bsz, 1, d)
        wt, ws = _prep_w_in(w_in[l])
        bf8 = jnp.concatenate([b_f[l], jnp.zeros((8 - FOX_HEADS,), F32)])
        (dq, iq, fq, nq, nqr, dv, fv, vs, vw, gate, iw, lft, gts,
         dk, ik, fk, ks, kw, kcin, vcin, lfs) = _projection(
            x, scale, shift, wt, ws, tabs, bf8.reshape(8, 1), bf8.reshape(1, 8))
        cum_t, cum_s = _forget_cumsum(lft, lfs)
        kc, vc_t = _compress(kcin, vcin, *_prep_compress(cmp_pe[l], cmp_w1[l], cmp_w2[l]))
        o_dsa = _dsa(iq, ik, iw, dq, dk, dv)
        o_fox = _fox(fq, fk, fv, cum_t, cum_s)
        o_nsa = _nsa(nq, nqr, kc, vc_t, ks, vs, kw, vw, gts, ovl, expand)
        x = _output(o_dsa, o_fox, o_nsa, gate, w_out[l].astype(BF16), x, gmod,
                    ln_g[l].reshape(1, d), ln_b[l].reshape(1, d))
    return x
```

```python
import functools

import jax
import jax.numpy as jnp
from jax import lax
from jax.experimental import pallas as pl
from jax.experimental.pallas import tpu as pltpu

F32 = jnp.float32
BF16 = jnp.bfloat16
I32 = jnp.int32

D_MODEL = 1024
SEQ = 2048
DEPTH = 2
HEAD_DIM = 64
DSA_HEADS = 4
DSA_TOPK = 256
IDX_HEADS = 8
IDX_DIM = 32
FOX_HEADS = 6
NSA_HEADS = 6
NSA_GROUPS = 2
NSA_HPG = NSA_HEADS // NSA_GROUPS
CMP_LEN = 32
CMP_STRIDE = 16
SEL_BLOCK = 64
SEL_N = 16
N_SEL_BLOCKS = SEQ // SEL_BLOCK
WINDOW = 512
ROPE_THETA = 10000.0
LN_EPS = 1e-5
ALPHA = (2.0 * DEPTH) ** 0.25

DSA_W = DSA_HEADS * HEAD_DIM
FOX_W = FOX_HEADS * HEAD_DIM
NSA_W = NSA_HEADS * HEAD_DIM
NSA_KV_W = NSA_GROUPS * HEAD_DIM
MIX_W = DSA_W + FOX_W + NSA_W
IDX_W = IDX_HEADS * IDX_DIM

IN_SPLITS = (
    ("dsa_q", DSA_W), ("dsa_k", HEAD_DIM), ("dsa_v", HEAD_DIM),
    ("idx_q", IDX_W), ("idx_k", IDX_DIM), ("idx_w", IDX_HEADS),
    ("fox_q", FOX_W), ("fox_k", FOX_W), ("fox_v", FOX_W), ("fox_f", FOX_HEADS),
    ("nsa_q", NSA_W),
    ("nsa_kc", NSA_KV_W), ("nsa_vc", NSA_KV_W),
    ("nsa_ks", NSA_KV_W), ("nsa_vs", NSA_KV_W),
    ("nsa_kw", NSA_KV_W), ("nsa_vw", NSA_KV_W),
    ("nsa_g", 3 * NSA_HEADS),
    ("gate", MIX_W),
)

LANES = 128
KEY_CHUNK = 512
PROJ_TOKENS = 512
Q_TILE = 256
DSA_Q_TILE = 512
FOX_Q_TILE = 512
N_KEY_CHUNKS = SEQ // KEY_CHUNK
WIN_CHUNK = 256
N_WIN_CHUNKS = SEQ // WIN_CHUNK
WIN_SPAN = WINDOW + Q_TILE
CUM_CHUNK = 256
VMEM_LIMIT = 56 * 1024 * 1024

FOLD_ROWS = 32
ONES_ROWS = 16
LOG2E = 1.4426950408889634
NEG = -1e30
INT_MIN = -(2 ** 31)
CODE_NEG_INF = 0x007FFFFF

T_DQ, T_IQ, T_FQ, T_NQ = 0, 256, 512, 896
T_DV, T_FV, T_VS, T_VW = 1280, 1408, 1792, 1920
T_GATE, T_SMALL, T_ROWS = 2048, 3072, 3120
S_DK, S_IK, S_FK, S_KS, S_KW, S_KC, S_VC, S_FF, S_COLS = 0, 128, 256, 640, 768, 896, 1024, 1152, 1280


def _cparams(sem):
    return pltpu.CompilerParams(dimension_semantics=sem, vmem_limit_bytes=VMEM_LIMIT)


def _dot(a, b):
    return jnp.dot(a, b, preferred_element_type=F32)


def _dot_nt(a, b):
    return lax.dot_general(a, b, (((1,), (1,)), ((), ())), preferred_element_type=F32)


def _log_sigmoid(x):
    return jnp.minimum(x, 0.0) - jnp.log(1.0 + jnp.exp(-jnp.abs(x)))


def _sigmoid(x):
    return 1.0 / (1.0 + jnp.exp(-x))


def _mod_kernel(c_ref, w_ref, b_ref, o_ref):
    o_ref[0] = _dot(c_ref[...].astype(BF16), w_ref[0].astype(BF16)) + b_ref[0]


def _modulation(c, w_ada, b_ada):
    depth, d, d3 = w_ada.shape
    bsz = c.shape[0]
    return pl.pallas_call(
        _mod_kernel,
        grid=(depth, d3 // d),
        in_specs=[
            pl.BlockSpec((bsz, d), lambda l, j: (0, 0)),
            pl.BlockSpec((1, d, d), lambda l, j: (l, 0, j)),
            pl.BlockSpec((1, 1, d), lambda l, j: (l, 0, j)),
        ],
        out_specs=pl.BlockSpec((1, bsz, d), lambda l, j: (l, 0, j)),
        out_shape=jax.ShapeDtypeStruct((depth, bsz, d3), F32),
        compiler_params=_cparams(("arbitrary", "arbitrary")),
        name="adaln_mod",
    )(c, w_ada, b_ada.reshape(depth, 1, d3))


def _proj_kernel(x_ref, sc_ref, sh_ref, wt_ref, ws_ref,
                 ct64_ref, st64_ref, ct32_ref, st32_ref,
                 ck64_ref, sk64_ref, ck32_ref, sk32_ref, bfc_ref, bfr_ref,
                 dq_ref, iq_ref, fq_ref, nq_ref, nqr_ref,
                 dv_ref, fv_ref, vs_ref, vw_ref, gate_ref,
                 iw_ref, lft_ref, gts_ref,
                 dk_ref, ik_ref, fk_ref, ks_ref, kw_ref, kcin_ref, vcin_ref, lfs_ref):
    tm = x_ref.shape[1]
    u = (x_ref[0] * (1.0 + sc_ref[0]) + sh_ref[0]).astype(BF16)

    def proj_t(r0, r1):
        return _dot_nt(wt_ref[r0:r1, :], u)

    def rope_t(h, n_heads, hd, c, s):
        half = hd // 2
        out = []
        for hh in range(n_heads):
            x1 = h[hh * hd:hh * hd + half]
            x2 = h[hh * hd + half:(hh + 1) * hd]
            out.append((hh * hd, x1 * c - x2 * s))
            out.append((hh * hd + half, x1 * s + x2 * c))
        return out

    c64, s64 = ct64_ref[...], st64_ref[...]
    c32, s32 = ct32_ref[...], st32_ref[...]
    qscale = HEAD_DIM ** -0.5 * LOG2E

    h = proj_t(T_DQ, T_DQ + DSA_W)
    for r, v in rope_t(h, DSA_HEADS, HEAD_DIM, c64, s64):
        dq_ref[0, r:r + HEAD_DIM // 2, :] = (v * qscale).astype(BF16)
    h = proj_t(T_IQ, T_IQ + IDX_W)
    for r, v in rope_t(h, IDX_HEADS, IDX_DIM, c32, s32):
        iq_ref[0, r:r + IDX_DIM // 2, :] = v.astype(BF16)
    fq_ref[0] = (proj_t(T_FQ, T_FQ + FOX_W) * qscale).astype(BF16)
    h = proj_t(T_NQ, T_NQ + NSA_W)
    nq_ref[0] = (h * qscale).astype(BF16)
    for r, v in rope_t(h, NSA_HEADS, HEAD_DIM, c64, s64):
        nqr_ref[0, r:r + HEAD_DIM // 2, :] = (v * qscale).astype(BF16)

    dv_ref[0, 0] = proj_t(T_DV, T_DV + HEAD_DIM).astype(BF16)
    h = proj_t(T_FV, T_FV + FOX_W)
    for hh in range(FOX_HEADS):
        fv_ref[0, hh, 0] = h[hh * HEAD_DIM:(hh + 1) * HEAD_DIM].astype(BF16)
    h = proj_t(T_VS, T_VS + NSA_KV_W)
    for g in range(NSA_GROUPS):
        vs_ref[0, g, 0] = h[g * HEAD_DIM:(g + 1) * HEAD_DIM].astype(BF16)
    h = proj_t(T_VW, T_VW + NSA_KV_W)
    for g in range(NSA_GROUPS):
        for j in range(tm // WIN_CHUNK):
            vw_ref[0, g, j] = h[g * HEAD_DIM:(g + 1) * HEAD_DIM, j * WIN_CHUNK:(j + 1) * WIN_CHUNK].astype(BF16)

    for r0 in range(0, MIX_W, 256):
        h = proj_t(T_GATE + r0, T_GATE + r0 + 256)
        gate_ref[0, r0:r0 + 256, :] = h * _sigmoid(h)

    h = proj_t(T_SMALL, T_ROWS)
    iw_ref[0] = h[0:8] * (IDX_HEADS ** -0.5)
    lft_ref[0] = _log_sigmoid(h[8:16] + bfc_ref[...])
    g_all = _sigmoid(h[16:48])
    gts_ref[0, 0] = g_all[0:16]
    gts_ref[0, 1] = g_all[16:32]

    lane = lax.broadcasted_iota(I32, (tm, LANES), 1)

    def rope_s(g, half, c, s_signed):
        first = (lane & (2 * half - 1)) < half
        sw = jnp.where(first, pltpu.roll(g, LANES - half, 1), pltpu.roll(g, half, 1))
        return g * c + sw * s_signed

    hs_all = _dot(u, ws_ref[...])

    def proj_s(c0, c1):
        return hs_all[:, c0:c1]

    ck64, sk64 = ck64_ref[...], sk64_ref[...]
    g = rope_s(proj_s(S_DK, S_DK + LANES), HEAD_DIM // 2, ck64, sk64)
    dk_ref[0] = g[:, :HEAD_DIM].astype(BF16)
    g = rope_s(proj_s(S_IK, S_IK + LANES), IDX_DIM // 2, ck32_ref[...], sk32_ref[...])
    ik_ref[0] = g[:, :IDX_DIM].astype(BF16)
    g = proj_s(S_FK, S_FK + FOX_W)
    for hh in range(FOX_HEADS):
        fk_ref[0, hh] = g[:, hh * HEAD_DIM:(hh + 1) * HEAD_DIM].astype(BF16)
    g = rope_s(proj_s(S_KS, S_KS + LANES), HEAD_DIM // 2, ck64, sk64)
    for gg in range(NSA_GROUPS):
        ks_ref[0, gg] = g[:, gg * HEAD_DIM:(gg + 1) * HEAD_DIM].astype(BF16)
    g = rope_s(proj_s(S_KW, S_KW + LANES), HEAD_DIM // 2, ck64, sk64)
    for gg in range(NSA_GROUPS):
        kw_ref[0, gg] = g[:, gg * HEAD_DIM:(gg + 1) * HEAD_DIM].astype(BF16)
    kcin_ref[0] = proj_s(S_KC, S_KC + LANES)
    vcin_ref[0] = proj_s(S_VC, S_VC + LANES)
    g = proj_s(S_FF, S_FF + LANES)
    lfs_ref[0] = _log_sigmoid(g[:, :8] + bfr_ref[...])


def _projection(x, scale, shift, wt, ws, tabs, bfc, bfr):
    bsz, seq, d = x.shape
    tm = PROJ_TOKENS
    nt = seq // tm
    ct64, st64, ct32, st32, ck64, sk64, ck32, sk32 = tabs
    const = lambda shape: pl.BlockSpec(shape, lambda b, i: (0,) * len(shape))
    feat = lambda rows: pl.BlockSpec((1, rows, tm), lambda b, i: (b, 0, i))
    in_specs = [
        pl.BlockSpec((1, tm, d), lambda b, i: (b, i, 0)),
        pl.BlockSpec((1, 1, d), lambda b, i: (b, 0, 0)),
        pl.BlockSpec((1, 1, d), lambda b, i: (b, 0, 0)),
        const((T_ROWS, d)),
        const((d, S_COLS)),
        pl.BlockSpec((HEAD_DIM // 2, tm), lambda b, i: (0, i)),
        pl.BlockSpec((HEAD_DIM // 2, tm), lambda b, i: (0, i)),
        pl.BlockSpec((IDX_DIM // 2, tm), lambda b, i: (0, i)),
        pl.BlockSpec((IDX_DIM // 2, tm), lambda b, i: (0, i)),
        pl.BlockSpec((tm, LANES), lambda b, i: (i, 0)),
        pl.BlockSpec((tm, LANES), lambda b, i: (i, 0)),
        pl.BlockSpec((tm, LANES), lambda b, i: (i, 0)),
        pl.BlockSpec((tm, LANES), lambda b, i: (i, 0)),
        const((8, 1)),
        const((1, 8)),
    ]
    sds = jax.ShapeDtypeStruct
    out_shape = [
        sds((bsz, DSA_W, seq), BF16), sds((bsz, IDX_W, seq), BF16), sds((bsz, FOX_W, seq), BF16),
        sds((bsz, NSA_W, seq), BF16), sds((bsz, NSA_W, seq), BF16),
        sds((bsz, N_KEY_CHUNKS, HEAD_DIM, KEY_CHUNK), BF16),
        sds((bsz, FOX_HEADS, N_KEY_CHUNKS, HEAD_DIM, KEY_CHUNK), BF16),
        sds((bsz, NSA_GROUPS, N_KEY_CHUNKS, HEAD_DIM, KEY_CHUNK), BF16),
        sds((bsz, NSA_GROUPS, N_WIN_CHUNKS, HEAD_DIM, WIN_CHUNK), BF16),
        sds((bsz, MIX_W, seq), F32),
        sds((bsz, 8, seq), F32), sds((bsz, 8, seq), F32), sds((bsz, NSA_GROUPS, 16, seq), F32),
        sds((bsz, seq, HEAD_DIM), BF16), sds((bsz, seq, IDX_DIM), BF16),
        sds((bsz, FOX_HEADS, seq, HEAD_DIM), BF16),
        sds((bsz, NSA_GROUPS, seq, HEAD_DIM), BF16), sds((bsz, NSA_GROUPS, seq, HEAD_DIM), BF16),
        sds((bsz, seq, LANES), F32), sds((bsz, seq, LANES), F32), sds((bsz, seq, 8), F32),
    ]
    wpc = tm // WIN_CHUNK
    out_specs = [
        feat(DSA_W), feat(IDX_W), feat(FOX_W), feat(NSA_W), feat(NSA_W),
        pl.BlockSpec((1, 1, HEAD_DIM, KEY_CHUNK), lambda b, i: (b, i, 0, 0)),
        pl.BlockSpec((1, FOX_HEADS, 1, HEAD_DIM, KEY_CHUNK), lambda b, i: (b, 0, i, 0, 0)),
        pl.BlockSpec((1, NSA_GROUPS, 1, HEAD_DIM, KEY_CHUNK), lambda b, i: (b, 0, i, 0, 0)),
        pl.BlockSpec((1, NSA_GROUPS, wpc, HEAD_DIM, WIN_CHUNK), lambda b, i: (b, 0, i, 0, 0)),
        feat(MIX_W),
        feat(8), feat(8),
        pl.BlockSpec((1, NSA_GROUPS, 16, tm), lambda b, i: (b, 0, 0, i)),
        pl.BlockSpec((1, tm, HEAD_DIM), lambda b, i: (b, i, 0)),
        pl.BlockSpec((1, tm, IDX_DIM), lambda b, i: (b, i, 0)),
        pl.BlockSpec((1, FOX_HEADS, tm, HEAD_DIM), lambda b, i: (b, 0, i, 0)),
        pl.BlockSpec((1, NSA_GROUPS, tm, HEAD_DIM), lambda b, i: (b, 0, i, 0)),
        pl.BlockSpec((1, NSA_GROUPS, tm, HEAD_DIM), lambda b, i: (b, 0, i, 0)),
        pl.BlockSpec((1, tm, LANES), lambda b, i: (b, i, 0)),
        pl.BlockSpec((1, tm, LANES), lambda b, i: (b, i, 0)),
        pl.BlockSpec((1, tm, 8), lambda b, i: (b, i, 0)),
    ]
    assert tm == KEY_CHUNK
    return pl.pallas_call(
        _proj_kernel,
        grid=(bsz, nt),
        in_specs=in_specs,
        out_specs=out_specs,
        out_shape=out_shape,
        compiler_params=_cparams(("arbitrary", "arbitrary")),
        name="in_proj",
    )(x, scale, shift, wt, ws, ct64, st64, ct32, st32, ck64, sk64, ck32, sk32, bfc, bfr)


def _cumsum_kernel(lft_ref, lfs_ref, cumt_ref, cums_ref):
    seq = lft_ref.shape[2]
    r = lax.broadcasted_iota(I32, (CUM_CHUNK, CUM_CHUNK), 0)
    c = lax.broadcasted_iota(I32, (CUM_CHUNK, CUM_CHUNK), 1)
    tri_u = (r <= c).astype(F32)
    tri_l = (c <= r).astype(F32)
    carry_t = jnp.zeros((8, 1), F32)
    carry_s = jnp.zeros((1, 8), F32)
    for k in range(seq // CUM_CHUNK):
        sl = slice(k * CUM_CHUNK, (k + 1) * CUM_CHUNK)
        ct = jnp.dot(lft_ref[0, :, sl], tri_u, preferred_element_type=F32,
                     precision=lax.Precision.HIGHEST) + carry_t
        for head in range(8):
            cumt_ref[0, head, :, sl] = ct[head:head + 1, :] * LOG2E
        carry_t = ct[:, CUM_CHUNK - 1:CUM_CHUNK]
        cs = jnp.dot(tri_l, lfs_ref[0, sl, :], preferred_element_type=F32,
                     precision=lax.Precision.HIGHEST) + carry_s
        cums_ref[0, sl, :] = cs * LOG2E
        carry_s = cs[CUM_CHUNK - 1:CUM_CHUNK, :]


def _forget_cumsum(lft, lfs):
    bsz, _, seq = lft.shape
    return pl.pallas_call(
        _cumsum_kernel,
        grid=(bsz,),
        in_specs=[pl.BlockSpec((1, 8, seq), lambda b: (b, 0, 0)),
                  pl.BlockSpec((1, seq, 8), lambda b: (b, 0, 0))],
        out_specs=[pl.BlockSpec((1, 8, 1, seq), lambda b: (b, 0, 0, 0)),
                   pl.BlockSpec((1, seq, 8), lambda b: (b, 0, 0))],
        out_shape=[jax.ShapeDtypeStruct((bsz, 8, 1, seq), F32),
                   jax.ShapeDtypeStruct((bsz, seq, 8), F32)],
        compiler_params=_cparams(("arbitrary",)),
        name="forget_cumsum",
    )(lft, lfs)


def _compress_kernel(kcin_ref, vcin_ref, pelo_ref, pehi_ref, w1lo_ref, w1hi_ref, w2_ref,
                     kc_ref, vct_ref):
    n_blk = kcin_ref.shape[1] // CMP_STRIDE
    for kv, src in enumerate((kcin_ref, vcin_ref)):
        lo, hi = [], []
        for j in range(CMP_STRIDE):
            piece = src[0, pl.ds(j, n_blk, stride=CMP_STRIDE), :]
            lo.append((piece + pelo_ref[kv, j:j + 1, :]).astype(BF16))
            hi.append((piece + pehi_ref[kv, j:j + 1, :]).astype(BF16))
        a = _dot(jnp.concatenate(lo, axis=1), w1lo_ref[kv])
        b = _dot(jnp.concatenate(hi, axis=1), w1hi_ref[kv])
        pre = a + pltpu.roll(b, n_blk - 1, 0)
        act = pre * _sigmoid(pre)
        out = _dot(act.astype(BF16), w2_ref[kv])
        if kv == 0:
            for g in range(NSA_GROUPS):
                kc_ref[0, g] = out[:, g * HEAD_DIM:(g + 1) * HEAD_DIM].astype(BF16)
        else:
            out_t = out.T
            for g in range(NSA_GROUPS):
                vct_ref[0, g] = out_t[g * HEAD_DIM:(g + 1) * HEAD_DIM].astype(BF16)


def _compress(kcin, vcin, pelo, pehi, w1lo, w1hi, w2):
    bsz, seq, _ = kcin.shape
    n_blk = seq // CMP_STRIDE
    full = lambda a: pl.BlockSpec(a.shape, lambda b: (0,) * a.ndim)
    return pl.pallas_call(
        _compress_kernel,
        grid=(bsz,),
        in_specs=[pl.BlockSpec((1, seq, LANES), lambda b: (b, 0, 0)),
                  pl.BlockSpec((1, seq, LANES), lambda b: (b, 0, 0)),
                  full(pelo), full(pehi), full(w1lo), full(w1hi), full(w2)],
        out_specs=[pl.BlockSpec((1, NSA_GROUPS, n_blk, HEAD_DIM), lambda b: (b, 0, 0, 0)),
                   pl.BlockSpec((1, NSA_GROUPS, HEAD_DIM, n_blk), lambda b: (b, 0, 0, 0))],
        out_shape=[jax.ShapeDtypeStruct((bsz, NSA_GROUPS, n_blk, HEAD_DIM), BF16),
                   jax.ShapeDtypeStruct((bsz, NSA_GROUPS, HEAD_DIM, n_blk), BF16)],
        compiler_params=_cparams(("arbitrary",)),
        name="nsa_compress",
    )(kcin, vcin, pelo, pehi, w1lo, w1hi, w2)


def _fold_rows(x, op):
    k, n = x.shape
    return op(x.reshape(k // FOLD_ROWS, FOLD_ROWS, n), axis=0)


def _reduce_rows(x, op):
    if x.shape[0] % FOLD_ROWS == 0 and x.shape[0] > FOLD_ROWS:
        x = _fold_rows(x, op)
    return op(x, axis=0, keepdims=True)


def _with_ones(v_t):
    return jnp.concatenate([v_t, jnp.ones((ONES_ROWS, v_t.shape[1]), v_t.dtype)], axis=0)


def _softmax_step(carry, s, v_aug):
    m, acc = carry
    m_new = jnp.maximum(m, _reduce_rows(s, jnp.max))
    alpha = jnp.exp2(m - m_new)
    p = jnp.exp2(s - m_new)
    acc = alpha * acc + _dot(v_aug, p.astype(BF16))
    return m_new, acc


def _softmax_init(n):
    return (jnp.full((1, n), NEG, F32), jnp.zeros((HEAD_DIM + ONES_ROWS, n), F32))


def _softmax_finish(carry):
    _, acc = carry
    return acc[:HEAD_DIM] * (1.0 / acc[HEAD_DIM:HEAD_DIM + 1])


def _tile_lanes(a, n):
    return jnp.concatenate([a] * n, axis=1)


def _dsa_kernel(iq_ref, ik_ref, iw_ref, dq_ref, dk_ref, dv_ref, o_ref, key_ref, jsel_ref):
    i = pl.program_id(1)
    tq = DSA_Q_TILE
    n_chunks = i + 1
    tpos = i * tq + lax.broadcasted_iota(I32, (1, tq), 1)
    srow = lax.broadcasted_iota(I32, (KEY_CHUNK, tq), 0)
    iw = iw_ref[0]

    def index_chunk(c, _):
        k0 = pl.multiple_of(c * KEY_CHUNK, KEY_CHUNK)
        ikc = ik_ref[0, pl.ds(k0, KEY_CHUNK), :]
        acc = jnp.zeros((KEY_CHUNK, tq), F32)
        for h in range(IDX_HEADS):
            x = _dot(ikc, iq_ref[0, h * IDX_DIM:(h + 1) * IDX_DIM, :])
            acc = acc + iw[h:h + 1, :] * jnp.maximum(x, 0.0)
        key_ref[pl.ds(k0, KEY_CHUNK), :] = jnp.where(srow + k0 <= tpos, acc, -jnp.inf)
        return 0

    lax.fori_loop(0, n_chunks, index_chunk, 0)

    def count(pred):
        def body(c, acc):
            k0 = pl.multiple_of(c * KEY_CHUNK, KEY_CHUNK)
            hit = pred(key_ref[pl.ds(k0, KEY_CHUNK), :], srow + k0)
            return acc + _fold_rows(jnp.where(hit, 1, 0), jnp.sum)
        acc = lax.fori_loop(0, n_chunks, body, jnp.zeros((FOLD_ROWS, tq), I32))
        return jnp.sum(acc, axis=0, keepdims=True)

    def decode(code):
        skey = code ^ INT_MIN
        val = lax.bitcast_convert_type(jnp.where(skey < 0, skey ^ 0x7FFFFFFF, skey), F32)
        return jnp.where(jnp.logical_and(code >= 0, code <= CODE_NEG_INF), -jnp.inf, val)

    def value_bit(it, code):
        cand = code | lax.shift_left(jnp.int32(1), 31 - it)
        cand_f = decode(cand)
        cnt = count(lambda key, spos: key >= cand_f)
        return jnp.where(cnt >= DSA_TOPK, cand, code)

    thr = decode(lax.fori_loop(0, 32, value_bit, jnp.zeros((1, tq), I32)))

    cnt_gt = count(lambda key, spos: key > thr)
    cnt_ge = count(lambda key, spos: key >= thr)
    need = DSA_TOPK - cnt_gt
    has_tie = jnp.logical_and(cnt_ge > DSA_TOPK, thr > -jnp.inf)
    jsel_ref[...] = jnp.full((1, tq), SEQ, I32)

    @pl.when(jnp.max(jnp.where(has_tie, 1.0, 0.0)) > 0.0)
    def _():
        def index_bit(it, j):
            cand = j | lax.shift_left(jnp.int32(1), 10 - it)
            cnt = count(lambda key, spos: jnp.logical_and(key == thr, spos < cand))
            return jnp.where(cnt < need, cand, j)
        jsel_ref[...] = lax.fori_loop(0, 11, index_bit, jnp.zeros((1, tq), I32))

    jsel = jsel_ref[...]

    def attend(c, carry):
        k0 = pl.multiple_of(c * KEY_CHUNK, KEY_CHUNK)
        key = key_ref[pl.ds(k0, KEY_CHUNK), :]
        spos = srow + k0
        sel = jnp.logical_or(key > thr, jnp.logical_and(key == thr, spos <= jsel))
        sel = jnp.logical_and(sel, spos <= tpos)
        bias = jnp.where(sel, 0.0, NEG)
        kc = dk_ref[0, pl.ds(k0, KEY_CHUNK), :]
        v_aug = _with_ones(dv_ref[0, c])
        return tuple(
            _softmax_step(carry[h], _dot(kc, dq_ref[0, h * HEAD_DIM:(h + 1) * HEAD_DIM, :]) + bias, v_aug)
            for h in range(DSA_HEADS))

    heads = lax.fori_loop(0, n_chunks, attend, tuple(_softmax_init(tq) for _ in range(DSA_HEADS)))
    for h, carry in enumerate(heads):
        o_ref[0, h * HEAD_DIM:(h + 1) * HEAD_DIM, :] = _softmax_finish(carry)


def _dsa(iq_t, ik, iw_t, dq_t, dk, dv_t):
    bsz, _, seq = dq_t.shape
    tq = DSA_Q_TILE
    assert tq == KEY_CHUNK
    return pl.pallas_call(
        _dsa_kernel,
        grid=(bsz, seq // tq),
        in_specs=[
            pl.BlockSpec((1, IDX_W, tq), lambda b, i: (b, 0, i)),
            pl.BlockSpec((1, seq, IDX_DIM), lambda b, i: (b, 0, 0)),
            pl.BlockSpec((1, 8, tq), lambda b, i: (b, 0, i)),
            pl.BlockSpec((1, DSA_W, tq), lambda b, i: (b, 0, i)),
            pl.BlockSpec((1, seq, HEAD_DIM), lambda b, i: (b, 0, 0)),
            pl.BlockSpec((1, N_KEY_CHUNKS, HEAD_DIM, KEY_CHUNK), lambda b, i: (b, 0, 0, 0)),
        ],
        out_specs=pl.BlockSpec((1, DSA_W, tq), lambda b, i: (b, 0, i)),
        out_shape=jax.ShapeDtypeStruct((bsz, DSA_W, seq), F32),
        scratch_shapes=[pltpu.VMEM((seq, tq), F32), pltpu.VMEM((1, tq), I32)],
        compiler_params=_cparams(("arbitrary", "arbitrary")),
        name="dsa_attn",
    )(iq_t, ik, iw_t, dq_t, dk, dv_t)


def _fox_kernel(q_ref, k_ref, v_ref, cumt_ref, cums_ref, o_ref):
    h = pl.program_id(1)
    tq = FOX_Q_TILE
    n_tiles = q_ref.shape[2] // tq
    head_lane = lax.broadcasted_iota(I32, (KEY_CHUNK, 8), 1) == h
    srow = lax.broadcasted_iota(I32, (KEY_CHUNK, tq), 0)
    tcol = lax.broadcasted_iota(I32, (KEY_CHUNK, tq), 1)
    ccols = []
    for c in range(n_tiles):
        col = jnp.sum(jnp.where(head_lane, cums_ref[0, c * KEY_CHUNK:(c + 1) * KEY_CHUNK, :], 0.0),
                      axis=1, keepdims=True)
        ccols.append(_tile_lanes(jnp.broadcast_to(col, (KEY_CHUNK, LANES)), tq // LANES))
    v_aug = [_with_ones(v_ref[0, 0, c]) for c in range(n_tiles)]
    for i in range(n_tiles):
        q = q_ref[0, :, i * tq:(i + 1) * tq]
        crow = cumt_ref[0, 0, :, i * tq:(i + 1) * tq]
        carry = _softmax_init(tq)
        for c in range(i + 1):
            s = _dot(k_ref[0, 0, c * KEY_CHUNK:(c + 1) * KEY_CHUNK, :], q) + crow - ccols[c]
            if c == i:
                s = jnp.where(srow <= tcol, s, NEG)
            carry = _softmax_step(carry, s, v_aug[c])
        o_ref[0, :, i * tq:(i + 1) * tq] = _softmax_finish(carry)


def _fox(fq_t, fk, fv_t, cum_t, cum_s):
    bsz, _, seq = fq_t.shape
    assert FOX_Q_TILE == KEY_CHUNK
    return pl.pallas_call(
        _fox_kernel,
        grid=(bsz, FOX_HEADS),
        in_specs=[
            pl.BlockSpec((1, HEAD_DIM, seq), lambda b, h: (b, h, 0)),
            pl.BlockSpec((1, 1, seq, HEAD_DIM), lambda b, h: (b, h, 0, 0)),
            pl.BlockSpec((1, 1, N_KEY_CHUNKS, HEAD_DIM, KEY_CHUNK), lambda b, h: (b, h, 0, 0, 0)),
            pl.BlockSpec((1, 1, 1, seq), lambda b, h: (b, h, 0, 0)),
            pl.BlockSpec((1, seq, 8), lambda b, h: (b, 0, 0)),
        ],
        out_specs=pl.BlockSpec((1, HEAD_DIM, seq), lambda b, h: (b, h, 0)),
        out_shape=jax.ShapeDtypeStruct((bsz, FOX_W, seq), F32),
        compiler_params=_cparams(("arbitrary", "arbitrary")),
        name="fox_attn",
    )(fq_t, fk, fv_t, cum_t, cum_s)


def _nsa_kernel(q_ref, qr_ref, kc_ref, vct_ref, ks_ref, vs_ref, kw_ref, vw_ref, g_ref,
                ovl_ref, expand_ref, o_ref, sel_ref):
    i = pl.program_id(2)
    nq = NSA_HPG * Q_TILE
    t0 = i * Q_TILE
    tpos = t0 + lax.broadcasted_iota(I32, (1, Q_TILE), 1)
    q3 = jnp.concatenate([q_ref[0, j * HEAD_DIM:(j + 1) * HEAD_DIM, :] for j in range(NSA_HPG)], axis=1)
    qr3 = jnp.concatenate([qr_ref[0, j * HEAD_DIM:(j + 1) * HEAD_DIM, :] for j in range(NSA_HPG)], axis=1)

    n_cmp = kc_ref.shape[2]
    cend = lax.broadcasted_iota(I32, (n_cmp, Q_TILE), 0) * CMP_STRIDE + (CMP_LEN - 1)
    cbias = _tile_lanes(jnp.where(cend <= tpos, 0.0, NEG), NSA_HPG)
    cvalid = _tile_lanes(jnp.where(cend <= tpos, 1.0, 0.0), NSA_HPG)
    s = _dot(kc_ref[0, 0], q3) + cbias
    m = _reduce_rows(s, jnp.max)
    e = jnp.exp2(s - m) * cvalid
    p_cmp = (e / jnp.maximum(_reduce_rows(e, jnp.sum), 1e-30)).astype(BF16)
    o_cmp = _dot(vct_ref[0, 0], p_cmp)

    p_stack = jnp.concatenate([p_cmp[:, j * Q_TILE:(j + 1) * Q_TILE] for j in range(NSA_HPG)], axis=0)
    score = _dot(ovl_ref[...], p_stack)
    blk = lax.broadcasted_iota(I32, (N_SEL_BLOCKS, Q_TILE), 0)
    cur = lax.shift_right_logical(tpos, 6)
    forced = jnp.logical_or(blk == 0, jnp.logical_or(blk == cur, blk == cur - 1))
    score = jnp.where(forced, jnp.inf, jnp.where(blk > cur, -jnp.inf, score))
    rank = jnp.zeros((N_SEL_BLOCKS, Q_TILE), F32)
    for mth in range(N_SEL_BLOCKS):
        row = score[mth:mth + 1, :]
        ahead = jnp.logical_or(row > score, jnp.logical_and(row == score, blk > mth))
        rank = rank + jnp.where(ahead, 1.0, 0.0)
    sel_ref[...] = jnp.where(rank < SEL_N, 1.0, 0.0)

    n_chunks = (i * Q_TILE + Q_TILE + KEY_CHUNK - 1) // KEY_CHUNK
    srow = lax.broadcasted_iota(I32, (KEY_CHUNK, Q_TILE), 0)
    blocks_per_chunk = KEY_CHUNK // SEL_BLOCK

    def attend_sel(c, carry):
        k0 = pl.multiple_of(c * KEY_CHUNK, KEY_CHUNK)
        b0 = pl.multiple_of(c * blocks_per_chunk, blocks_per_chunk)
        picked = _dot(expand_ref[...], sel_ref[pl.ds(b0, blocks_per_chunk), :])
        ok = jnp.logical_and(picked > 0.5, srow + k0 <= tpos)
        bias = _tile_lanes(jnp.where(ok, 0.0, NEG), NSA_HPG)
        s = _dot(ks_ref[0, 0, pl.ds(k0, KEY_CHUNK), :], qr3) + bias
        return _softmax_step(carry, s, _with_ones(vs_ref[0, 0, c]))

    o_slc = _softmax_finish(lax.fori_loop(0, n_chunks, attend_sel, _softmax_init(nq)))

    wc = jnp.maximum(i - WINDOW // Q_TILE, 0) * (Q_TILE // WIN_CHUNK)
    w0 = pl.multiple_of(wc * WIN_CHUNK, WIN_CHUNK)
    kpos = w0 + lax.broadcasted_iota(I32, (WIN_SPAN, Q_TILE), 0)
    wok = jnp.logical_and(kpos <= tpos, kpos > tpos - WINDOW)
    s = _dot(kw_ref[0, 0, pl.ds(w0, WIN_SPAN), :], qr3) + _tile_lanes(jnp.where(wok, 0.0, NEG), NSA_HPG)
    m = _reduce_rows(s, jnp.max)
    e = jnp.exp2(s - m).astype(BF16)
    acc = jnp.zeros((HEAD_DIM + ONES_ROWS, nq), F32)
    for j in range(WIN_SPAN // WIN_CHUNK):
        acc = acc + _dot(_with_ones(vw_ref[0, 0, wc + j]), e[j * WIN_CHUNK:(j + 1) * WIN_CHUNK, :])
    o_win = _softmax_finish((m, acc))

    gts = g_ref[0, 0]
    for j in range(NSA_HPG):
        sl = slice(j * Q_TILE, (j + 1) * Q_TILE)
        o_ref[0, j * HEAD_DIM:(j + 1) * HEAD_DIM, :] = (
            gts[j:j + 1, :] * o_cmp[:, sl]
            + gts[NSA_HPG + j:NSA_HPG + j + 1, :] * o_slc[:, sl]
            + gts[2 * NSA_HPG + j:2 * NSA_HPG + j + 1, :] * o_win[:, sl])


def _nsa(nq_t, nqr_t, kc, vc_t, ks, vs_t, kw, vw_t, gts, ovl, expand):
    bsz, _, seq = nq_t.shape
    n_cmp = kc.shape[2]
    gw = NSA_HPG * HEAD_DIM
    per_group = lambda shape: pl.BlockSpec((1, 1) + shape, lambda b, g, i: (b, g) + (0,) * len(shape))
    return pl.pallas_call(
        _nsa_kernel,
        grid=(bsz, NSA_GROUPS, seq // Q_TILE),
        in_specs=[
            pl.BlockSpec((1, gw, Q_TILE), lambda b, g, i: (b, g, i)),
            pl.BlockSpec((1, gw, Q_TILE), lambda b, g, i: (b, g, i)),
            per_group((n_cmp, HEAD_DIM)),
            per_group((HEAD_DIM, n_cmp)),
            per_group((seq, HEAD_DIM)),
            per_group((N_KEY_CHUNKS, HEAD_DIM, KEY_CHUNK)),
            per_group((seq, HEAD_DIM)),
            per_group((N_WIN_CHUNKS, HEAD_DIM, WIN_CHUNK)),
            pl.BlockSpec((1, 1, 16, Q_TILE), lambda b, g, i: (b, g, 0, i)),
            pl.BlockSpec(ovl.shape, lambda b, g, i: (0, 0)),
            pl.BlockSpec(expand.shape, lambda b, g, i: (0, 0)),
        ],
        out_specs=pl.BlockSpec((1, gw, Q_TILE), lambda b, g, i: (b, g, i)),
        out_shape=jax.ShapeDtypeStruct((bsz, NSA_W, seq), F32),
        scratch_shapes=[pltpu.VMEM((N_SEL_BLOCKS, Q_TILE), F32)],
        compiler_params=_cparams(("arbitrary", "arbitrary", "arbitrary")),
        name="nsa_attn",
    )(nq_t, nqr_t, kc, vc_t, ks, vs_t, kw, vw_t, gts, ovl, expand)


def _out_kernel(od_ref, of_ref, on_ref, gate_ref, w_ref, x_ref, gm_ref, lg_ref, lb_ref, o_ref):
    z = jnp.concatenate([od_ref[0], of_ref[0], on_ref[0]], axis=0) * gate_ref[0]
    y = lax.dot_general(z.astype(BF16), w_ref[...], (((0,), (0,)), ((), ())),
                        preferred_element_type=F32)
    r = ALPHA * x_ref[0] + (1.0 + gm_ref[0]) * y
    mu = jnp.mean(r, axis=-1, keepdims=True)
    rc = r - mu
    var = jnp.mean(rc * rc, axis=-1, keepdims=True)
    o_ref[0] = rc * lax.rsqrt(var + LN_EPS) * lg_ref[...] + lb_ref[...]


def _output(od_t, of_t, on_t, gate_t, w_out, x, gmod, ln_g, ln_b):
    bsz, seq, d = x.shape
    tm = PROJ_TOKENS
    feat = lambda rows: pl.BlockSpec((1, rows, tm), lambda b, i: (b, 0, i))
    return pl.pallas_call(
        _out_kernel,
        grid=(bsz, seq // tm),
        in_specs=[
            feat(DSA_W), feat(FOX_W), feat(NSA_W), feat(MIX_W),
            pl.BlockSpec((MIX_W, d), lambda b, i: (0, 0)),
            pl.BlockSpec((1, tm, d), lambda b, i: (b, i, 0)),
            pl.BlockSpec((1, 1, d), lambda b, i: (b, 0, 0)),
            pl.BlockSpec((1, d), lambda b, i: (0, 0)),
            pl.BlockSpec((1, d), lambda b, i: (0, 0)),
        ],
        out_specs=pl.BlockSpec((1, tm, d), lambda b, i: (b, i, 0)),
        out_shape=jax.ShapeDtypeStruct((bsz, seq, d), F32),
        compiler_params=_cparams(("arbitrary", "arbitrary")),
        name="out_proj_ln",
    )(od_t, of_t, on_t, gate_t, w_out, x, gmod, ln_g, ln_b)


def _split_cols(w):
    out, off = {}, 0
    for name, width in IN_SPLITS:
        out[name] = w[:, off:off + width]
        off += width
    return out


def _prep_w_in(w_in):
    p = _split_cols(w_in)
    d = w_in.shape[0]
    z = lambda n: jnp.zeros((d, n), w_in.dtype)
    gcols = p["nsa_g"].reshape(d, 3, NSA_GROUPS, NSA_HPG)
    gparts = []
    for g in range(NSA_GROUPS):
        gparts += [gcols[:, :, g, :].reshape(d, 3 * NSA_HPG), z(16 - 3 * NSA_HPG)]
    wt = jnp.concatenate(
        [p["dsa_q"], p["idx_q"], p["fox_q"], p["nsa_q"],
         p["dsa_v"], z(HEAD_DIM), p["fox_v"], p["nsa_vs"], p["nsa_vw"], p["gate"],
         p["idx_w"], p["fox_f"], z(8 - FOX_HEADS)] + gparts, axis=1)
    assert wt.shape[1] == T_ROWS
    ws = jnp.concatenate(
        [p["dsa_k"], z(LANES - HEAD_DIM), p["idx_k"], z(LANES - IDX_DIM), p["fox_k"],
         p["nsa_ks"], p["nsa_kw"], p["nsa_kc"], p["nsa_vc"], p["fox_f"], z(LANES - FOX_HEADS)], axis=1)
    assert ws.shape[1] == S_COLS
    return wt.T.astype(BF16), ws.astype(BF16)


def _rope_tables(seq):
    pos = jnp.arange(seq, dtype=F32)
    tabs = []
    for hd in (HEAD_DIM, IDX_DIM):
        half = hd // 2
        inv = ROPE_THETA ** (-jnp.arange(half, dtype=F32) / half)
        ang = inv[:, None] * pos[None, :]
        tabs += [jnp.cos(ang), jnp.sin(ang)]
    lane = jnp.arange(LANES)
    for hd in (HEAD_DIM, IDX_DIM):
        half = hd // 2
        inv = ROPE_THETA ** (-jnp.arange(half, dtype=F32) / half)
        ang = pos[:, None] * inv[lane % half][None, :]
        sign = jnp.where((lane % hd) < half, -1.0, 1.0).astype(F32)
        tabs += [jnp.cos(ang), jnp.sin(ang) * sign[None, :]]
    return tuple(tabs)


def _prep_compress(cmp_pe, cmp_w1, cmp_w2):
    eye = jnp.eye(NSA_GROUPS, dtype=F32)
    w1 = cmp_w1.reshape(2, CMP_LEN, HEAD_DIM, HEAD_DIM)
    big1 = jnp.einsum("kjde,gh->kjgdhe", w1, eye).reshape(2, CMP_LEN, LANES, LANES)
    w1lo = big1[:, :CMP_STRIDE].reshape(2, CMP_STRIDE * LANES, LANES).astype(BF16)
    w1hi = big1[:, CMP_STRIDE:].reshape(2, CMP_STRIDE * LANES, LANES).astype(BF16)
    w2 = jnp.einsum("kde,gh->kgdhe", cmp_w2, eye).reshape(2, LANES, LANES).astype(BF16)
    pe2 = jnp.concatenate([cmp_pe] * NSA_GROUPS, axis=-1)
    return pe2[:, :CMP_STRIDE], pe2[:, CMP_STRIDE:], w1lo, w1hi, w2


def _selection_constants(n_cmp):
    cstart = jnp.arange(n_cmp) * CMP_STRIDE
    bstart = jnp.arange(N_SEL_BLOCKS) * SEL_BLOCK
    real = (jnp.arange(n_cmp) < (SEQ - CMP_LEN) // CMP_STRIDE + 1)[:, None]
    ovl = ((cstart[:, None] < bstart[None, :] + SEL_BLOCK) & (cstart[:, None] + CMP_LEN > bstart[None, :]) & real)
    ovl_t = jnp.concatenate([ovl.T.astype(BF16)] * NSA_HPG, axis=1)
    expand = (jnp.arange(KEY_CHUNK)[:, None] // SEL_BLOCK == jnp.arange(KEY_CHUNK // SEL_BLOCK)[None, :]).astype(F32)
    return ovl_t, expand


def kernel(x, c, w_ada, b_ada, w_in, b_f, cmp_pe, cmp_w1, cmp_w2, w_out, ln_g, ln_b):
    bsz, seq, d = x.shape
    assert (seq, d) == (SEQ, D_MODEL)
    mod = _modulation(c, w_ada, b_ada)
    tabs = _rope_tables(seq)
    ovl, expand = _selection_constants(seq // CMP_STRIDE)
    for l in range(DEPTH):
        shift = mod[l, :, :d].reshape(bsz, 1, d)
        scale = mod[l, :, d:2 * d].reshape(bsz, 1, d)
        gmod = mod[l, :, 2 * d:].reshape(bsz, 1, d)
        wt, ws = _prep_w_in(w_in[l])
        bf8 = jnp.concatenate([b_f[l], jnp.zeros((8 - FOX_HEADS,), F32)])
        (dq, iq, fq, nq, nqr, dv, fv, vs, vw, gate, iw, lft, gts,
         dk, ik, fk, ks, kw, kcin, vcin, lfs) = _projection(
            x, scale, shift, wt, ws, tabs, bf8.reshape(8, 1), bf8.reshape(1, 8))
        cum_t, cum_s = _forget_cumsum(lft, lfs)
        kc, vc_t = _compress(kcin, vcin, *_prep_compress(cmp_pe[l], cmp_w1[l], cmp_w2[l]))
        o_dsa = _dsa(iq, ik, iw, dq, dk, dv)
        o_fox = _fox(fq, fk, fv, cum_t, cum_s)
        o_nsa = _nsa(nq, nqr, kc, vc_t, ks, vs, kw, vw, gts, ovl, expand)
        x = _output(o_dsa, o_fox, o_nsa, gate, w_out[l].astype(BF16), x, gmod,
                    ln_g[l].reshape(1, d), ln_b[l].reshape(1, d))
    return x
```

```python
import functools

import jax
import jax.numpy as jnp
from jax import lax
from jax.experimental import pallas as pl
from jax.experimental.pallas import tpu as pltpu

F32 = jnp.float32
BF16 = jnp.bfloat16
I32 = jnp.int32

D_MODEL = 1024
SEQ = 2048
DEPTH = 2
HEAD_DIM = 64
DSA_HEADS = 4
DSA_TOPK = 256
IDX_HEADS = 8
IDX_DIM = 32
FOX_HEADS = 6
NSA_HEADS = 6
NSA_GROUPS = 2
NSA_HPG = NSA_HEADS // NSA_GROUPS
CMP_LEN = 32
CMP_STRIDE = 16
SEL_BLOCK = 64
SEL_N = 16
N_SEL_BLOCKS = SEQ // SEL_BLOCK
WINDOW = 512
ROPE_THETA = 10000.0
LN_EPS = 1e-5
ALPHA = (2.0 * DEPTH) ** 0.25

DSA_W = DSA_HEADS * HEAD_DIM
FOX_W = FOX_HEADS * HEAD_DIM
NSA_W = NSA_HEADS * HEAD_DIM
NSA_KV_W = NSA_GROUPS * HEAD_DIM
MIX_W = DSA_W + FOX_W + NSA_W
IDX_W = IDX_HEADS * IDX_DIM

IN_SPLITS = (
    ("dsa_q", DSA_W), ("dsa_k", HEAD_DIM), ("dsa_v", HEAD_DIM),
    ("idx_q", IDX_W), ("idx_k", IDX_DIM), ("idx_w", IDX_HEADS),
    ("fox_q", FOX_W), ("fox_k", FOX_W), ("fox_v", FOX_W), ("fox_f", FOX_HEADS),
    ("nsa_q", NSA_W),
    ("nsa_kc", NSA_KV_W), ("nsa_vc", NSA_KV_W),
    ("nsa_ks", NSA_KV_W), ("nsa_vs", NSA_KV_W),
    ("nsa_kw", NSA_KV_W), ("nsa_vw", NSA_KV_W),
    ("nsa_g", 3 * NSA_HEADS),
    ("gate", MIX_W),
)

LANES = 128
KEY_CHUNK = 512
PROJ_TOKENS = 512
Q_TILE = 256
DSA_Q_TILE = 512
FOX_Q_TILE = 512
N_KEY_CHUNKS = SEQ // KEY_CHUNK
WIN_CHUNK = 256
N_WIN_CHUNKS = SEQ // WIN_CHUNK
WIN_SPAN = WINDOW + Q_TILE
CUM_CHUNK = 256
VMEM_LIMIT = 56 * 1024 * 1024

FOLD_ROWS = 32
ONES_ROWS = 16
LOG2E = 1.4426950408889634
NEG = -(2.0 ** 100)
INT_MIN = -(2 ** 31)
CODE_NEG_INF = 0x007FFFFF

T_DQ, T_IQ, T_FQ, T_NQ = 0, 256, 512, 896
T_DV, T_FV, T_VS, T_VW = 1280, 1408, 1792, 1920
T_GATE, T_SMALL, T_ROWS = 2048, 3072, 3120
S_DK, S_IK, S_FK, S_KS, S_KW, S_KC, S_VC, S_FF, S_COLS = 0, 128, 256, 640, 768, 896, 1024, 1152, 1280


def _cparams(sem):
    return pltpu.CompilerParams(dimension_semantics=sem, vmem_limit_bytes=VMEM_LIMIT)


def _dot(a, b):
    return jnp.dot(a, b, preferred_element_type=F32)


def _dot_nt(a, b):
    return lax.dot_general(a, b, (((1,), (1,)), ((), ())), preferred_element_type=F32)


def _log_sigmoid(x):
    return jnp.minimum(x, 0.0) - jnp.log(1.0 + jnp.exp(-jnp.abs(x)))


def _sigmoid(x):
    return 1.0 / (1.0 + jnp.exp(-x))


def _mod_kernel(c_ref, w_ref, b_ref, o_ref):
    o_ref[0] = _dot(c_ref[...].astype(BF16), w_ref[0].astype(BF16)) + b_ref[0]


def _modulation(c, w_ada, b_ada):
    depth, d, d3 = w_ada.shape
    bsz = c.shape[0]
    return pl.pallas_call(
        _mod_kernel,
        grid=(depth, d3 // d),
        in_specs=[
            pl.BlockSpec((bsz, d), lambda l, j: (0, 0)),
            pl.BlockSpec((1, d, d), lambda l, j: (l, 0, j)),
            pl.BlockSpec((1, 1, d), lambda l, j: (l, 0, j)),
        ],
        out_specs=pl.BlockSpec((1, bsz, d), lambda l, j: (l, 0, j)),
        out_shape=jax.ShapeDtypeStruct((depth, bsz, d3), F32),
        compiler_params=_cparams(("arbitrary", "arbitrary")),
        name="adaln_mod",
    )(c, w_ada, b_ada.reshape(depth, 1, d3))


def _proj_kernel(x_ref, sc_ref, sh_ref, wt_ref, ws_ref,
                 ct64_ref, st64_ref, ct32_ref, st32_ref,
                 ck64_ref, sk64_ref, ck32_ref, sk32_ref, bfc_ref, bfr_ref,
                 dq_ref, iq_ref, fq_ref, nq_ref, nqr_ref,
                 dv_ref, fv_ref, vs_ref, vw_ref, gate_ref,
                 iw_ref, lft_ref, gts_ref,
                 dk_ref, ik_ref, fk_ref, ks_ref, kw_ref, kcin_ref, vcin_ref, lfs_ref):
    tm = x_ref.shape[1]
    u = (x_ref[0] * (1.0 + sc_ref[0]) + sh_ref[0]).astype(BF16)

    ht_all = _dot_nt(wt_ref[...], u)

    def proj_t(r0, r1):
        return ht_all[r0:r1]

    def rope_t(h, n_heads, hd, c, s):
        half = hd // 2
        out = []
        for hh in range(n_heads):
            x1 = h[hh * hd:hh * hd + half]
            x2 = h[hh * hd + half:(hh + 1) * hd]
            out.append((hh * hd, x1 * c - x2 * s))
            out.append((hh * hd + half, x1 * s + x2 * c))
        return out

    c64, s64 = ct64_ref[...], st64_ref[...]
    c32, s32 = ct32_ref[...], st32_ref[...]
    qscale = HEAD_DIM ** -0.5 * LOG2E

    h = proj_t(T_DQ, T_DQ + DSA_W)
    for r, v in rope_t(h, DSA_HEADS, HEAD_DIM, c64, s64):
        dq_ref[0, r:r + HEAD_DIM // 2, :] = (v * qscale).astype(BF16)
    h = proj_t(T_IQ, T_IQ + IDX_W)
    for r, v in rope_t(h, IDX_HEADS, IDX_DIM, c32, s32):
        iq_ref[0, r:r + IDX_DIM // 2, :] = v.astype(BF16)
    fq_ref[0] = (proj_t(T_FQ, T_FQ + FOX_W) * qscale).astype(BF16)
    h = proj_t(T_NQ, T_NQ + NSA_W)
    nq_ref[0] = (h * qscale).astype(BF16)
    for r, v in rope_t(h, NSA_HEADS, HEAD_DIM, c64, s64):
        nqr_ref[0, r:r + HEAD_DIM // 2, :] = (v * qscale).astype(BF16)

    dv_ref[0, 0] = proj_t(T_DV, T_DV + HEAD_DIM).astype(BF16)
    h = proj_t(T_FV, T_FV + FOX_W)
    for hh in range(FOX_HEADS):
        fv_ref[0, hh, 0] = h[hh * HEAD_DIM:(hh + 1) * HEAD_DIM].astype(BF16)
    h = proj_t(T_VS, T_VS + NSA_KV_W)
    for g in range(NSA_GROUPS):
        vs_ref[0, g, 0] = h[g * HEAD_DIM:(g + 1) * HEAD_DIM].astype(BF16)
    h = proj_t(T_VW, T_VW + NSA_KV_W)
    for g in range(NSA_GROUPS):
        for j in range(tm // WIN_CHUNK):
            vw_ref[0, g, j] = h[g * HEAD_DIM:(g + 1) * HEAD_DIM, j * WIN_CHUNK:(j + 1) * WIN_CHUNK].astype(BF16)

    for r0 in range(0, MIX_W, 256):
        h = proj_t(T_GATE + r0, T_GATE + r0 + 256)
        gate_ref[0, r0:r0 + 256, :] = h * _sigmoid(h)

    h = proj_t(T_SMALL, T_ROWS)
    iw_ref[0] = h[0:8] * (IDX_HEADS ** -0.5)
    lft_ref[0] = _log_sigmoid(h[8:16] + bfc_ref[...])
    g_all = _sigmoid(h[16:48])
    gts_ref[0, 0] = g_all[0:16]
    gts_ref[0, 1] = g_all[16:32]

    lane = lax.broadcasted_iota(I32, (tm, LANES), 1)

    def rope_s(g, half, c, s_signed):
        first = (lane & (2 * half - 1)) < half
        sw = jnp.where(first, pltpu.roll(g, LANES - half, 1), pltpu.roll(g, half, 1))
        return g * c + sw * s_signed

    hs_all = _dot(u, ws_ref[...])

    def proj_s(c0, c1):
        return hs_all[:, c0:c1]

    ck64, sk64 = ck64_ref[...], sk64_ref[...]
    g = rope_s(proj_s(S_DK, S_DK + LANES), HEAD_DIM // 2, ck64, sk64)
    dk_ref[0] = g[:, :HEAD_DIM].astype(BF16)
    g = rope_s(proj_s(S_IK, S_IK + LANES), IDX_DIM // 2, ck32_ref[...], sk32_ref[...])
    ik_ref[0] = g[:, :IDX_DIM].astype(BF16)
    g = proj_s(S_FK, S_FK + FOX_W)
    for hh in range(FOX_HEADS):
        fk_ref[0, hh] = g[:, hh * HEAD_DIM:(hh + 1) * HEAD_DIM].astype(BF16)
    g = rope_s(proj_s(S_KS, S_KS + LANES), HEAD_DIM // 2, ck64, sk64)
    for gg in range(NSA_GROUPS):
        ks_ref[0, gg] = g[:, gg * HEAD_DIM:(gg + 1) * HEAD_DIM].astype(BF16)
    g = rope_s(proj_s(S_KW, S_KW + LANES), HEAD_DIM // 2, ck64, sk64)
    for gg in range(NSA_GROUPS):
        kw_ref[0, gg] = g[:, gg * HEAD_DIM:(gg + 1) * HEAD_DIM].astype(BF16)
    kcin_ref[0] = proj_s(S_KC, S_KC + LANES)
    vcin_ref[0] = proj_s(S_VC, S_VC + LANES)
    g = proj_s(S_FF, S_FF + LANES)
    lfs_ref[0] = _log_sigmoid(g[:, :8] + bfr_ref[...])


def _projection(x, scale, shift, wt, ws, tabs, bfc, bfr):
    bsz, seq, d = x.shape
    tm = PROJ_TOKENS
    nt = seq // tm
    ct64, st64, ct32, st32, ck64, sk64, ck32, sk32 = tabs
    const = lambda shape: pl.BlockSpec(shape, lambda b, i: (0,) * len(shape))
    feat = lambda rows: pl.BlockSpec((1, rows, tm), lambda b, i: (b, 0, i))
    in_specs = [
        pl.BlockSpec((1, tm, d), lambda b, i: (b, i, 0)),
        pl.BlockSpec((1, 1, d), lambda b, i: (b, 0, 0)),
        pl.BlockSpec((1, 1, d), lambda b, i: (b, 0, 0)),
        const((T_ROWS, d)),
        const((d, S_COLS)),
        pl.BlockSpec((HEAD_DIM // 2, tm), lambda b, i: (0, i)),
        pl.BlockSpec((HEAD_DIM // 2, tm), lambda b, i: (0, i)),
        pl.BlockSpec((IDX_DIM // 2, tm), lambda b, i: (0, i)),
        pl.BlockSpec((IDX_DIM // 2, tm), lambda b, i: (0, i)),
        pl.BlockSpec((tm, LANES), lambda b, i: (i, 0)),
        pl.BlockSpec((tm, LANES), lambda b, i: (i, 0)),
        pl.BlockSpec((tm, LANES), lambda b, i: (i, 0)),
        pl.BlockSpec((tm, LANES), lambda b, i: (i, 0)),
        const((8, 1)),
        const((1, 8)),
    ]
    sds = jax.ShapeDtypeStruct
    out_shape = [
        sds((bsz, DSA_W, seq), BF16), sds((bsz, IDX_W, seq), BF16), sds((bsz, FOX_W, seq), BF16),
        sds((bsz, NSA_W, seq), BF16), sds((bsz, NSA_W, seq), BF16),
        sds((bsz, N_KEY_CHUNKS, HEAD_DIM, KEY_CHUNK), BF16),
        sds((bsz, FOX_HEADS, N_KEY_CHUNKS, HEAD_DIM, KEY_CHUNK), BF16),
        sds((bsz, NSA_GROUPS, N_KEY_CHUNKS, HEAD_DIM, KEY_CHUNK), BF16),
        sds((bsz, NSA_GROUPS, N_WIN_CHUNKS, HEAD_DIM, WIN_CHUNK), BF16),
        sds((bsz, MIX_W, seq), F32),
        sds((bsz, 8, seq), F32), sds((bsz, 8, seq), F32), sds((bsz, NSA_GROUPS, 16, seq), F32),
        sds((bsz, seq, HEAD_DIM), BF16), sds((bsz, seq, IDX_DIM), BF16),
        sds((bsz, FOX_HEADS, seq, HEAD_DIM), BF16),
        sds((bsz, NSA_GROUPS, seq, HEAD_DIM), BF16), sds((bsz, NSA_GROUPS, seq, HEAD_DIM), BF16),
        sds((bsz, seq, LANES), F32), sds((bsz, seq, LANES), F32), sds((bsz, seq, 8), F32),
    ]
    wpc = tm // WIN_CHUNK
    out_specs = [
        feat(DSA_W), feat(IDX_W), feat(FOX_W), feat(NSA_W), feat(NSA_W),
        pl.BlockSpec((1, 1, HEAD_DIM, KEY_CHUNK), lambda b, i: (b, i, 0, 0)),
        pl.BlockSpec((1, FOX_HEADS, 1, HEAD_DIM, KEY_CHUNK), lambda b, i: (b, 0, i, 0, 0)),
        pl.BlockSpec((1, NSA_GROUPS, 1, HEAD_DIM, KEY_CHUNK), lambda b, i: (b, 0, i, 0, 0)),
        pl.BlockSpec((1, NSA_GROUPS, wpc, HEAD_DIM, WIN_CHUNK), lambda b, i: (b, 0, i, 0, 0)),
        feat(MIX_W),
        feat(8), feat(8),
        pl.BlockSpec((1, NSA_GROUPS, 16, tm), lambda b, i: (b, 0, 0, i)),
        pl.BlockSpec((1, tm, HEAD_DIM), lambda b, i: (b, i, 0)),
        pl.BlockSpec((1, tm, IDX_DIM), lambda b, i: (b, i, 0)),
        pl.BlockSpec((1, FOX_HEADS, tm, HEAD_DIM), lambda b, i: (b, 0, i, 0)),
        pl.BlockSpec((1, NSA_GROUPS, tm, HEAD_DIM), lambda b, i: (b, 0, i, 0)),
        pl.BlockSpec((1, NSA_GROUPS, tm, HEAD_DIM), lambda b, i: (b, 0, i, 0)),
        pl.BlockSpec((1, tm, LANES), lambda b, i: (b, i, 0)),
        pl.BlockSpec((1, tm, LANES), lambda b, i: (b, i, 0)),
        pl.BlockSpec((1, tm, 8), lambda b, i: (b, i, 0)),
    ]
    assert tm == KEY_CHUNK
    return pl.pallas_call(
        _proj_kernel,
        grid=(bsz, nt),
        in_specs=in_specs,
        out_specs=out_specs,
        out_shape=out_shape,
        compiler_params=_cparams(("arbitrary", "arbitrary")),
        name="in_proj",
    )(x, scale, shift, wt, ws, ct64, st64, ct32, st32, ck64, sk64, ck32, sk32, bfc, bfr)


def _cumsum_kernel(lft_ref, lfs_ref, cumt_ref, cums_ref):
    seq = lft_ref.shape[2]
    r = lax.broadcasted_iota(I32, (CUM_CHUNK, CUM_CHUNK), 0)
    c = lax.broadcasted_iota(I32, (CUM_CHUNK, CUM_CHUNK), 1)
    tri_u = (r <= c).astype(F32)
    tri_l = (c <= r).astype(F32)
    carry_t = jnp.zeros((8, 1), F32)
    carry_s = jnp.zeros((1, 8), F32)
    for k in range(seq // CUM_CHUNK):
        sl = slice(k * CUM_CHUNK, (k + 1) * CUM_CHUNK)
        ct = jnp.dot(lft_ref[0, :, sl], tri_u, preferred_element_type=F32,
                     precision=lax.Precision.HIGHEST) + carry_t
        for head in range(8):
            cumt_ref[0, head, :, sl] = ct[head:head + 1, :] * LOG2E
        carry_t = ct[:, CUM_CHUNK - 1:CUM_CHUNK]
        cs = jnp.dot(tri_l, lfs_ref[0, sl, :], preferred_element_type=F32,
                     precision=lax.Precision.HIGHEST) + carry_s
        cums_ref[0, sl, :] = cs * LOG2E
        carry_s = cs[CUM_CHUNK - 1:CUM_CHUNK, :]


def _forget_cumsum(lft, lfs):
    bsz, _, seq = lft.shape
    return pl.pallas_call(
        _cumsum_kernel,
        grid=(bsz,),
        in_specs=[pl.BlockSpec((1, 8, seq), lambda b: (b, 0, 0)),
                  pl.BlockSpec((1, seq, 8), lambda b: (b, 0, 0))],
        out_specs=[pl.BlockSpec((1, 8, 1, seq), lambda b: (b, 0, 0, 0)),
                   pl.BlockSpec((1, seq, 8), lambda b: (b, 0, 0))],
        out_shape=[jax.ShapeDtypeStruct((bsz, 8, 1, seq), F32),
                   jax.ShapeDtypeStruct((bsz, seq, 8), F32)],
        compiler_params=_cparams(("arbitrary",)),
        name="forget_cumsum",
    )(lft, lfs)


def _compress_kernel(kcin_ref, vcin_ref, pelo_ref, pehi_ref, w1lo_ref, w1hi_ref, w2_ref,
                     kc_ref, vct_ref):
    n_blk = kcin_ref.shape[1] // CMP_STRIDE
    for kv, src in enumerate((kcin_ref, vcin_ref)):
        lo, hi = [], []
        for j in range(CMP_STRIDE):
            piece = src[0, pl.ds(j, n_blk, stride=CMP_STRIDE), :]
            lo.append((piece + pelo_ref[kv, j:j + 1, :]).astype(BF16))
            hi.append((piece + pehi_ref[kv, j:j + 1, :]).astype(BF16))
        a = _dot(jnp.concatenate(lo, axis=1), w1lo_ref[kv])
        b = _dot(jnp.concatenate(hi, axis=1), w1hi_ref[kv])
        pre = a + pltpu.roll(b, n_blk - 1, 0)
        act = pre * _sigmoid(pre)
        out = _dot(act.astype(BF16), w2_ref[kv])
        if kv == 0:
            for g in range(NSA_GROUPS):
                kc_ref[0, g] = out[:, g * HEAD_DIM:(g + 1) * HEAD_DIM].astype(BF16)
        else:
            out_t = out.T
            for g in range(NSA_GROUPS):
                vct_ref[0, g] = out_t[g * HEAD_DIM:(g + 1) * HEAD_DIM].astype(BF16)


def _compress(kcin, vcin, pelo, pehi, w1lo, w1hi, w2):
    bsz, seq, _ = kcin.shape
    n_blk = seq // CMP_STRIDE
    full = lambda a: pl.BlockSpec(a.shape, lambda b: (0,) * a.ndim)
    return pl.pallas_call(
        _compress_kernel,
        grid=(bsz,),
        in_specs=[pl.BlockSpec((1, seq, LANES), lambda b: (b, 0, 0)),
                  pl.BlockSpec((1, seq, LANES), lambda b: (b, 0, 0)),
                  full(pelo), full(pehi), full(w1lo), full(w1hi), full(w2)],
        out_specs=[pl.BlockSpec((1, NSA_GROUPS, n_blk, HEAD_DIM), lambda b: (b, 0, 0, 0)),
                   pl.BlockSpec((1, NSA_GROUPS, HEAD_DIM, n_blk), lambda b: (b, 0, 0, 0))],
        out_shape=[jax.ShapeDtypeStruct((bsz, NSA_GROUPS, n_blk, HEAD_DIM), BF16),
                   jax.ShapeDtypeStruct((bsz, NSA_GROUPS, HEAD_DIM, n_blk), BF16)],
        compiler_params=_cparams(("arbitrary",)),
        name="nsa_compress",
    )(kcin, vcin, pelo, pehi, w1lo, w1hi, w2)


def _fold_rows(x, op):
    k, n = x.shape
    return op(x.reshape(k // FOLD_ROWS, FOLD_ROWS, n), axis=0)


def _reduce_rows(x, op):
    if x.shape[0] % FOLD_ROWS == 0 and x.shape[0] > FOLD_ROWS:
        x = _fold_rows(x, op)
    return op(x, axis=0, keepdims=True)


def _with_ones(v_t):
    return jnp.concatenate([v_t, jnp.ones((ONES_ROWS, v_t.shape[1]), v_t.dtype)], axis=0)


def _softmax_step(carry, s, v_aug):
    m, acc = carry
    m_new = jnp.maximum(m, _reduce_rows(s, jnp.max))
    alpha = jnp.exp2(m - m_new)
    p = jnp.exp2(s - m_new)
    acc = alpha * acc + _dot(v_aug, p.astype(BF16))
    return m_new, acc


def _softmax_init(n):
    return (jnp.full((1, n), NEG, F32), jnp.zeros((HEAD_DIM + ONES_ROWS, n), F32))


def _softmax_finish(carry):
    _, acc = carry
    return acc[:HEAD_DIM] * (1.0 / acc[HEAD_DIM:HEAD_DIM + 1])


def _tile_lanes(a, n):
    return jnp.concatenate([a] * n, axis=1)


def _dsa_kernel(iq_ref, ik_ref, iw_ref, dq_ref, dk_ref, dv_ref, g_ref, o_ref, key_ref, jsel_ref):
    i = pl.program_id(1)
    tq = DSA_Q_TILE
    n_chunks = i + 1
    tpos = i * tq + lax.broadcasted_iota(I32, (1, tq), 1)
    srow = lax.broadcasted_iota(I32, (KEY_CHUNK, tq), 0)
    iw = iw_ref[0]

    def index_chunk(c, _):
        k0 = pl.multiple_of(c * KEY_CHUNK, KEY_CHUNK)
        ikc = ik_ref[0, pl.ds(k0, KEY_CHUNK), :]
        acc = jnp.zeros((KEY_CHUNK, tq), F32)
        for h in range(IDX_HEADS):
            x = _dot(ikc, iq_ref[0, h * IDX_DIM:(h + 1) * IDX_DIM, :])
            acc = acc + iw[h:h + 1, :] * jnp.maximum(x, 0.0)
        key_ref[pl.ds(k0, KEY_CHUNK), :] = jnp.where(srow + k0 <= tpos, acc, -jnp.inf)
        return 0

    lax.fori_loop(0, n_chunks, index_chunk, 0)

    def count(pred):
        def body(c, acc):
            k0 = pl.multiple_of(c * KEY_CHUNK, KEY_CHUNK)
            hit = pred(key_ref[pl.ds(k0, KEY_CHUNK), :], srow + k0)
            return acc + _fold_rows(jnp.where(hit, 1, 0), jnp.sum)
        acc = lax.fori_loop(0, n_chunks, body, jnp.zeros((FOLD_ROWS, tq), I32))
        return jnp.sum(acc, axis=0, keepdims=True)

    def decode(code):
        skey = code ^ INT_MIN
        val = lax.bitcast_convert_type(jnp.where(skey < 0, skey ^ 0x7FFFFFFF, skey), F32)
        return jnp.where(jnp.logical_and(code >= 0, code <= CODE_NEG_INF), -jnp.inf, val)

    def value_bit(it, code):
        cand = code | lax.shift_left(jnp.int32(1), 31 - it)
        cand_f = decode(cand)
        cnt = count(lambda key, spos: key >= cand_f)
        return jnp.where(cnt >= DSA_TOPK, cand, code)

    thr = decode(lax.fori_loop(0, 32, value_bit, jnp.zeros((1, tq), I32)))

    cnt_gt = count(lambda key, spos: key > thr)
    cnt_ge = count(lambda key, spos: key >= thr)
    need = DSA_TOPK - cnt_gt
    has_tie = jnp.logical_and(cnt_ge > DSA_TOPK, thr > -jnp.inf)
    jsel_ref[...] = jnp.full((1, tq), SEQ, I32)

    @pl.when(jnp.max(jnp.where(has_tie, 1.0, 0.0)) > 0.0)
    def _():
        def index_bit(it, j):
            cand = j | lax.shift_left(jnp.int32(1), 10 - it)
            cnt = count(lambda key, spos: jnp.logical_and(key == thr, spos < cand))
            return jnp.where(cnt < need, cand, j)
        jsel_ref[...] = lax.fori_loop(0, 11, index_bit, jnp.zeros((1, tq), I32))

    jsel = jsel_ref[...]

    def attend(c, carry):
        k0 = pl.multiple_of(c * KEY_CHUNK, KEY_CHUNK)
        key = key_ref[pl.ds(k0, KEY_CHUNK), :]
        spos = srow + k0
        sel = jnp.logical_or(key > thr, jnp.logical_and(key == thr, spos <= jsel))
        sel = jnp.logical_and(sel, spos <= tpos)
        bias = jnp.where(sel, 0.0, NEG)
        kc = dk_ref[0, pl.ds(k0, KEY_CHUNK), :]
        v_aug = _with_ones(dv_ref[0, c])
        return tuple(
            _softmax_step(carry[h], _dot(kc, dq_ref[0, h * HEAD_DIM:(h + 1) * HEAD_DIM, :]) + bias, v_aug)
            for h in range(DSA_HEADS))

    heads = lax.fori_loop(0, n_chunks, attend, tuple(_softmax_init(tq) for _ in range(DSA_HEADS)))
    for h, carry in enumerate(heads):
        rows = slice(h * HEAD_DIM, (h + 1) * HEAD_DIM)
        o_ref[0, rows, :] = (_softmax_finish(carry) * g_ref[0, rows, :]).astype(BF16)


def _dsa(iq_t, ik, iw_t, dq_t, dk, dv_t, gate_t):
    bsz, _, seq = dq_t.shape
    tq = DSA_Q_TILE
    assert tq == KEY_CHUNK
    return pl.pallas_call(
        _dsa_kernel,
        grid=(bsz, seq // tq),
        in_specs=[
            pl.BlockSpec((1, IDX_W, tq), lambda b, i: (b, 0, i)),
            pl.BlockSpec((1, seq, IDX_DIM), lambda b, i: (b, 0, 0)),
            pl.BlockSpec((1, 8, tq), lambda b, i: (b, 0, i)),
            pl.BlockSpec((1, DSA_W, tq), lambda b, i: (b, 0, i)),
            pl.BlockSpec((1, seq, HEAD_DIM), lambda b, i: (b, 0, 0)),
            pl.BlockSpec((1, N_KEY_CHUNKS, HEAD_DIM, KEY_CHUNK), lambda b, i: (b, 0, 0, 0)),
            pl.BlockSpec((1, DSA_W, tq), lambda b, i: (b, 0, i)),
        ],
        out_specs=pl.BlockSpec((1, DSA_W, tq), lambda b, i: (b, 0, i)),
        out_shape=jax.ShapeDtypeStruct((bsz, DSA_W, seq), BF16),
        scratch_shapes=[pltpu.VMEM((seq, tq), F32), pltpu.VMEM((1, tq), I32)],
        compiler_params=_cparams(("arbitrary", "arbitrary")),
        name="dsa_attn",
    )(iq_t, ik, iw_t, dq_t, dk, dv_t, gate_t)


def _fox_kernel(q_ref, k_ref, v_ref, cumt_ref, cums_ref, g_ref, o_ref):
    h = pl.program_id(1)
    tq = FOX_Q_TILE
    n_tiles = q_ref.shape[2] // tq
    head_lane = lax.broadcasted_iota(I32, (KEY_CHUNK, 8), 1) == h
    srow = lax.broadcasted_iota(I32, (KEY_CHUNK, tq), 0)
    tcol = lax.broadcasted_iota(I32, (KEY_CHUNK, tq), 1)
    ccols = []
    for c in range(n_tiles):
        col = jnp.sum(jnp.where(head_lane, cums_ref[0, c * KEY_CHUNK:(c + 1) * KEY_CHUNK, :], 0.0),
                      axis=1, keepdims=True)
        ccols.append(_tile_lanes(jnp.broadcast_to(col, (KEY_CHUNK, LANES)), tq // LANES))
    v_aug = [_with_ones(v_ref[0, 0, c]) for c in range(n_tiles)]
    for i in range(n_tiles):
        q = q_ref[0, :, i * tq:(i + 1) * tq]
        crow = cumt_ref[0, 0, :, i * tq:(i + 1) * tq]
        carry = _softmax_init(tq)
        for c in range(i + 1):
            s = _dot(k_ref[0, 0, c * KEY_CHUNK:(c + 1) * KEY_CHUNK, :], q) + crow - ccols[c]
            if c == i:
                s = jnp.where(srow <= tcol, s, NEG)
            carry = _softmax_step(carry, s, v_aug[c])
        cols = slice(i * tq, (i + 1) * tq)
        o_ref[0, :, cols] = (_softmax_finish(carry) * g_ref[0, :, cols]).astype(BF16)


def _fox(fq_t, fk, fv_t, cum_t, cum_s, gate_t):
    bsz, _, seq = fq_t.shape
    assert FOX_Q_TILE == KEY_CHUNK
    return pl.pallas_call(
        _fox_kernel,
        grid=(bsz, FOX_HEADS),
        in_specs=[
            pl.BlockSpec((1, HEAD_DIM, seq), lambda b, h: (b, h, 0)),
            pl.BlockSpec((1, 1, seq, HEAD_DIM), lambda b, h: (b, h, 0, 0)),
            pl.BlockSpec((1, 1, N_KEY_CHUNKS, HEAD_DIM, KEY_CHUNK), lambda b, h: (b, h, 0, 0, 0)),
            pl.BlockSpec((1, 1, 1, seq), lambda b, h: (b, h, 0, 0)),
            pl.BlockSpec((1, seq, 8), lambda b, h: (b, 0, 0)),
            pl.BlockSpec((1, HEAD_DIM, seq), lambda b, h: (b, DSA_HEADS + h, 0)),
        ],
        out_specs=pl.BlockSpec((1, HEAD_DIM, seq), lambda b, h: (b, h, 0)),
        out_shape=jax.ShapeDtypeStruct((bsz, FOX_W, seq), BF16),
        compiler_params=_cparams(("arbitrary", "arbitrary")),
        name="fox_attn",
    )(fq_t, fk, fv_t, cum_t, cum_s, gate_t)


def _nsa_kernel(q_ref, qr_ref, kc_ref, vct_ref, ks_ref, vs_ref, kw_ref, vw_ref, g_ref,
                ovl_ref, expand_ref, og0_ref, og1_ref, og2_ref, o_ref, sel_ref):
    i = pl.program_id(2)
    nq = NSA_HPG * Q_TILE
    t0 = i * Q_TILE
    tpos = t0 + lax.broadcasted_iota(I32, (1, Q_TILE), 1)
    q3 = jnp.concatenate([q_ref[0, j * HEAD_DIM:(j + 1) * HEAD_DIM, :] for j in range(NSA_HPG)], axis=1)
    qr3 = jnp.concatenate([qr_ref[0, j * HEAD_DIM:(j + 1) * HEAD_DIM, :] for j in range(NSA_HPG)], axis=1)

    n_cmp = kc_ref.shape[2]
    cend = lax.broadcasted_iota(I32, (n_cmp, Q_TILE), 0) * CMP_STRIDE + (CMP_LEN - 1)
    cbias = _tile_lanes(jnp.where(cend <= tpos, 0.0, NEG), NSA_HPG)
    cvalid = _tile_lanes(jnp.where(cend <= tpos, 1.0, 0.0), NSA_HPG)
    s = _dot(kc_ref[0, 0], q3) + cbias
    m = _reduce_rows(s, jnp.max)
    e = jnp.exp2(s - m) * cvalid
    p_cmp = (e / jnp.maximum(_reduce_rows(e, jnp.sum), 1e-30)).astype(BF16)
    o_cmp = _dot(vct_ref[0, 0], p_cmp)

    p_stack = jnp.concatenate([p_cmp[:, j * Q_TILE:(j + 1) * Q_TILE] for j in range(NSA_HPG)], axis=0)
    score = _dot(ovl_ref[...], p_stack)
    blk = lax.broadcasted_iota(I32, (N_SEL_BLOCKS, Q_TILE), 0)
    cur = lax.shift_right_logical(tpos, 6)
    forced = jnp.logical_or(blk == 0, jnp.logical_or(blk == cur, blk == cur - 1))
    score = jnp.where(forced, jnp.inf, jnp.where(blk > cur, -jnp.inf, score))
    rank = jnp.zeros((N_SEL_BLOCKS, Q_TILE), F32)
    for mth in range(N_SEL_BLOCKS):
        row = score[mth:mth + 1, :]
        ahead = jnp.logical_or(row > score, jnp.logical_and(row == score, blk > mth))
        rank = rank + jnp.where(ahead, 1.0, 0.0)
    sel_ref[...] = jnp.where(rank < SEL_N, 1.0, 0.0)

    n_chunks = (i * Q_TILE + Q_TILE + KEY_CHUNK - 1) // KEY_CHUNK
    srow = lax.broadcasted_iota(I32, (KEY_CHUNK, Q_TILE), 0)
    blocks_per_chunk = KEY_CHUNK // SEL_BLOCK

    def attend_sel(c, carry):
        k0 = pl.multiple_of(c * KEY_CHUNK, KEY_CHUNK)
        b0 = pl.multiple_of(c * blocks_per_chunk, blocks_per_chunk)
        picked = _dot(expand_ref[...], sel_ref[pl.ds(b0, blocks_per_chunk), :])
        ok = jnp.logical_and(picked > 0.5, srow + k0 <= tpos)
        bias = _tile_lanes(jnp.where(ok, 0.0, NEG), NSA_HPG)
        s = _dot(ks_ref[0, 0, pl.ds(k0, KEY_CHUNK), :], qr3) + bias
        return _softmax_step(carry, s, _with_ones(vs_ref[0, 0, c]))

    o_slc = _softmax_finish(lax.fori_loop(0, n_chunks, attend_sel, _softmax_init(nq)))

    wc = jnp.maximum(i - WINDOW // Q_TILE, 0) * (Q_TILE // WIN_CHUNK)
    w0 = pl.multiple_of(wc * WIN_CHUNK, WIN_CHUNK)
    kpos = w0 + lax.broadcasted_iota(I32, (WIN_SPAN, Q_TILE), 0)
    wok = jnp.logical_and(kpos <= tpos, kpos > tpos - WINDOW)
    s = _dot(kw_ref[0, 0, pl.ds(w0, WIN_SPAN), :], qr3) + _tile_lanes(jnp.where(wok, 0.0, NEG), NSA_HPG)
    m = _reduce_rows(s, jnp.max)
    e = jnp.exp2(s - m).astype(BF16)
    acc = jnp.zeros((HEAD_DIM + ONES_ROWS, nq), F32)
    for j in range(WIN_SPAN // WIN_CHUNK):
        acc = acc + _dot(_with_ones(vw_ref[0, 0, wc + j]), e[j * WIN_CHUNK:(j + 1) * WIN_CHUNK, :])
    o_win = _softmax_finish((m, acc))

    gts = g_ref[0, 0]
    out_gates = (og0_ref, og1_ref, og2_ref)
    for j in range(NSA_HPG):
        sl = slice(j * Q_TILE, (j + 1) * Q_TILE)
        mixed = (gts[j:j + 1, :] * o_cmp[:, sl]
                 + gts[NSA_HPG + j:NSA_HPG + j + 1, :] * o_slc[:, sl]
                 + gts[2 * NSA_HPG + j:2 * NSA_HPG + j + 1, :] * o_win[:, sl])
        o_ref[0, j * HEAD_DIM:(j + 1) * HEAD_DIM, :] = (mixed * out_gates[j][0]).astype(BF16)


def _nsa(nq_t, nqr_t, kc, vc_t, ks, vs_t, kw, vw_t, gts, ovl, expand, gate_t):
    bsz, _, seq = nq_t.shape
    n_cmp = kc.shape[2]
    gw = NSA_HPG * HEAD_DIM
    per_group = lambda shape: pl.BlockSpec((1, 1) + shape, lambda b, g, i: (b, g) + (0,) * len(shape))
    return pl.pallas_call(
        _nsa_kernel,
        grid=(bsz, NSA_GROUPS, seq // Q_TILE),
        in_specs=[
            pl.BlockSpec((1, gw, Q_TILE), lambda b, g, i: (b, g, i)),
            pl.BlockSpec((1, gw, Q_TILE), lambda b, g, i: (b, g, i)),
            per_group((n_cmp, HEAD_DIM)),
            per_group((HEAD_DIM, n_cmp)),
            per_group((seq, HEAD_DIM)),
            per_group((N_KEY_CHUNKS, HEAD_DIM, KEY_CHUNK)),
            per_group((seq, HEAD_DIM)),
            per_group((N_WIN_CHUNKS, HEAD_DIM, WIN_CHUNK)),
            pl.BlockSpec((1, 1, 16, Q_TILE), lambda b, g, i: (b, g, 0, i)),
            pl.BlockSpec(ovl.shape, lambda b, g, i: (0, 0)),
            pl.BlockSpec(expand.shape, lambda b, g, i: (0, 0)),
        ] + [
            pl.BlockSpec((1, HEAD_DIM, Q_TILE),
                         lambda b, g, i, j=j: (b, DSA_HEADS + FOX_HEADS + g * NSA_HPG + j, i))
            for j in range(NSA_HPG)
        ],
        out_specs=pl.BlockSpec((1, gw, Q_TILE), lambda b, g, i: (b, g, i)),
        out_shape=jax.ShapeDtypeStruct((bsz, NSA_W, seq), BF16),
        scratch_shapes=[pltpu.VMEM((N_SEL_BLOCKS, Q_TILE), F32)],
        compiler_params=_cparams(("arbitrary", "arbitrary", "arbitrary")),
        name="nsa_attn",
    )(nq_t, nqr_t, kc, vc_t, ks, vs_t, kw, vw_t, gts, ovl, expand, gate_t, gate_t, gate_t)


def _out_kernel(od_ref, of_ref, on_ref, w_ref, x_ref, gm_ref, lg_ref, lb_ref, o_ref):
    z = jnp.concatenate([od_ref[0], of_ref[0], on_ref[0]], axis=0)
    y = lax.dot_general(z, w_ref[...], (((0,), (0,)), ((), ())),
                        preferred_element_type=F32)
    r = ALPHA * x_ref[0] + (1.0 + gm_ref[0]) * y
    mu = jnp.mean(r, axis=-1, keepdims=True)
    rc = r - mu
    var = jnp.mean(rc * rc, axis=-1, keepdims=True)
    o_ref[0] = rc * lax.rsqrt(var + LN_EPS) * lg_ref[...] + lb_ref[...]


def _output(od_t, of_t, on_t, w_out, x, gmod, ln_g, ln_b):
    bsz, seq, d = x.shape
    tm = PROJ_TOKENS
    feat = lambda rows: pl.BlockSpec((1, rows, tm), lambda b, i: (b, 0, i))
    return pl.pallas_call(
        _out_kernel,
        grid=(bsz, seq // tm),
        in_specs=[
            feat(DSA_W), feat(FOX_W), feat(NSA_W),
            pl.BlockSpec((MIX_W, d), lambda b, i: (0, 0)),
            pl.BlockSpec((1, tm, d), lambda b, i: (b, i, 0)),
            pl.BlockSpec((1, 1, d), lambda b, i: (b, 0, 0)),
            pl.BlockSpec((1, d), lambda b, i: (0, 0)),
            pl.BlockSpec((1, d), lambda b, i: (0, 0)),
        ],
        out_specs=pl.BlockSpec((1, tm, d), lambda b, i: (b, i, 0)),
        out_shape=jax.ShapeDtypeStruct((bsz, seq, d), F32),
        compiler_params=_cparams(("arbitrary", "arbitrary")),
        name="out_proj_ln",
    )(od_t, of_t, on_t, w_out, x, gmod, ln_g, ln_b)


def _split_cols(w):
    out, off = {}, 0
    for name, width in IN_SPLITS:
        out[name] = w[:, off:off + width]
        off += width
    return out


def _prep_w_in(w_in):
    p = _split_cols(w_in)
    d = w_in.shape[0]
    z = lambda n: jnp.zeros((d, n), w_in.dtype)
    gcols = p["nsa_g"].reshape(d, 3, NSA_GROUPS, NSA_HPG)
    gparts = []
    for g in range(NSA_GROUPS):
        gparts += [gcols[:, :, g, :].reshape(d, 3 * NSA_HPG), z(16 - 3 * NSA_HPG)]
    wt = jnp.concatenate(
        [p["dsa_q"], p["idx_q"], p["fox_q"], p["nsa_q"],
         p["dsa_v"], z(HEAD_DIM), p["fox_v"], p["nsa_vs"], p["nsa_vw"], p["gate"],
         p["idx_w"], p["fox_f"], z(8 - FOX_HEADS)] + gparts, axis=1)
    assert wt.shape[1] == T_ROWS
    ws = jnp.concatenate(
        [p["dsa_k"], z(LANES - HEAD_DIM), p["idx_k"], z(LANES - IDX_DIM), p["fox_k"],
         p["nsa_ks"], p["nsa_kw"], p["nsa_kc"], p["nsa_vc"], p["fox_f"], z(LANES - FOX_HEADS)], axis=1)
    assert ws.shape[1] == S_COLS
    return wt.T.astype(BF16), ws.astype(BF16)


def _rope_tables(seq):
    pos = jnp.arange(seq, dtype=F32)
    tabs = []
    for hd in (HEAD_DIM, IDX_DIM):
        half = hd // 2
        inv = ROPE_THETA ** (-jnp.arange(half, dtype=F32) / half)
        ang = inv[:, None] * pos[None, :]
        tabs += [jnp.cos(ang), jnp.sin(ang)]
    lane = jnp.arange(LANES)
    for hd in (HEAD_DIM, IDX_DIM):
        half = hd // 2
        inv = ROPE_THETA ** (-jnp.arange(half, dtype=F32) / half)
        ang = pos[:, None] * inv[lane % half][None, :]
        sign = jnp.where((lane % hd) < half, -1.0, 1.0).astype(F32)
        tabs += [jnp.cos(ang), jnp.sin(ang) * sign[None, :]]
    return tuple(tabs)


def _prep_compress(cmp_pe, cmp_w1, cmp_w2):
    eye = jnp.eye(NSA_GROUPS, dtype=F32)
    w1 = cmp_w1.reshape(2, CMP_LEN, HEAD_DIM, HEAD_DIM)
    big1 = jnp.einsum("kjde,gh->kjgdhe", w1, eye).reshape(2, CMP_LEN, LANES, LANES)
    w1lo = big1[:, :CMP_STRIDE].reshape(2, CMP_STRIDE * LANES, LANES).astype(BF16)
    w1hi = big1[:, CMP_STRIDE:].reshape(2, CMP_STRIDE * LANES, LANES).astype(BF16)
    w2 = jnp.einsum("kde,gh->kgdhe", cmp_w2, eye).reshape(2, LANES, LANES).astype(BF16)
    pe2 = jnp.concatenate([cmp_pe] * NSA_GROUPS, axis=-1)
    return pe2[:, :CMP_STRIDE], pe2[:, CMP_STRIDE:], w1lo, w1hi, w2


def _selection_constants(n_cmp):
    cstart = jnp.arange(n_cmp) * CMP_STRIDE
    bstart = jnp.arange(N_SEL_BLOCKS) * SEL_BLOCK
    real = (jnp.arange(n_cmp) < (SEQ - CMP_LEN) // CMP_STRIDE + 1)[:, None]
    ovl = ((cstart[:, None] < bstart[None, :] + SEL_BLOCK) & (cstart[:, None] + CMP_LEN > bstart[None, :]) & real)
    ovl_t = jnp.concatenate([ovl.T.astype(BF16)] * NSA_HPG, axis=1)
    expand = (jnp.arange(KEY_CHUNK)[:, None] // SEL_BLOCK == jnp.arange(KEY_CHUNK // SEL_BLOCK)[None, :]).astype(F32)
    return ovl_t, expand


def kernel(x, c, w_ada, b_ada, w_in, b_f, cmp_pe, cmp_w1, cmp_w2, w_out, ln_g, ln_b):
    bsz, seq, d = x.shape
    assert (seq, d) == (SEQ, D_MODEL)
    mod = _modulation(c, w_ada, b_ada)
    tabs = _rope_tables(seq)
    ovl, expand = _selection_constants(seq // CMP_STRIDE)
    for l in range(DEPTH):
        shift = mod[l, :, :d].reshape(bsz, 1, d)
        scale = mod[l, :, d:2 * d].reshape(bsz, 1, d)
        gmod = mod[l, :, 2 * d:].reshape(bsz, 1, d)
        wt, ws = _prep_w_in(w_in[l])
        bf8 = jnp.concatenate([b_f[l], jnp.zeros((8 - FOX_HEADS,), F32)])
        (dq, iq, fq, nq, nqr, dv, fv, vs, vw, gate, iw, lft, gts,
         dk, ik, fk, ks, kw, kcin, vcin, lfs) = _projection(
            x, scale, shift, wt, ws, tabs, bf8.reshape(8, 1), bf8.reshape(1, 8))
        cum_t, cum_s = _forget_cumsum(lft, lfs)
        kc, vc_t = _compress(kcin, vcin, *_prep_compress(cmp_pe[l], cmp_w1[l], cmp_w2[l]))
        o_dsa = _dsa(iq, ik, iw, dq, dk, dv, gate)
        o_fox = _fox(fq, fk, fv, cum_t, cum_s, gate)
        o_nsa = _nsa(nq, nqr, kc, vc_t, ks, vs, kw, vw, gts, ovl, expand, gate)
        x = _output(o_dsa, o_fox, o_nsa, w_out[l].astype(BF16), x, gmod,
                    ln_g[l].reshape(1, d), ln_b[l].reshape(1, d))
    return x
```

```python
import functools

import jax
import jax.numpy as jnp
from jax import lax
from jax.experimental import pallas as pl
from jax.experimental.pallas import tpu as pltpu

F32 = jnp.float32
BF16 = jnp.bfloat16
I32 = jnp.int32

D_MODEL = 1024
SEQ = 2048
DEPTH = 2
HEAD_DIM = 64
DSA_HEADS = 4
DSA_TOPK = 256
IDX_HEADS = 8
IDX_DIM = 32
FOX_HEADS = 6
NSA_HEADS = 6
NSA_GROUPS = 2
NSA_HPG = NSA_HEADS // NSA_GROUPS
CMP_LEN = 32
CMP_STRIDE = 16
SEL_BLOCK = 64
SEL_N = 16
N_SEL_BLOCKS = SEQ // SEL_BLOCK
WINDOW = 512
ROPE_THETA = 10000.0
LN_EPS = 1e-5
ALPHA = (2.0 * DEPTH) ** 0.25

DSA_W = DSA_HEADS * HEAD_DIM
FOX_W = FOX_HEADS * HEAD_DIM
NSA_W = NSA_HEADS * HEAD_DIM
NSA_KV_W = NSA_GROUPS * HEAD_DIM
MIX_W = DSA_W + FOX_W + NSA_W
IDX_W = IDX_HEADS * IDX_DIM

IN_SPLITS = (
    ("dsa_q", DSA_W), ("dsa_k", HEAD_DIM), ("dsa_v", HEAD_DIM),
    ("idx_q", IDX_W), ("idx_k", IDX_DIM), ("idx_w", IDX_HEADS),
    ("fox_q", FOX_W), ("fox_k", FOX_W), ("fox_v", FOX_W), ("fox_f", FOX_HEADS),
    ("nsa_q", NSA_W),
    ("nsa_kc", NSA_KV_W), ("nsa_vc", NSA_KV_W),
    ("nsa_ks", NSA_KV_W), ("nsa_vs", NSA_KV_W),
    ("nsa_kw", NSA_KV_W), ("nsa_vw", NSA_KV_W),
    ("nsa_g", 3 * NSA_HEADS),
    ("gate", MIX_W),
)

LANES = 128
KEY_CHUNK = 512
PROJ_TOKENS = 512
Q_TILE = 256
DSA_Q_TILE = 512
FOX_Q_TILE = 512
N_KEY_CHUNKS = SEQ // KEY_CHUNK
WIN_CHUNK = 256
N_WIN_CHUNKS = SEQ // WIN_CHUNK
WIN_SPAN = WINDOW + Q_TILE
CUM_CHUNK = 256
VMEM_LIMIT = 56 * 1024 * 1024

FOLD_ROWS = 32
ONES_ROWS = 16
LOG2E = 1.4426950408889634
NEG = -(2.0 ** 100)
FLT_LOWEST = -3.4028234663852886e38
INT_MIN = -(2 ** 31)
CODE_NEG_INF = 0x007FFFFF

T_DQ, T_IQ, T_FQ, T_NQ = 0, 256, 512, 896
T_DV, T_FV, T_VS, T_VW = 1280, 1408, 1792, 1920
T_GATE, T_SMALL, T_ROWS = 2048, 3072, 3120
S_DK, S_IK, S_FK, S_KS, S_KW, S_KC, S_VC, S_COLS = 0, 128, 256, 640, 768, 896, 1024, 1152


def _cparams(sem):
    return pltpu.CompilerParams(dimension_semantics=sem, vmem_limit_bytes=VMEM_LIMIT)


def _dot(a, b):
    return jnp.dot(a, b, preferred_element_type=F32)


def _dot_nt(a, b):
    return lax.dot_general(a, b, (((1,), (1,)), ((), ())), preferred_element_type=F32)


def _log_sigmoid(x):
    return jnp.minimum(x, 0.0) - jnp.log(1.0 + jnp.exp(-jnp.abs(x)))


def _sigmoid(x):
    return 1.0 / (1.0 + jnp.exp(-x))


def _mod_kernel(c_ref, w_ref, b_ref, o_ref):
    o_ref[0] = _dot(c_ref[...].astype(BF16), w_ref[0].astype(BF16)) + b_ref[0]


def _modulation(c, w_ada, b_ada):
    depth, d, d3 = w_ada.shape
    bsz = c.shape[0]
    return pl.pallas_call(
        _mod_kernel,
        grid=(depth, d3 // d),
        in_specs=[
            pl.BlockSpec((bsz, d), lambda l, j: (0, 0)),
            pl.BlockSpec((1, d, d), lambda l, j: (l, 0, j)),
            pl.BlockSpec((1, 1, d), lambda l, j: (l, 0, j)),
        ],
        out_specs=pl.BlockSpec((1, bsz, d), lambda l, j: (l, 0, j)),
        out_shape=jax.ShapeDtypeStruct((depth, bsz, d3), F32),
        compiler_params=_cparams(("arbitrary", "arbitrary")),
        name="adaln_mod",
    )(c, w_ada, b_ada.reshape(depth, 1, d3))


def _proj_kernel(x_ref, sc_ref, sh_ref, wt_ref, ws_ref,
                 ct64_ref, st64_ref, ct32_ref, st32_ref,
                 ck64_ref, sk64_ref, ck32_ref, sk32_ref, bfc_ref,
                 dq_ref, iq_ref, fq_ref, nq_ref, nqr_ref,
                 dv_ref, fv_ref, vs_ref, vw_ref, gate_ref,
                 iw_ref, lft_ref, gts_ref,
                 dk_ref, ik_ref, fk_ref, ks_ref, kw_ref, kcin_ref, vcin_ref):
    tm = x_ref.shape[1]
    u = (x_ref[0] * (1.0 + sc_ref[0]) + sh_ref[0]).astype(BF16)

    ht_all = _dot_nt(wt_ref[...], u)

    def proj_t(r0, r1):
        return ht_all[r0:r1]

    def rope_t(h, n_heads, hd, c, s):
        half = hd // 2
        out = []
        for hh in range(n_heads):
            x1 = h[hh * hd:hh * hd + half]
            x2 = h[hh * hd + half:(hh + 1) * hd]
            out.append((hh * hd, x1 * c - x2 * s))
            out.append((hh * hd + half, x1 * s + x2 * c))
        return out

    c64, s64 = ct64_ref[...], st64_ref[...]
    c32, s32 = ct32_ref[...], st32_ref[...]
    qscale = HEAD_DIM ** -0.5 * LOG2E

    h = proj_t(T_DQ, T_DQ + DSA_W)
    for r, v in rope_t(h, DSA_HEADS, HEAD_DIM, c64, s64):
        dq_ref[0, r:r + HEAD_DIM // 2, :] = (v * qscale).astype(BF16)
    h = proj_t(T_IQ, T_IQ + IDX_W)
    for r, v in rope_t(h, IDX_HEADS, IDX_DIM, c32, s32):
        iq_ref[0, r:r + IDX_DIM // 2, :] = v.astype(BF16)
    fq_ref[0] = (proj_t(T_FQ, T_FQ + FOX_W) * qscale).astype(BF16)
    h = proj_t(T_NQ, T_NQ + NSA_W)
    nq_ref[0] = (h * qscale).astype(BF16)
    for r, v in rope_t(h, NSA_HEADS, HEAD_DIM, c64, s64):
        nqr_ref[0, r:r + HEAD_DIM // 2, :] = (v * qscale).astype(BF16)

    dv_ref[0, 0] = proj_t(T_DV, T_DV + HEAD_DIM).astype(BF16)
    h = proj_t(T_FV, T_FV + FOX_W)
    for hh in range(FOX_HEADS):
        fv_ref[0, hh, 0] = h[hh * HEAD_DIM:(hh + 1) * HEAD_DIM].astype(BF16)
    h = proj_t(T_VS, T_VS + NSA_KV_W)
    for g in range(NSA_GROUPS):
        vs_ref[0, g, 0] = h[g * HEAD_DIM:(g + 1) * HEAD_DIM].astype(BF16)
    h = proj_t(T_VW, T_VW + NSA_KV_W)
    for g in range(NSA_GROUPS):
        for j in range(tm // WIN_CHUNK):
            vw_ref[0, g, j] = h[g * HEAD_DIM:(g + 1) * HEAD_DIM, j * WIN_CHUNK:(j + 1) * WIN_CHUNK].astype(BF16)

    for r0 in range(0, MIX_W, 256):
        h = proj_t(T_GATE + r0, T_GATE + r0 + 256)
        gate_ref[0, r0:r0 + 256, :] = h * _sigmoid(h)

    h = proj_t(T_SMALL, T_ROWS)
    iw_ref[0] = h[0:8] * (IDX_HEADS ** -0.5)
    lft_ref[0] = _log_sigmoid(h[8:16] + bfc_ref[...])
    g_all = _sigmoid(h[16:48])
    gts_ref[0, 0] = g_all[0:16]
    gts_ref[0, 1] = g_all[16:32]

    lane = lax.broadcasted_iota(I32, (tm, LANES), 1)

    def rope_s(g, half, c, s_signed):
        first = (lane & (2 * half - 1)) < half
        sw = jnp.where(first, pltpu.roll(g, LANES - half, 1), pltpu.roll(g, half, 1))
        return g * c + sw * s_signed

    hs_all = _dot(u, ws_ref[...])

    def proj_s(c0, c1):
        return hs_all[:, c0:c1]

    ck64, sk64 = ck64_ref[...], sk64_ref[...]
    g = rope_s(proj_s(S_DK, S_DK + LANES), HEAD_DIM // 2, ck64, sk64)
    dk_ref[0] = g[:, :HEAD_DIM].astype(BF16)
    g = rope_s(proj_s(S_IK, S_IK + LANES), IDX_DIM // 2, ck32_ref[...], sk32_ref[...])
    ik_ref[0] = g[:, :IDX_DIM].astype(BF16)
    g = proj_s(S_FK, S_FK + FOX_W)
    for hh in range(FOX_HEADS):
        fk_ref[0, hh] = g[:, hh * HEAD_DIM:(hh + 1) * HEAD_DIM].astype(BF16)
    g = rope_s(proj_s(S_KS, S_KS + LANES), HEAD_DIM // 2, ck64, sk64)
    for gg in range(NSA_GROUPS):
        ks_ref[0, gg] = g[:, gg * HEAD_DIM:(gg + 1) * HEAD_DIM].astype(BF16)
    g = rope_s(proj_s(S_KW, S_KW + LANES), HEAD_DIM // 2, ck64, sk64)
    for gg in range(NSA_GROUPS):
        kw_ref[0, gg] = g[:, gg * HEAD_DIM:(gg + 1) * HEAD_DIM].astype(BF16)
    kcin_ref[0] = proj_s(S_KC, S_KC + LANES)
    vcin_ref[0] = proj_s(S_VC, S_VC + LANES)


def _projection(x, scale, shift, wt, ws, tabs, bfc):
    bsz, seq, d = x.shape
    tm = PROJ_TOKENS
    nt = seq // tm
    ct64, st64, ct32, st32, ck64, sk64, ck32, sk32 = tabs
    const = lambda shape: pl.BlockSpec(shape, lambda b, i: (0,) * len(shape))
    feat = lambda rows: pl.BlockSpec((1, rows, tm), lambda b, i: (b, 0, i))
    in_specs = [
        pl.BlockSpec((1, tm, d), lambda b, i: (b, i, 0)),
        pl.BlockSpec((1, 1, d), lambda b, i: (b, 0, 0)),
        pl.BlockSpec((1, 1, d), lambda b, i: (b, 0, 0)),
        const((T_ROWS, d)),
        const((d, S_COLS)),
        pl.BlockSpec((HEAD_DIM // 2, tm), lambda b, i: (0, i)),
        pl.BlockSpec((HEAD_DIM // 2, tm), lambda b, i: (0, i)),
        pl.BlockSpec((IDX_DIM // 2, tm), lambda b, i: (0, i)),
        pl.BlockSpec((IDX_DIM // 2, tm), lambda b, i: (0, i)),
        pl.BlockSpec((tm, LANES), lambda b, i: (i, 0)),
        pl.BlockSpec((tm, LANES), lambda b, i: (i, 0)),
        pl.BlockSpec((tm, LANES), lambda b, i: (i, 0)),
        pl.BlockSpec((tm, LANES), lambda b, i: (i, 0)),
        const((8, 1)),
    ]
    sds = jax.ShapeDtypeStruct
    out_shape = [
        sds((bsz, DSA_W, seq), BF16), sds((bsz, IDX_W, seq), BF16), sds((bsz, FOX_W, seq), BF16),
        sds((bsz, NSA_W, seq), BF16), sds((bsz, NSA_W, seq), BF16),
        sds((bsz, N_KEY_CHUNKS, HEAD_DIM, KEY_CHUNK), BF16),
        sds((bsz, FOX_HEADS, N_KEY_CHUNKS, HEAD_DIM, KEY_CHUNK), BF16),
        sds((bsz, NSA_GROUPS, N_KEY_CHUNKS, HEAD_DIM, KEY_CHUNK), BF16),
        sds((bsz, NSA_GROUPS, N_WIN_CHUNKS, HEAD_DIM, WIN_CHUNK), BF16),
        sds((bsz, MIX_W, seq), F32),
        sds((bsz, 8, seq), F32), sds((bsz, 8, seq), F32), sds((bsz, NSA_GROUPS, 16, seq), F32),
        sds((bsz, seq, HEAD_DIM), BF16), sds((bsz, seq, IDX_DIM), BF16),
        sds((bsz, FOX_HEADS, seq, HEAD_DIM), BF16),
        sds((bsz, NSA_GROUPS, seq, HEAD_DIM), BF16), sds((bsz, NSA_GROUPS, seq, HEAD_DIM), BF16),
        sds((bsz, seq, LANES), F32), sds((bsz, seq, LANES), F32),
    ]
    wpc = tm // WIN_CHUNK
    out_specs = [
        feat(DSA_W), feat(IDX_W), feat(FOX_W), feat(NSA_W), feat(NSA_W),
        pl.BlockSpec((1, 1, HEAD_DIM, KEY_CHUNK), lambda b, i: (b, i, 0, 0)),
        pl.BlockSpec((1, FOX_HEADS, 1, HEAD_DIM, KEY_CHUNK), lambda b, i: (b, 0, i, 0, 0)),
        pl.BlockSpec((1, NSA_GROUPS, 1, HEAD_DIM, KEY_CHUNK), lambda b, i: (b, 0, i, 0, 0)),
        pl.BlockSpec((1, NSA_GROUPS, wpc, HEAD_DIM, WIN_CHUNK), lambda b, i: (b, 0, i, 0, 0)),
        feat(MIX_W),
        feat(8), feat(8),
        pl.BlockSpec((1, NSA_GROUPS, 16, tm), lambda b, i: (b, 0, 0, i)),
        pl.BlockSpec((1, tm, HEAD_DIM), lambda b, i: (b, i, 0)),
        pl.BlockSpec((1, tm, IDX_DIM), lambda b, i: (b, i, 0)),
        pl.BlockSpec((1, FOX_HEADS, tm, HEAD_DIM), lambda b, i: (b, 0, i, 0)),
        pl.BlockSpec((1, NSA_GROUPS, tm, HEAD_DIM), lambda b, i: (b, 0, i, 0)),
        pl.BlockSpec((1, NSA_GROUPS, tm, HEAD_DIM), lambda b, i: (b, 0, i, 0)),
        pl.BlockSpec((1, tm, LANES), lambda b, i: (b, i, 0)),
        pl.BlockSpec((1, tm, LANES), lambda b, i: (b, i, 0)),
    ]
    assert tm == KEY_CHUNK
    return pl.pallas_call(
        _proj_kernel,
        grid=(bsz, nt),
        in_specs=in_specs,
        out_specs=out_specs,
        out_shape=out_shape,
        compiler_params=_cparams(("arbitrary", "arbitrary")),
        name="in_proj",
    )(x, scale, shift, wt, ws, ct64, st64, ct32, st32, ck64, sk64, ck32, sk32, bfc)


def _cumsum_kernel(lft_ref, cumt_ref, cums_ref):
    seq = lft_ref.shape[2]
    r = lax.broadcasted_iota(I32, (CUM_CHUNK, CUM_CHUNK), 0)
    c = lax.broadcasted_iota(I32, (CUM_CHUNK, CUM_CHUNK), 1)
    tri_u = (r <= c).astype(F32)
    blocks = [slice(k * CUM_CHUNK, (k + 1) * CUM_CHUNK) for k in range(seq // CUM_CHUNK)]
    local = [jnp.dot(lft_ref[0, :, sl], tri_u, preferred_element_type=F32, precision=lax.Precision.HIGHEST)
             for sl in blocks]
    carry = jnp.zeros((8, 1), F32)
    pad = jnp.zeros((LANES - 8, CUM_CHUNK), F32)
    for k, sl in enumerate(blocks):
        ct = (local[k] + carry) * LOG2E
        carry = local[k][:, CUM_CHUNK - 1:CUM_CHUNK] + carry
        for head in range(8):
            cumt_ref[0, head, :, sl] = ct[head:head + 1, :]
        cums_ref[0, sl, :] = jnp.concatenate([ct, pad], axis=0).T[:, :8]


def _forget_cumsum(lft):
    bsz, _, seq = lft.shape
    return pl.pallas_call(
        _cumsum_kernel,
        grid=(bsz,),
        in_specs=[pl.BlockSpec((1, 8, seq), lambda b: (b, 0, 0))],
        out_specs=[pl.BlockSpec((1, 8, 1, seq), lambda b: (b, 0, 0, 0)),
                   pl.BlockSpec((1, seq, 8), lambda b: (b, 0, 0))],
        out_shape=[jax.ShapeDtypeStruct((bsz, 8, 1, seq), F32),
                   jax.ShapeDtypeStruct((bsz, seq, 8), F32)],
        compiler_params=_cparams(("arbitrary",)),
        name="forget_cumsum",
    )(lft)


def _compress_kernel(kcin_ref, vcin_ref, pelo_ref, pehi_ref, w1lo_ref, w1hi_ref, w2_ref,
                     kc_ref, vct_ref):
    n_blk = kcin_ref.shape[1] // CMP_STRIDE
    for kv, src in enumerate((kcin_ref, vcin_ref)):
        lo, hi = [], []
        for j in range(CMP_STRIDE):
            piece = src[0, pl.ds(j, n_blk, stride=CMP_STRIDE), :]
            lo.append((piece + pelo_ref[kv, j:j + 1, :]).astype(BF16))
            hi.append((piece + pehi_ref[kv, j:j + 1, :]).astype(BF16))
        a = _dot(jnp.concatenate(lo, axis=1), w1lo_ref[kv])
        b = _dot(jnp.concatenate(hi, axis=1), w1hi_ref[kv])
        pre = a + pltpu.roll(b, n_blk - 1, 0)
        act = pre * _sigmoid(pre)
        out = _dot(act.astype(BF16), w2_ref[kv])
        if kv == 0:
            for g in range(NSA_GROUPS):
                kc_ref[0, g] = out[:, g * HEAD_DIM:(g + 1) * HEAD_DIM].astype(BF16)
        else:
            out_t = out.T
            for g in range(NSA_GROUPS):
                vct_ref[0, g] = out_t[g * HEAD_DIM:(g + 1) * HEAD_DIM].astype(BF16)


def _compress(kcin, vcin, pelo, pehi, w1lo, w1hi, w2):
    bsz, seq, _ = kcin.shape
    n_blk = seq // CMP_STRIDE
    full = lambda a: pl.BlockSpec(a.shape, lambda b: (0,) * a.ndim)
    return pl.pallas_call(
        _compress_kernel,
        grid=(bsz,),
        in_specs=[pl.BlockSpec((1, seq, LANES), lambda b: (b, 0, 0)),
                  pl.BlockSpec((1, seq, LANES), lambda b: (b, 0, 0)),
                  full(pelo), full(pehi), full(w1lo), full(w1hi), full(w2)],
        out_specs=[pl.BlockSpec((1, NSA_GROUPS, n_blk, HEAD_DIM), lambda b: (b, 0, 0, 0)),
                   pl.BlockSpec((1, NSA_GROUPS, HEAD_DIM, n_blk), lambda b: (b, 0, 0, 0))],
        out_shape=[jax.ShapeDtypeStruct((bsz, NSA_GROUPS, n_blk, HEAD_DIM), BF16),
                   jax.ShapeDtypeStruct((bsz, NSA_GROUPS, HEAD_DIM, n_blk), BF16)],
        compiler_params=_cparams(("arbitrary",)),
        name="nsa_compress",
    )(kcin, vcin, pelo, pehi, w1lo, w1hi, w2)


def _fold_rows(x, op):
    k, n = x.shape
    return op(x.reshape(k // FOLD_ROWS, FOLD_ROWS, n), axis=0)


def _reduce_rows(x, op):
    if x.shape[0] % FOLD_ROWS == 0 and x.shape[0] > FOLD_ROWS:
        x = _fold_rows(x, op)
    return op(x, axis=0, keepdims=True)


def _with_ones(v_t):
    return jnp.concatenate([v_t, jnp.ones((ONES_ROWS, v_t.shape[1]), v_t.dtype)], axis=0)


def _softmax_step(carry, s, v_aug):
    m, acc = carry
    m_new = jnp.maximum(m, _reduce_rows(s, jnp.max))
    alpha = jnp.exp2(m - m_new)
    p = jnp.exp2(s - m_new)
    acc = alpha * acc + _dot(v_aug, p.astype(BF16))
    return m_new, acc


def _softmax_init(n):
    return (jnp.full((1, n), NEG, F32), jnp.zeros((HEAD_DIM + ONES_ROWS, n), F32))


def _softmax_finish(carry):
    _, acc = carry
    return acc[:HEAD_DIM] * (1.0 / acc[HEAD_DIM:HEAD_DIM + 1])


def _tile_lanes(a, n):
    return jnp.concatenate([a] * n, axis=1)


def _dsa_kernel(iq_ref, ik_ref, iw_ref, dq_ref, dk_ref, dv_ref, g_ref, o_ref, key_ref, jsel_ref):
    i = pl.program_id(1)
    tq = DSA_Q_TILE
    n_chunks = i + 1
    tpos = i * tq + lax.broadcasted_iota(I32, (1, tq), 1)
    srow = lax.broadcasted_iota(I32, (KEY_CHUNK, tq), 0)
    iw = iw_ref[0]

    def index_chunk(c, _):
        k0 = pl.multiple_of(c * KEY_CHUNK, KEY_CHUNK)
        ikc = ik_ref[0, pl.ds(k0, KEY_CHUNK), :]
        acc = jnp.zeros((KEY_CHUNK, tq), F32)
        for h in range(IDX_HEADS):
            x = _dot(ikc, iq_ref[0, h * IDX_DIM:(h + 1) * IDX_DIM, :])
            acc = acc + iw[h:h + 1, :] * jnp.maximum(x, 0.0)
        key_ref[pl.ds(k0, KEY_CHUNK), :] = jnp.where(srow + k0 <= tpos, acc, -jnp.inf)
        return 0

    lax.fori_loop(0, n_chunks, index_chunk, 0)

    def count(pred):
        def body(c, acc):
            k0 = pl.multiple_of(c * KEY_CHUNK, KEY_CHUNK)
            hit = pred(key_ref[pl.ds(k0, KEY_CHUNK), :], srow + k0)
            return acc + _fold_rows(jnp.where(hit, 1, 0), jnp.sum)
        acc = lax.fori_loop(0, n_chunks, body, jnp.zeros((FOLD_ROWS, tq), I32))
        return jnp.sum(acc, axis=0, keepdims=True)

    def decode(code):
        skey = code ^ INT_MIN
        val = lax.bitcast_convert_type(jnp.where(skey < 0, skey ^ 0x7FFFFFFF, skey), F32)
        return jnp.where(jnp.logical_and(code >= 0, code <= CODE_NEG_INF), -jnp.inf, val)

    def value_bit(it, code):
        cand = code | lax.shift_left(jnp.int32(1), 31 - it)
        cand_f = decode(cand)
        cnt = count(lambda key, spos: key >= cand_f)
        return jnp.where(cnt >= DSA_TOPK, cand, code)

    thr = decode(lax.fori_loop(0, 32, value_bit, jnp.zeros((1, tq), I32)))

    cnt_gt = count(lambda key, spos: key > thr)
    cnt_ge = count(lambda key, spos: key >= thr)
    need = DSA_TOPK - cnt_gt
    has_tie = jnp.logical_and(cnt_ge > DSA_TOPK, thr > -jnp.inf)
    jsel_ref[...] = jnp.full((1, tq), SEQ, I32)
    any_tie = jnp.max(jnp.where(has_tie, 1.0, 0.0)) > 0.0

    @pl.when(any_tie)
    def _():
        def index_bit(it, j):
            cand = j | lax.shift_left(jnp.int32(1), 10 - it)
            cnt = count(lambda key, spos: jnp.logical_and(key == thr, spos < cand))
            return jnp.where(cnt < need, cand, j)
        jsel_ref[...] = lax.fori_loop(0, 11, index_bit, jnp.zeros((1, tq), I32))

    jsel = jsel_ref[...]

    def select_with_ties(key, spos):
        sel = jnp.logical_or(key > thr, jnp.logical_and(key == thr, spos <= jsel))
        return jnp.logical_and(sel, spos <= tpos)

    thr_finite = jnp.maximum(thr, FLT_LOWEST)

    def attend_all(select):
        def attend(c, carry):
            k0 = pl.multiple_of(c * KEY_CHUNK, KEY_CHUNK)
            bias = jnp.where(select(key_ref[pl.ds(k0, KEY_CHUNK), :], srow + k0), 0.0, NEG)
            kc = dk_ref[0, pl.ds(k0, KEY_CHUNK), :]
            v_aug = _with_ones(dv_ref[0, c])
            return tuple(
                _softmax_step(carry[h], _dot(kc, dq_ref[0, h * HEAD_DIM:(h + 1) * HEAD_DIM, :]) + bias, v_aug)
                for h in range(DSA_HEADS))
        return lax.fori_loop(0, n_chunks, attend, tuple(_softmax_init(tq) for _ in range(DSA_HEADS)))

    heads = lax.cond(any_tie,
                     lambda: attend_all(select_with_ties),
                     lambda: attend_all(lambda key, spos: key >= thr_finite))
    for h, carry in enumerate(heads):
        rows = slice(h * HEAD_DIM, (h + 1) * HEAD_DIM)
        o_ref[0, rows, :] = (_softmax_finish(carry) * g_ref[0, rows, :]).astype(BF16)


def _dsa(iq_t, ik, iw_t, dq_t, dk, dv_t, gate_t):
    bsz, _, seq = dq_t.shape
    tq = DSA_Q_TILE
    assert tq == KEY_CHUNK
    return pl.pallas_call(
        _dsa_kernel,
        grid=(bsz, seq // tq),
        in_specs=[
            pl.BlockSpec((1, IDX_W, tq), lambda b, i: (b, 0, i)),
            pl.BlockSpec((1, seq, IDX_DIM), lambda b, i: (b, 0, 0)),
            pl.BlockSpec((1, 8, tq), lambda b, i: (b, 0, i)),
            pl.BlockSpec((1, DSA_W, tq), lambda b, i: (b, 0, i)),
            pl.BlockSpec((1, seq, HEAD_DIM), lambda b, i: (b, 0, 0)),
            pl.BlockSpec((1, N_KEY_CHUNKS, HEAD_DIM, KEY_CHUNK), lambda b, i: (b, 0, 0, 0)),
            pl.BlockSpec((1, DSA_W, tq), lambda b, i: (b, 0, i)),
        ],
        out_specs=pl.BlockSpec((1, DSA_W, tq), lambda b, i: (b, 0, i)),
        out_shape=jax.ShapeDtypeStruct((bsz, DSA_W, seq), BF16),
        scratch_shapes=[pltpu.VMEM((seq, tq), F32), pltpu.VMEM((1, tq), I32)],
        compiler_params=_cparams(("arbitrary", "arbitrary")),
        name="dsa_attn",
    )(iq_t, ik, iw_t, dq_t, dk, dv_t, gate_t)


def _fox_kernel(q_ref, k_ref, v_ref, cumt_ref, cums_ref, g_ref, o_ref):
    h = pl.program_id(1)
    tq = FOX_Q_TILE
    n_tiles = q_ref.shape[2] // tq
    head_lane = lax.broadcasted_iota(I32, (KEY_CHUNK, 8), 1) == h
    srow = lax.broadcasted_iota(I32, (KEY_CHUNK, tq), 0)
    tcol = lax.broadcasted_iota(I32, (KEY_CHUNK, tq), 1)
    ccols = []
    for c in range(n_tiles):
        col = jnp.sum(jnp.where(head_lane, cums_ref[0, c * KEY_CHUNK:(c + 1) * KEY_CHUNK, :], 0.0),
                      axis=1, keepdims=True)
        ccols.append(_tile_lanes(jnp.broadcast_to(col, (KEY_CHUNK, LANES)), tq // LANES))
    v_aug = [_with_ones(v_ref[0, 0, c]) for c in range(n_tiles)]
    for i in range(n_tiles):
        q = q_ref[0, :, i * tq:(i + 1) * tq]
        crow = cumt_ref[0, 0, :, i * tq:(i + 1) * tq]
        carry = _softmax_init(tq)
        for c in range(i + 1):
            s = _dot(k_ref[0, 0, c * KEY_CHUNK:(c + 1) * KEY_CHUNK, :], q) + crow - ccols[c]
            if c == i:
                s = jnp.where(srow <= tcol, s, NEG)
            carry = _softmax_step(carry, s, v_aug[c])
        cols = slice(i * tq, (i + 1) * tq)
        o_ref[0, :, cols] = (_softmax_finish(carry) * g_ref[0, :, cols]).astype(BF16)


def _fox(fq_t, fk, fv_t, cum_t, cum_s, gate_t):
    bsz, _, seq = fq_t.shape
    assert FOX_Q_TILE == KEY_CHUNK
    return pl.pallas_call(
        _fox_kernel,
        grid=(bsz, FOX_HEADS),
        in_specs=[
            pl.BlockSpec((1, HEAD_DIM, seq), lambda b, h: (b, h, 0)),
            pl.BlockSpec((1, 1, seq, HEAD_DIM), lambda b, h: (b, h, 0, 0)),
            pl.BlockSpec((1, 1, N_KEY_CHUNKS, HEAD_DIM, KEY_CHUNK), lambda b, h: (b, h, 0, 0, 0)),
            pl.BlockSpec((1, 1, 1, seq), lambda b, h: (b, h, 0, 0)),
            pl.BlockSpec((1, seq, 8), lambda b, h: (b, 0, 0)),
            pl.BlockSpec((1, HEAD_DIM, seq), lambda b, h: (b, DSA_HEADS + h, 0)),
        ],
        out_specs=pl.BlockSpec((1, HEAD_DIM, seq), lambda b, h: (b, h, 0)),
        out_shape=jax.ShapeDtypeStruct((bsz, FOX_W, seq), BF16),
        compiler_params=_cparams(("arbitrary", "arbitrary")),
        name="fox_attn",
    )(fq_t, fk, fv_t, cum_t, cum_s, gate_t)


def _nsa_kernel(q_ref, qr_ref, kc_ref, vct_ref, ks_ref, vs_ref, kw_ref, vw_ref, g_ref,
                ovl_ref, expand_ref, og0_ref, og1_ref, og2_ref, o_ref, sel_ref):
    i = pl.program_id(2)
    nq = NSA_HPG * Q_TILE
    t0 = i * Q_TILE
    tpos = t0 + lax.broadcasted_iota(I32, (1, Q_TILE), 1)
    q3 = jnp.concatenate([q_ref[0, j * HEAD_DIM:(j + 1) * HEAD_DIM, :] for j in range(NSA_HPG)], axis=1)
    qr3 = jnp.concatenate([qr_ref[0, j * HEAD_DIM:(j + 1) * HEAD_DIM, :] for j in range(NSA_HPG)], axis=1)

    n_cmp = kc_ref.shape[2]
    cend = lax.broadcasted_iota(I32, (n_cmp, Q_TILE), 0) * CMP_STRIDE + (CMP_LEN - 1)
    cbias = _tile_lanes(jnp.where(cend <= tpos, 0.0, NEG), NSA_HPG)
    cvalid = _tile_lanes(jnp.where(cend <= tpos, 1.0, 0.0), NSA_HPG)
    s = _dot(kc_ref[0, 0], q3) + cbias
    m = _reduce_rows(s, jnp.max)
    e = jnp.exp2(s - m) * cvalid
    p_cmp = (e / jnp.maximum(_reduce_rows(e, jnp.sum), 1e-30)).astype(BF16)
    o_cmp = _dot(vct_ref[0, 0], p_cmp)

    p_stack = jnp.concatenate([p_cmp[:, j * Q_TILE:(j + 1) * Q_TILE] for j in range(NSA_HPG)], axis=0)
    score = _dot(ovl_ref[...], p_stack)
    blk = lax.broadcasted_iota(I32, (N_SEL_BLOCKS, Q_TILE), 0)
    cur = lax.shift_right_logical(tpos, 6)
    forced = jnp.logical_or(blk == 0, jnp.logical_or(blk == cur, blk == cur - 1))
    score = jnp.where(forced, jnp.inf, jnp.where(blk > cur, -jnp.inf, score))
    rank = jnp.zeros((N_SEL_BLOCKS, Q_TILE), F32)
    for mth in range(N_SEL_BLOCKS):
        row = score[mth:mth + 1, :]
        ahead = jnp.logical_or(row > score, jnp.logical_and(row == score, blk > mth))
        rank = rank + jnp.where(ahead, 1.0, 0.0)
    sel_ref[...] = jnp.where(rank < SEL_N, 1.0, 0.0)

    n_chunks = (i * Q_TILE + Q_TILE + KEY_CHUNK - 1) // KEY_CHUNK
    srow = lax.broadcasted_iota(I32, (KEY_CHUNK, Q_TILE), 0)
    blocks_per_chunk = KEY_CHUNK // SEL_BLOCK

    def attend_sel(c, carry):
        k0 = pl.multiple_of(c * KEY_CHUNK, KEY_CHUNK)
        b0 = pl.multiple_of(c * blocks_per_chunk, blocks_per_chunk)
        picked = _dot(expand_ref[...], sel_ref[pl.ds(b0, blocks_per_chunk), :])
        ok = jnp.logical_and(picked > 0.5, srow + k0 <= tpos)
        bias = _tile_lanes(jnp.where(ok, 0.0, NEG), NSA_HPG)
        s = _dot(ks_ref[0, 0, pl.ds(k0, KEY_CHUNK), :], qr3) + bias
        return _softmax_step(carry, s, _with_ones(vs_ref[0, 0, c]))

    o_slc = _softmax_finish(lax.fori_loop(0, n_chunks, attend_sel, _softmax_init(nq)))

    wc = jnp.maximum(i - WINDOW // Q_TILE, 0) * (Q_TILE // WIN_CHUNK)
    w0 = pl.multiple_of(wc * WIN_CHUNK, WIN_CHUNK)
    kpos = w0 + lax.broadcasted_iota(I32, (WIN_SPAN, Q_TILE), 0)
    wok = jnp.logical_and(kpos <= tpos, kpos > tpos - WINDOW)
    s = _dot(kw_ref[0, 0, pl.ds(w0, WIN_SPAN), :], qr3) + _tile_lanes(jnp.where(wok, 0.0, NEG), NSA_HPG)
    m = _reduce_rows(s, jnp.max)
    e = jnp.exp2(s - m).astype(BF16)
    acc = jnp.zeros((HEAD_DIM + ONES_ROWS, nq), F32)
    for j in range(WIN_SPAN // WIN_CHUNK):
        acc = acc + _dot(_with_ones(vw_ref[0, 0, wc + j]), e[j * WIN_CHUNK:(j + 1) * WIN_CHUNK, :])
    o_win = _softmax_finish((m, acc))

    gts = g_ref[0, 0]
    out_gates = (og0_ref, og1_ref, og2_ref)
    for j in range(NSA_HPG):
        sl = slice(j * Q_TILE, (j + 1) * Q_TILE)
        mixed = (gts[j:j + 1, :] * o_cmp[:, sl]
                 + gts[NSA_HPG + j:NSA_HPG + j + 1, :] * o_slc[:, sl]
                 + gts[2 * NSA_HPG + j:2 * NSA_HPG + j + 1, :] * o_win[:, sl])
        o_ref[0, j * HEAD_DIM:(j + 1) * HEAD_DIM, :] = (mixed * out_gates[j][0]).astype(BF16)


def _nsa(nq_t, nqr_t, kc, vc_t, ks, vs_t, kw, vw_t, gts, ovl, expand, gate_t):
    bsz, _, seq = nq_t.shape
    n_cmp = kc.shape[2]
    gw = NSA_HPG * HEAD_DIM
    per_group = lambda shape: pl.BlockSpec((1, 1) + shape, lambda b, g, i: (b, g) + (0,) * len(shape))
    return pl.pallas_call(
        _nsa_kernel,
        grid=(bsz, NSA_GROUPS, seq // Q_TILE),
        in_specs=[
            pl.BlockSpec((1, gw, Q_TILE), lambda b, g, i: (b, g, i)),
            pl.BlockSpec((1, gw, Q_TILE), lambda b, g, i: (b, g, i)),
            per_group((n_cmp, HEAD_DIM)),
            per_group((HEAD_DIM, n_cmp)),
            per_group((seq, HEAD_DIM)),
            per_group((N_KEY_CHUNKS, HEAD_DIM, KEY_CHUNK)),
            per_group((seq, HEAD_DIM)),
            per_group((N_WIN_CHUNKS, HEAD_DIM, WIN_CHUNK)),
            pl.BlockSpec((1, 1, 16, Q_TILE), lambda b, g, i: (b, g, 0, i)),
            pl.BlockSpec(ovl.shape, lambda b, g, i: (0, 0)),
            pl.BlockSpec(expand.shape, lambda b, g, i: (0, 0)),
        ] + [
            pl.BlockSpec((1, HEAD_DIM, Q_TILE),
                         lambda b, g, i, j=j: (b, DSA_HEADS + FOX_HEADS + g * NSA_HPG + j, i))
            for j in range(NSA_HPG)
        ],
        out_specs=pl.BlockSpec((1, gw, Q_TILE), lambda b, g, i: (b, g, i)),
        out_shape=jax.ShapeDtypeStruct((bsz, NSA_W, seq), BF16),
        scratch_shapes=[pltpu.VMEM((N_SEL_BLOCKS, Q_TILE), F32)],
        compiler_params=_cparams(("arbitrary", "arbitrary", "arbitrary")),
        name="nsa_attn",
    )(nq_t, nqr_t, kc, vc_t, ks, vs_t, kw, vw_t, gts, ovl, expand, gate_t, gate_t, gate_t)


def _out_kernel(od_ref, of_ref, on_ref, w_ref, x_ref, gm_ref, lg_ref, lb_ref, o_ref):
    z = jnp.concatenate([od_ref[0], of_ref[0], on_ref[0]], axis=0)
    y = lax.dot_general(z, w_ref[...], (((0,), (0,)), ((), ())),
                        preferred_element_type=F32)
    r = ALPHA * x_ref[0] + (1.0 + gm_ref[0]) * y
    mu = jnp.mean(r, axis=-1, keepdims=True)
    rc = r - mu
    var = jnp.mean(rc * rc, axis=-1, keepdims=True)
    o_ref[0] = rc * lax.rsqrt(var + LN_EPS) * lg_ref[...] + lb_ref[...]


def _output(od_t, of_t, on_t, w_out, x, gmod, ln_g, ln_b):
    bsz, seq, d = x.shape
    tm = PROJ_TOKENS
    feat = lambda rows: pl.BlockSpec((1, rows, tm), lambda b, i: (b, 0, i))
    return pl.pallas_call(
        _out_kernel,
        grid=(bsz, seq // tm),
        in_specs=[
            feat(DSA_W), feat(FOX_W), feat(NSA_W),
            pl.BlockSpec((MIX_W, d), lambda b, i: (0, 0)),
            pl.BlockSpec((1, tm, d), lambda b, i: (b, i, 0)),
            pl.BlockSpec((1, 1, d), lambda b, i: (b, 0, 0)),
            pl.BlockSpec((1, d), lambda b, i: (0, 0)),
            pl.BlockSpec((1, d), lambda b, i: (0, 0)),
        ],
        out_specs=pl.BlockSpec((1, tm, d), lambda b, i: (b, i, 0)),
        out_shape=jax.ShapeDtypeStruct((bsz, seq, d), F32),
        compiler_params=_cparams(("arbitrary", "arbitrary")),
        name="out_proj_ln",
    )(od_t, of_t, on_t, w_out, x, gmod, ln_g, ln_b)


def _split_cols(w):
    out, off = {}, 0
    for name, width in IN_SPLITS:
        out[name] = w[:, off:off + width]
        off += width
    return out


def _prep_w_in(w_in):
    p = _split_cols(w_in)
    d = w_in.shape[0]
    z = lambda n: jnp.zeros((d, n), w_in.dtype)
    gcols = p["nsa_g"].reshape(d, 3, NSA_GROUPS, NSA_HPG)
    gparts = []
    for g in range(NSA_GROUPS):
        gparts += [gcols[:, :, g, :].reshape(d, 3 * NSA_HPG), z(16 - 3 * NSA_HPG)]
    wt = jnp.concatenate(
        [p["dsa_q"], p["idx_q"], p["fox_q"], p["nsa_q"],
         p["dsa_v"], z(HEAD_DIM), p["fox_v"], p["nsa_vs"], p["nsa_vw"], p["gate"],
         p["idx_w"], p["fox_f"], z(8 - FOX_HEADS)] + gparts, axis=1)
    assert wt.shape[1] == T_ROWS
    ws = jnp.concatenate(
        [p["dsa_k"], z(LANES - HEAD_DIM), p["idx_k"], z(LANES - IDX_DIM), p["fox_k"],
         p["nsa_ks"], p["nsa_kw"], p["nsa_kc"], p["nsa_vc"]], axis=1)
    assert ws.shape[1] == S_COLS
    return wt.T.astype(BF16), ws.astype(BF16)


def _rope_tables(seq):
    pos = jnp.arange(seq, dtype=F32)
    feature_major, token_major = [], []
    lane = jnp.arange(LANES)
    for hd in (HEAD_DIM, IDX_DIM):
        half = hd // 2
        inv = ROPE_THETA ** (-jnp.arange(half, dtype=F32) / half)
        ang = inv[:, None] * pos[None, :]
        cos, sin = jnp.cos(ang), jnp.sin(ang)
        feature_major += [cos, sin]
        sign = jnp.where((lane % hd) < half, -1.0, 1.0).astype(F32)
        reps = LANES // half
        token_major += [jnp.tile(cos.T, (1, reps)), jnp.tile(sin.T, (1, reps)) * sign[None, :]]
    return tuple(feature_major + token_major)


def _prep_compress(cmp_pe, cmp_w1, cmp_w2):
    eye = jnp.eye(NSA_GROUPS, dtype=F32)
    w1 = cmp_w1.reshape(2, CMP_LEN, HEAD_DIM, HEAD_DIM)
    big1 = jnp.einsum("kjde,gh->kjgdhe", w1, eye).reshape(2, CMP_LEN, LANES, LANES)
    w1lo = big1[:, :CMP_STRIDE].reshape(2, CMP_STRIDE * LANES, LANES).astype(BF16)
    w1hi = big1[:, CMP_STRIDE:].reshape(2, CMP_STRIDE * LANES, LANES).astype(BF16)
    w2 = jnp.einsum("kde,gh->kgdhe", cmp_w2, eye).reshape(2, LANES, LANES).astype(BF16)
    pe2 = jnp.concatenate([cmp_pe] * NSA_GROUPS, axis=-1)
    return pe2[:, :CMP_STRIDE], pe2[:, CMP_STRIDE:], w1lo, w1hi, w2


def _selection_constants(n_cmp):
    cstart = jnp.arange(n_cmp) * CMP_STRIDE
    bstart = jnp.arange(N_SEL_BLOCKS) * SEL_BLOCK
    real = (jnp.arange(n_cmp) < (SEQ - CMP_LEN) // CMP_STRIDE + 1)[:, None]
    ovl = ((cstart[:, None] < bstart[None, :] + SEL_BLOCK) & (cstart[:, None] + CMP_LEN > bstart[None, :]) & real)
    ovl_t = jnp.concatenate([ovl.T.astype(BF16)] * NSA_HPG, axis=1)
    expand = (jnp.arange(KEY_CHUNK)[:, None] // SEL_BLOCK == jnp.arange(KEY_CHUNK // SEL_BLOCK)[None, :]).astype(F32)
    return ovl_t, expand


def kernel(x, c, w_ada, b_ada, w_in, b_f, cmp_pe, cmp_w1, cmp_w2, w_out, ln_g, ln_b):
    bsz, seq, d = x.shape
    assert (seq, d) == (SEQ, D_MODEL)
    mod = _modulation(c, w_ada, b_ada)
    tabs = _rope_tables(seq)
    ovl, expand = _selection_constants(seq // CMP_STRIDE)
    for l in range(DEPTH):
        shift = mod[l, :, :d].reshape(bsz, 1, d)
        scale = mod[l, :, d:2 * d].reshape(bsz, 1, d)
        gmod = mod[l, :, 2 * d:].reshape(bsz, 1, d)
        wt, ws = _prep_w_in(w_in[l])
        bf8 = jnp.concatenate([b_f[l], jnp.zeros((8 - FOX_HEADS,), F32)])
        (dq, iq, fq, nq, nqr, dv, fv, vs, vw, gate, iw, lft, gts,
         dk, ik, fk, ks, kw, kcin, vcin) = _projection(
            x, scale, shift, wt, ws, tabs, bf8.reshape(8, 1))
        cum_t, cum_s = _forget_cumsum(lft)
        kc, vc_t = _compress(kcin, vcin, *_prep_compress(cmp_pe[l], cmp_w1[l], cmp_w2[l]))
        o_dsa = _dsa(iq, ik, iw, dq, dk, dv, gate)
        o_fox = _fox(fq, fk, fv, cum_t, cum_s, gate)
        o_nsa = _nsa(nq, nqr, kc, vc_t, ks, vs, kw, vw, gts, ovl, expand, gate)
        x = _output(o_dsa, o_fox, o_nsa, w_out[l].astype(BF16), x, gmod,
                    ln_g[l].reshape(1, d), ln_b[l].reshape(1, d))
    return x
```

```python
import functools

import jax
import jax.numpy as jnp
from jax import lax
from jax.experimental import pallas as pl
from jax.experimental.pallas import tpu as pltpu

F32 = jnp.float32
BF16 = jnp.bfloat16
I32 = jnp.int32

D_MODEL = 1024
SEQ = 2048
DEPTH = 2
HEAD_DIM = 64
DSA_HEADS = 4
DSA_TOPK = 256
IDX_HEADS = 8
IDX_DIM = 32
FOX_HEADS = 6
NSA_HEADS = 6
NSA_GROUPS = 2
NSA_HPG = NSA_HEADS // NSA_GROUPS
CMP_LEN = 32
CMP_STRIDE = 16
SEL_BLOCK = 64
SEL_N = 16
N_SEL_BLOCKS = SEQ // SEL_BLOCK
WINDOW = 512
ROPE_THETA = 10000.0
LN_EPS = 1e-5
ALPHA = (2.0 * DEPTH) ** 0.25

DSA_W = DSA_HEADS * HEAD_DIM
FOX_W = FOX_HEADS * HEAD_DIM
NSA_W = NSA_HEADS * HEAD_DIM
NSA_KV_W = NSA_GROUPS * HEAD_DIM
MIX_W = DSA_W + FOX_W + NSA_W
IDX_W = IDX_HEADS * IDX_DIM

IN_SPLITS = (
    ("dsa_q", DSA_W), ("dsa_k", HEAD_DIM), ("dsa_v", HEAD_DIM),
    ("idx_q", IDX_W), ("idx_k", IDX_DIM), ("idx_w", IDX_HEADS),
    ("fox_q", FOX_W), ("fox_k", FOX_W), ("fox_v", FOX_W), ("fox_f", FOX_HEADS),
    ("nsa_q", NSA_W),
    ("nsa_kc", NSA_KV_W), ("nsa_vc", NSA_KV_W),
    ("nsa_ks", NSA_KV_W), ("nsa_vs", NSA_KV_W),
    ("nsa_kw", NSA_KV_W), ("nsa_vw", NSA_KV_W),
    ("nsa_g", 3 * NSA_HEADS),
    ("gate", MIX_W),
)

LANES = 128
KEY_CHUNK = 512
PROJ_TOKENS = 512
Q_TILE = 256
DSA_Q_TILE = 512
FOX_Q_TILE = 512
N_KEY_CHUNKS = SEQ // KEY_CHUNK
WIN_CHUNK = 256
N_WIN_CHUNKS = SEQ // WIN_CHUNK
WIN_SPAN = WINDOW + Q_TILE
CUM_CHUNK = 256
VMEM_LIMIT = 56 * 1024 * 1024

FOLD_ROWS = 32
ONES_ROWS = 16
LOG2E = 1.4426950408889634
NEG = -(2.0 ** 100)
FLT_LOWEST = -3.4028234663852886e38
INT_MIN = -(2 ** 31)
CODE_NEG_INF = 0x007FFFFF

T_DQ, T_IQ, T_FQ, T_NQ = 0, 256, 512, 896
T_DV, T_FV, T_VS, T_VW = 1280, 1408, 1792, 1920
T_GATE, T_SMALL, T_ROWS = 2048, 3072, 3120
S_DK, S_IK, S_FK, S_KS, S_KW, S_KC, S_VC, S_COLS = 0, 128, 256, 640, 768, 896, 1024, 1152


def _cparams(sem):
    return pltpu.CompilerParams(dimension_semantics=sem, vmem_limit_bytes=VMEM_LIMIT)


def _dot(a, b):
    return jnp.dot(a, b, preferred_element_type=F32)


def _dot_nt(a, b):
    return lax.dot_general(a, b, (((1,), (1,)), ((), ())), preferred_element_type=F32)


def _log_sigmoid(x):
    return jnp.minimum(x, 0.0) - jnp.log(1.0 + jnp.exp(-jnp.abs(x)))


def _sigmoid(x):
    return 1.0 / (1.0 + jnp.exp(-x))


def _mod_kernel(c_ref, w_ref, b_ref, o_ref):
    o_ref[0] = _dot(c_ref[...].astype(BF16), w_ref[0].astype(BF16)) + b_ref[0]


def _modulation(c, w_ada, b_ada):
    depth, d, d3 = w_ada.shape
    bsz = c.shape[0]
    return pl.pallas_call(
        _mod_kernel,
        grid=(depth, d3 // d),
        in_specs=[
            pl.BlockSpec((bsz, d), lambda l, j: (0, 0)),
            pl.BlockSpec((1, d, d), lambda l, j: (l, 0, j)),
            pl.BlockSpec((1, 1, d), lambda l, j: (l, 0, j)),
        ],
        out_specs=pl.BlockSpec((1, bsz, d), lambda l, j: (l, 0, j)),
        out_shape=jax.ShapeDtypeStruct((depth, bsz, d3), F32),
        compiler_params=_cparams(("arbitrary", "arbitrary")),
        name="adaln_mod",
    )(c, w_ada, b_ada.reshape(depth, 1, d3))


def _proj_kernel(x_ref, sc_ref, sh_ref, wt_ref, ws_ref,
                 ct64_ref, st64_ref, ct32_ref, st32_ref,
                 ck64_ref, sk64_ref, ck32_ref, sk32_ref, bfc_ref,
                 dq_ref, iq_ref, fq_ref, nq_ref, nqr_ref,
                 dv_ref, fv_ref, vs_ref, vw_ref, gate_ref,
                 iw_ref, lft_ref, gts_ref,
                 dk_ref, ik_ref, fk_ref, ks_ref, kw_ref, kcin_ref, vcin_ref):
    tm = x_ref.shape[1]
    u = (x_ref[0] * (1.0 + sc_ref[0]) + sh_ref[0]).astype(BF16)

    ht_all = _dot_nt(wt_ref[...], u)

    def proj_t(r0, r1):
        return ht_all[r0:r1]

    def rope_t(h, n_heads, hd, c, s):
        half = hd // 2
        out = []
        for hh in range(n_heads):
            x1 = h[hh * hd:hh * hd + half]
            x2 = h[hh * hd + half:(hh + 1) * hd]
            out.append((hh * hd, x1 * c - x2 * s))
            out.append((hh * hd + half, x1 * s + x2 * c))
        return out

    c64, s64 = ct64_ref[...], st64_ref[...]
    c32, s32 = ct32_ref[...], st32_ref[...]
    qscale = HEAD_DIM ** -0.5 * LOG2E

    h = proj_t(T_DQ, T_DQ + DSA_W)
    for r, v in rope_t(h, DSA_HEADS, HEAD_DIM, c64, s64):
        dq_ref[0, r:r + HEAD_DIM // 2, :] = (v * qscale).astype(BF16)
    h = proj_t(T_IQ, T_IQ + IDX_W)
    for r, v in rope_t(h, IDX_HEADS, IDX_DIM, c32, s32):
        iq_ref[0, r:r + IDX_DIM // 2, :] = v.astype(BF16)
    fq_ref[0] = (proj_t(T_FQ, T_FQ + FOX_W) * qscale).astype(BF16)
    h = proj_t(T_NQ, T_NQ + NSA_W)
    nq_ref[0] = (h * qscale).astype(BF16)
    for r, v in rope_t(h, NSA_HEADS, HEAD_DIM, c64, s64):
        nqr_ref[0, r:r + HEAD_DIM // 2, :] = (v * qscale).astype(BF16)

    dv_ref[0, 0] = proj_t(T_DV, T_DV + HEAD_DIM).astype(BF16)
    h = proj_t(T_FV, T_FV + FOX_W)
    for hh in range(FOX_HEADS):
        fv_ref[0, hh, 0] = h[hh * HEAD_DIM:(hh + 1) * HEAD_DIM].astype(BF16)
    h = proj_t(T_VS, T_VS + NSA_KV_W)
    for g in range(NSA_GROUPS):
        vs_ref[0, g, 0] = h[g * HEAD_DIM:(g + 1) * HEAD_DIM].astype(BF16)
    h = proj_t(T_VW, T_VW + NSA_KV_W)
    for g in range(NSA_GROUPS):
        for j in range(tm // WIN_CHUNK):
            vw_ref[0, g, j] = h[g * HEAD_DIM:(g + 1) * HEAD_DIM, j * WIN_CHUNK:(j + 1) * WIN_CHUNK].astype(BF16)

    for r0 in range(0, MIX_W, 256):
        h = proj_t(T_GATE + r0, T_GATE + r0 + 256)
        gate_ref[0, r0:r0 + 256, :] = h * _sigmoid(h)

    h = proj_t(T_SMALL, T_ROWS)
    iw_ref[0] = h[0:8] * (IDX_HEADS ** -0.5)
    lft_ref[0] = _log_sigmoid(h[8:16] + bfc_ref[...])
    g_all = _sigmoid(h[16:48])
    gts_ref[0, 0] = g_all[0:16]
    gts_ref[0, 1] = g_all[16:32]

    lane = lax.broadcasted_iota(I32, (tm, LANES), 1)

    def rope_s(g, half, c, s_signed):
        first = (lane & (2 * half - 1)) < half
        sw = jnp.where(first, pltpu.roll(g, LANES - half, 1), pltpu.roll(g, half, 1))
        return g * c + sw * s_signed

    hs_all = _dot(u, ws_ref[...])

    def proj_s(c0, c1):
        return hs_all[:, c0:c1]

    ck64, sk64 = ck64_ref[...], sk64_ref[...]
    g = rope_s(proj_s(S_DK, S_DK + LANES), HEAD_DIM // 2, ck64, sk64)
    dk_ref[0] = g[:, :HEAD_DIM].astype(BF16)
    g = rope_s(proj_s(S_IK, S_IK + LANES), IDX_DIM // 2, ck32_ref[...], sk32_ref[...])
    ik_ref[0] = g[:, :IDX_DIM].astype(BF16)
    g = proj_s(S_FK, S_FK + FOX_W)
    for hh in range(FOX_HEADS):
        fk_ref[0, hh] = g[:, hh * HEAD_DIM:(hh + 1) * HEAD_DIM].astype(BF16)
    g = rope_s(proj_s(S_KS, S_KS + LANES), HEAD_DIM // 2, ck64, sk64)
    for gg in range(NSA_GROUPS):
        ks_ref[0, gg] = g[:, gg * HEAD_DIM:(gg + 1) * HEAD_DIM].astype(BF16)
    g = rope_s(proj_s(S_KW, S_KW + LANES), HEAD_DIM // 2, ck64, sk64)
    for gg in range(NSA_GROUPS):
        kw_ref[0, gg] = g[:, gg * HEAD_DIM:(gg + 1) * HEAD_DIM].astype(BF16)
    kcin_ref[0] = proj_s(S_KC, S_KC + LANES)
    vcin_ref[0] = proj_s(S_VC, S_VC + LANES)


def _projection(x, scale, shift, wt, ws, tabs, bfc):
    bsz, seq, d = x.shape
    tm = PROJ_TOKENS
    nt = seq // tm
    ct64, st64, ct32, st32, ck64, sk64, ck32, sk32 = tabs
    const = lambda shape: pl.BlockSpec(shape, lambda b, i: (0,) * len(shape))
    feat = lambda rows: pl.BlockSpec((1, rows, tm), lambda b, i: (b, 0, i))
    in_specs = [
        pl.BlockSpec((1, tm, d), lambda b, i: (b, i, 0)),
        pl.BlockSpec((1, 1, d), lambda b, i: (b, 0, 0)),
        pl.BlockSpec((1, 1, d), lambda b, i: (b, 0, 0)),
        const((T_ROWS, d)),
        const((d, S_COLS)),
        pl.BlockSpec((HEAD_DIM // 2, tm), lambda b, i: (0, i)),
        pl.BlockSpec((HEAD_DIM // 2, tm), lambda b, i: (0, i)),
        pl.BlockSpec((IDX_DIM // 2, tm), lambda b, i: (0, i)),
        pl.BlockSpec((IDX_DIM // 2, tm), lambda b, i: (0, i)),
        pl.BlockSpec((tm, LANES), lambda b, i: (i, 0)),
        pl.BlockSpec((tm, LANES), lambda b, i: (i, 0)),
        pl.BlockSpec((tm, LANES), lambda b, i: (i, 0)),
        pl.BlockSpec((tm, LANES), lambda b, i: (i, 0)),
        const((8, 1)),
    ]
    sds = jax.ShapeDtypeStruct
    out_shape = [
        sds((bsz, DSA_W, seq), BF16), sds((bsz, IDX_W, seq), BF16), sds((bsz, FOX_W, seq), BF16),
        sds((bsz, NSA_W, seq), BF16), sds((bsz, NSA_W, seq), BF16),
        sds((bsz, N_KEY_CHUNKS, HEAD_DIM, KEY_CHUNK), BF16),
        sds((bsz, FOX_HEADS, N_KEY_CHUNKS, HEAD_DIM, KEY_CHUNK), BF16),
        sds((bsz, NSA_GROUPS, N_KEY_CHUNKS, HEAD_DIM, KEY_CHUNK), BF16),
        sds((bsz, NSA_GROUPS, N_WIN_CHUNKS, HEAD_DIM, WIN_CHUNK), BF16),
        sds((bsz, MIX_W, seq), F32),
        sds((bsz, 8, seq), F32), sds((bsz, 8, seq), F32), sds((bsz, NSA_GROUPS, 16, seq), F32),
        sds((bsz, seq, HEAD_DIM), BF16), sds((bsz, seq, IDX_DIM), BF16),
        sds((bsz, FOX_HEADS, seq, HEAD_DIM), BF16),
        sds((bsz, NSA_GROUPS, seq, HEAD_DIM), BF16), sds((bsz, NSA_GROUPS, seq, HEAD_DIM), BF16),
        sds((bsz, seq, LANES), F32), sds((bsz, seq, LANES), F32),
    ]
    wpc = tm // WIN_CHUNK
    out_specs = [
        feat(DSA_W), feat(IDX_W), feat(FOX_W), feat(NSA_W), feat(NSA_W),
        pl.BlockSpec((1, 1, HEAD_DIM, KEY_CHUNK), lambda b, i: (b, i, 0, 0)),
        pl.BlockSpec((1, FOX_HEADS, 1, HEAD_DIM, KEY_CHUNK), lambda b, i: (b, 0, i, 0, 0)),
        pl.BlockSpec((1, NSA_GROUPS, 1, HEAD_DIM, KEY_CHUNK), lambda b, i: (b, 0, i, 0, 0)),
        pl.BlockSpec((1, NSA_GROUPS, wpc, HEAD_DIM, WIN_CHUNK), lambda b, i: (b, 0, i, 0, 0)),
        feat(MIX_W),
        feat(8), feat(8),
        pl.BlockSpec((1, NSA_GROUPS, 16, tm), lambda b, i: (b, 0, 0, i)),
        pl.BlockSpec((1, tm, HEAD_DIM), lambda b, i: (b, i, 0)),
        pl.BlockSpec((1, tm, IDX_DIM), lambda b, i: (b, i, 0)),
        pl.BlockSpec((1, FOX_HEADS, tm, HEAD_DIM), lambda b, i: (b, 0, i, 0)),
        pl.BlockSpec((1, NSA_GROUPS, tm, HEAD_DIM), lambda b, i: (b, 0, i, 0)),
        pl.BlockSpec((1, NSA_GROUPS, tm, HEAD_DIM), lambda b, i: (b, 0, i, 0)),
        pl.BlockSpec((1, tm, LANES), lambda b, i: (b, i, 0)),
        pl.BlockSpec((1, tm, LANES), lambda b, i: (b, i, 0)),
    ]
    assert tm == KEY_CHUNK
    return pl.pallas_call(
        _proj_kernel,
        grid=(bsz, nt),
        in_specs=in_specs,
        out_specs=out_specs,
        out_shape=out_shape,
        compiler_params=_cparams(("arbitrary", "arbitrary")),
        name="in_proj",
    )(x, scale, shift, wt, ws, ct64, st64, ct32, st32, ck64, sk64, ck32, sk32, bfc)


def _cumsum_kernel(lft_ref, cumt_ref, cums_ref):
    seq = lft_ref.shape[2]
    r = lax.broadcasted_iota(I32, (CUM_CHUNK, CUM_CHUNK), 0)
    c = lax.broadcasted_iota(I32, (CUM_CHUNK, CUM_CHUNK), 1)
    tri_u = (r <= c).astype(F32)
    blocks = [slice(k * CUM_CHUNK, (k + 1) * CUM_CHUNK) for k in range(seq // CUM_CHUNK)]
    local = [jnp.dot(lft_ref[0, :, sl], tri_u, preferred_element_type=F32, precision=lax.Precision.HIGHEST)
             for sl in blocks]
    carry = jnp.zeros((8, 1), F32)
    pad = jnp.zeros((LANES - 8, CUM_CHUNK), F32)
    for k, sl in enumerate(blocks):
        ct = (local[k] + carry) * LOG2E
        carry = local[k][:, CUM_CHUNK - 1:CUM_CHUNK] + carry
        for head in range(8):
            cumt_ref[0, head, :, sl] = ct[head:head + 1, :]
        cums_ref[0, sl, :] = jnp.concatenate([ct, pad], axis=0).T[:, :8]


def _forget_cumsum(lft):
    bsz, _, seq = lft.shape
    return pl.pallas_call(
        _cumsum_kernel,
        grid=(bsz,),
        in_specs=[pl.BlockSpec((1, 8, seq), lambda b: (b, 0, 0))],
        out_specs=[pl.BlockSpec((1, 8, 1, seq), lambda b: (b, 0, 0, 0)),
                   pl.BlockSpec((1, seq, 8), lambda b: (b, 0, 0))],
        out_shape=[jax.ShapeDtypeStruct((bsz, 8, 1, seq), F32),
                   jax.ShapeDtypeStruct((bsz, seq, 8), F32)],
        compiler_params=_cparams(("arbitrary",)),
        name="forget_cumsum",
    )(lft)


def _compress_kernel(kcin_ref, vcin_ref, pelo_ref, pehi_ref, w1lo_ref, w1hi_ref, w2_ref,
                     kc_ref, vct_ref):
    n_blk = kcin_ref.shape[1] // CMP_STRIDE
    for kv, src in enumerate((kcin_ref, vcin_ref)):
        lo, hi = [], []
        for j in range(CMP_STRIDE):
            piece = src[0, pl.ds(j, n_blk, stride=CMP_STRIDE), :]
            lo.append((piece + pelo_ref[kv, j:j + 1, :]).astype(BF16))
            hi.append((piece + pehi_ref[kv, j:j + 1, :]).astype(BF16))
        a = _dot(jnp.concatenate(lo, axis=1), w1lo_ref[kv])
        b = _dot(jnp.concatenate(hi, axis=1), w1hi_ref[kv])
        pre = a + pltpu.roll(b, n_blk - 1, 0)
        act = pre * _sigmoid(pre)
        out = _dot(act.astype(BF16), w2_ref[kv])
        if kv == 0:
            for g in range(NSA_GROUPS):
                kc_ref[0, g] = out[:, g * HEAD_DIM:(g + 1) * HEAD_DIM].astype(BF16)
        else:
            out_t = out.T
            for g in range(NSA_GROUPS):
                vct_ref[0, g] = out_t[g * HEAD_DIM:(g + 1) * HEAD_DIM].astype(BF16)


def _compress(kcin, vcin, pelo, pehi, w1lo, w1hi, w2):
    bsz, seq, _ = kcin.shape
    n_blk = seq // CMP_STRIDE
    full = lambda a: pl.BlockSpec(a.shape, lambda b: (0,) * a.ndim)
    return pl.pallas_call(
        _compress_kernel,
        grid=(bsz,),
        in_specs=[pl.BlockSpec((1, seq, LANES), lambda b: (b, 0, 0)),
                  pl.BlockSpec((1, seq, LANES), lambda b: (b, 0, 0)),
                  full(pelo), full(pehi), full(w1lo), full(w1hi), full(w2)],
        out_specs=[pl.BlockSpec((1, NSA_GROUPS, n_blk, HEAD_DIM), lambda b: (b, 0, 0, 0)),
                   pl.BlockSpec((1, NSA_GROUPS, HEAD_DIM, n_blk), lambda b: (b, 0, 0, 0))],
        out_shape=[jax.ShapeDtypeStruct((bsz, NSA_GROUPS, n_blk, HEAD_DIM), BF16),
                   jax.ShapeDtypeStruct((bsz, NSA_GROUPS, HEAD_DIM, n_blk), BF16)],
        compiler_params=_cparams(("arbitrary",)),
        name="nsa_compress",
    )(kcin, vcin, pelo, pehi, w1lo, w1hi, w2)


def _fold_rows(x, op):
    k, n = x.shape
    return op(x.reshape(k // FOLD_ROWS, FOLD_ROWS, n), axis=0)


def _reduce_rows(x, op):
    if x.shape[0] % FOLD_ROWS == 0 and x.shape[0] > FOLD_ROWS:
        x = _fold_rows(x, op)
    return op(x, axis=0, keepdims=True)


def _with_ones(v_t):
    return jnp.concatenate([v_t, jnp.ones((ONES_ROWS, v_t.shape[1]), v_t.dtype)], axis=0)


def _softmax_step(carry, s, v_aug):
    m, acc = carry
    m_new = jnp.maximum(m, _reduce_rows(s, jnp.max))
    alpha = jnp.exp2(m - m_new)
    p = jnp.exp2(s - m_new)
    acc = alpha * acc + _dot(v_aug, p.astype(BF16))
    return m_new, acc


def _softmax_init(n):
    return (jnp.full((1, n), NEG, F32), jnp.zeros((HEAD_DIM + ONES_ROWS, n), F32))


def _softmax_finish(carry):
    _, acc = carry
    return acc[:HEAD_DIM] * (1.0 / acc[HEAD_DIM:HEAD_DIM + 1])


def _tile_lanes(a, n):
    return jnp.concatenate([a] * n, axis=1)


def _dsa_kernel(iq_ref, ik_ref, iw_ref, dq_ref, dk_ref, dv_ref, g_ref, o_ref, key_ref):
    i = pl.program_id(1)
    tq = DSA_Q_TILE
    n_chunks = i + 1
    tpos = i * tq + lax.broadcasted_iota(I32, (1, tq), 1)
    srow = lax.broadcasted_iota(I32, (KEY_CHUNK, tq), 0)
    iw = iw_ref[0]

    def index_chunk(c, _):
        k0 = pl.multiple_of(c * KEY_CHUNK, KEY_CHUNK)
        ikc = ik_ref[0, pl.ds(k0, KEY_CHUNK), :]
        acc = jnp.zeros((KEY_CHUNK, tq), F32)
        for h in range(IDX_HEADS):
            x = _dot(ikc, iq_ref[0, h * IDX_DIM:(h + 1) * IDX_DIM, :])
            acc = acc + iw[h:h + 1, :] * jnp.maximum(x, 0.0)
        key_ref[pl.ds(k0, KEY_CHUNK), :] = jnp.where(srow + k0 <= tpos, acc, -jnp.inf)
        return 0

    lax.fori_loop(0, n_chunks, index_chunk, 0)

    def count(pred):
        def body(c, acc):
            k0 = pl.multiple_of(c * KEY_CHUNK, KEY_CHUNK)
            hit = pred(key_ref[pl.ds(k0, KEY_CHUNK), :], srow + k0)
            return acc + _fold_rows(jnp.where(hit, 1, 0), jnp.sum)
        acc = lax.fori_loop(0, n_chunks, body, jnp.zeros((FOLD_ROWS, tq), I32))
        return jnp.sum(acc, axis=0, keepdims=True)

    def decode(code):
        skey = code ^ INT_MIN
        val = lax.bitcast_convert_type(jnp.where(skey < 0, skey ^ 0x7FFFFFFF, skey), F32)
        return jnp.where(jnp.logical_and(code >= 0, code <= CODE_NEG_INF), -jnp.inf, val)

    def value_bit(it, code):
        cand = code | lax.shift_left(jnp.int32(1), 31 - it)
        cand_f = decode(cand)
        cnt = count(lambda key, spos: key >= cand_f)
        return jnp.where(cnt >= DSA_TOPK, cand, code)

    thr = decode(lax.fori_loop(0, 32, value_bit, jnp.zeros((1, tq), I32)))

    cnt_gt = count(lambda key, spos: key > thr)
    cnt_ge = count(lambda key, spos: key >= thr)
    need = (DSA_TOPK - cnt_gt).astype(F32)
    has_tie = jnp.logical_and(cnt_ge > DSA_TOPK, thr > -jnp.inf)
    any_tie = jnp.max(jnp.where(has_tie, 1.0, 0.0)) > 0.0

    def attend_all(select, extra):
        def attend(c, carry):
            k0 = pl.multiple_of(c * KEY_CHUNK, KEY_CHUNK)
            keep, extra = select(key_ref[pl.ds(k0, KEY_CHUNK), :], srow + k0, carry[1])
            bias = jnp.where(keep, 0.0, NEG)
            kc = dk_ref[0, pl.ds(k0, KEY_CHUNK), :]
            v_aug = _with_ones(dv_ref[0, c])
            heads = tuple(
                _softmax_step(carry[0][h], _dot(kc, dq_ref[0, h * HEAD_DIM:(h + 1) * HEAD_DIM, :]) + bias, v_aug)
                for h in range(DSA_HEADS))
            return heads, extra
        init = tuple(_softmax_init(tq) for _ in range(DSA_HEADS))
        return lax.fori_loop(0, n_chunks, attend, (init, extra))[0]

    def attend_with_ties():
        r = lax.broadcasted_iota(I32, (KEY_CHUNK, KEY_CHUNK), 0)
        c = lax.broadcasted_iota(I32, (KEY_CHUNK, KEY_CHUNK), 1)
        prefix = jnp.where(c <= r, 1.0, 0.0).astype(BF16)

        def select(key, spos, ties_before):
            tied = key == thr
            rank = _dot(prefix, jnp.where(tied, 1.0, 0.0).astype(BF16)) + ties_before
            keep = jnp.logical_or(key > thr, jnp.logical_and(tied, rank <= need))
            return jnp.logical_and(keep, spos <= tpos), rank[KEY_CHUNK - 1:KEY_CHUNK, :]

        return attend_all(select, jnp.zeros((1, tq), F32))

    def attend_without_ties():
        thr_finite = jnp.maximum(thr, FLT_LOWEST)
        return attend_all(lambda key, spos, extra: (key >= thr_finite, extra), jnp.zeros((1, tq), F32))

    heads = lax.cond(any_tie, attend_with_ties, attend_without_ties)
    for h, carry in enumerate(heads):
        rows = slice(h * HEAD_DIM, (h + 1) * HEAD_DIM)
        o_ref[0, rows, :] = (_softmax_finish(carry) * g_ref[0, rows, :]).astype(BF16)


def _dsa(iq_t, ik, iw_t, dq_t, dk, dv_t, gate_t):
    bsz, _, seq = dq_t.shape
    tq = DSA_Q_TILE
    assert tq == KEY_CHUNK
    return pl.pallas_call(
        _dsa_kernel,
        grid=(bsz, seq // tq),
        in_specs=[
            pl.BlockSpec((1, IDX_W, tq), lambda b, i: (b, 0, i)),
            pl.BlockSpec((1, seq, IDX_DIM), lambda b, i: (b, 0, 0)),
            pl.BlockSpec((1, 8, tq), lambda b, i: (b, 0, i)),
            pl.BlockSpec((1, DSA_W, tq), lambda b, i: (b, 0, i)),
            pl.BlockSpec((1, seq, HEAD_DIM), lambda b, i: (b, 0, 0)),
            pl.BlockSpec((1, N_KEY_CHUNKS, HEAD_DIM, KEY_CHUNK), lambda b, i: (b, 0, 0, 0)),
            pl.BlockSpec((1, DSA_W, tq), lambda b, i: (b, 0, i)),
        ],
        out_specs=pl.BlockSpec((1, DSA_W, tq), lambda b, i: (b, 0, i)),
        out_shape=jax.ShapeDtypeStruct((bsz, DSA_W, seq), BF16),
        scratch_shapes=[pltpu.VMEM((seq, tq), F32)],
        compiler_params=_cparams(("arbitrary", "arbitrary")),
        name="dsa_attn",
    )(iq_t, ik, iw_t, dq_t, dk, dv_t, gate_t)


def _fox_kernel(q_ref, k_ref, v_ref, cumt_ref, cums_ref, g_ref, o_ref):
    h = pl.program_id(1)
    tq = FOX_Q_TILE
    n_tiles = q_ref.shape[2] // tq
    head_lane = lax.broadcasted_iota(I32, (KEY_CHUNK, 8), 1) == h
    srow = lax.broadcasted_iota(I32, (KEY_CHUNK, tq), 0)
    tcol = lax.broadcasted_iota(I32, (KEY_CHUNK, tq), 1)
    ccols = []
    for c in range(n_tiles):
        col = jnp.sum(jnp.where(head_lane, cums_ref[0, c * KEY_CHUNK:(c + 1) * KEY_CHUNK, :], 0.0),
                      axis=1, keepdims=True)
        ccols.append(_tile_lanes(jnp.broadcast_to(col, (KEY_CHUNK, LANES)), tq // LANES))
    v_aug = [_with_ones(v_ref[0, 0, c]) for c in range(n_tiles)]
    for i in range(n_tiles):
        q = q_ref[0, :, i * tq:(i + 1) * tq]
        crow = cumt_ref[0, 0, :, i * tq:(i + 1) * tq]
        carry = _softmax_init(tq)
        for c in range(i + 1):
            s = _dot(k_ref[0, 0, c * KEY_CHUNK:(c + 1) * KEY_CHUNK, :], q) + crow - ccols[c]
            if c == i:
                s = jnp.where(srow <= tcol, s, NEG)
            carry = _softmax_step(carry, s, v_aug[c])
        cols = slice(i * tq, (i + 1) * tq)
        o_ref[0, :, cols] = (_softmax_finish(carry) * g_ref[0, :, cols]).astype(BF16)


def _fox(fq_t, fk, fv_t, cum_t, cum_s, gate_t):
    bsz, _, seq = fq_t.shape
    assert FOX_Q_TILE == KEY_CHUNK
    return pl.pallas_call(
        _fox_kernel,
        grid=(bsz, FOX_HEADS),
        in_specs=[
            pl.BlockSpec((1, HEAD_DIM, seq), lambda b, h: (b, h, 0)),
            pl.BlockSpec((1, 1, seq, HEAD_DIM), lambda b, h: (b, h, 0, 0)),
            pl.BlockSpec((1, 1, N_KEY_CHUNKS, HEAD_DIM, KEY_CHUNK), lambda b, h: (b, h, 0, 0, 0)),
            pl.BlockSpec((1, 1, 1, seq), lambda b, h: (b, h, 0, 0)),
            pl.BlockSpec((1, seq, 8), lambda b, h: (b, 0, 0)),
            pl.BlockSpec((1, HEAD_DIM, seq), lambda b, h: (b, DSA_HEADS + h, 0)),
        ],
        out_specs=pl.BlockSpec((1, HEAD_DIM, seq), lambda b, h: (b, h, 0)),
        out_shape=jax.ShapeDtypeStruct((bsz, FOX_W, seq), BF16),
        compiler_params=_cparams(("arbitrary", "arbitrary")),
        name="fox_attn",
    )(fq_t, fk, fv_t, cum_t, cum_s, gate_t)


def _nsa_kernel(q_ref, qr_ref, kc_ref, vct_ref, ks_ref, vs_ref, kw_ref, vw_ref, g_ref,
                ovl_ref, expand_ref, og0_ref, og1_ref, og2_ref, o_ref, sel_ref):
    i = pl.program_id(2)
    nq = NSA_HPG * Q_TILE
    t0 = i * Q_TILE
    tpos = t0 + lax.broadcasted_iota(I32, (1, Q_TILE), 1)
    q3 = jnp.concatenate([q_ref[0, j * HEAD_DIM:(j + 1) * HEAD_DIM, :] for j in range(NSA_HPG)], axis=1)
    qr3 = jnp.concatenate([qr_ref[0, j * HEAD_DIM:(j + 1) * HEAD_DIM, :] for j in range(NSA_HPG)], axis=1)

    n_cmp = kc_ref.shape[2]
    cend = lax.broadcasted_iota(I32, (n_cmp, Q_TILE), 0) * CMP_STRIDE + (CMP_LEN - 1)
    cbias = _tile_lanes(jnp.where(cend <= tpos, 0.0, NEG), NSA_HPG)
    cvalid = _tile_lanes(jnp.where(cend <= tpos, 1.0, 0.0), NSA_HPG)
    s = _dot(kc_ref[0, 0], q3) + cbias
    m = _reduce_rows(s, jnp.max)
    e = jnp.exp2(s - m) * cvalid
    p_cmp = (e / jnp.maximum(_reduce_rows(e, jnp.sum), 1e-30)).astype(BF16)
    o_cmp = _dot(vct_ref[0, 0], p_cmp)

    p_stack = jnp.concatenate([p_cmp[:, j * Q_TILE:(j + 1) * Q_TILE] for j in range(NSA_HPG)], axis=0)
    score = _dot(ovl_ref[...], p_stack)
    blk = lax.broadcasted_iota(I32, (N_SEL_BLOCKS, Q_TILE), 0)
    cur = lax.shift_right_logical(tpos, 6)
    forced = jnp.logical_or(blk == 0, jnp.logical_or(blk == cur, blk == cur - 1))
    score = jnp.where(forced, jnp.inf, jnp.where(blk > cur, -jnp.inf, score))
    rank = jnp.zeros((N_SEL_BLOCKS, Q_TILE), F32)
    for mth in range(N_SEL_BLOCKS):
        row = score[mth:mth + 1, :]
        ahead = jnp.logical_or(row > score, jnp.logical_and(row == score, blk > mth))
        rank = rank + jnp.where(ahead, 1.0, 0.0)
    sel_ref[...] = jnp.where(rank < SEL_N, 1.0, 0.0)

    n_chunks = (i * Q_TILE + Q_TILE + KEY_CHUNK - 1) // KEY_CHUNK
    srow = lax.broadcasted_iota(I32, (KEY_CHUNK, Q_TILE), 0)
    blocks_per_chunk = KEY_CHUNK // SEL_BLOCK

    def attend_sel(c, carry):
        k0 = pl.multiple_of(c * KEY_CHUNK, KEY_CHUNK)
        b0 = pl.multiple_of(c * blocks_per_chunk, blocks_per_chunk)
        picked = _dot(expand_ref[...], sel_ref[pl.ds(b0, blocks_per_chunk), :])
        ok = jnp.logical_and(picked > 0.5, srow + k0 <= tpos)
        bias = _tile_lanes(jnp.where(ok, 0.0, NEG), NSA_HPG)
        s = _dot(ks_ref[0, 0, pl.ds(k0, KEY_CHUNK), :], qr3) + bias
        return _softmax_step(carry, s, _with_ones(vs_ref[0, 0, c]))

    o_slc = _softmax_finish(lax.fori_loop(0, n_chunks, attend_sel, _softmax_init(nq)))

    wc = jnp.maximum(i - WINDOW // Q_TILE, 0) * (Q_TILE // WIN_CHUNK)
    w0 = pl.multiple_of(wc * WIN_CHUNK, WIN_CHUNK)
    kpos = w0 + lax.broadcasted_iota(I32, (WIN_SPAN, Q_TILE), 0)
    wok = jnp.logical_and(kpos <= tpos, kpos > tpos - WINDOW)
    s = _dot(kw_ref[0, 0, pl.ds(w0, WIN_SPAN), :], qr3) + _tile_lanes(jnp.where(wok, 0.0, NEG), NSA_HPG)
    m = _reduce_rows(s, jnp.max)
    e = jnp.exp2(s - m).astype(BF16)
    acc = jnp.zeros((HEAD_DIM + ONES_ROWS, nq), F32)
    for j in range(WIN_SPAN // WIN_CHUNK):
        acc = acc + _dot(_with_ones(vw_ref[0, 0, wc + j]), e[j * WIN_CHUNK:(j + 1) * WIN_CHUNK, :])
    o_win = _softmax_finish((m, acc))

    gts = g_ref[0, 0]
    out_gates = (og0_ref, og1_ref, og2_ref)
    for j in range(NSA_HPG):
        sl = slice(j * Q_TILE, (j + 1) * Q_TILE)
        mixed = (gts[j:j + 1, :] * o_cmp[:, sl]
                 + gts[NSA_HPG + j:NSA_HPG + j + 1, :] * o_slc[:, sl]
                 + gts[2 * NSA_HPG + j:2 * NSA_HPG + j + 1, :] * o_win[:, sl])
        o_ref[0, j * HEAD_DIM:(j + 1) * HEAD_DIM, :] = (mixed * out_gates[j][0]).astype(BF16)


def _nsa(nq_t, nqr_t, kc, vc_t, ks, vs_t, kw, vw_t, gts, ovl, expand, gate_t):
    bsz, _, seq = nq_t.shape
    n_cmp = kc.shape[2]
    gw = NSA_HPG * HEAD_DIM
    per_group = lambda shape: pl.BlockSpec((1, 1) + shape, lambda b, g, i: (b, g) + (0,) * len(shape))
    return pl.pallas_call(
        _nsa_kernel,
        grid=(bsz, NSA_GROUPS, seq // Q_TILE),
        in_specs=[
            pl.BlockSpec((1, gw, Q_TILE), lambda b, g, i: (b, g, i)),
            pl.BlockSpec((1, gw, Q_TILE), lambda b, g, i: (b, g, i)),
            per_group((n_cmp, HEAD_DIM)),
            per_group((HEAD_DIM, n_cmp)),
            per_group((seq, HEAD_DIM)),
            per_group((N_KEY_CHUNKS, HEAD_DIM, KEY_CHUNK)),
            per_group((seq, HEAD_DIM)),
            per_group((N_WIN_CHUNKS, HEAD_DIM, WIN_CHUNK)),
            pl.BlockSpec((1, 1, 16, Q_TILE), lambda b, g, i: (b, g, 0, i)),
            pl.BlockSpec(ovl.shape, lambda b, g, i: (0, 0)),
            pl.BlockSpec(expand.shape, lambda b, g, i: (0, 0)),
        ] + [
            pl.BlockSpec((1, HEAD_DIM, Q_TILE),
                         lambda b, g, i, j=j: (b, DSA_HEADS + FOX_HEADS + g * NSA_HPG + j, i))
            for j in range(NSA_HPG)
        ],
        out_specs=pl.BlockSpec((1, gw, Q_TILE), lambda b, g, i: (b, g, i)),
        out_shape=jax.ShapeDtypeStruct((bsz, NSA_W, seq), BF16),
        scratch_shapes=[pltpu.VMEM((N_SEL_BLOCKS, Q_TILE), F32)],
        compiler_params=_cparams(("arbitrary", "arbitrary", "arbitrary")),
        name="nsa_attn",
    )(nq_t, nqr_t, kc, vc_t, ks, vs_t, kw, vw_t, gts, ovl, expand, gate_t, gate_t, gate_t)


def _out_kernel(od_ref, of_ref, on_ref, w_ref, x_ref, gm_ref, lg_ref, lb_ref, o_ref):
    z = jnp.concatenate([od_ref[0], of_ref[0], on_ref[0]], axis=0)
    y = lax.dot_general(z, w_ref[...], (((0,), (0,)), ((), ())),
                        preferred_element_type=F32)
    r = ALPHA * x_ref[0] + (1.0 + gm_ref[0]) * y
    mu = jnp.mean(r, axis=-1, keepdims=True)
    rc = r - mu
    var = jnp.mean(rc * rc, axis=-1, keepdims=True)
    o_ref[0] = rc * lax.rsqrt(var + LN_EPS) * lg_ref[...] + lb_ref[...]


def _output(od_t, of_t, on_t, w_out, x, gmod, ln_g, ln_b):
    bsz, seq, d = x.shape
    tm = PROJ_TOKENS
    feat = lambda rows: pl.BlockSpec((1, rows, tm), lambda b, i: (b, 0, i))
    return pl.pallas_call(
        _out_kernel,
        grid=(bsz, seq // tm),
        in_specs=[
            feat(DSA_W), feat(FOX_W), feat(NSA_W),
            pl.BlockSpec((MIX_W, d), lambda b, i: (0, 0)),
            pl.BlockSpec((1, tm, d), lambda b, i: (b, i, 0)),
            pl.BlockSpec((1, 1, d), lambda b, i: (b, 0, 0)),
            pl.BlockSpec((1, d), lambda b, i: (0, 0)),
            pl.BlockSpec((1, d), lambda b, i: (0, 0)),
        ],
        out_specs=pl.BlockSpec((1, tm, d), lambda b, i: (b, i, 0)),
        out_shape=jax.ShapeDtypeStruct((bsz, seq, d), F32),
        compiler_params=_cparams(("arbitrary", "arbitrary")),
        name="out_proj_ln",
    )(od_t, of_t, on_t, w_out, x, gmod, ln_g, ln_b)


def _split_cols(w):
    out, off = {}, 0
    for name, width in IN_SPLITS:
        out[name] = w[:, off:off + width]
        off += width
    return out


def _prep_w_in(w_in):
    p = _split_cols(w_in)
    d = w_in.shape[0]
    z = lambda n: jnp.zeros((d, n), w_in.dtype)
    gcols = p["nsa_g"].reshape(d, 3, NSA_GROUPS, NSA_HPG)
    gparts = []
    for g in range(NSA_GROUPS):
        gparts += [gcols[:, :, g, :].reshape(d, 3 * NSA_HPG), z(16 - 3 * NSA_HPG)]
    wt = jnp.concatenate(
        [p["dsa_q"], p["idx_q"], p["fox_q"], p["nsa_q"],
         p["dsa_v"], z(HEAD_DIM), p["fox_v"], p["nsa_vs"], p["nsa_vw"], p["gate"],
         p["idx_w"], p["fox_f"], z(8 - FOX_HEADS)] + gparts, axis=1)
    assert wt.shape[1] == T_ROWS
    ws = jnp.concatenate(
        [p["dsa_k"], z(LANES - HEAD_DIM), p["idx_k"], z(LANES - IDX_DIM), p["fox_k"],
         p["nsa_ks"], p["nsa_kw"], p["nsa_kc"], p["nsa_vc"]], axis=1)
    assert ws.shape[1] == S_COLS
    return wt.T.astype(BF16), ws.astype(BF16)


def _rope_tables(seq):
    pos = jnp.arange(seq, dtype=F32)
    feature_major, token_major = [], []
    lane = jnp.arange(LANES)
    for hd in (HEAD_DIM, IDX_DIM):
        half = hd // 2
        inv = ROPE_THETA ** (-jnp.arange(half, dtype=F32) / half)
        ang = inv[:, None] * pos[None, :]
        cos, sin = jnp.cos(ang), jnp.sin(ang)
        feature_major += [cos, sin]
        sign = jnp.where((lane % hd) < half, -1.0, 1.0).astype(F32)
        reps = LANES // half
        token_major += [jnp.tile(cos.T, (1, reps)), jnp.tile(sin.T, (1, reps)) * sign[None, :]]
    return tuple(feature_major + token_major)


def _prep_compress(cmp_pe, cmp_w1, cmp_w2):
    eye = jnp.eye(NSA_GROUPS, dtype=F32)
    w1 = cmp_w1.reshape(2, CMP_LEN, HEAD_DIM, HEAD_DIM)
    big1 = jnp.einsum("kjde,gh->kjgdhe", w1, eye).reshape(2, CMP_LEN, LANES, LANES)
    w1lo = big1[:, :CMP_STRIDE].reshape(2, CMP_STRIDE * LANES, LANES).astype(BF16)
    w1hi = big1[:, CMP_STRIDE:].reshape(2, CMP_STRIDE * LANES, LANES).astype(BF16)
    w2 = jnp.einsum("kde,gh->kgdhe", cmp_w2, eye).reshape(2, LANES, LANES).astype(BF16)
    pe2 = jnp.concatenate([cmp_pe] * NSA_GROUPS, axis=-1)
    return pe2[:, :CMP_STRIDE], pe2[:, CMP_STRIDE:], w1lo, w1hi, w2


def _selection_constants(n_cmp):
    cstart = jnp.arange(n_cmp) * CMP_STRIDE
    bstart = jnp.arange(N_SEL_BLOCKS) * SEL_BLOCK
    real = (jnp.arange(n_cmp) < (SEQ - CMP_LEN) // CMP_STRIDE + 1)[:, None]
    ovl = ((cstart[:, None] < bstart[None, :] + SEL_BLOCK) & (cstart[:, None] + CMP_LEN > bstart[None, :]) & real)
    ovl_t = jnp.concatenate([ovl.T.astype(BF16)] * NSA_HPG, axis=1)
    expand = (jnp.arange(KEY_CHUNK)[:, None] // SEL_BLOCK == jnp.arange(KEY_CHUNK // SEL_BLOCK)[None, :]).astype(F32)
    return ovl_t, expand


def kernel(x, c, w_ada, b_ada, w_in, b_f, cmp_pe, cmp_w1, cmp_w2, w_out, ln_g, ln_b):
    bsz, seq, d = x.shape
    assert (seq, d) == (SEQ, D_MODEL)
    mod = _modulation(c, w_ada, b_ada)
    tabs = _rope_tables(seq)
    ovl, expand = _selection_constants(seq // CMP_STRIDE)
    for l in range(DEPTH):
        shift = mod[l, :, :d].reshape(bsz, 1, d)
        scale = mod[l, :, d:2 * d].reshape(bsz, 1, d)
        gmod = mod[l, :, 2 * d:].reshape(bsz, 1, d)
        wt, ws = _prep_w_in(w_in[l])
        bf8 = jnp.concatenate([b_f[l], jnp.zeros((8 - FOX_HEADS,), F32)])
        (dq, iq, fq, nq, nqr, dv, fv, vs, vw, gate, iw, lft, gts,
         dk, ik, fk, ks, kw, kcin, vcin) = _projection(
            x, scale, shift, wt, ws, tabs, bf8.reshape(8, 1))
        cum_t, cum_s = _forget_cumsum(lft)
        kc, vc_t = _compress(kcin, vcin, *_prep_compress(cmp_pe[l], cmp_w1[l], cmp_w2[l]))
        o_dsa = _dsa(iq, ik, iw, dq, dk, dv, gate)
        o_fox = _fox(fq, fk, fv, cum_t, cum_s, gate)
        o_nsa = _nsa(nq, nqr, kc, vc_t, ks, vs, kw, vw, gts, ovl, expand, gate)
        x = _output(o_dsa, o_fox, o_nsa, w_out[l].astype(BF16), x, gmod,
                    ln_g[l].reshape(1, d), ln_b[l].reshape(1, d))
    return x
```

```python
import functools

import jax
import jax.numpy as jnp
from jax import lax
from jax.experimental import pallas as pl
from jax.experimental.pallas import tpu as pltpu

F32 = jnp.float32
BF16 = jnp.bfloat16
I32 = jnp.int32

D_MODEL = 1024
SEQ = 2048
DEPTH = 2
HEAD_DIM = 64
DSA_HEADS = 4
DSA_TOPK = 256
IDX_HEADS = 8
IDX_DIM = 32
FOX_HEADS = 6
NSA_HEADS = 6
NSA_GROUPS = 2
NSA_HPG = NSA_HEADS // NSA_GROUPS
CMP_LEN = 32
CMP_STRIDE = 16
SEL_BLOCK = 64
SEL_N = 16
N_SEL_BLOCKS = SEQ // SEL_BLOCK
WINDOW = 512
ROPE_THETA = 10000.0
LN_EPS = 1e-5
ALPHA = (2.0 * DEPTH) ** 0.25

DSA_W = DSA_HEADS * HEAD_DIM
FOX_W = FOX_HEADS * HEAD_DIM
NSA_W = NSA_HEADS * HEAD_DIM
NSA_KV_W = NSA_GROUPS * HEAD_DIM
MIX_W = DSA_W + FOX_W + NSA_W
IDX_W = IDX_HEADS * IDX_DIM

IN_SPLITS = (
    ("dsa_q", DSA_W), ("dsa_k", HEAD_DIM), ("dsa_v", HEAD_DIM),
    ("idx_q", IDX_W), ("idx_k", IDX_DIM), ("idx_w", IDX_HEADS),
    ("fox_q", FOX_W), ("fox_k", FOX_W), ("fox_v", FOX_W), ("fox_f", FOX_HEADS),
    ("nsa_q", NSA_W),
    ("nsa_kc", NSA_KV_W), ("nsa_vc", NSA_KV_W),
    ("nsa_ks", NSA_KV_W), ("nsa_vs", NSA_KV_W),
    ("nsa_kw", NSA_KV_W), ("nsa_vw", NSA_KV_W),
    ("nsa_g", 3 * NSA_HEADS),
    ("gate", MIX_W),
)

LANES = 128
KEY_CHUNK = 512
PROJ_TOKENS = 512
Q_TILE = 256
DSA_Q_TILE = 512
FOX_Q_TILE = 512
N_KEY_CHUNKS = SEQ // KEY_CHUNK
WIN_CHUNK = 256
N_WIN_CHUNKS = SEQ // WIN_CHUNK
WIN_SPAN = WINDOW + Q_TILE
CUM_CHUNK = 256
VMEM_LIMIT = 56 * 1024 * 1024

FOLD_ROWS = 32
ONES_ROWS = 16
LOG2E = 1.4426950408889634
NEG = -(2.0 ** 100)
INT_MIN = -(2 ** 31)
CODE_NEG_INF = 0x007FFFFF

T_DQ, T_IQ, T_FQ, T_NQ = 0, 256, 512, 896
T_DV, T_FV, T_VS, T_VW = 1280, 1408, 1792, 1920
T_GATE, T_SMALL, T_ROWS = 2048, 3072, 3120
S_DK, S_IK, S_FK, S_KS, S_KW, S_KC, S_VC, S_COLS = 0, 128, 256, 640, 768, 896, 1024, 1152


def _cparams(sem):
    return pltpu.CompilerParams(dimension_semantics=sem, vmem_limit_bytes=VMEM_LIMIT)


def _dot(a, b):
    return jnp.dot(a, b, preferred_element_type=F32)


def _dot_nt(a, b):
    return lax.dot_general(a, b, (((1,), (1,)), ((), ())), preferred_element_type=F32)


def _log_sigmoid(x):
    return jnp.minimum(x, 0.0) - jnp.log(1.0 + jnp.exp(-jnp.abs(x)))


def _sigmoid(x):
    return 1.0 / (1.0 + jnp.exp(-x))


def _mod_kernel(c_ref, w_ref, b_ref, o_ref):
    o_ref[0] = _dot(c_ref[...].astype(BF16), w_ref[0].astype(BF16)) + b_ref[0]


def _modulation(c, w_ada, b_ada):
    depth, d, d3 = w_ada.shape
    bsz = c.shape[0]
    return pl.pallas_call(
        _mod_kernel,
        grid=(depth, d3 // d),
        in_specs=[
            pl.BlockSpec((bsz, d), lambda l, j: (0, 0)),
            pl.BlockSpec((1, d, d), lambda l, j: (l, 0, j)),
            pl.BlockSpec((1, 1, d), lambda l, j: (l, 0, j)),
        ],
        out_specs=pl.BlockSpec((1, bsz, d), lambda l, j: (l, 0, j)),
        out_shape=jax.ShapeDtypeStruct((depth, bsz, d3), F32),
        compiler_params=_cparams(("arbitrary", "arbitrary")),
        name="adaln_mod",
    )(c, w_ada, b_ada.reshape(depth, 1, d3))


def _proj_kernel(x_ref, sc_ref, sh_ref, wt_ref, ws_ref,
                 ct64_ref, st64_ref, ct32_ref, st32_ref,
                 ck64_ref, sk64_ref, ck32_ref, sk32_ref, bfc_ref,
                 dq_ref, iq_ref, fq_ref, nq_ref, nqr_ref,
                 dv_ref, fv_ref, vs_ref, vw_ref, gate_ref,
                 iw_ref, lft_ref, gts_ref,
                 dk_ref, ik_ref, fk_ref, ks_ref, kw_ref, kcin_ref, vcin_ref):
    tm = x_ref.shape[1]
    u = (x_ref[0] * (1.0 + sc_ref[0]) + sh_ref[0]).astype(BF16)

    ht_all = _dot_nt(wt_ref[...], u)

    def proj_t(r0, r1):
        return ht_all[r0:r1]

    def rope_t(h, n_heads, hd, c, s):
        half = hd // 2
        out = []
        for hh in range(n_heads):
            x1 = h[hh * hd:hh * hd + half]
            x2 = h[hh * hd + half:(hh + 1) * hd]
            out.append((hh * hd, x1 * c - x2 * s))
            out.append((hh * hd + half, x1 * s + x2 * c))
        return out

    c64, s64 = ct64_ref[...], st64_ref[...]
    c32, s32 = ct32_ref[...], st32_ref[...]
    qscale = HEAD_DIM ** -0.5 * LOG2E

    h = proj_t(T_DQ, T_DQ + DSA_W)
    for r, v in rope_t(h, DSA_HEADS, HEAD_DIM, c64, s64):
        dq_ref[0, r:r + HEAD_DIM // 2, :] = (v * qscale).astype(BF16)
    h = proj_t(T_IQ, T_IQ + IDX_W)
    for r, v in rope_t(h, IDX_HEADS, IDX_DIM, c32, s32):
        iq_ref[0, r:r + IDX_DIM // 2, :] = v.astype(BF16)
    fq_ref[0] = (proj_t(T_FQ, T_FQ + FOX_W) * qscale).astype(BF16)
    h = proj_t(T_NQ, T_NQ + NSA_W)
    nq_ref[0] = (h * qscale).astype(BF16)
    for r, v in rope_t(h, NSA_HEADS, HEAD_DIM, c64, s64):
        nqr_ref[0, r:r + HEAD_DIM // 2, :] = (v * qscale).astype(BF16)

    dv_ref[0, 0] = proj_t(T_DV, T_DV + HEAD_DIM).astype(BF16)
    h = proj_t(T_FV, T_FV + FOX_W)
    for hh in range(FOX_HEADS):
        fv_ref[0, hh, 0] = h[hh * HEAD_DIM:(hh + 1) * HEAD_DIM].astype(BF16)
    h = proj_t(T_VS, T_VS + NSA_KV_W)
    for g in range(NSA_GROUPS):
        vs_ref[0, g, 0] = h[g * HEAD_DIM:(g + 1) * HEAD_DIM].astype(BF16)
    h = proj_t(T_VW, T_VW + NSA_KV_W)
    for g in range(NSA_GROUPS):
        for j in range(tm // WIN_CHUNK):
            vw_ref[0, g, j] = h[g * HEAD_DIM:(g + 1) * HEAD_DIM, j * WIN_CHUNK:(j + 1) * WIN_CHUNK].astype(BF16)

    for r0 in range(0, MIX_W, 256):
        h = proj_t(T_GATE + r0, T_GATE + r0 + 256)
        gate_ref[0, r0:r0 + 256, :] = h * _sigmoid(h)

    h = proj_t(T_SMALL, T_ROWS)
    iw_ref[0] = h[0:8] * (IDX_HEADS ** -0.5)
    lft_ref[0] = _log_sigmoid(h[8:16] + bfc_ref[...])
    g_all = _sigmoid(h[16:48])
    gts_ref[0, 0] = g_all[0:16]
    gts_ref[0, 1] = g_all[16:32]

    lane = lax.broadcasted_iota(I32, (tm, LANES), 1)

    def rope_s(g, half, c, s_signed):
        first = (lane & (2 * half - 1)) < half
        sw = jnp.where(first, pltpu.roll(g, LANES - half, 1), pltpu.roll(g, half, 1))
        return g * c + sw * s_signed

    hs_all = _dot(u, ws_ref[...])

    def proj_s(c0, c1):
        return hs_all[:, c0:c1]

    ck64, sk64 = ck64_ref[...], sk64_ref[...]
    g = rope_s(proj_s(S_DK, S_DK + LANES), HEAD_DIM // 2, ck64, sk64)
    dk_ref[0] = g[:, :HEAD_DIM].astype(BF16)
    g = rope_s(proj_s(S_IK, S_IK + LANES), IDX_DIM // 2, ck32_ref[...], sk32_ref[...])
    ik_ref[0] = g[:, :IDX_DIM].astype(BF16)
    g = proj_s(S_FK, S_FK + FOX_W)
    for hh in range(FOX_HEADS):
        fk_ref[0, hh] = g[:, hh * HEAD_DIM:(hh + 1) * HEAD_DIM].astype(BF16)
    g = rope_s(proj_s(S_KS, S_KS + LANES), HEAD_DIM // 2, ck64, sk64)
    for gg in range(NSA_GROUPS):
        ks_ref[0, gg] = g[:, gg * HEAD_DIM:(gg + 1) * HEAD_DIM].astype(BF16)
    g = rope_s(proj_s(S_KW, S_KW + LANES), HEAD_DIM // 2, ck64, sk64)
    for gg in range(NSA_GROUPS):
        kw_ref[0, gg] = g[:, gg * HEAD_DIM:(gg + 1) * HEAD_DIM].astype(BF16)
    kcin_ref[0] = proj_s(S_KC, S_KC + LANES)
    vcin_ref[0] = proj_s(S_VC, S_VC + LANES)


def _projection(x, scale, shift, wt, ws, tabs, bfc):
    bsz, seq, d = x.shape
    tm = PROJ_TOKENS
    nt = seq // tm
    ct64, st64, ct32, st32, ck64, sk64, ck32, sk32 = tabs
    const = lambda shape: pl.BlockSpec(shape, lambda b, i: (0,) * len(shape))
    feat = lambda rows: pl.BlockSpec((1, rows, tm), lambda b, i: (b, 0, i))
    in_specs = [
        pl.BlockSpec((1, tm, d), lambda b, i: (b, i, 0)),
        pl.BlockSpec((1, 1, d), lambda b, i: (b, 0, 0)),
        pl.BlockSpec((1, 1, d), lambda b, i: (b, 0, 0)),
        const((T_ROWS, d)),
        const((d, S_COLS)),
        pl.BlockSpec((HEAD_DIM // 2, tm), lambda b, i: (0, i)),
        pl.BlockSpec((HEAD_DIM // 2, tm), lambda b, i: (0, i)),
        pl.BlockSpec((IDX_DIM // 2, tm), lambda b, i: (0, i)),
        pl.BlockSpec((IDX_DIM // 2, tm), lambda b, i: (0, i)),
        pl.BlockSpec((tm, LANES), lambda b, i: (i, 0)),
        pl.BlockSpec((tm, LANES), lambda b, i: (i, 0)),
        pl.BlockSpec((tm, LANES), lambda b, i: (i, 0)),
        pl.BlockSpec((tm, LANES), lambda b, i: (i, 0)),
        const((8, 1)),
    ]
    sds = jax.ShapeDtypeStruct
    out_shape = [
        sds((bsz, DSA_W, seq), BF16), sds((bsz, IDX_W, seq), BF16), sds((bsz, FOX_W, seq), BF16),
        sds((bsz, NSA_W, seq), BF16), sds((bsz, NSA_W, seq), BF16),
        sds((bsz, N_KEY_CHUNKS, HEAD_DIM, KEY_CHUNK), BF16),
        sds((bsz, FOX_HEADS, N_KEY_CHUNKS, HEAD_DIM, KEY_CHUNK), BF16),
        sds((bsz, NSA_GROUPS, N_KEY_CHUNKS, HEAD_DIM, KEY_CHUNK), BF16),
        sds((bsz, NSA_GROUPS, N_WIN_CHUNKS, HEAD_DIM, WIN_CHUNK), BF16),
        sds((bsz, MIX_W, seq), F32),
        sds((bsz, 8, seq), F32), sds((bsz, 8, seq), F32), sds((bsz, NSA_GROUPS, 16, seq), F32),
        sds((bsz, seq, HEAD_DIM), BF16), sds((bsz, seq, IDX_DIM), BF16),
        sds((bsz, FOX_HEADS, seq, HEAD_DIM), BF16),
        sds((bsz, NSA_GROUPS, seq, HEAD_DIM), BF16), sds((bsz, NSA_GROUPS, seq, HEAD_DIM), BF16),
        sds((bsz, seq, LANES), F32), sds((bsz, seq, LANES), F32),
    ]
    wpc = tm // WIN_CHUNK
    out_specs = [
        feat(DSA_W), feat(IDX_W), feat(FOX_W), feat(NSA_W), feat(NSA_W),
        pl.BlockSpec((1, 1, HEAD_DIM, KEY_CHUNK), lambda b, i: (b, i, 0, 0)),
        pl.BlockSpec((1, FOX_HEADS, 1, HEAD_DIM, KEY_CHUNK), lambda b, i: (b, 0, i, 0, 0)),
        pl.BlockSpec((1, NSA_GROUPS, 1, HEAD_DIM, KEY_CHUNK), lambda b, i: (b, 0, i, 0, 0)),
        pl.BlockSpec((1, NSA_GROUPS, wpc, HEAD_DIM, WIN_CHUNK), lambda b, i: (b, 0, i, 0, 0)),
        feat(MIX_W),
        feat(8), feat(8),
        pl.BlockSpec((1, NSA_GROUPS, 16, tm), lambda b, i: (b, 0, 0, i)),
        pl.BlockSpec((1, tm, HEAD_DIM), lambda b, i: (b, i, 0)),
        pl.BlockSpec((1, tm, IDX_DIM), lambda b, i: (b, i, 0)),
        pl.BlockSpec((1, FOX_HEADS, tm, HEAD_DIM), lambda b, i: (b, 0, i, 0)),
        pl.BlockSpec((1, NSA_GROUPS, tm, HEAD_DIM), lambda b, i: (b, 0, i, 0)),
        pl.BlockSpec((1, NSA_GROUPS, tm, HEAD_DIM), lambda b, i: (b, 0, i, 0)),
        pl.BlockSpec((1, tm, LANES), lambda b, i: (b, i, 0)),
        pl.BlockSpec((1, tm, LANES), lambda b, i: (b, i, 0)),
    ]
    assert tm == KEY_CHUNK
    return pl.pallas_call(
        _proj_kernel,
        grid=(bsz, nt),
        in_specs=in_specs,
        out_specs=out_specs,
        out_shape=out_shape,
        compiler_params=_cparams(("arbitrary", "arbitrary")),
        name="in_proj",
    )(x, scale, shift, wt, ws, ct64, st64, ct32, st32, ck64, sk64, ck32, sk32, bfc)


def _cumsum_kernel(lft_ref, cumt_ref, cums_ref):
    seq = lft_ref.shape[2]
    r = lax.broadcasted_iota(I32, (CUM_CHUNK, CUM_CHUNK), 0)
    c = lax.broadcasted_iota(I32, (CUM_CHUNK, CUM_CHUNK), 1)
    tri_u = (r <= c).astype(F32)
    blocks = [slice(k * CUM_CHUNK, (k + 1) * CUM_CHUNK) for k in range(seq // CUM_CHUNK)]
    local = [jnp.dot(lft_ref[0, :, sl], tri_u, preferred_element_type=F32, precision=lax.Precision.HIGHEST)
             for sl in blocks]
    carry = jnp.zeros((8, 1), F32)
    pad = jnp.zeros((LANES - 8, CUM_CHUNK), F32)
    for k, sl in enumerate(blocks):
        ct = (local[k] + carry) * LOG2E
        carry = local[k][:, CUM_CHUNK - 1:CUM_CHUNK] + carry
        for head in range(8):
            cumt_ref[0, head, :, sl] = ct[head:head + 1, :]
        cums_ref[0, sl, :] = jnp.concatenate([ct, pad], axis=0).T[:, :8]


def _forget_cumsum(lft):
    bsz, _, seq = lft.shape
    return pl.pallas_call(
        _cumsum_kernel,
        grid=(bsz,),
        in_specs=[pl.BlockSpec((1, 8, seq), lambda b: (b, 0, 0))],
        out_specs=[pl.BlockSpec((1, 8, 1, seq), lambda b: (b, 0, 0, 0)),
                   pl.BlockSpec((1, seq, 8), lambda b: (b, 0, 0))],
        out_shape=[jax.ShapeDtypeStruct((bsz, 8, 1, seq), F32),
                   jax.ShapeDtypeStruct((bsz, seq, 8), F32)],
        compiler_params=_cparams(("arbitrary",)),
        name="forget_cumsum",
    )(lft)


def _compress_kernel(kcin_ref, vcin_ref, pelo_ref, pehi_ref, w1lo_ref, w1hi_ref, w2_ref,
                     kc_ref, vct_ref):
    n_blk = kcin_ref.shape[1] // CMP_STRIDE
    for kv, src in enumerate((kcin_ref, vcin_ref)):
        lo, hi = [], []
        for j in range(CMP_STRIDE):
            piece = src[0, pl.ds(j, n_blk, stride=CMP_STRIDE), :]
            lo.append((piece + pelo_ref[kv, j:j + 1, :]).astype(BF16))
            hi.append((piece + pehi_ref[kv, j:j + 1, :]).astype(BF16))
        a = _dot(jnp.concatenate(lo, axis=1), w1lo_ref[kv])
        b = _dot(jnp.concatenate(hi, axis=1), w1hi_ref[kv])
        pre = a + pltpu.roll(b, n_blk - 1, 0)
        act = pre * _sigmoid(pre)
        out = _dot(act.astype(BF16), w2_ref[kv])
        if kv == 0:
            for g in range(NSA_GROUPS):
                kc_ref[0, g] = out[:, g * HEAD_DIM:(g + 1) * HEAD_DIM].astype(BF16)
        else:
            out_t = out.T
            for g in range(NSA_GROUPS):
                vct_ref[0, g] = out_t[g * HEAD_DIM:(g + 1) * HEAD_DIM].astype(BF16)


def _compress(kcin, vcin, pelo, pehi, w1lo, w1hi, w2):
    bsz, seq, _ = kcin.shape
    n_blk = seq // CMP_STRIDE
    full = lambda a: pl.BlockSpec(a.shape, lambda b: (0,) * a.ndim)
    return pl.pallas_call(
        _compress_kernel,
        grid=(bsz,),
        in_specs=[pl.BlockSpec((1, seq, LANES), lambda b: (b, 0, 0)),
                  pl.BlockSpec((1, seq, LANES), lambda b: (b, 0, 0)),
                  full(pelo), full(pehi), full(w1lo), full(w1hi), full(w2)],
        out_specs=[pl.BlockSpec((1, NSA_GROUPS, n_blk, HEAD_DIM), lambda b: (b, 0, 0, 0)),
                   pl.BlockSpec((1, NSA_GROUPS, HEAD_DIM, n_blk), lambda b: (b, 0, 0, 0))],
        out_shape=[jax.ShapeDtypeStruct((bsz, NSA_GROUPS, n_blk, HEAD_DIM), BF16),
                   jax.ShapeDtypeStruct((bsz, NSA_GROUPS, HEAD_DIM, n_blk), BF16)],
        compiler_params=_cparams(("arbitrary",)),
        name="nsa_compress",
    )(kcin, vcin, pelo, pehi, w1lo, w1hi, w2)


def _fold_rows(x, op):
    k, n = x.shape
    return op(x.reshape(k // FOLD_ROWS, FOLD_ROWS, n), axis=0)


def _reduce_rows(x, op):
    if x.shape[0] % FOLD_ROWS == 0 and x.shape[0] > FOLD_ROWS:
        x = _fold_rows(x, op)
    return op(x, axis=0, keepdims=True)


def _with_ones(v_t):
    return jnp.concatenate([v_t, jnp.ones((ONES_ROWS, v_t.shape[1]), v_t.dtype)], axis=0)


def _softmax_step(carry, s, v_aug):
    m, acc = carry
    m_new = jnp.maximum(m, _reduce_rows(s, jnp.max))
    alpha = jnp.exp2(m - m_new)
    p = jnp.exp2(s - m_new)
    acc = alpha * acc + _dot(v_aug, p.astype(BF16))
    return m_new, acc


def _softmax_init(n):
    return (jnp.full((1, n), NEG, F32), jnp.zeros((HEAD_DIM + ONES_ROWS, n), F32))


def _softmax_finish(carry):
    _, acc = carry
    return acc[:HEAD_DIM] * (1.0 / acc[HEAD_DIM:HEAD_DIM + 1])


def _tile_lanes(a, n):
    return jnp.concatenate([a] * n, axis=1)


def _dsa_kernel(iq_ref, ik_ref, iw_ref, dq_ref, dk_ref, dv_ref, g_ref, o_ref, key_ref):
    i = pl.program_id(1)
    tq = DSA_Q_TILE
    n_chunks = i + 1
    tpos = i * tq + lax.broadcasted_iota(I32, (1, tq), 1)
    srow = lax.broadcasted_iota(I32, (KEY_CHUNK, tq), 0)
    iw = iw_ref[0]

    def index_chunk(c, _):
        k0 = pl.multiple_of(c * KEY_CHUNK, KEY_CHUNK)
        ikc = ik_ref[0, pl.ds(k0, KEY_CHUNK), :]
        acc = jnp.zeros((KEY_CHUNK, tq), F32)
        for h in range(IDX_HEADS):
            x = _dot(ikc, iq_ref[0, h * IDX_DIM:(h + 1) * IDX_DIM, :])
            acc = acc + iw[h:h + 1, :] * jnp.maximum(x, 0.0)
        key_ref[pl.ds(k0, KEY_CHUNK), :] = jnp.where(srow + k0 <= tpos, acc, -jnp.inf)
        return 0

    lax.fori_loop(0, n_chunks, index_chunk, 0)

    def count(pred):
        def body(c, acc):
            k0 = pl.multiple_of(c * KEY_CHUNK, KEY_CHUNK)
            hit = pred(key_ref[pl.ds(k0, KEY_CHUNK), :], srow + k0)
            return acc + _fold_rows(jnp.where(hit, 1, 0), jnp.sum)
        acc = lax.fori_loop(0, n_chunks, body, jnp.zeros((FOLD_ROWS, tq), I32))
        return jnp.sum(acc, axis=0, keepdims=True)

    def decode(code):
        skey = code ^ INT_MIN
        val = lax.bitcast_convert_type(jnp.where(skey < 0, skey ^ 0x7FFFFFFF, skey), F32)
        return jnp.where(jnp.logical_and(code >= 0, code <= CODE_NEG_INF), -jnp.inf, val)

    def value_bit(it, code):
        cand = code | lax.shift_left(jnp.int32(1), 31 - it)
        cand_f = decode(cand)
        cnt = count(lambda key, spos: key >= cand_f)
        return jnp.where(cnt >= DSA_TOPK, cand, code)

    thr = decode(lax.fori_loop(0, 32, value_bit, jnp.zeros((1, tq), I32)))

    need = (DSA_TOPK - count(lambda key, spos: key > thr)).astype(F32)
    r = lax.broadcasted_iota(I32, (KEY_CHUNK, KEY_CHUNK), 0)
    c = lax.broadcasted_iota(I32, (KEY_CHUNK, KEY_CHUNK), 1)
    prefix = jnp.where(c <= r, 1.0, 0.0).astype(BF16)

    def attend(c, carry):
        heads, ties_before = carry
        k0 = pl.multiple_of(c * KEY_CHUNK, KEY_CHUNK)
        key = key_ref[pl.ds(k0, KEY_CHUNK), :]
        tied = key == thr
        rank = _dot(prefix, jnp.where(tied, 1.0, 0.0).astype(BF16)) + ties_before
        keep = jnp.logical_or(key > thr, jnp.logical_and(tied, rank <= need))
        bias = jnp.where(jnp.logical_and(keep, srow + k0 <= tpos), 0.0, NEG)
        kc = dk_ref[0, pl.ds(k0, KEY_CHUNK), :]
        v_aug = _with_ones(dv_ref[0, c])
        heads = tuple(
            _softmax_step(heads[h], _dot(kc, dq_ref[0, h * HEAD_DIM:(h + 1) * HEAD_DIM, :]) + bias, v_aug)
            for h in range(DSA_HEADS))
        return heads, rank[KEY_CHUNK - 1:KEY_CHUNK, :]

    init = tuple(_softmax_init(tq) for _ in range(DSA_HEADS))
    heads, _ = lax.fori_loop(0, n_chunks, attend, (init, jnp.zeros((1, tq), F32)))
    for h, carry in enumerate(heads):
        rows = slice(h * HEAD_DIM, (h + 1) * HEAD_DIM)
        o_ref[0, rows, :] = (_softmax_finish(carry) * g_ref[0, rows, :]).astype(BF16)


def _dsa(iq_t, ik, iw_t, dq_t, dk, dv_t, gate_t):
    bsz, _, seq = dq_t.shape
    tq = DSA_Q_TILE
    assert tq == KEY_CHUNK
    return pl.pallas_call(
        _dsa_kernel,
        grid=(bsz, seq // tq),
        in_specs=[
            pl.BlockSpec((1, IDX_W, tq), lambda b, i: (b, 0, i)),
            pl.BlockSpec((1, seq, IDX_DIM), lambda b, i: (b, 0, 0)),
            pl.BlockSpec((1, 8, tq), lambda b, i: (b, 0, i)),
            pl.BlockSpec((1, DSA_W, tq), lambda b, i: (b, 0, i)),
            pl.BlockSpec((1, seq, HEAD_DIM), lambda b, i: (b, 0, 0)),
            pl.BlockSpec((1, N_KEY_CHUNKS, HEAD_DIM, KEY_CHUNK), lambda b, i: (b, 0, 0, 0)),
            pl.BlockSpec((1, DSA_W, tq), lambda b, i: (b, 0, i)),
        ],
        out_specs=pl.BlockSpec((1, DSA_W, tq), lambda b, i: (b, 0, i)),
        out_shape=jax.ShapeDtypeStruct((bsz, DSA_W, seq), BF16),
        scratch_shapes=[pltpu.VMEM((seq, tq), F32)],
        compiler_params=_cparams(("arbitrary", "arbitrary")),
        name="dsa_attn",
    )(iq_t, ik, iw_t, dq_t, dk, dv_t, gate_t)


def _fox_kernel(q_ref, k_ref, v_ref, cumt_ref, cums_ref, g_ref, o_ref):
    h = pl.program_id(1)
    tq = FOX_Q_TILE
    n_tiles = q_ref.shape[2] // tq
    head_lane = lax.broadcasted_iota(I32, (KEY_CHUNK, 8), 1) == h
    srow = lax.broadcasted_iota(I32, (KEY_CHUNK, tq), 0)
    tcol = lax.broadcasted_iota(I32, (KEY_CHUNK, tq), 1)
    ccols = []
    for c in range(n_tiles):
        col = jnp.sum(jnp.where(head_lane, cums_ref[0, c * KEY_CHUNK:(c + 1) * KEY_CHUNK, :], 0.0),
                      axis=1, keepdims=True)
        ccols.append(_tile_lanes(jnp.broadcast_to(col, (KEY_CHUNK, LANES)), tq // LANES))
    v_aug = [_with_ones(v_ref[0, 0, c]) for c in range(n_tiles)]
    for i in range(n_tiles):
        q = q_ref[0, :, i * tq:(i + 1) * tq]
        crow = cumt_ref[0, 0, :, i * tq:(i + 1) * tq]
        carry = _softmax_init(tq)
        for c in range(i + 1):
            s = _dot(k_ref[0, 0, c * KEY_CHUNK:(c + 1) * KEY_CHUNK, :], q) + crow - ccols[c]
            if c == i:
                s = jnp.where(srow <= tcol, s, NEG)
            carry = _softmax_step(carry, s, v_aug[c])
        cols = slice(i * tq, (i + 1) * tq)
        o_ref[0, :, cols] = (_softmax_finish(carry) * g_ref[0, :, cols]).astype(BF16)


def _fox(fq_t, fk, fv_t, cum_t, cum_s, gate_t):
    bsz, _, seq = fq_t.shape
    assert FOX_Q_TILE == KEY_CHUNK
    return pl.pallas_call(
        _fox_kernel,
        grid=(bsz, FOX_HEADS),
        in_specs=[
            pl.BlockSpec((1, HEAD_DIM, seq), lambda b, h: (b, h, 0)),
            pl.BlockSpec((1, 1, seq, HEAD_DIM), lambda b, h: (b, h, 0, 0)),
            pl.BlockSpec((1, 1, N_KEY_CHUNKS, HEAD_DIM, KEY_CHUNK), lambda b, h: (b, h, 0, 0, 0)),
            pl.BlockSpec((1, 1, 1, seq), lambda b, h: (b, h, 0, 0)),
            pl.BlockSpec((1, seq, 8), lambda b, h: (b, 0, 0)),
            pl.BlockSpec((1, HEAD_DIM, seq), lambda b, h: (b, DSA_HEADS + h, 0)),
        ],
        out_specs=pl.BlockSpec((1, HEAD_DIM, seq), lambda b, h: (b, h, 0)),
        out_shape=jax.ShapeDtypeStruct((bsz, FOX_W, seq), BF16),
        compiler_params=_cparams(("arbitrary", "arbitrary")),
        name="fox_attn",
    )(fq_t, fk, fv_t, cum_t, cum_s, gate_t)


def _nsa_kernel(q_ref, qr_ref, kc_ref, vct_ref, ks_ref, vs_ref, kw_ref, vw_ref, g_ref,
                ovl_ref, expand_ref, og0_ref, og1_ref, og2_ref, o_ref, sel_ref):
    i = pl.program_id(2)
    nq = NSA_HPG * Q_TILE
    t0 = i * Q_TILE
    tpos = t0 + lax.broadcasted_iota(I32, (1, Q_TILE), 1)
    q3 = jnp.concatenate([q_ref[0, j * HEAD_DIM:(j + 1) * HEAD_DIM, :] for j in range(NSA_HPG)], axis=1)
    qr3 = jnp.concatenate([qr_ref[0, j * HEAD_DIM:(j + 1) * HEAD_DIM, :] for j in range(NSA_HPG)], axis=1)

    n_cmp = kc_ref.shape[2]
    cend = lax.broadcasted_iota(I32, (n_cmp, Q_TILE), 0) * CMP_STRIDE + (CMP_LEN - 1)
    cbias = _tile_lanes(jnp.where(cend <= tpos, 0.0, NEG), NSA_HPG)
    cvalid = _tile_lanes(jnp.where(cend <= tpos, 1.0, 0.0), NSA_HPG)
    s = _dot(kc_ref[0, 0], q3) + cbias
    m = _reduce_rows(s, jnp.max)
    e = jnp.exp2(s - m) * cvalid
    p_cmp = (e / jnp.maximum(_reduce_rows(e, jnp.sum), 1e-30)).astype(BF16)
    o_cmp = _dot(vct_ref[0, 0], p_cmp)

    p_stack = jnp.concatenate([p_cmp[:, j * Q_TILE:(j + 1) * Q_TILE] for j in range(NSA_HPG)], axis=0)
    score = _dot(ovl_ref[...], p_stack)
    blk = lax.broadcasted_iota(I32, (N_SEL_BLOCKS, Q_TILE), 0)
    cur = lax.shift_right_logical(tpos, 6)
    forced = jnp.logical_or(blk == 0, jnp.logical_or(blk == cur, blk == cur - 1))
    score = jnp.where(forced, jnp.inf, jnp.where(blk > cur, -jnp.inf, score))
    rank = jnp.zeros((N_SEL_BLOCKS, Q_TILE), F32)
    for mth in range(N_SEL_BLOCKS):
        row = score[mth:mth + 1, :]
        ahead = jnp.logical_or(row > score, jnp.logical_and(row == score, blk > mth))
        rank = rank + jnp.where(ahead, 1.0, 0.0)
    sel_ref[...] = jnp.where(rank < SEL_N, 1.0, 0.0)

    n_chunks = (i * Q_TILE + Q_TILE + KEY_CHUNK - 1) // KEY_CHUNK
    srow = lax.broadcasted_iota(I32, (KEY_CHUNK, Q_TILE), 0)
    blocks_per_chunk = KEY_CHUNK // SEL_BLOCK

    def attend_sel(c, carry):
        k0 = pl.multiple_of(c * KEY_CHUNK, KEY_CHUNK)
        b0 = pl.multiple_of(c * blocks_per_chunk, blocks_per_chunk)
        picked = _dot(expand_ref[...], sel_ref[pl.ds(b0, blocks_per_chunk), :])
        ok = jnp.logical_and(picked > 0.5, srow + k0 <= tpos)
        bias = _tile_lanes(jnp.where(ok, 0.0, NEG), NSA_HPG)
        s = _dot(ks_ref[0, 0, pl.ds(k0, KEY_CHUNK), :], qr3) + bias
        return _softmax_step(carry, s, _with_ones(vs_ref[0, 0, c]))

    wc = jnp.maximum(i - WINDOW // Q_TILE, 0) * (Q_TILE // WIN_CHUNK)
    w0 = pl.multiple_of(wc * WIN_CHUNK, WIN_CHUNK)
    kpos = w0 + lax.broadcasted_iota(I32, (WIN_SPAN, Q_TILE), 0)
    wok = jnp.logical_and(kpos <= tpos, kpos > tpos - WINDOW)
    s = _dot(kw_ref[0, 0, pl.ds(w0, WIN_SPAN), :], qr3) + _tile_lanes(jnp.where(wok, 0.0, NEG), NSA_HPG)
    m = _reduce_rows(s, jnp.max)
    e = jnp.exp2(s - m).astype(BF16)
    acc = jnp.zeros((HEAD_DIM + ONES_ROWS, nq), F32)
    for j in range(WIN_SPAN // WIN_CHUNK):
        acc = acc + _dot(_with_ones(vw_ref[0, 0, wc + j]), e[j * WIN_CHUNK:(j + 1) * WIN_CHUNK, :])
    o_win = _softmax_finish((m, acc))

    o_slc = _softmax_finish(lax.fori_loop(0, n_chunks, attend_sel, _softmax_init(nq)))

    gts = g_ref[0, 0]
    out_gates = (og0_ref, og1_ref, og2_ref)
    for j in range(NSA_HPG):
        sl = slice(j * Q_TILE, (j + 1) * Q_TILE)
        mixed = (gts[j:j + 1, :] * o_cmp[:, sl]
                 + gts[NSA_HPG + j:NSA_HPG + j + 1, :] * o_slc[:, sl]
                 + gts[2 * NSA_HPG + j:2 * NSA_HPG + j + 1, :] * o_win[:, sl])
        o_ref[0, j * HEAD_DIM:(j + 1) * HEAD_DIM, :] = (mixed * out_gates[j][0]).astype(BF16)


def _nsa(nq_t, nqr_t, kc, vc_t, ks, vs_t, kw, vw_t, gts, ovl, expand, gate_t):
    bsz, _, seq = nq_t.shape
    n_cmp = kc.shape[2]
    gw = NSA_HPG * HEAD_DIM
    per_group = lambda shape: pl.BlockSpec((1, 1) + shape, lambda b, g, i: (b, g) + (0,) * len(shape))
    return pl.pallas_call(
        _nsa_kernel,
        grid=(bsz, NSA_GROUPS, seq // Q_TILE),
        in_specs=[
            pl.BlockSpec((1, gw, Q_TILE), lambda b, g, i: (b, g, i)),
            pl.BlockSpec((1, gw, Q_TILE), lambda b, g, i: (b, g, i)),
            per_group((n_cmp, HEAD_DIM)),
            per_group((HEAD_DIM, n_cmp)),
            per_group((seq, HEAD_DIM)),
            per_group((N_KEY_CHUNKS, HEAD_DIM, KEY_CHUNK)),
            per_group((seq, HEAD_DIM)),
            per_group((N_WIN_CHUNKS, HEAD_DIM, WIN_CHUNK)),
            pl.BlockSpec((1, 1, 16, Q_TILE), lambda b, g, i: (b, g, 0, i)),
            pl.BlockSpec(ovl.shape, lambda b, g, i: (0, 0)),
            pl.BlockSpec(expand.shape, lambda b, g, i: (0, 0)),
        ] + [
            pl.BlockSpec((1, HEAD_DIM, Q_TILE),
                         lambda b, g, i, j=j: (b, DSA_HEADS + FOX_HEADS + g * NSA_HPG + j, i))
            for j in range(NSA_HPG)
        ],
        out_specs=pl.BlockSpec((1, gw, Q_TILE), lambda b, g, i: (b, g, i)),
        out_shape=jax.ShapeDtypeStruct((bsz, NSA_W, seq), BF16),
        scratch_shapes=[pltpu.VMEM((N_SEL_BLOCKS, Q_TILE), F32)],
        compiler_params=_cparams(("arbitrary", "arbitrary", "arbitrary")),
        name="nsa_attn",
    )(nq_t, nqr_t, kc, vc_t, ks, vs_t, kw, vw_t, gts, ovl, expand, gate_t, gate_t, gate_t)


def _out_kernel(od_ref, of_ref, on_ref, w_ref, x_ref, gm_ref, lg_ref, lb_ref, o_ref):
    z = jnp.concatenate([od_ref[0], of_ref[0], on_ref[0]], axis=0)
    y = lax.dot_general(z, w_ref[...], (((0,), (0,)), ((), ())),
                        preferred_element_type=F32)
    r = ALPHA * x_ref[0] + (1.0 + gm_ref[0]) * y
    mu = jnp.mean(r, axis=-1, keepdims=True)
    rc = r - mu
    var = jnp.mean(rc * rc, axis=-1, keepdims=True)
    o_ref[0] = rc * lax.rsqrt(var + LN_EPS) * lg_ref[...] + lb_ref[...]


def _output(od_t, of_t, on_t, w_out, x, gmod, ln_g, ln_b):
    bsz, seq, d = x.shape
    tm = PROJ_TOKENS
    feat = lambda rows: pl.BlockSpec((1, rows, tm), lambda b, i: (b, 0, i))
    return pl.pallas_call(
        _out_kernel,
        grid=(bsz, seq // tm),
        in_specs=[
            feat(DSA_W), feat(FOX_W), feat(NSA_W),
            pl.BlockSpec((MIX_W, d), lambda b, i: (0, 0)),
            pl.BlockSpec((1, tm, d), lambda b, i: (b, i, 0)),
            pl.BlockSpec((1, 1, d), lambda b, i: (b, 0, 0)),
            pl.BlockSpec((1, d), lambda b, i: (0, 0)),
            pl.BlockSpec((1, d), lambda b, i: (0, 0)),
        ],
        out_specs=pl.BlockSpec((1, tm, d), lambda b, i: (b, i, 0)),
        out_shape=jax.ShapeDtypeStruct((bsz, seq, d), F32),
        compiler_params=_cparams(("arbitrary", "arbitrary")),
        name="out_proj_ln",
    )(od_t, of_t, on_t, w_out, x, gmod, ln_g, ln_b)


def _split_cols(w):
    out, off = {}, 0
    for name, width in IN_SPLITS:
        out[name] = w[:, off:off + width]
        off += width
    return out


def _prep_w_in(w_in):
    p = _split_cols(w_in)
    d = w_in.shape[0]
    z = lambda n: jnp.zeros((d, n), w_in.dtype)
    gcols = p["nsa_g"].reshape(d, 3, NSA_GROUPS, NSA_HPG)
    gparts = []
    for g in range(NSA_GROUPS):
        gparts += [gcols[:, :, g, :].reshape(d, 3 * NSA_HPG), z(16 - 3 * NSA_HPG)]
    wt = jnp.concatenate(
        [p["dsa_q"], p["idx_q"], p["fox_q"], p["nsa_q"],
         p["dsa_v"], z(HEAD_DIM), p["fox_v"], p["nsa_vs"], p["nsa_vw"], p["gate"],
         p["idx_w"], p["fox_f"], z(8 - FOX_HEADS)] + gparts, axis=1)
    assert wt.shape[1] == T_ROWS
    ws = jnp.concatenate(
        [p["dsa_k"], z(LANES - HEAD_DIM), p["idx_k"], z(LANES - IDX_DIM), p["fox_k"],
         p["nsa_ks"], p["nsa_kw"], p["nsa_kc"], p["nsa_vc"]], axis=1)
    assert ws.shape[1] == S_COLS
    return wt.T.astype(BF16), ws.astype(BF16)


def _rope_tables(seq):
    pos = jnp.arange(seq, dtype=F32)
    feature_major, token_major = [], []
    lane = jnp.arange(LANES)
    for hd in (HEAD_DIM, IDX_DIM):
        half = hd // 2
        inv = ROPE_THETA ** (-jnp.arange(half, dtype=F32) / half)
        ang = inv[:, None] * pos[None, :]
        cos, sin = jnp.cos(ang), jnp.sin(ang)
        feature_major += [cos, sin]
        sign = jnp.where((lane % hd) < half, -1.0, 1.0).astype(F32)
        reps = LANES // half
        token_major += [jnp.tile(cos.T, (1, reps)), jnp.tile(sin.T, (1, reps)) * sign[None, :]]
    return tuple(feature_major + token_major)


def _prep_compress(cmp_pe, cmp_w1, cmp_w2):
    eye = jnp.eye(NSA_GROUPS, dtype=F32)
    w1 = cmp_w1.reshape(2, CMP_LEN, HEAD_DIM, HEAD_DIM)
    big1 = jnp.einsum("kjde,gh->kjgdhe", w1, eye).reshape(2, CMP_LEN, LANES, LANES)
    w1lo = big1[:, :CMP_STRIDE].reshape(2, CMP_STRIDE * LANES, LANES).astype(BF16)
    w1hi = big1[:, CMP_STRIDE:].reshape(2, CMP_STRIDE * LANES, LANES).astype(BF16)
    w2 = jnp.einsum("kde,gh->kgdhe", cmp_w2, eye).reshape(2, LANES, LANES).astype(BF16)
    pe2 = jnp.concatenate([cmp_pe] * NSA_GROUPS, axis=-1)
    return pe2[:, :CMP_STRIDE], pe2[:, CMP_STRIDE:], w1lo, w1hi, w2


def _selection_constants(n_cmp):
    cstart = jnp.arange(n_cmp) * CMP_STRIDE
    bstart = jnp.arange(N_SEL_BLOCKS) * SEL_BLOCK
    real = (jnp.arange(n_cmp) < (SEQ - CMP_LEN) // CMP_STRIDE + 1)[:, None]
    ovl = ((cstart[:, None] < bstart[None, :] + SEL_BLOCK) & (cstart[:, None] + CMP_LEN > bstart[None, :]) & real)
    ovl_t = jnp.concatenate([ovl.T.astype(BF16)] * NSA_HPG, axis=1)
    expand = (jnp.arange(KEY_CHUNK)[:, None] // SEL_BLOCK == jnp.arange(KEY_CHUNK // SEL_BLOCK)[None, :]).astype(F32)
    return ovl_t, expand


def kernel(x, c, w_ada, b_ada, w_in, b_f, cmp_pe, cmp_w1, cmp_w2, w_out, ln_g, ln_b):
    bsz, seq, d = x.shape
    assert (seq, d) == (SEQ, D_MODEL)
    mod = _modulation(c, w_ada, b_ada)
    tabs = _rope_tables(seq)
    ovl, expand = _selection_constants(seq // CMP_STRIDE)
    for l in range(DEPTH):
        shift = mod[l, :, :d].reshape(bsz, 1, d)
        scale = mod[l, :, d:2 * d].reshape(bsz, 1, d)
        gmod = mod[l, :, 2 * d:].reshape(bsz, 1, d)
        wt, ws = _prep_w_in(w_in[l])
        bf8 = jnp.concatenate([b_f[l], jnp.zeros((8 - FOX_HEADS,), F32)])
        (dq, iq, fq, nq, nqr, dv, fv, vs, vw, gate, iw, lft, gts,
         dk, ik, fk, ks, kw, kcin, vcin) = _projection(
            x, scale, shift, wt, ws, tabs, bf8.reshape(8, 1))
        cum_t, cum_s = _forget_cumsum(lft)
        kc, vc_t = _compress(kcin, vcin, *_prep_compress(cmp_pe[l], cmp_w1[l], cmp_w2[l]))
        o_dsa = _dsa(iq, ik, iw, dq, dk, dv, gate)
        o_fox = _fox(fq, fk, fv, cum_t, cum_s, gate)
        o_nsa = _nsa(nq, nqr, kc, vc_t, ks, vs, kw, vw, gts, ovl, expand, gate)
        x = _output(o_dsa, o_fox, o_nsa, w_out[l].astype(BF16), x, gmod,
                    ln_g[l].reshape(1, d), ln_b[l].reshape(1, d))
    return x
```

```python
import functools

import jax
import jax.numpy as jnp
from jax import lax
from jax.experimental import pallas as pl
from jax.experimental.pallas import tpu as pltpu

F32 = jnp.float32
BF16 = jnp.bfloat16
I32 = jnp.int32

D_MODEL = 1024
SEQ = 2048
DEPTH = 2
HEAD_DIM = 64
DSA_HEADS = 4
DSA_TOPK = 256
IDX_HEADS = 8
IDX_DIM = 32
FOX_HEADS = 6
NSA_HEADS = 6
NSA_GROUPS = 2
NSA_HPG = NSA_HEADS // NSA_GROUPS
CMP_LEN = 32
CMP_STRIDE = 16
SEL_BLOCK = 64
SEL_N = 16
N_SEL_BLOCKS = SEQ // SEL_BLOCK
WINDOW = 512
ROPE_THETA = 10000.0
LN_EPS = 1e-5
ALPHA = (2.0 * DEPTH) ** 0.25

DSA_W = DSA_HEADS * HEAD_DIM
FOX_W = FOX_HEADS * HEAD_DIM
NSA_W = NSA_HEADS * HEAD_DIM
NSA_KV_W = NSA_GROUPS * HEAD_DIM
MIX_W = DSA_W + FOX_W + NSA_W
IDX_W = IDX_HEADS * IDX_DIM

IN_SPLITS = (
    ("dsa_q", DSA_W), ("dsa_k", HEAD_DIM), ("dsa_v", HEAD_DIM),
    ("idx_q", IDX_W), ("idx_k", IDX_DIM), ("idx_w", IDX_HEADS),
    ("fox_q", FOX_W), ("fox_k", FOX_W), ("fox_v", FOX_W), ("fox_f", FOX_HEADS),
    ("nsa_q", NSA_W),
    ("nsa_kc", NSA_KV_W), ("nsa_vc", NSA_KV_W),
    ("nsa_ks", NSA_KV_W), ("nsa_vs", NSA_KV_W),
    ("nsa_kw", NSA_KV_W), ("nsa_vw", NSA_KV_W),
    ("nsa_g", 3 * NSA_HEADS),
    ("gate", MIX_W),
)

LANES = 128
KEY_CHUNK = 512
PROJ_TOKENS = 512
Q_TILE = 256
DSA_Q_TILE = 512
FOX_Q_TILE = 512
N_KEY_CHUNKS = SEQ // KEY_CHUNK
WIN_CHUNK = 256
N_WIN_CHUNKS = SEQ // WIN_CHUNK
WIN_SPAN = WINDOW + Q_TILE
CUM_CHUNK = 256
VMEM_LIMIT = 56 * 1024 * 1024

FOLD_ROWS = 32
ONES_ROWS = 16
LOG2E = 1.4426950408889634
NEG = -(2.0 ** 100)
INT_MIN = -(2 ** 31)
CODE_NEG_INF = 0x007FFFFF

T_DQ, T_IQ, T_FQ, T_NQ = 0, 256, 512, 896
T_DV, T_FV, T_VS, T_VW = 1280, 1408, 1792, 1920
T_GATE, T_SMALL, T_ROWS = 2048, 3072, 3120
S_DK, S_IK, S_FK, S_KS, S_KW, S_KC, S_VC, S_COLS = 0, 128, 256, 640, 768, 896, 1024, 1152


def _cparams(sem):
    return pltpu.CompilerParams(dimension_semantics=sem, vmem_limit_bytes=VMEM_LIMIT)


def _dot(a, b):
    return jnp.dot(a, b, preferred_element_type=F32)


def _dot_nt(a, b):
    return lax.dot_general(a, b, (((1,), (1,)), ((), ())), preferred_element_type=F32)


def _log_sigmoid(x):
    return jnp.minimum(x, 0.0) - jnp.log(1.0 + jnp.exp(-jnp.abs(x)))


def _sigmoid(x):
    return 1.0 / (1.0 + jnp.exp(-x))


def _mod_kernel(c_ref, w_ref, b_ref, o_ref):
    o_ref[0] = _dot(c_ref[...].astype(BF16), w_ref[0].astype(BF16)) + b_ref[0]


def _modulation(c, w_ada, b_ada):
    depth, d, d3 = w_ada.shape
    bsz = c.shape[0]
    return pl.pallas_call(
        _mod_kernel,
        grid=(depth, d3 // d),
        in_specs=[
            pl.BlockSpec((bsz, d), lambda l, j: (0, 0)),
            pl.BlockSpec((1, d, d), lambda l, j: (l, 0, j)),
            pl.BlockSpec((1, 1, d), lambda l, j: (l, 0, j)),
        ],
        out_specs=pl.BlockSpec((1, bsz, d), lambda l, j: (l, 0, j)),
        out_shape=jax.ShapeDtypeStruct((depth, bsz, d3), F32),
        compiler_params=_cparams(("arbitrary", "arbitrary")),
        name="adaln_mod",
    )(c, w_ada, b_ada.reshape(depth, 1, d3))


def _proj_kernel(x_ref, sc_ref, sh_ref, wt_ref, ws_ref,
                 ct64_ref, st64_ref, ct32_ref, st32_ref,
                 ck64_ref, sk64_ref, ck32_ref, sk32_ref, bfc_ref,
                 dq_ref, iq_ref, fq_ref, nq_ref, nqr_ref,
                 dv_ref, fv_ref, vs_ref, vw_ref, gate_ref,
                 iw_ref, lft_ref, gts_ref,
                 dk_ref, ik_ref, fk_ref, ks_ref, kw_ref, kcin_ref, vcin_ref):
    tm = x_ref.shape[1]
    u = (x_ref[0] * (1.0 + sc_ref[0]) + sh_ref[0]).astype(BF16)

    ht_all = _dot_nt(wt_ref[...], u)

    def proj_t(r0, r1):
        return ht_all[r0:r1]

    def rope_t(h, n_heads, hd, c, s):
        half = hd // 2
        out = []
        for hh in range(n_heads):
            x1 = h[hh * hd:hh * hd + half]
            x2 = h[hh * hd + half:(hh + 1) * hd]
            out.append((hh * hd, x1 * c - x2 * s))
            out.append((hh * hd + half, x1 * s + x2 * c))
        return out

    c64, s64 = ct64_ref[...], st64_ref[...]
    c32, s32 = ct32_ref[...], st32_ref[...]
    qscale = HEAD_DIM ** -0.5 * LOG2E

    h = proj_t(T_DQ, T_DQ + DSA_W)
    for r, v in rope_t(h, DSA_HEADS, HEAD_DIM, c64, s64):
        dq_ref[0, r:r + HEAD_DIM // 2, :] = (v * qscale).astype(BF16)
    h = proj_t(T_IQ, T_IQ + IDX_W)
    for r, v in rope_t(h, IDX_HEADS, IDX_DIM, c32, s32):
        iq_ref[0, r:r + IDX_DIM // 2, :] = v.astype(BF16)
    fq_ref[0] = (proj_t(T_FQ, T_FQ + FOX_W) * qscale).astype(BF16)
    h = proj_t(T_NQ, T_NQ + NSA_W)
    nq_ref[0] = (h * qscale).astype(BF16)
    for r, v in rope_t(h, NSA_HEADS, HEAD_DIM, c64, s64):
        nqr_ref[0, r:r + HEAD_DIM // 2, :] = (v * qscale).astype(BF16)

    dv_ref[0, 0] = proj_t(T_DV, T_DV + HEAD_DIM).astype(BF16)
    h = proj_t(T_FV, T_FV + FOX_W)
    for hh in range(FOX_HEADS):
        fv_ref[0, hh, 0] = h[hh * HEAD_DIM:(hh + 1) * HEAD_DIM].astype(BF16)
    h = proj_t(T_VS, T_VS + NSA_KV_W)
    for g in range(NSA_GROUPS):
        vs_ref[0, g, 0] = h[g * HEAD_DIM:(g + 1) * HEAD_DIM].astype(BF16)
    h = proj_t(T_VW, T_VW + NSA_KV_W)
    for g in range(NSA_GROUPS):
        for j in range(tm // WIN_CHUNK):
            vw_ref[0, g, j] = h[g * HEAD_DIM:(g + 1) * HEAD_DIM, j * WIN_CHUNK:(j + 1) * WIN_CHUNK].astype(BF16)

    for r0 in range(0, MIX_W, 256):
        h = proj_t(T_GATE + r0, T_GATE + r0 + 256)
        gate_ref[0, r0:r0 + 256, :] = h * _sigmoid(h)

    h = proj_t(T_SMALL, T_ROWS)
    iw_ref[0] = h[0:8] * (IDX_HEADS ** -0.5)
    lft_ref[0] = _log_sigmoid(h[8:16] + bfc_ref[...])
    g_all = _sigmoid(h[16:48])
    gts_ref[0, 0] = g_all[0:16]
    gts_ref[0, 1] = g_all[16:32]

    lane = lax.broadcasted_iota(I32, (tm, LANES), 1)

    def rope_s(g, half, c, s_signed):
        first = (lane & (2 * half - 1)) < half
        sw = jnp.where(first, pltpu.roll(g, LANES - half, 1), pltpu.roll(g, half, 1))
        return g * c + sw * s_signed

    hs_all = _dot(u, ws_ref[...])

    def proj_s(c0, c1):
        return hs_all[:, c0:c1]

    ck64, sk64 = ck64_ref[...], sk64_ref[...]
    g = rope_s(proj_s(S_DK, S_DK + LANES), HEAD_DIM // 2, ck64, sk64)
    dk_ref[0] = g[:, :HEAD_DIM].astype(BF16)
    g = rope_s(proj_s(S_IK, S_IK + LANES), IDX_DIM // 2, ck32_ref[...], sk32_ref[...])
    ik_ref[0] = g[:, :IDX_DIM].astype(BF16)
    g = proj_s(S_FK, S_FK + FOX_W)
    for hh in range(FOX_HEADS):
        fk_ref[0, hh] = g[:, hh * HEAD_DIM:(hh + 1) * HEAD_DIM].astype(BF16)
    g = rope_s(proj_s(S_KS, S_KS + LANES), HEAD_DIM // 2, ck64, sk64)
    for gg in range(NSA_GROUPS):
        ks_ref[0, gg] = g[:, gg * HEAD_DIM:(gg + 1) * HEAD_DIM].astype(BF16)
    g = rope_s(proj_s(S_KW, S_KW + LANES), HEAD_DIM // 2, ck64, sk64)
    for gg in range(NSA_GROUPS):
        kw_ref[0, gg] = g[:, gg * HEAD_DIM:(gg + 1) * HEAD_DIM].astype(BF16)
    kcin_ref[0] = proj_s(S_KC, S_KC + LANES)
    vcin_ref[0] = proj_s(S_VC, S_VC + LANES)


def _projection(x, scale, shift, wt, ws, tabs, bfc):
    bsz, seq, d = x.shape
    tm = PROJ_TOKENS
    nt = seq // tm
    ct64, st64, ct32, st32, ck64, sk64, ck32, sk32 = tabs
    const = lambda shape: pl.BlockSpec(shape, lambda b, i: (0,) * len(shape))
    feat = lambda rows: pl.BlockSpec((1, rows, tm), lambda b, i: (b, 0, i))
    in_specs = [
        pl.BlockSpec((1, tm, d), lambda b, i: (b, i, 0)),
        pl.BlockSpec((1, 1, d), lambda b, i: (b, 0, 0)),
        pl.BlockSpec((1, 1, d), lambda b, i: (b, 0, 0)),
        const((T_ROWS, d)),
        const((d, S_COLS)),
        pl.BlockSpec((HEAD_DIM // 2, tm), lambda b, i: (0, i)),
        pl.BlockSpec((HEAD_DIM // 2, tm), lambda b, i: (0, i)),
        pl.BlockSpec((IDX_DIM // 2, tm), lambda b, i: (0, i)),
        pl.BlockSpec((IDX_DIM // 2, tm), lambda b, i: (0, i)),
        pl.BlockSpec((tm, LANES), lambda b, i: (i, 0)),
        pl.BlockSpec((tm, LANES), lambda b, i: (i, 0)),
        pl.BlockSpec((tm, LANES), lambda b, i: (i, 0)),
        pl.BlockSpec((tm, LANES), lambda b, i: (i, 0)),
        const((8, 1)),
    ]
    sds = jax.ShapeDtypeStruct
    out_shape = [
        sds((bsz, DSA_W, seq), BF16), sds((bsz, IDX_W, seq), BF16), sds((bsz, FOX_W, seq), BF16),
        sds((bsz, NSA_W, seq), BF16), sds((bsz, NSA_W, seq), BF16),
        sds((bsz, N_KEY_CHUNKS, HEAD_DIM, KEY_CHUNK), BF16),
        sds((bsz, FOX_HEADS, N_KEY_CHUNKS, HEAD_DIM, KEY_CHUNK), BF16),
        sds((bsz, NSA_GROUPS, N_KEY_CHUNKS, HEAD_DIM, KEY_CHUNK), BF16),
        sds((bsz, NSA_GROUPS, N_WIN_CHUNKS, HEAD_DIM, WIN_CHUNK), BF16),
        sds((bsz, MIX_W, seq), F32),
        sds((bsz, 8, seq), F32), sds((bsz, 8, seq), F32), sds((bsz, NSA_GROUPS, 16, seq), F32),
        sds((bsz, seq, HEAD_DIM), BF16), sds((bsz, seq, IDX_DIM), BF16),
        sds((bsz, FOX_HEADS, seq, HEAD_DIM), BF16),
        sds((bsz, NSA_GROUPS, seq, HEAD_DIM), BF16), sds((bsz, NSA_GROUPS, seq, HEAD_DIM), BF16),
        sds((bsz, seq, LANES), F32), sds((bsz, seq, LANES), F32),
    ]
    wpc = tm // WIN_CHUNK
    out_specs = [
        feat(DSA_W), feat(IDX_W), feat(FOX_W), feat(NSA_W), feat(NSA_W),
        pl.BlockSpec((1, 1, HEAD_DIM, KEY_CHUNK), lambda b, i: (b, i, 0, 0)),
        pl.BlockSpec((1, FOX_HEADS, 1, HEAD_DIM, KEY_CHUNK), lambda b, i: (b, 0, i, 0, 0)),
        pl.BlockSpec((1, NSA_GROUPS, 1, HEAD_DIM, KEY_CHUNK), lambda b, i: (b, 0, i, 0, 0)),
        pl.BlockSpec((1, NSA_GROUPS, wpc, HEAD_DIM, WIN_CHUNK), lambda b, i: (b, 0, i, 0, 0)),
        feat(MIX_W),
        feat(8), feat(8),
        pl.BlockSpec((1, NSA_GROUPS, 16, tm), lambda b, i: (b, 0, 0, i)),
        pl.BlockSpec((1, tm, HEAD_DIM), lambda b, i: (b, i, 0)),
        pl.BlockSpec((1, tm, IDX_DIM), lambda b, i: (b, i, 0)),
        pl.BlockSpec((1, FOX_HEADS, tm, HEAD_DIM), lambda b, i: (b, 0, i, 0)),
        pl.BlockSpec((1, NSA_GROUPS, tm, HEAD_DIM), lambda b, i: (b, 0, i, 0)),
        pl.BlockSpec((1, NSA_GROUPS, tm, HEAD_DIM), lambda b, i: (b, 0, i, 0)),
        pl.BlockSpec((1, tm, LANES), lambda b, i: (b, i, 0)),
        pl.BlockSpec((1, tm, LANES), lambda b, i: (b, i, 0)),
    ]
    assert tm == KEY_CHUNK
    return pl.pallas_call(
        _proj_kernel,
        grid=(bsz, nt),
        in_specs=in_specs,
        out_specs=out_specs,
        out_shape=out_shape,
        compiler_params=_cparams(("arbitrary", "arbitrary")),
        name="in_proj",
    )(x, scale, shift, wt, ws, ct64, st64, ct32, st32, ck64, sk64, ck32, sk32, bfc)


def _cumsum_kernel(lft_ref, cumt_ref, cums_ref):
    seq = lft_ref.shape[2]
    r = lax.broadcasted_iota(I32, (CUM_CHUNK, CUM_CHUNK), 0)
    c = lax.broadcasted_iota(I32, (CUM_CHUNK, CUM_CHUNK), 1)
    tri_u = (r <= c).astype(F32)
    blocks = [slice(k * CUM_CHUNK, (k + 1) * CUM_CHUNK) for k in range(seq // CUM_CHUNK)]
    local = [jnp.dot(lft_ref[0, :, sl], tri_u, preferred_element_type=F32, precision=lax.Precision.HIGHEST)
             for sl in blocks]
    carry = jnp.zeros((8, 1), F32)
    pad = jnp.zeros((LANES - 8, CUM_CHUNK), F32)
    for k, sl in enumerate(blocks):
        ct = (local[k] + carry) * LOG2E
        carry = local[k][:, CUM_CHUNK - 1:CUM_CHUNK] + carry
        for head in range(8):
            cumt_ref[0, head, :, sl] = ct[head:head + 1, :]
        cums_ref[0, sl, :] = jnp.concatenate([ct, pad], axis=0).T[:, :8]


def _forget_cumsum(lft):
    bsz, _, seq = lft.shape
    return pl.pallas_call(
        _cumsum_kernel,
        grid=(bsz,),
        in_specs=[pl.BlockSpec((1, 8, seq), lambda b: (b, 0, 0))],
        out_specs=[pl.BlockSpec((1, 8, 1, seq), lambda b: (b, 0, 0, 0)),
                   pl.BlockSpec((1, seq, 8), lambda b: (b, 0, 0))],
        out_shape=[jax.ShapeDtypeStruct((bsz, 8, 1, seq), F32),
                   jax.ShapeDtypeStruct((bsz, seq, 8), F32)],
        compiler_params=_cparams(("arbitrary",)),
        name="forget_cumsum",
    )(lft)


def _compress_kernel(kcin_ref, vcin_ref, pelo_ref, pehi_ref, w1lo_ref, w1hi_ref, w2_ref,
                     kc_ref, vct_ref):
    n_blk = kcin_ref.shape[1] // CMP_STRIDE
    for kv, src in enumerate((kcin_ref, vcin_ref)):
        lo, hi = [], []
        for j in range(CMP_STRIDE):
            piece = src[0, pl.ds(j, n_blk, stride=CMP_STRIDE), :]
            lo.append((piece + pelo_ref[kv, j:j + 1, :]).astype(BF16))
            hi.append((piece + pehi_ref[kv, j:j + 1, :]).astype(BF16))
        a = _dot(jnp.concatenate(lo, axis=1), w1lo_ref[kv])
        b = _dot(jnp.concatenate(hi, axis=1), w1hi_ref[kv])
        pre = a + pltpu.roll(b, n_blk - 1, 0)
        act = pre * _sigmoid(pre)
        out = _dot(act.astype(BF16), w2_ref[kv])
        if kv == 0:
            for g in range(NSA_GROUPS):
                kc_ref[0, g] = out[:, g * HEAD_DIM:(g + 1) * HEAD_DIM].astype(BF16)
        else:
            out_t = out.T
            for g in range(NSA_GROUPS):
                vct_ref[0, g] = out_t[g * HEAD_DIM:(g + 1) * HEAD_DIM].astype(BF16)


def _compress(kcin, vcin, pelo, pehi, w1lo, w1hi, w2):
    bsz, seq, _ = kcin.shape
    n_blk = seq // CMP_STRIDE
    full = lambda a: pl.BlockSpec(a.shape, lambda b: (0,) * a.ndim)
    return pl.pallas_call(
        _compress_kernel,
        grid=(bsz,),
        in_specs=[pl.BlockSpec((1, seq, LANES), lambda b: (b, 0, 0)),
                  pl.BlockSpec((1, seq, LANES), lambda b: (b, 0, 0)),
                  full(pelo), full(pehi), full(w1lo), full(w1hi), full(w2)],
        out_specs=[pl.BlockSpec((1, NSA_GROUPS, n_blk, HEAD_DIM), lambda b: (b, 0, 0, 0)),
                   pl.BlockSpec((1, NSA_GROUPS, HEAD_DIM, n_blk), lambda b: (b, 0, 0, 0))],
        out_shape=[jax.ShapeDtypeStruct((bsz, NSA_GROUPS, n_blk, HEAD_DIM), BF16),
                   jax.ShapeDtypeStruct((bsz, NSA_GROUPS, HEAD_DIM, n_blk), BF16)],
        compiler_params=_cparams(("arbitrary",)),
        name="nsa_compress",
    )(kcin, vcin, pelo, pehi, w1lo, w1hi, w2)


def _fold_rows(x, op):
    k, n = x.shape
    return op(x.reshape(k // FOLD_ROWS, FOLD_ROWS, n), axis=0)


def _reduce_rows(x, op):
    if x.shape[0] % FOLD_ROWS == 0 and x.shape[0] > FOLD_ROWS:
        x = _fold_rows(x, op)
    return op(x, axis=0, keepdims=True)


def _with_ones(v_t):
    return jnp.concatenate([v_t, jnp.ones((ONES_ROWS, v_t.shape[1]), v_t.dtype)], axis=0)


def _softmax_step(carry, s, v_aug):
    m, acc = carry
    m_new = jnp.maximum(m, _reduce_rows(s, jnp.max))
    alpha = jnp.exp2(m - m_new)
    p = jnp.exp2(s - m_new)
    acc = alpha * acc + _dot(v_aug, p.astype(BF16))
    return m_new, acc


def _softmax_init(n):
    return (jnp.full((1, n), NEG, F32), jnp.zeros((HEAD_DIM + ONES_ROWS, n), F32))


def _softmax_finish(carry):
    _, acc = carry
    return acc[:HEAD_DIM] * (1.0 / acc[HEAD_DIM:HEAD_DIM + 1])


def _tile_lanes(a, n):
    return jnp.concatenate([a] * n, axis=1)


def _dsa_kernel(iq_ref, ik_ref, iw_ref, dq_ref, dk_ref, dv_ref, g_ref, o_ref, key_ref):
    i = pl.program_id(1)
    tq = DSA_Q_TILE
    n_chunks = i + 1
    tpos = i * tq + lax.broadcasted_iota(I32, (1, tq), 1)
    srow = lax.broadcasted_iota(I32, (KEY_CHUNK, tq), 0)
    iw = iw_ref[0]

    def index_chunk(c, _):
        k0 = pl.multiple_of(c * KEY_CHUNK, KEY_CHUNK)
        ikc = ik_ref[0, pl.ds(k0, KEY_CHUNK), :]
        acc = jnp.zeros((KEY_CHUNK, tq), F32)
        for h in range(IDX_HEADS):
            x = _dot(ikc, iq_ref[0, h * IDX_DIM:(h + 1) * IDX_DIM, :])
            acc = acc + iw[h:h + 1, :] * jnp.maximum(x, 0.0)
        key_ref[pl.ds(k0, KEY_CHUNK), :] = jnp.where(srow + k0 <= tpos, acc, -jnp.inf)
        return 0

    lax.fori_loop(0, n_chunks, index_chunk, 0)

    def count(pred):
        def body(c, acc):
            k0 = pl.multiple_of(c * KEY_CHUNK, KEY_CHUNK)
            hit = pred(key_ref[pl.ds(k0, KEY_CHUNK), :], srow + k0)
            return acc + _fold_rows(jnp.where(hit, 1, 0), jnp.sum)
        acc = lax.fori_loop(0, n_chunks, body, jnp.zeros((FOLD_ROWS, tq), I32))
        return jnp.sum(acc, axis=0, keepdims=True)

    def decode(code):
        skey = code ^ INT_MIN
        val = lax.bitcast_convert_type(jnp.where(skey < 0, skey ^ 0x7FFFFFFF, skey), F32)
        return jnp.where(jnp.logical_and(code >= 0, code <= CODE_NEG_INF), -jnp.inf, val)

    def value_bit(it, code):
        cand = code | lax.shift_left(jnp.int32(1), 31 - it)
        cand_f = decode(cand)
        cnt = count(lambda key, spos: key >= cand_f)
        return jnp.where(cnt >= DSA_TOPK, cand, code)

    thr = decode(lax.fori_loop(0, 32, value_bit, jnp.zeros((1, tq), I32)))

    need = (DSA_TOPK - count(lambda key, spos: key > thr)).astype(F32)
    r = lax.broadcasted_iota(I32, (KEY_CHUNK, KEY_CHUNK), 0)
    c = lax.broadcasted_iota(I32, (KEY_CHUNK, KEY_CHUNK), 1)
    prefix = jnp.where(c <= r, 1.0, 0.0).astype(BF16)

    def attend(c, carry):
        heads, ties_before = carry
        k0 = pl.multiple_of(c * KEY_CHUNK, KEY_CHUNK)
        key = key_ref[pl.ds(k0, KEY_CHUNK), :]
        tied = key == thr
        rank = _dot(prefix, jnp.where(tied, 1.0, 0.0).astype(BF16)) + ties_before
        keep = jnp.logical_or(key > thr, jnp.logical_and(tied, rank <= need))
        bias = jnp.where(jnp.logical_and(keep, srow + k0 <= tpos), 0.0, NEG)
        kc = dk_ref[0, pl.ds(k0, KEY_CHUNK), :]
        v_aug = _with_ones(dv_ref[0, c])
        heads = tuple(
            _softmax_step(heads[h], _dot(kc, dq_ref[0, h * HEAD_DIM:(h + 1) * HEAD_DIM, :]) + bias, v_aug)
            for h in range(DSA_HEADS))
        return heads, rank[KEY_CHUNK - 1:KEY_CHUNK, :]

    init = tuple(_softmax_init(tq) for _ in range(DSA_HEADS))
    heads, _ = lax.fori_loop(0, n_chunks, attend, (init, jnp.zeros((1, tq), F32)))
    for h, carry in enumerate(heads):
        rows = slice(h * HEAD_DIM, (h + 1) * HEAD_DIM)
        o_ref[0, rows, :] = (_softmax_finish(carry) * g_ref[0, rows, :]).astype(BF16)


def _dsa(iq_t, ik, iw_t, dq_t, dk, dv_t, gate_t):
    bsz, _, seq = dq_t.shape
    tq = DSA_Q_TILE
    assert tq == KEY_CHUNK
    return pl.pallas_call(
        _dsa_kernel,
        grid=(bsz, seq // tq),
        in_specs=[
            pl.BlockSpec((1, IDX_W, tq), lambda b, i: (b, 0, i)),
            pl.BlockSpec((1, seq, IDX_DIM), lambda b, i: (b, 0, 0)),
            pl.BlockSpec((1, 8, tq), lambda b, i: (b, 0, i)),
            pl.BlockSpec((1, DSA_W, tq), lambda b, i: (b, 0, i)),
            pl.BlockSpec((1, seq, HEAD_DIM), lambda b, i: (b, 0, 0)),
            pl.BlockSpec((1, N_KEY_CHUNKS, HEAD_DIM, KEY_CHUNK), lambda b, i: (b, 0, 0, 0)),
            pl.BlockSpec((1, DSA_W, tq), lambda b, i: (b, 0, i)),
        ],
        out_specs=pl.BlockSpec((1, DSA_W, tq), lambda b, i: (b, 0, i)),
        out_shape=jax.ShapeDtypeStruct((bsz, DSA_W, seq), BF16),
        scratch_shapes=[pltpu.VMEM((seq, tq), F32)],
        compiler_params=_cparams(("arbitrary", "arbitrary")),
        name="dsa_attn",
    )(iq_t, ik, iw_t, dq_t, dk, dv_t, gate_t)


def _fox_kernel(q_ref, k_ref, v_ref, cumt_ref, cums_ref, g_ref, o_ref):
    h = pl.program_id(1)
    tq = FOX_Q_TILE
    n_tiles = q_ref.shape[2] // tq
    head_lane = lax.broadcasted_iota(I32, (KEY_CHUNK, 8), 1) == h
    srow = lax.broadcasted_iota(I32, (KEY_CHUNK, tq), 0)
    tcol = lax.broadcasted_iota(I32, (KEY_CHUNK, tq), 1)
    ccols = []
    for c in range(n_tiles):
        col = jnp.sum(jnp.where(head_lane, cums_ref[0, c * KEY_CHUNK:(c + 1) * KEY_CHUNK, :], 0.0),
                      axis=1, keepdims=True)
        ccols.append(_tile_lanes(jnp.broadcast_to(col, (KEY_CHUNK, LANES)), tq // LANES))
    v_aug = [_with_ones(v_ref[0, 0, c]) for c in range(n_tiles)]
    for i in range(n_tiles):
        q = q_ref[0, :, i * tq:(i + 1) * tq]
        crow = cumt_ref[0, 0, :, i * tq:(i + 1) * tq]
        carry = _softmax_init(tq)
        for c in range(i + 1):
            s = _dot(k_ref[0, 0, c * KEY_CHUNK:(c + 1) * KEY_CHUNK, :], q) + crow - ccols[c]
            if c == i:
                s = jnp.where(srow <= tcol, s, NEG)
            carry = _softmax_step(carry, s, v_aug[c])
        cols = slice(i * tq, (i + 1) * tq)
        o_ref[0, :, cols] = (_softmax_finish(carry) * g_ref[0, :, cols]).astype(BF16)


def _fox(fq_t, fk, fv_t, cum_t, cum_s, gate_t):
    bsz, _, seq = fq_t.shape
    assert FOX_Q_TILE == KEY_CHUNK
    return pl.pallas_call(
        _fox_kernel,
        grid=(bsz, FOX_HEADS),
        in_specs=[
            pl.BlockSpec((1, HEAD_DIM, seq), lambda b, h: (b, h, 0)),
            pl.BlockSpec((1, 1, seq, HEAD_DIM), lambda b, h: (b, h, 0, 0)),
            pl.BlockSpec((1, 1, N_KEY_CHUNKS, HEAD_DIM, KEY_CHUNK), lambda b, h: (b, h, 0, 0, 0)),
            pl.BlockSpec((1, 1, 1, seq), lambda b, h: (b, h, 0, 0)),
            pl.BlockSpec((1, seq, 8), lambda b, h: (b, 0, 0)),
            pl.BlockSpec((1, HEAD_DIM, seq), lambda b, h: (b, DSA_HEADS + h, 0)),
        ],
        out_specs=pl.BlockSpec((1, HEAD_DIM, seq), lambda b, h: (b, h, 0)),
        out_shape=jax.ShapeDtypeStruct((bsz, FOX_W, seq), BF16),
        compiler_params=_cparams(("arbitrary", "arbitrary")),
        name="fox_attn",
    )(fq_t, fk, fv_t, cum_t, cum_s, gate_t)


def _nsa_kernel(q_ref, qr_ref, kc_ref, vct_ref, ks_ref, vs_ref, kw_ref, vw_ref, g_ref,
                ovl_ref, expand_ref, og0_ref, og1_ref, og2_ref, o_ref, sel_ref):
    i = pl.program_id(2)
    nq = NSA_HPG * Q_TILE
    t0 = i * Q_TILE
    tpos = t0 + lax.broadcasted_iota(I32, (1, Q_TILE), 1)
    q3 = jnp.concatenate([q_ref[0, j * HEAD_DIM:(j + 1) * HEAD_DIM, :] for j in range(NSA_HPG)], axis=1)
    qr3 = jnp.concatenate([qr_ref[0, j * HEAD_DIM:(j + 1) * HEAD_DIM, :] for j in range(NSA_HPG)], axis=1)

    n_cmp = kc_ref.shape[2]
    cend = lax.broadcasted_iota(I32, (n_cmp, Q_TILE), 0) * CMP_STRIDE + (CMP_LEN - 1)
    cbias = _tile_lanes(jnp.where(cend <= tpos, 0.0, NEG), NSA_HPG)
    cvalid = _tile_lanes(jnp.where(cend <= tpos, 1.0, 0.0), NSA_HPG)
    s = _dot(kc_ref[0, 0], q3) + cbias
    m = _reduce_rows(s, jnp.max)
    e = jnp.exp2(s - m) * cvalid
    p_cmp = (e / jnp.maximum(_reduce_rows(e, jnp.sum), 1e-30)).astype(BF16)
    o_cmp = _dot(vct_ref[0, 0], p_cmp)

    p_stack = jnp.concatenate([p_cmp[:, j * Q_TILE:(j + 1) * Q_TILE] for j in range(NSA_HPG)], axis=0)
    score = _dot(ovl_ref[...], p_stack)
    blk = lax.broadcasted_iota(I32, (N_SEL_BLOCKS, Q_TILE), 0)
    cur = lax.shift_right_logical(tpos, 6)
    forced = jnp.logical_or(blk == 0, jnp.logical_or(blk == cur, blk == cur - 1))
    score = jnp.where(forced, jnp.inf, jnp.where(blk > cur, -jnp.inf, score))
    rank = jnp.zeros((N_SEL_BLOCKS, Q_TILE), F32)
    for mth in range(N_SEL_BLOCKS):
        row = score[mth:mth + 1, :]
        ahead = jnp.logical_or(row > score, jnp.logical_and(row == score, blk > mth))
        rank = rank + jnp.where(ahead, 1.0, 0.0)
    sel_ref[...] = jnp.where(rank < SEL_N, 1.0, 0.0)

    n_chunks = (i * Q_TILE + Q_TILE + KEY_CHUNK - 1) // KEY_CHUNK
    srow = lax.broadcasted_iota(I32, (KEY_CHUNK, Q_TILE), 0)
    blocks_per_chunk = KEY_CHUNK // SEL_BLOCK

    def attend_sel(c, carry):
        k0 = pl.multiple_of(c * KEY_CHUNK, KEY_CHUNK)
        b0 = pl.multiple_of(c * blocks_per_chunk, blocks_per_chunk)
        picked = _dot(expand_ref[...], sel_ref[pl.ds(b0, blocks_per_chunk), :])
        ok = jnp.logical_and(picked > 0.5, srow + k0 <= tpos)
        bias = _tile_lanes(jnp.where(ok, 0.0, NEG), NSA_HPG)
        s = _dot(ks_ref[0, 0, pl.ds(k0, KEY_CHUNK), :], qr3) + bias
        return _softmax_step(carry, s, _with_ones(vs_ref[0, 0, c]))

    wc = jnp.maximum(i - WINDOW // Q_TILE, 0) * (Q_TILE // WIN_CHUNK)
    w0 = pl.multiple_of(wc * WIN_CHUNK, WIN_CHUNK)
    kpos = w0 + lax.broadcasted_iota(I32, (WIN_SPAN, Q_TILE), 0)
    wok = jnp.logical_and(kpos <= tpos, kpos > tpos - WINDOW)
    s = _dot(kw_ref[0, 0, pl.ds(w0, WIN_SPAN), :], qr3) + _tile_lanes(jnp.where(wok, 0.0, NEG), NSA_HPG)
    m = _reduce_rows(s, jnp.max)
    e = jnp.exp2(s - m).astype(BF16)
    acc = jnp.zeros((HEAD_DIM + ONES_ROWS, nq), F32)
    for j in range(WIN_SPAN // WIN_CHUNK):
        acc = acc + _dot(_with_ones(vw_ref[0, 0, wc + j]), e[j * WIN_CHUNK:(j + 1) * WIN_CHUNK, :])
    o_win = _softmax_finish((m, acc))

    first = attend_sel(0, _softmax_init(nq))
    o_slc = _softmax_finish(lax.fori_loop(1, n_chunks, attend_sel, first))

    gts = g_ref[0, 0]
    out_gates = (og0_ref, og1_ref, og2_ref)
    for j in range(NSA_HPG):
        sl = slice(j * Q_TILE, (j + 1) * Q_TILE)
        mixed = (gts[j:j + 1, :] * o_cmp[:, sl]
                 + gts[NSA_HPG + j:NSA_HPG + j + 1, :] * o_slc[:, sl]
                 + gts[2 * NSA_HPG + j:2 * NSA_HPG + j + 1, :] * o_win[:, sl])
        o_ref[0, j * HEAD_DIM:(j + 1) * HEAD_DIM, :] = (mixed * out_gates[j][0]).astype(BF16)


def _nsa(nq_t, nqr_t, kc, vc_t, ks, vs_t, kw, vw_t, gts, ovl, expand, gate_t):
    bsz, _, seq = nq_t.shape
    n_cmp = kc.shape[2]
    gw = NSA_HPG * HEAD_DIM
    per_group = lambda shape: pl.BlockSpec((1, 1) + shape, lambda b, g, i: (b, g) + (0,) * len(shape))
    return pl.pallas_call(
        _nsa_kernel,
        grid=(bsz, NSA_GROUPS, seq // Q_TILE),
        in_specs=[
            pl.BlockSpec((1, gw, Q_TILE), lambda b, g, i: (b, g, i)),
            pl.BlockSpec((1, gw, Q_TILE), lambda b, g, i: (b, g, i)),
            per_group((n_cmp, HEAD_DIM)),
            per_group((HEAD_DIM, n_cmp)),
            per_group((seq, HEAD_DIM)),
            per_group((N_KEY_CHUNKS, HEAD_DIM, KEY_CHUNK)),
            per_group((seq, HEAD_DIM)),
            per_group((N_WIN_CHUNKS, HEAD_DIM, WIN_CHUNK)),
            pl.BlockSpec((1, 1, 16, Q_TILE), lambda b, g, i: (b, g, 0, i)),
            pl.BlockSpec(ovl.shape, lambda b, g, i: (0, 0)),
            pl.BlockSpec(expand.shape, lambda b, g, i: (0, 0)),
        ] + [
            pl.BlockSpec((1, HEAD_DIM, Q_TILE),
                         lambda b, g, i, j=j: (b, DSA_HEADS + FOX_HEADS + g * NSA_HPG + j, i))
            for j in range(NSA_HPG)
        ],
        out_specs=pl.BlockSpec((1, gw, Q_TILE), lambda b, g, i: (b, g, i)),
        out_shape=jax.ShapeDtypeStruct((bsz, NSA_W, seq), BF16),
        scratch_shapes=[pltpu.VMEM((N_SEL_BLOCKS, Q_TILE), F32)],
        compiler_params=_cparams(("arbitrary", "arbitrary", "arbitrary")),
        name="nsa_attn",
    )(nq_t, nqr_t, kc, vc_t, ks, vs_t, kw, vw_t, gts, ovl, expand, gate_t, gate_t, gate_t)


def _out_kernel(od_ref, of_ref, on_ref, w_ref, x_ref, gm_ref, lg_ref, lb_ref, o_ref):
    z = jnp.concatenate([od_ref[0], of_ref[0], on_ref[0]], axis=0)
    y = lax.dot_general(z, w_ref[...], (((0,), (0,)), ((), ())),
                        preferred_element_type=F32)
    r = ALPHA * x_ref[0] + (1.0 + gm_ref[0]) * y
    mu = jnp.mean(r, axis=-1, keepdims=True)
    rc = r - mu
    var = jnp.mean(rc * rc, axis=-1, keepdims=True)
    o_ref[0] = rc * lax.rsqrt(var + LN_EPS) * lg_ref[...] + lb_ref[...]


def _output(od_t, of_t, on_t, w_out, x, gmod, ln_g, ln_b):
    bsz, seq, d = x.shape
    tm = PROJ_TOKENS
    feat = lambda rows: pl.BlockSpec((1, rows, tm), lambda b, i: (b, 0, i))
    return pl.pallas_call(
        _out_kernel,
        grid=(bsz, seq // tm),
        in_specs=[
            feat(DSA_W), feat(FOX_W), feat(NSA_W),
            pl.BlockSpec((MIX_W, d), lambda b, i: (0, 0)),
            pl.BlockSpec((1, tm, d), lambda b, i: (b, i, 0)),
            pl.BlockSpec((1, 1, d), lambda b, i: (b, 0, 0)),
            pl.BlockSpec((1, d), lambda b, i: (0, 0)),
            pl.BlockSpec((1, d), lambda b, i: (0, 0)),
        ],
        out_specs=pl.BlockSpec((1, tm, d), lambda b, i: (b, i, 0)),
        out_shape=jax.ShapeDtypeStruct((bsz, seq, d), F32),
        compiler_params=_cparams(("arbitrary", "arbitrary")),
        name="out_proj_ln",
    )(od_t, of_t, on_t, w_out, x, gmod, ln_g, ln_b)


def _split_cols(w):
    out, off = {}, 0
    for name, width in IN_SPLITS:
        out[name] = w[:, off:off + width]
        off += width
    return out


def _prep_w_in(w_in):
    p = _split_cols(w_in)
    d = w_in.shape[0]
    z = lambda n: jnp.zeros((d, n), w_in.dtype)
    gcols = p["nsa_g"].reshape(d, 3, NSA_GROUPS, NSA_HPG)
    gparts = []
    for g in range(NSA_GROUPS):
        gparts += [gcols[:, :, g, :].reshape(d, 3 * NSA_HPG), z(16 - 3 * NSA_HPG)]
    wt = jnp.concatenate(
        [p["dsa_q"], p["idx_q"], p["fox_q"], p["nsa_q"],
         p["dsa_v"], z(HEAD_DIM), p["fox_v"], p["nsa_vs"], p["nsa_vw"], p["gate"],
         p["idx_w"], p["fox_f"], z(8 - FOX_HEADS)] + gparts, axis=1)
    assert wt.shape[1] == T_ROWS
    ws = jnp.concatenate(
        [p["dsa_k"], z(LANES - HEAD_DIM), p["idx_k"], z(LANES - IDX_DIM), p["fox_k"],
         p["nsa_ks"], p["nsa_kw"], p["nsa_kc"], p["nsa_vc"]], axis=1)
    assert ws.shape[1] == S_COLS
    return wt.T.astype(BF16), ws.astype(BF16)


def _rope_tables(seq):
    pos = jnp.arange(seq, dtype=F32)
    feature_major, token_major = [], []
    lane = jnp.arange(LANES)
    for hd in (HEAD_DIM, IDX_DIM):
        half = hd // 2
        inv = ROPE_THETA ** (-jnp.arange(half, dtype=F32) / half)
        ang = inv[:, None] * pos[None, :]
        cos, sin = jnp.cos(ang), jnp.sin(ang)
        feature_major += [cos, sin]
        sign = jnp.where((lane % hd) < half, -1.0, 1.0).astype(F32)
        reps = LANES // half
        token_major += [jnp.tile(cos.T, (1, reps)), jnp.tile(sin.T, (1, reps)) * sign[None, :]]
    return tuple(feature_major + token_major)


def _prep_compress(cmp_pe, cmp_w1, cmp_w2):
    assert NSA_GROUPS == 2

    def block_diag(w):
        z = jnp.zeros_like(w)
        return jnp.concatenate([jnp.concatenate([w, z], axis=-1), jnp.concatenate([z, w], axis=-1)], axis=-2)

    big1 = block_diag(cmp_w1.astype(BF16).reshape(2, CMP_LEN, HEAD_DIM, HEAD_DIM))
    w1lo = big1[:, :CMP_STRIDE].reshape(2, CMP_STRIDE * LANES, LANES)
    w1hi = big1[:, CMP_STRIDE:].reshape(2, CMP_STRIDE * LANES, LANES)
    w2 = block_diag(cmp_w2.astype(BF16))
    pe2 = jnp.concatenate([cmp_pe] * NSA_GROUPS, axis=-1)
    return pe2[:, :CMP_STRIDE], pe2[:, CMP_STRIDE:], w1lo, w1hi, w2


def _selection_constants(n_cmp):
    cstart = jnp.arange(n_cmp) * CMP_STRIDE
    bstart = jnp.arange(N_SEL_BLOCKS) * SEL_BLOCK
    real = (jnp.arange(n_cmp) < (SEQ - CMP_LEN) // CMP_STRIDE + 1)[:, None]
    ovl = ((cstart[:, None] < bstart[None, :] + SEL_BLOCK) & (cstart[:, None] + CMP_LEN > bstart[None, :]) & real)
    ovl_t = jnp.concatenate([ovl.T.astype(BF16)] * NSA_HPG, axis=1)
    expand = (jnp.arange(KEY_CHUNK)[:, None] // SEL_BLOCK == jnp.arange(KEY_CHUNK // SEL_BLOCK)[None, :]).astype(F32)
    return ovl_t, expand


def kernel(x, c, w_ada, b_ada, w_in, b_f, cmp_pe, cmp_w1, cmp_w2, w_out, ln_g, ln_b):
    bsz, seq, d = x.shape
    assert (seq, d) == (SEQ, D_MODEL)
    mod = _modulation(c, w_ada, b_ada)
    tabs = _rope_tables(seq)
    ovl, expand = _selection_constants(seq // CMP_STRIDE)
    for l in range(DEPTH):
        shift = mod[l, :, :d].reshape(bsz, 1, d)
        scale = mod[l, :, d:2 * d].reshape(bsz, 1, d)
        gmod = mod[l, :, 2 * d:].reshape(bsz, 1, d)
        wt, ws = _prep_w_in(w_in[l])
        bf8 = jnp.concatenate([b_f[l], jnp.zeros((8 - FOX_HEADS,), F32)])
        (dq, iq, fq, nq, nqr, dv, fv, vs, vw, gate, iw, lft, gts,
         dk, ik, fk, ks, kw, kcin, vcin) = _projection(
            x, scale, shift, wt, ws, tabs, bf8.reshape(8, 1))
        cum_t, cum_s = _forget_cumsum(lft)
        kc, vc_t = _compress(kcin, vcin, *_prep_compress(cmp_pe[l], cmp_w1[l], cmp_w2[l]))
        o_dsa = _dsa(iq, ik, iw, dq, dk, dv, gate)
        o_fox = _fox(fq, fk, fv, cum_t, cum_s, gate)
        o_nsa = _nsa(nq, nqr, kc, vc_t, ks, vs, kw, vw, gts, ovl, expand, gate)
        x = _output(o_dsa, o_fox, o_nsa, w_out[l].astype(BF16), x, gmod,
                    ln_g[l].reshape(1, d), ln_b[l].reshape(1, d))
    return x
```

```python
import jax
import jax.numpy as jnp
import numpy as np
from jax import lax
from jax.experimental import pallas as pl
from jax.experimental.pallas import tpu as pltpu

F32 = jnp.float32
BF16 = jnp.bfloat16
I32 = jnp.int32

D_MODEL = 1024
SEQ = 2048
DEPTH = 2
HEAD_DIM = 64
DSA_HEADS = 4
DSA_TOPK = 256
IDX_HEADS = 8
IDX_DIM = 32
FOX_HEADS = 6
NSA_HEADS = 6
NSA_GROUPS = 2
NSA_HPG = NSA_HEADS // NSA_GROUPS
CMP_LEN = 32
CMP_STRIDE = 16
SEL_BLOCK = 64
SEL_N = 16
N_SEL_BLOCKS = SEQ // SEL_BLOCK
WINDOW = 512
ROPE_THETA = 10000.0
LN_EPS = 1e-5
ALPHA = (2.0 * DEPTH) ** 0.25

DSA_W = DSA_HEADS * HEAD_DIM
FOX_W = FOX_HEADS * HEAD_DIM
NSA_W = NSA_HEADS * HEAD_DIM
NSA_KV_W = NSA_GROUPS * HEAD_DIM
MIX_W = DSA_W + FOX_W + NSA_W
IDX_W = IDX_HEADS * IDX_DIM

IN_SPLITS = (
    ("dsa_q", DSA_W), ("dsa_k", HEAD_DIM), ("dsa_v", HEAD_DIM),
    ("idx_q", IDX_W), ("idx_k", IDX_DIM), ("idx_w", IDX_HEADS),
    ("fox_q", FOX_W), ("fox_k", FOX_W), ("fox_v", FOX_W), ("fox_f", FOX_HEADS),
    ("nsa_q", NSA_W),
    ("nsa_kc", NSA_KV_W), ("nsa_vc", NSA_KV_W),
    ("nsa_ks", NSA_KV_W), ("nsa_vs", NSA_KV_W),
    ("nsa_kw", NSA_KV_W), ("nsa_vw", NSA_KV_W),
    ("nsa_g", 3 * NSA_HEADS),
    ("gate", MIX_W),
)

LANES = 128
KEY_CHUNK = 512
PROJ_TOKENS = 512
Q_TILE = 256
DSA_Q_TILE = 512
FOX_Q_TILE = 512
N_KEY_CHUNKS = SEQ // KEY_CHUNK
WIN_CHUNK = 256
N_WIN_CHUNKS = SEQ // WIN_CHUNK
WIN_SPAN = WINDOW + Q_TILE
CUM_CHUNK = 256
VMEM_LIMIT = 56 * 1024 * 1024

FOLD_ROWS = 32
ONES_ROWS = 16
LOG2E = 1.4426950408889634
NEG = -(2.0 ** 100)
INT_MIN = -(2 ** 31)
CODE_NEG_INF = 0x007FFFFF

T_DQ, T_IQ, T_FQ, T_NQ = 0, 256, 512, 896
T_DV, T_FV, T_VS, T_VW = 1280, 1408, 1792, 1920
T_GATE, T_SMALL, T_ROWS = 2048, 3072, 3120
S_DK, S_IK, S_FK, S_KS, S_KW, S_KC, S_VC, S_COLS = 0, 128, 256, 640, 768, 896, 1024, 1152


def _cparams(sem):
    return pltpu.CompilerParams(dimension_semantics=sem, vmem_limit_bytes=VMEM_LIMIT)


def _dot(a, b):
    return jnp.dot(a, b, preferred_element_type=F32)


def _dot_nt(a, b):
    return lax.dot_general(a, b, (((1,), (1,)), ((), ())), preferred_element_type=F32)


def _log_sigmoid(x):
    return jnp.minimum(x, 0.0) - jnp.log(1.0 + jnp.exp(-jnp.abs(x)))


def _sigmoid(x):
    return 1.0 / (1.0 + jnp.exp(-x))


def _mod_kernel(c_ref, w_ref, b_ref, o_ref):
    o_ref[0] = _dot(c_ref[...].astype(BF16), w_ref[0].astype(BF16)) + b_ref[0]


def _modulation(c, w_ada, b_ada):
    depth, d, d3 = w_ada.shape
    bsz = c.shape[0]
    return pl.pallas_call(
        _mod_kernel,
        grid=(depth, d3 // d),
        in_specs=[
            pl.BlockSpec((bsz, d), lambda l, j: (0, 0)),
            pl.BlockSpec((1, d, d), lambda l, j: (l, 0, j)),
            pl.BlockSpec((1, 1, d), lambda l, j: (l, 0, j)),
        ],
        out_specs=pl.BlockSpec((1, bsz, d), lambda l, j: (l, 0, j)),
        out_shape=jax.ShapeDtypeStruct((depth, bsz, d3), F32),
        compiler_params=_cparams(("arbitrary", "arbitrary")),
        name="adaln_mod",
    )(c, w_ada, b_ada.reshape(depth, 1, d3))


def _proj_kernel(x_ref, sc_ref, sh_ref, wt_ref, ws_ref,
                 ct64_ref, st64_ref, ct32_ref, st32_ref,
                 ck64_ref, sk64_ref, ck32_ref, sk32_ref, bfc_ref,
                 dq_ref, iq_ref, fq_ref, nq_ref, nqr_ref,
                 dv_ref, fv_ref, vs_ref, vw_ref, gate_ref,
                 iw_ref, lft_ref, gts_ref,
                 dk_ref, ik_ref, fk_ref, ks_ref, kw_ref, kcin_ref, vcin_ref):
    tm = x_ref.shape[1]
    u = (x_ref[0] * (1.0 + sc_ref[0]) + sh_ref[0]).astype(BF16)

    ht_all = _dot_nt(wt_ref[...], u)

    def proj_t(r0, r1):
        return ht_all[r0:r1]

    def rope_t(h, n_heads, hd, c, s):
        half = hd // 2
        out = []
        for hh in range(n_heads):
            x1 = h[hh * hd:hh * hd + half]
            x2 = h[hh * hd + half:(hh + 1) * hd]
            out.append((hh * hd, x1 * c - x2 * s))
            out.append((hh * hd + half, x1 * s + x2 * c))
        return out

    c64, s64 = ct64_ref[...], st64_ref[...]
    c32, s32 = ct32_ref[...], st32_ref[...]
    qscale = HEAD_DIM ** -0.5 * LOG2E

    h = proj_t(T_DQ, T_DQ + DSA_W)
    for r, v in rope_t(h, DSA_HEADS, HEAD_DIM, c64, s64):
        dq_ref[0, r:r + HEAD_DIM // 2, :] = (v * qscale).astype(BF16)
    h = proj_t(T_IQ, T_IQ + IDX_W)
    for r, v in rope_t(h, IDX_HEADS, IDX_DIM, c32, s32):
        iq_ref[0, r:r + IDX_DIM // 2, :] = v.astype(BF16)
    fq_ref[0] = (proj_t(T_FQ, T_FQ + FOX_W) * qscale).astype(BF16)
    h = proj_t(T_NQ, T_NQ + NSA_W)
    nq_ref[0] = (h * qscale).astype(BF16)
    for r, v in rope_t(h, NSA_HEADS, HEAD_DIM, c64, s64):
        nqr_ref[0, r:r + HEAD_DIM // 2, :] = (v * qscale).astype(BF16)

    dv_ref[0, 0] = proj_t(T_DV, T_DV + HEAD_DIM).astype(BF16)
    h = proj_t(T_FV, T_FV + FOX_W)
    for hh in range(FOX_HEADS):
        fv_ref[0, hh, 0] = h[hh * HEAD_DIM:(hh + 1) * HEAD_DIM].astype(BF16)
    h = proj_t(T_VS, T_VS + NSA_KV_W)
    for g in range(NSA_GROUPS):
        vs_ref[0, g, 0] = h[g * HEAD_DIM:(g + 1) * HEAD_DIM].astype(BF16)
    h = proj_t(T_VW, T_VW + NSA_KV_W)
    for g in range(NSA_GROUPS):
        for j in range(tm // WIN_CHUNK):
            vw_ref[0, g, j] = h[g * HEAD_DIM:(g + 1) * HEAD_DIM, j * WIN_CHUNK:(j + 1) * WIN_CHUNK].astype(BF16)

    for r0 in range(0, MIX_W, 256):
        h = proj_t(T_GATE + r0, T_GATE + r0 + 256)
        gate_ref[0, r0:r0 + 256, :] = h * _sigmoid(h)

    h = proj_t(T_SMALL, T_ROWS)
    iw_ref[0] = h[0:8] * (IDX_HEADS ** -0.5)
    lft_ref[0] = _log_sigmoid(h[8:16] + bfc_ref[...])
    g_all = _sigmoid(h[16:48])
    gts_ref[0, 0] = g_all[0:16]
    gts_ref[0, 1] = g_all[16:32]

    lane = lax.broadcasted_iota(I32, (tm, LANES), 1)

    def rope_s(g, half, c, s_signed):
        first = (lane & (2 * half - 1)) < half
        sw = jnp.where(first, pltpu.roll(g, LANES - half, 1), pltpu.roll(g, half, 1))
        return g * c + sw * s_signed

    hs_all = _dot(u, ws_ref[...])

    def proj_s(c0, c1):
        return hs_all[:, c0:c1]

    ck64, sk64 = ck64_ref[...], sk64_ref[...]
    g = rope_s(proj_s(S_DK, S_DK + LANES), HEAD_DIM // 2, ck64, sk64)
    dk_ref[0] = g[:, :HEAD_DIM].astype(BF16)
    g = rope_s(proj_s(S_IK, S_IK + LANES), IDX_DIM // 2, ck32_ref[...], sk32_ref[...])
    ik_ref[0] = g[:, :IDX_DIM].astype(BF16)
    g = proj_s(S_FK, S_FK + FOX_W)
    for hh in range(FOX_HEADS):
        fk_ref[0, hh] = g[:, hh * HEAD_DIM:(hh + 1) * HEAD_DIM].astype(BF16)
    g = rope_s(proj_s(S_KS, S_KS + LANES), HEAD_DIM // 2, ck64, sk64)
    for gg in range(NSA_GROUPS):
        ks_ref[0, gg] = g[:, gg * HEAD_DIM:(gg + 1) * HEAD_DIM].astype(BF16)
    g = rope_s(proj_s(S_KW, S_KW + LANES), HEAD_DIM // 2, ck64, sk64)
    for gg in range(NSA_GROUPS):
        kw_ref[0, gg] = g[:, gg * HEAD_DIM:(gg + 1) * HEAD_DIM].astype(BF16)
    kcin_ref[0] = proj_s(S_KC, S_KC + LANES)
    vcin_ref[0] = proj_s(S_VC, S_VC + LANES)


def _projection(x, scale, shift, wt, ws, tabs, bfc):
    bsz, seq, d = x.shape
    tm = PROJ_TOKENS
    nt = seq // tm
    ct64, st64, ct32, st32, ck64, sk64, ck32, sk32 = tabs
    const = lambda shape: pl.BlockSpec(shape, lambda b, i: (0,) * len(shape))
    feat = lambda rows: pl.BlockSpec((1, rows, tm), lambda b, i: (b, 0, i))
    in_specs = [
        pl.BlockSpec((1, tm, d), lambda b, i: (b, i, 0)),
        pl.BlockSpec((1, 1, d), lambda b, i: (b, 0, 0)),
        pl.BlockSpec((1, 1, d), lambda b, i: (b, 0, 0)),
        const((T_ROWS, d)),
        const((d, S_COLS)),
        pl.BlockSpec((HEAD_DIM // 2, tm), lambda b, i: (0, i)),
        pl.BlockSpec((HEAD_DIM // 2, tm), lambda b, i: (0, i)),
        pl.BlockSpec((IDX_DIM // 2, tm), lambda b, i: (0, i)),
        pl.BlockSpec((IDX_DIM // 2, tm), lambda b, i: (0, i)),
        pl.BlockSpec((tm, LANES), lambda b, i: (i, 0)),
        pl.BlockSpec((tm, LANES), lambda b, i: (i, 0)),
        pl.BlockSpec((tm, LANES), lambda b, i: (i, 0)),
        pl.BlockSpec((tm, LANES), lambda b, i: (i, 0)),
        const((8, 1)),
    ]
    sds = jax.ShapeDtypeStruct
    out_shape = [
        sds((bsz, DSA_W, seq), BF16), sds((bsz, IDX_W, seq), BF16), sds((bsz, FOX_W, seq), BF16),
        sds((bsz, NSA_W, seq), BF16), sds((bsz, NSA_W, seq), BF16),
        sds((bsz, N_KEY_CHUNKS, HEAD_DIM, KEY_CHUNK), BF16),
        sds((bsz, FOX_HEADS, N_KEY_CHUNKS, HEAD_DIM, KEY_CHUNK), BF16),
        sds((bsz, NSA_GROUPS, N_KEY_CHUNKS, HEAD_DIM, KEY_CHUNK), BF16),
        sds((bsz, NSA_GROUPS, N_WIN_CHUNKS, HEAD_DIM, WIN_CHUNK), BF16),
        sds((bsz, MIX_W, seq), F32),
        sds((bsz, 8, seq), F32), sds((bsz, 8, seq), F32), sds((bsz, NSA_GROUPS, 16, seq), F32),
        sds((bsz, seq, HEAD_DIM), BF16), sds((bsz, seq, IDX_DIM), BF16),
        sds((bsz, FOX_HEADS, seq, HEAD_DIM), BF16),
        sds((bsz, NSA_GROUPS, seq, HEAD_DIM), BF16), sds((bsz, NSA_GROUPS, seq, HEAD_DIM), BF16),
        sds((bsz, seq, LANES), F32), sds((bsz, seq, LANES), F32),
    ]
    wpc = tm // WIN_CHUNK
    out_specs = [
        feat(DSA_W), feat(IDX_W), feat(FOX_W), feat(NSA_W), feat(NSA_W),
        pl.BlockSpec((1, 1, HEAD_DIM, KEY_CHUNK), lambda b, i: (b, i, 0, 0)),
        pl.BlockSpec((1, FOX_HEADS, 1, HEAD_DIM, KEY_CHUNK), lambda b, i: (b, 0, i, 0, 0)),
        pl.BlockSpec((1, NSA_GROUPS, 1, HEAD_DIM, KEY_CHUNK), lambda b, i: (b, 0, i, 0, 0)),
        pl.BlockSpec((1, NSA_GROUPS, wpc, HEAD_DIM, WIN_CHUNK), lambda b, i: (b, 0, i, 0, 0)),
        feat(MIX_W),
        feat(8), feat(8),
        pl.BlockSpec((1, NSA_GROUPS, 16, tm), lambda b, i: (b, 0, 0, i)),
        pl.BlockSpec((1, tm, HEAD_DIM), lambda b, i: (b, i, 0)),
        pl.BlockSpec((1, tm, IDX_DIM), lambda b, i: (b, i, 0)),
        pl.BlockSpec((1, FOX_HEADS, tm, HEAD_DIM), lambda b, i: (b, 0, i, 0)),
        pl.BlockSpec((1, NSA_GROUPS, tm, HEAD_DIM), lambda b, i: (b, 0, i, 0)),
        pl.BlockSpec((1, NSA_GROUPS, tm, HEAD_DIM), lambda b, i: (b, 0, i, 0)),
        pl.BlockSpec((1, tm, LANES), lambda b, i: (b, i, 0)),
        pl.BlockSpec((1, tm, LANES), lambda b, i: (b, i, 0)),
    ]
    assert tm == KEY_CHUNK
    return pl.pallas_call(
        _proj_kernel,
        grid=(bsz, nt),
        in_specs=in_specs,
        out_specs=out_specs,
        out_shape=out_shape,
        compiler_params=_cparams(("arbitrary", "arbitrary")),
        name="in_proj",
    )(x, scale, shift, wt, ws, ct64, st64, ct32, st32, ck64, sk64, ck32, sk32, bfc)


def _cumsum_kernel(lft_ref, cumt_ref, cums_ref):
    seq = lft_ref.shape[2]
    r = lax.broadcasted_iota(I32, (CUM_CHUNK, CUM_CHUNK), 0)
    c = lax.broadcasted_iota(I32, (CUM_CHUNK, CUM_CHUNK), 1)
    tri_u = (r <= c).astype(F32)
    blocks = [slice(k * CUM_CHUNK, (k + 1) * CUM_CHUNK) for k in range(seq // CUM_CHUNK)]
    local = [jnp.dot(lft_ref[0, :, sl], tri_u, preferred_element_type=F32, precision=lax.Precision.HIGHEST)
             for sl in blocks]
    carry = jnp.zeros((8, 1), F32)
    pad = jnp.zeros((LANES - 8, CUM_CHUNK), F32)
    for k, sl in enumerate(blocks):
        ct = (local[k] + carry) * LOG2E
        carry = local[k][:, CUM_CHUNK - 1:CUM_CHUNK] + carry
        for head in range(8):
            cumt_ref[0, head, :, sl] = ct[head:head + 1, :]
        cums_ref[0, sl, :] = jnp.concatenate([ct, pad], axis=0).T[:, :8]


def _forget_cumsum(lft):
    bsz, _, seq = lft.shape
    return pl.pallas_call(
        _cumsum_kernel,
        grid=(bsz,),
        in_specs=[pl.BlockSpec((1, 8, seq), lambda b: (b, 0, 0))],
        out_specs=[pl.BlockSpec((1, 8, 1, seq), lambda b: (b, 0, 0, 0)),
                   pl.BlockSpec((1, seq, 8), lambda b: (b, 0, 0))],
        out_shape=[jax.ShapeDtypeStruct((bsz, 8, 1, seq), F32),
                   jax.ShapeDtypeStruct((bsz, seq, 8), F32)],
        compiler_params=_cparams(("arbitrary",)),
        name="forget_cumsum",
    )(lft)


def _compress_kernel(kcin_ref, vcin_ref, pelo_ref, pehi_ref, w1lo_ref, w1hi_ref, w2_ref,
                     kc_ref, vct_ref):
    n_blk = kcin_ref.shape[1] // CMP_STRIDE
    for kv, src in enumerate((kcin_ref, vcin_ref)):
        lo, hi = [], []
        for j in range(CMP_STRIDE):
            piece = src[0, pl.ds(j, n_blk, stride=CMP_STRIDE), :]
            lo.append((piece + pelo_ref[kv, j:j + 1, :]).astype(BF16))
            hi.append((piece + pehi_ref[kv, j:j + 1, :]).astype(BF16))
        a = _dot(jnp.concatenate(lo, axis=1), w1lo_ref[kv])
        b = _dot(jnp.concatenate(hi, axis=1), w1hi_ref[kv])
        pre = a + pltpu.roll(b, n_blk - 1, 0)
        act = pre * _sigmoid(pre)
        out = _dot(act.astype(BF16), w2_ref[kv])
        if kv == 0:
            for g in range(NSA_GROUPS):
                kc_ref[0, g] = out[:, g * HEAD_DIM:(g + 1) * HEAD_DIM].astype(BF16)
        else:
            out_t = out.T
            for g in range(NSA_GROUPS):
                vct_ref[0, g] = out_t[g * HEAD_DIM:(g + 1) * HEAD_DIM].astype(BF16)


def _compress(kcin, vcin, pelo, pehi, w1lo, w1hi, w2):
    bsz, seq, _ = kcin.shape
    n_blk = seq // CMP_STRIDE
    full = lambda a: pl.BlockSpec(a.shape, lambda b: (0,) * a.ndim)
    return pl.pallas_call(
        _compress_kernel,
        grid=(bsz,),
        in_specs=[pl.BlockSpec((1, seq, LANES), lambda b: (b, 0, 0)),
                  pl.BlockSpec((1, seq, LANES), lambda b: (b, 0, 0)),
                  full(pelo), full(pehi), full(w1lo), full(w1hi), full(w2)],
        out_specs=[pl.BlockSpec((1, NSA_GROUPS, n_blk, HEAD_DIM), lambda b: (b, 0, 0, 0)),
                   pl.BlockSpec((1, NSA_GROUPS, HEAD_DIM, n_blk), lambda b: (b, 0, 0, 0))],
        out_shape=[jax.ShapeDtypeStruct((bsz, NSA_GROUPS, n_blk, HEAD_DIM), BF16),
                   jax.ShapeDtypeStruct((bsz, NSA_GROUPS, HEAD_DIM, n_blk), BF16)],
        compiler_params=_cparams(("arbitrary",)),
        name="nsa_compress",
    )(kcin, vcin, pelo, pehi, w1lo, w1hi, w2)


def _fold_rows(x, op):
    k, n = x.shape
    return op(x.reshape(k // FOLD_ROWS, FOLD_ROWS, n), axis=0)


def _reduce_rows(x, op):
    if x.shape[0] % FOLD_ROWS == 0 and x.shape[0] > FOLD_ROWS:
        x = _fold_rows(x, op)
    return op(x, axis=0, keepdims=True)


def _with_ones(v_t):
    return jnp.concatenate([v_t, jnp.ones((ONES_ROWS, v_t.shape[1]), v_t.dtype)], axis=0)


def _softmax_step(carry, s, v_aug):
    m, acc = carry
    m_new = jnp.maximum(m, _reduce_rows(s, jnp.max))
    alpha = jnp.exp2(m - m_new)
    p = jnp.exp2(s - m_new)
    acc = alpha * acc + _dot(v_aug, p.astype(BF16))
    return m_new, acc


def _softmax_init(n):
    return (jnp.full((1, n), NEG, F32), jnp.zeros((HEAD_DIM + ONES_ROWS, n), F32))


def _softmax_finish(carry):
    _, acc = carry
    return acc[:HEAD_DIM] * (1.0 / acc[HEAD_DIM:HEAD_DIM + 1])


def _tile_lanes(a, n):
    return jnp.concatenate([a] * n, axis=1)


def _dsa_kernel(iq_ref, ik_ref, iw_ref, dq_ref, dk_ref, dv_ref, g_ref, o_ref, key_ref):
    i = pl.program_id(1)
    tq = DSA_Q_TILE
    n_chunks = i + 1
    tpos = i * tq + lax.broadcasted_iota(I32, (1, tq), 1)
    srow = lax.broadcasted_iota(I32, (KEY_CHUNK, tq), 0)
    iw = iw_ref[0]

    def index_chunk(c, _):
        k0 = pl.multiple_of(c * KEY_CHUNK, KEY_CHUNK)
        ikc = ik_ref[0, pl.ds(k0, KEY_CHUNK), :]
        acc = jnp.zeros((KEY_CHUNK, tq), F32)
        for h in range(IDX_HEADS):
            x = _dot(ikc, iq_ref[0, h * IDX_DIM:(h + 1) * IDX_DIM, :])
            acc = acc + iw[h:h + 1, :] * jnp.maximum(x, 0.0)
        key_ref[pl.ds(k0, KEY_CHUNK), :] = jnp.where(srow + k0 <= tpos, acc, -jnp.inf)
        return 0

    lax.fori_loop(0, n_chunks, index_chunk, 0)

    def count(pred):
        def body(c, acc):
            k0 = pl.multiple_of(c * KEY_CHUNK, KEY_CHUNK)
            hit = pred(key_ref[pl.ds(k0, KEY_CHUNK), :], srow + k0)
            return acc + _fold_rows(jnp.where(hit, 1, 0), jnp.sum)
        acc = lax.fori_loop(0, n_chunks, body, jnp.zeros((FOLD_ROWS, tq), I32))
        return jnp.sum(acc, axis=0, keepdims=True)

    def decode(code):
        skey = code ^ INT_MIN
        val = lax.bitcast_convert_type(jnp.where(skey < 0, skey ^ 0x7FFFFFFF, skey), F32)
        return jnp.where(jnp.logical_and(code >= 0, code <= CODE_NEG_INF), -jnp.inf, val)

    def value_bit(it, code):
        cand = code | lax.shift_left(jnp.int32(1), 31 - it)
        cand_f = decode(cand)
        cnt = count(lambda key, spos: key >= cand_f)
        return jnp.where(cnt >= DSA_TOPK, cand, code)

    thr = decode(lax.fori_loop(0, 32, value_bit, jnp.zeros((1, tq), I32)))

    need = (DSA_TOPK - count(lambda key, spos: key > thr)).astype(F32)
    r = lax.broadcasted_iota(I32, (KEY_CHUNK, KEY_CHUNK), 0)
    c = lax.broadcasted_iota(I32, (KEY_CHUNK, KEY_CHUNK), 1)
    prefix = jnp.where(c <= r, 1.0, 0.0).astype(BF16)

    def attend(c, carry):
        heads, ties_before = carry
        k0 = pl.multiple_of(c * KEY_CHUNK, KEY_CHUNK)
        key = key_ref[pl.ds(k0, KEY_CHUNK), :]
        tied = key == thr
        rank = _dot(prefix, jnp.where(tied, 1.0, 0.0).astype(BF16)) + ties_before
        keep = jnp.logical_or(key > thr, jnp.logical_and(tied, rank <= need))
        bias = jnp.where(jnp.logical_and(keep, srow + k0 <= tpos), 0.0, NEG)
        kc = dk_ref[0, pl.ds(k0, KEY_CHUNK), :]
        v_aug = _with_ones(dv_ref[0, c])
        heads = tuple(
            _softmax_step(heads[h], _dot(kc, dq_ref[0, h * HEAD_DIM:(h + 1) * HEAD_DIM, :]) + bias, v_aug)
            for h in range(DSA_HEADS))
        return heads, rank[KEY_CHUNK - 1:KEY_CHUNK, :]

    init = tuple(_softmax_init(tq) for _ in range(DSA_HEADS))
    heads, _ = lax.fori_loop(0, n_chunks, attend, (init, jnp.zeros((1, tq), F32)))
    for h, carry in enumerate(heads):
        rows = slice(h * HEAD_DIM, (h + 1) * HEAD_DIM)
        o_ref[0, rows, :] = (_softmax_finish(carry) * g_ref[0, rows, :]).astype(BF16)


def _dsa(iq_t, ik, iw_t, dq_t, dk, dv_t, gate_t):
    bsz, _, seq = dq_t.shape
    tq = DSA_Q_TILE
    assert tq == KEY_CHUNK
    return pl.pallas_call(
        _dsa_kernel,
        grid=(bsz, seq // tq),
        in_specs=[
            pl.BlockSpec((1, IDX_W, tq), lambda b, i: (b, 0, i)),
            pl.BlockSpec((1, seq, IDX_DIM), lambda b, i: (b, 0, 0)),
            pl.BlockSpec((1, 8, tq), lambda b, i: (b, 0, i)),
            pl.BlockSpec((1, DSA_W, tq), lambda b, i: (b, 0, i)),
            pl.BlockSpec((1, seq, HEAD_DIM), lambda b, i: (b, 0, 0)),
            pl.BlockSpec((1, N_KEY_CHUNKS, HEAD_DIM, KEY_CHUNK), lambda b, i: (b, 0, 0, 0)),
            pl.BlockSpec((1, DSA_W, tq), lambda b, i: (b, 0, i)),
        ],
        out_specs=pl.BlockSpec((1, DSA_W, tq), lambda b, i: (b, 0, i)),
        out_shape=jax.ShapeDtypeStruct((bsz, DSA_W, seq), BF16),
        scratch_shapes=[pltpu.VMEM((seq, tq), F32)],
        compiler_params=_cparams(("arbitrary", "arbitrary")),
        name="dsa_attn",
    )(iq_t, ik, iw_t, dq_t, dk, dv_t, gate_t)


def _fox_kernel(q_ref, k_ref, v_ref, cumt_ref, cums_ref, g_ref, o_ref):
    h = pl.program_id(1)
    tq = FOX_Q_TILE
    n_tiles = q_ref.shape[2] // tq
    head_lane = lax.broadcasted_iota(I32, (KEY_CHUNK, 8), 1) == h
    srow = lax.broadcasted_iota(I32, (KEY_CHUNK, tq), 0)
    tcol = lax.broadcasted_iota(I32, (KEY_CHUNK, tq), 1)
    ccols = []
    for c in range(n_tiles):
        col = jnp.sum(jnp.where(head_lane, cums_ref[0, c * KEY_CHUNK:(c + 1) * KEY_CHUNK, :], 0.0),
                      axis=1, keepdims=True)
        ccols.append(_tile_lanes(jnp.broadcast_to(col, (KEY_CHUNK, LANES)), tq // LANES))
    v_aug = [_with_ones(v_ref[0, 0, c]) for c in range(n_tiles)]
    for i in range(n_tiles):
        q = q_ref[0, :, i * tq:(i + 1) * tq]
        crow = cumt_ref[0, 0, :, i * tq:(i + 1) * tq]
        carry = _softmax_init(tq)
        for c in range(i + 1):
            s = _dot(k_ref[0, 0, c * KEY_CHUNK:(c + 1) * KEY_CHUNK, :], q) + crow - ccols[c]
            if c == i:
                s = jnp.where(srow <= tcol, s, NEG)
            carry = _softmax_step(carry, s, v_aug[c])
        cols = slice(i * tq, (i + 1) * tq)
        o_ref[0, :, cols] = (_softmax_finish(carry) * g_ref[0, :, cols]).astype(BF16)


def _fox(fq_t, fk, fv_t, cum_t, cum_s, gate_t):
    bsz, _, seq = fq_t.shape
    assert FOX_Q_TILE == KEY_CHUNK
    return pl.pallas_call(
        _fox_kernel,
        grid=(bsz, FOX_HEADS),
        in_specs=[
            pl.BlockSpec((1, HEAD_DIM, seq), lambda b, h: (b, h, 0)),
            pl.BlockSpec((1, 1, seq, HEAD_DIM), lambda b, h: (b, h, 0, 0)),
            pl.BlockSpec((1, 1, N_KEY_CHUNKS, HEAD_DIM, KEY_CHUNK), lambda b, h: (b, h, 0, 0, 0)),
            pl.BlockSpec((1, 1, 1, seq), lambda b, h: (b, h, 0, 0)),
            pl.BlockSpec((1, seq, 8), lambda b, h: (b, 0, 0)),
            pl.BlockSpec((1, HEAD_DIM, seq), lambda b, h: (b, DSA_HEADS + h, 0)),
        ],
        out_specs=pl.BlockSpec((1, HEAD_DIM, seq), lambda b, h: (b, h, 0)),
        out_shape=jax.ShapeDtypeStruct((bsz, FOX_W, seq), BF16),
        compiler_params=_cparams(("arbitrary", "arbitrary")),
        name="fox_attn",
    )(fq_t, fk, fv_t, cum_t, cum_s, gate_t)


def _nsa_kernel(q_ref, qr_ref, kc_ref, vct_ref, ks_ref, vs_ref, kw_ref, vw_ref, g_ref,
                ovl_ref, expand_ref, og0_ref, og1_ref, og2_ref, o_ref, sel_ref):
    i = pl.program_id(2)
    nq = NSA_HPG * Q_TILE
    t0 = i * Q_TILE
    tpos = t0 + lax.broadcasted_iota(I32, (1, Q_TILE), 1)
    q3 = jnp.concatenate([q_ref[0, j * HEAD_DIM:(j + 1) * HEAD_DIM, :] for j in range(NSA_HPG)], axis=1)
    qr3 = jnp.concatenate([qr_ref[0, j * HEAD_DIM:(j + 1) * HEAD_DIM, :] for j in range(NSA_HPG)], axis=1)

    n_cmp = kc_ref.shape[2]
    cend = lax.broadcasted_iota(I32, (n_cmp, Q_TILE), 0) * CMP_STRIDE + (CMP_LEN - 1)
    cbias = _tile_lanes(jnp.where(cend <= tpos, 0.0, NEG), NSA_HPG)
    cvalid = _tile_lanes(jnp.where(cend <= tpos, 1.0, 0.0), NSA_HPG)
    s = _dot(kc_ref[0, 0], q3) + cbias
    m = _reduce_rows(s, jnp.max)
    e = jnp.exp2(s - m) * cvalid
    p_cmp = (e / jnp.maximum(_reduce_rows(e, jnp.sum), 1e-30)).astype(BF16)
    o_cmp = _dot(vct_ref[0, 0], p_cmp)

    p_stack = jnp.concatenate([p_cmp[:, j * Q_TILE:(j + 1) * Q_TILE] for j in range(NSA_HPG)], axis=0)
    score = _dot(ovl_ref[...], p_stack)
    blk = lax.broadcasted_iota(I32, (N_SEL_BLOCKS, Q_TILE), 0)
    cur = lax.shift_right_logical(tpos, 6)
    forced = jnp.logical_or(blk == 0, jnp.logical_or(blk == cur, blk == cur - 1))
    score = jnp.where(forced, jnp.inf, jnp.where(blk > cur, -jnp.inf, score))
    rank = jnp.zeros((N_SEL_BLOCKS, Q_TILE), F32)
    for mth in range(N_SEL_BLOCKS):
        row = score[mth:mth + 1, :]
        ahead = jnp.logical_or(row > score, jnp.logical_and(row == score, blk > mth))
        rank = rank + jnp.where(ahead, 1.0, 0.0)
    sel_ref[...] = jnp.where(rank < SEL_N, 1.0, 0.0)

    n_chunks = (i * Q_TILE + Q_TILE + KEY_CHUNK - 1) // KEY_CHUNK
    srow = lax.broadcasted_iota(I32, (KEY_CHUNK, Q_TILE), 0)
    blocks_per_chunk = KEY_CHUNK // SEL_BLOCK

    def attend_sel(c, carry):
        k0 = pl.multiple_of(c * KEY_CHUNK, KEY_CHUNK)
        b0 = pl.multiple_of(c * blocks_per_chunk, blocks_per_chunk)
        picked = _dot(expand_ref[...], sel_ref[pl.ds(b0, blocks_per_chunk), :])
        ok = jnp.logical_and(picked > 0.5, srow + k0 <= tpos)
        bias = _tile_lanes(jnp.where(ok, 0.0, NEG), NSA_HPG)
        s = _dot(ks_ref[0, 0, pl.ds(k0, KEY_CHUNK), :], qr3) + bias
        return _softmax_step(carry, s, _with_ones(vs_ref[0, 0, c]))

    wc = jnp.maximum(i - WINDOW // Q_TILE, 0) * (Q_TILE // WIN_CHUNK)
    w0 = pl.multiple_of(wc * WIN_CHUNK, WIN_CHUNK)
    kpos = w0 + lax.broadcasted_iota(I32, (WIN_SPAN, Q_TILE), 0)
    wok = jnp.logical_and(kpos <= tpos, kpos > tpos - WINDOW)
    s = _dot(kw_ref[0, 0, pl.ds(w0, WIN_SPAN), :], qr3) + _tile_lanes(jnp.where(wok, 0.0, NEG), NSA_HPG)
    m = _reduce_rows(s, jnp.max)
    e = jnp.exp2(s - m).astype(BF16)
    acc = jnp.zeros((HEAD_DIM + ONES_ROWS, nq), F32)
    for j in range(WIN_SPAN // WIN_CHUNK):
        acc = acc + _dot(_with_ones(vw_ref[0, 0, wc + j]), e[j * WIN_CHUNK:(j + 1) * WIN_CHUNK, :])
    o_win = _softmax_finish((m, acc))

    first = attend_sel(0, _softmax_init(nq))
    o_slc = _softmax_finish(lax.fori_loop(1, n_chunks, attend_sel, first))

    gts = g_ref[0, 0]
    out_gates = (og0_ref, og1_ref, og2_ref)
    for j in range(NSA_HPG):
        sl = slice(j * Q_TILE, (j + 1) * Q_TILE)
        mixed = (gts[j:j + 1, :] * o_cmp[:, sl]
                 + gts[NSA_HPG + j:NSA_HPG + j + 1, :] * o_slc[:, sl]
                 + gts[2 * NSA_HPG + j:2 * NSA_HPG + j + 1, :] * o_win[:, sl])
        o_ref[0, j * HEAD_DIM:(j + 1) * HEAD_DIM, :] = (mixed * out_gates[j][0]).astype(BF16)


def _nsa(nq_t, nqr_t, kc, vc_t, ks, vs_t, kw, vw_t, gts, ovl, expand, gate_t):
    bsz, _, seq = nq_t.shape
    n_cmp = kc.shape[2]
    gw = NSA_HPG * HEAD_DIM
    per_group = lambda shape: pl.BlockSpec((1, 1) + shape, lambda b, g, i: (b, g) + (0,) * len(shape))
    return pl.pallas_call(
        _nsa_kernel,
        grid=(bsz, NSA_GROUPS, seq // Q_TILE),
        in_specs=[
            pl.BlockSpec((1, gw, Q_TILE), lambda b, g, i: (b, g, i)),
            pl.BlockSpec((1, gw, Q_TILE), lambda b, g, i: (b, g, i)),
            per_group((n_cmp, HEAD_DIM)),
            per_group((HEAD_DIM, n_cmp)),
            per_group((seq, HEAD_DIM)),
            per_group((N_KEY_CHUNKS, HEAD_DIM, KEY_CHUNK)),
            per_group((seq, HEAD_DIM)),
            per_group((N_WIN_CHUNKS, HEAD_DIM, WIN_CHUNK)),
            pl.BlockSpec((1, 1, 16, Q_TILE), lambda b, g, i: (b, g, 0, i)),
            pl.BlockSpec(ovl.shape, lambda b, g, i: (0, 0)),
            pl.BlockSpec(expand.shape, lambda b, g, i: (0, 0)),
        ] + [
            pl.BlockSpec((1, HEAD_DIM, Q_TILE),
                         lambda b, g, i, j=j: (b, DSA_HEADS + FOX_HEADS + g * NSA_HPG + j, i))
            for j in range(NSA_HPG)
        ],
        out_specs=pl.BlockSpec((1, gw, Q_TILE), lambda b, g, i: (b, g, i)),
        out_shape=jax.ShapeDtypeStruct((bsz, NSA_W, seq), BF16),
        scratch_shapes=[pltpu.VMEM((N_SEL_BLOCKS, Q_TILE), F32)],
        compiler_params=_cparams(("arbitrary", "arbitrary", "arbitrary")),
        name="nsa_attn",
    )(nq_t, nqr_t, kc, vc_t, ks, vs_t, kw, vw_t, gts, ovl, expand, gate_t, gate_t, gate_t)


def _out_kernel(od_ref, of_ref, on_ref, w_ref, x_ref, gm_ref, lg_ref, lb_ref, o_ref):
    z = jnp.concatenate([od_ref[0], of_ref[0], on_ref[0]], axis=0)
    y = lax.dot_general(z, w_ref[...], (((0,), (0,)), ((), ())),
                        preferred_element_type=F32)
    r = ALPHA * x_ref[0] + (1.0 + gm_ref[0]) * y
    mu = jnp.mean(r, axis=-1, keepdims=True)
    rc = r - mu
    var = jnp.mean(rc * rc, axis=-1, keepdims=True)
    o_ref[0] = rc * lax.rsqrt(var + LN_EPS) * lg_ref[...] + lb_ref[...]


def _output(od_t, of_t, on_t, w_out, x, gmod, ln_g, ln_b):
    bsz, seq, d = x.shape
    tm = PROJ_TOKENS
    feat = lambda rows: pl.BlockSpec((1, rows, tm), lambda b, i: (b, 0, i))
    return pl.pallas_call(
        _out_kernel,
        grid=(bsz, seq // tm),
        in_specs=[
            feat(DSA_W), feat(FOX_W), feat(NSA_W),
            pl.BlockSpec((MIX_W, d), lambda b, i: (0, 0)),
            pl.BlockSpec((1, tm, d), lambda b, i: (b, i, 0)),
            pl.BlockSpec((1, 1, d), lambda b, i: (b, 0, 0)),
            pl.BlockSpec((1, d), lambda b, i: (0, 0)),
            pl.BlockSpec((1, d), lambda b, i: (0, 0)),
        ],
        out_specs=pl.BlockSpec((1, tm, d), lambda b, i: (b, i, 0)),
        out_shape=jax.ShapeDtypeStruct((bsz, seq, d), F32),
        compiler_params=_cparams(("arbitrary", "arbitrary")),
        name="out_proj_ln",
    )(od_t, of_t, on_t, w_out, x, gmod, ln_g, ln_b)


def _split_cols(w):
    out, off = {}, 0
    for name, width in IN_SPLITS:
        out[name] = w[:, off:off + width]
        off += width
    return out


def _prep_w_in(w_in):
    w_in = w_in.astype(BF16)
    p = _split_cols(w_in)
    d = w_in.shape[0]
    z = lambda n: jnp.zeros((d, n), w_in.dtype)
    gcols = p["nsa_g"].reshape(d, 3, NSA_GROUPS, NSA_HPG)
    gparts = []
    for g in range(NSA_GROUPS):
        gparts += [gcols[:, :, g, :].reshape(d, 3 * NSA_HPG), z(16 - 3 * NSA_HPG)]
    wt = jnp.concatenate(
        [p["dsa_q"], p["idx_q"], p["fox_q"], p["nsa_q"],
         p["dsa_v"], z(HEAD_DIM), p["fox_v"], p["nsa_vs"], p["nsa_vw"], p["gate"],
         p["idx_w"], p["fox_f"], z(8 - FOX_HEADS)] + gparts, axis=1)
    assert wt.shape[1] == T_ROWS
    ws = jnp.concatenate(
        [p["dsa_k"], z(LANES - HEAD_DIM), p["idx_k"], z(LANES - IDX_DIM), p["fox_k"],
         p["nsa_ks"], p["nsa_kw"], p["nsa_kc"], p["nsa_vc"]], axis=1)
    assert ws.shape[1] == S_COLS
    return wt.T, ws


def _rope_tables(seq):
    pos = jnp.arange(seq, dtype=F32)
    feature_major, token_major = [], []
    lane = np.arange(LANES)
    for hd in (HEAD_DIM, IDX_DIM):
        half = hd // 2
        inv = ROPE_THETA ** (-jnp.arange(half, dtype=F32) / half)
        ang = inv[:, None] * pos[None, :]
        cos, sin = jnp.cos(ang), jnp.sin(ang)
        feature_major += [cos, sin]
        sign = jnp.asarray(np.where((lane % hd) < half, -1.0, 1.0), F32)
        reps = LANES // half
        token_major += [jnp.tile(cos.T, (1, reps)), jnp.tile(sin.T, (1, reps)) * sign[None, :]]
    return tuple(feature_major + token_major)


def _prep_compress(cmp_pe, cmp_w1, cmp_w2):
    assert NSA_GROUPS == 2

    def block_diag(w):
        z = jnp.zeros_like(w)
        return jnp.concatenate([jnp.concatenate([w, z], axis=-1), jnp.concatenate([z, w], axis=-1)], axis=-2)

    big1 = block_diag(cmp_w1.astype(BF16).reshape(2, CMP_LEN, HEAD_DIM, HEAD_DIM))
    w1lo = big1[:, :CMP_STRIDE].reshape(2, CMP_STRIDE * LANES, LANES)
    w1hi = big1[:, CMP_STRIDE:].reshape(2, CMP_STRIDE * LANES, LANES)
    w2 = block_diag(cmp_w2.astype(BF16))
    pe2 = jnp.concatenate([cmp_pe] * NSA_GROUPS, axis=-1)
    return pe2[:, :CMP_STRIDE], pe2[:, CMP_STRIDE:], w1lo, w1hi, w2


def _selection_constants(n_cmp):
    cstart = np.arange(n_cmp) * CMP_STRIDE
    bstart = np.arange(N_SEL_BLOCKS) * SEL_BLOCK
    real = (np.arange(n_cmp) < (SEQ - CMP_LEN) // CMP_STRIDE + 1)[:, None]
    ovl = ((cstart[:, None] < bstart[None, :] + SEL_BLOCK) & (cstart[:, None] + CMP_LEN > bstart[None, :]) & real)
    ovl_t = np.concatenate([ovl.T.astype(np.float32)] * NSA_HPG, axis=1)
    expand = (np.arange(KEY_CHUNK)[:, None] // SEL_BLOCK == np.arange(KEY_CHUNK // SEL_BLOCK)[None, :])
    return jnp.asarray(ovl_t, BF16), jnp.asarray(expand, F32)


def kernel(x, c, w_ada, b_ada, w_in, b_f, cmp_pe, cmp_w1, cmp_w2, w_out, ln_g, ln_b):
    bsz, seq, d = x.shape
    assert (seq, d) == (SEQ, D_MODEL)
    mod = _modulation(c, w_ada, b_ada)
    tabs = _rope_tables(seq)
    ovl, expand = _selection_constants(seq // CMP_STRIDE)
    for l in range(DEPTH):
        shift = mod[l, :, :d].reshape(bsz, 1, d)
        scale = mod[l, :, d:2 * d].reshape(bsz, 1, d)
        gmod = mod[l, :, 2 * d:].reshape(bsz, 1, d)
        wt, ws = _prep_w_in(w_in[l])
        bf8 = jnp.concatenate([b_f[l], jnp.zeros((8 - FOX_HEADS,), F32)])
        (dq, iq, fq, nq, nqr, dv, fv, vs, vw, gate, iw, lft, gts,
         dk, ik, fk, ks, kw, kcin, vcin) = _projection(
            x, scale, shift, wt, ws, tabs, bf8.reshape(8, 1))
        cum_t, cum_s = _forget_cumsum(lft)
        kc, vc_t = _compress(kcin, vcin, *_prep_compress(cmp_pe[l], cmp_w1[l], cmp_w2[l]))
        o_dsa = _dsa(iq, ik, iw, dq, dk, dv, gate)
        o_fox = _fox(fq, fk, fv, cum_t, cum_s, gate)
        o_nsa = _nsa(nq, nqr, kc, vc_t, ks, vs, kw, vw, gts, ovl, expand, gate)
        x = _output(o_dsa, o_fox, o_nsa, w_out[l].astype(BF16), x, gmod,
                    ln_g[l].reshape(1, d), ln_b[l].reshape(1, d))
    return x
```

```python
import jax
import jax.numpy as jnp
import numpy as np
from jax import lax
from jax.experimental import pallas as pl
from jax.experimental.pallas import tpu as pltpu

F32 = jnp.float32
BF16 = jnp.bfloat16
I32 = jnp.int32

D_MODEL = 1024
SEQ = 2048
DEPTH = 2
HEAD_DIM = 64
DSA_HEADS = 4
DSA_TOPK = 256
IDX_HEADS = 8
IDX_DIM = 32
FOX_HEADS = 6
NSA_HEADS = 6
NSA_GROUPS = 2
NSA_HPG = NSA_HEADS // NSA_GROUPS
CMP_LEN = 32
CMP_STRIDE = 16
SEL_BLOCK = 64
SEL_N = 16
N_SEL_BLOCKS = SEQ // SEL_BLOCK
WINDOW = 512
ROPE_THETA = 10000.0
LN_EPS = 1e-5
ALPHA = (2.0 * DEPTH) ** 0.25

DSA_W = DSA_HEADS * HEAD_DIM
FOX_W = FOX_HEADS * HEAD_DIM
NSA_W = NSA_HEADS * HEAD_DIM
NSA_KV_W = NSA_GROUPS * HEAD_DIM
MIX_W = DSA_W + FOX_W + NSA_W
IDX_W = IDX_HEADS * IDX_DIM

IN_SPLITS = (
    ("dsa_q", DSA_W), ("dsa_k", HEAD_DIM), ("dsa_v", HEAD_DIM),
    ("idx_q", IDX_W), ("idx_k", IDX_DIM), ("idx_w", IDX_HEADS),
    ("fox_q", FOX_W), ("fox_k", FOX_W), ("fox_v", FOX_W), ("fox_f", FOX_HEADS),
    ("nsa_q", NSA_W),
    ("nsa_kc", NSA_KV_W), ("nsa_vc", NSA_KV_W),
    ("nsa_ks", NSA_KV_W), ("nsa_vs", NSA_KV_W),
    ("nsa_kw", NSA_KV_W), ("nsa_vw", NSA_KV_W),
    ("nsa_g", 3 * NSA_HEADS),
    ("gate", MIX_W),
)

LANES = 128
KEY_CHUNK = 512
PROJ_TOKENS = 512
Q_TILE = 256
DSA_Q_TILE = 512
FOX_Q_TILE = 512
N_KEY_CHUNKS = SEQ // KEY_CHUNK
WIN_CHUNK = 256
N_WIN_CHUNKS = SEQ // WIN_CHUNK
WIN_SPAN = WINDOW + Q_TILE
CUM_CHUNK = 256
VMEM_LIMIT = 56 * 1024 * 1024

FOLD_ROWS = 32
ONES_ROWS = 16
LOG2E = 1.4426950408889634
NEG = -(2.0 ** 100)
INT_MIN = -(2 ** 31)
CODE_NEG_INF = 0x007FFFFF

T_DQ, T_IQ, T_FQ, T_NQ = 0, 256, 512, 896
T_DV, T_FV, T_VS, T_VW = 1280, 1408, 1792, 1920
T_GATE, T_SMALL, T_ROWS = 2048, 3072, 3120
S_DK, S_IK, S_FK, S_KS, S_KW, S_KC, S_VC, S_COLS = 0, 128, 256, 640, 768, 896, 1024, 1152


def _cparams(sem):
    return pltpu.CompilerParams(dimension_semantics=sem, vmem_limit_bytes=VMEM_LIMIT)


def _dot(a, b):
    return jnp.dot(a, b, preferred_element_type=F32)


def _dot_nt(a, b):
    return lax.dot_general(a, b, (((1,), (1,)), ((), ())), preferred_element_type=F32)


def _log_sigmoid(x):
    return jnp.minimum(x, 0.0) - jnp.log(1.0 + jnp.exp(-jnp.abs(x)))


def _sigmoid(x):
    return 1.0 / (1.0 + jnp.exp(-x))


def _mod_kernel(c_ref, w_ref, b_ref, o_ref):
    o_ref[0] = _dot(c_ref[...].astype(BF16), w_ref[0].astype(BF16)) + b_ref[0]


def _modulation(c, w_ada, b_ada):
    depth, d, d3 = w_ada.shape
    bsz = c.shape[0]
    return pl.pallas_call(
        _mod_kernel,
        grid=(depth, d3 // d),
        in_specs=[
            pl.BlockSpec((bsz, d), lambda l, j: (0, 0)),
            pl.BlockSpec((1, d, d), lambda l, j: (l, 0, j)),
            pl.BlockSpec((1, 1, d), lambda l, j: (l, 0, j)),
        ],
        out_specs=pl.BlockSpec((1, bsz, d), lambda l, j: (l, 0, j)),
        out_shape=jax.ShapeDtypeStruct((depth, bsz, d3), F32),
        compiler_params=_cparams(("arbitrary", "arbitrary")),
        name="adaln_mod",
    )(c, w_ada, b_ada.reshape(depth, 1, d3))


def _proj_kernel(x_ref, sc_ref, sh_ref, wt_ref, ws_ref,
                 ct64_ref, st64_ref, ct32_ref, st32_ref,
                 ck64_ref, sk64_ref, ck32_ref, sk32_ref, bfc_ref,
                 dq_ref, iq_ref, fq_ref, nq_ref, nqr_ref,
                 dv_ref, fv_ref, vs_ref, vw_ref, gate_ref,
                 iw_ref, lft_ref, gts_ref,
                 dk_ref, ik_ref, fk_ref, ks_ref, kw_ref, kcin_ref, vcin_ref):
    tm = x_ref.shape[1]
    u = (x_ref[0] * (1.0 + sc_ref[0]) + sh_ref[0]).astype(BF16)

    ht_all = _dot_nt(wt_ref[...], u)

    def proj_t(r0, r1):
        return ht_all[r0:r1]

    def rope_t(h, n_heads, hd, c, s):
        half = hd // 2
        out = []
        for hh in range(n_heads):
            x1 = h[hh * hd:hh * hd + half]
            x2 = h[hh * hd + half:(hh + 1) * hd]
            out.append((hh * hd, x1 * c - x2 * s))
            out.append((hh * hd + half, x1 * s + x2 * c))
        return out

    c64, s64 = ct64_ref[...], st64_ref[...]
    c32, s32 = ct32_ref[...], st32_ref[...]
    qscale = HEAD_DIM ** -0.5 * LOG2E

    h = proj_t(T_DQ, T_DQ + DSA_W)
    for r, v in rope_t(h, DSA_HEADS, HEAD_DIM, c64, s64):
        dq_ref[0, r:r + HEAD_DIM // 2, :] = (v * qscale).astype(BF16)
    h = proj_t(T_IQ, T_IQ + IDX_W)
    for r, v in rope_t(h, IDX_HEADS, IDX_DIM, c32, s32):
        iq_ref[0, r:r + IDX_DIM // 2, :] = v.astype(BF16)
    fq_ref[0] = (proj_t(T_FQ, T_FQ + FOX_W) * qscale).astype(BF16)
    h = proj_t(T_NQ, T_NQ + NSA_W)
    nq_ref[0] = (h * qscale).astype(BF16)
    for r, v in rope_t(h, NSA_HEADS, HEAD_DIM, c64, s64):
        nqr_ref[0, r:r + HEAD_DIM // 2, :] = (v * qscale).astype(BF16)

    dv_ref[0, 0] = proj_t(T_DV, T_DV + HEAD_DIM).astype(BF16)
    h = proj_t(T_FV, T_FV + FOX_W)
    for hh in range(FOX_HEADS):
        fv_ref[0, hh, 0] = h[hh * HEAD_DIM:(hh + 1) * HEAD_DIM].astype(BF16)
    h = proj_t(T_VS, T_VS + NSA_KV_W)
    for g in range(NSA_GROUPS):
        vs_ref[0, g, 0] = h[g * HEAD_DIM:(g + 1) * HEAD_DIM].astype(BF16)
    h = proj_t(T_VW, T_VW + NSA_KV_W)
    for g in range(NSA_GROUPS):
        for j in range(tm // WIN_CHUNK):
            vw_ref[0, g, j] = h[g * HEAD_DIM:(g + 1) * HEAD_DIM, j * WIN_CHUNK:(j + 1) * WIN_CHUNK].astype(BF16)

    for r0 in range(0, MIX_W, 256):
        h = proj_t(T_GATE + r0, T_GATE + r0 + 256)
        gate_ref[0, r0:r0 + 256, :] = h * _sigmoid(h)

    h = proj_t(T_SMALL, T_ROWS)
    iw_ref[0] = h[0:8] * (IDX_HEADS ** -0.5)
    lft_ref[0] = _log_sigmoid(h[8:16] + bfc_ref[...])
    g_all = _sigmoid(h[16:48])
    gts_ref[0, 0] = g_all[0:16]
    gts_ref[0, 1] = g_all[16:32]

    lane = lax.broadcasted_iota(I32, (tm, LANES), 1)

    def rope_s(g, half, c, s_signed):
        first = (lane & (2 * half - 1)) < half
        sw = jnp.where(first, pltpu.roll(g, LANES - half, 1), pltpu.roll(g, half, 1))
        return g * c + sw * s_signed

    hs_all = _dot(u, ws_ref[...])

    def proj_s(c0, c1):
        return hs_all[:, c0:c1]

    ck64, sk64 = ck64_ref[...], sk64_ref[...]
    g = rope_s(proj_s(S_DK, S_DK + LANES), HEAD_DIM // 2, ck64, sk64)
    dk_ref[0] = g[:, :HEAD_DIM].astype(BF16)
    g = rope_s(proj_s(S_IK, S_IK + LANES), IDX_DIM // 2, ck32_ref[...], sk32_ref[...])
    ik_ref[0] = g[:, :IDX_DIM].astype(BF16)
    g = proj_s(S_FK, S_FK + FOX_W)
    for hh in range(FOX_HEADS):
        fk_ref[0, hh] = g[:, hh * HEAD_DIM:(hh + 1) * HEAD_DIM].astype(BF16)
    g = rope_s(proj_s(S_KS, S_KS + LANES), HEAD_DIM // 2, ck64, sk64)
    for gg in range(NSA_GROUPS):
        ks_ref[0, gg] = g[:, gg * HEAD_DIM:(gg + 1) * HEAD_DIM].astype(BF16)
    g = rope_s(proj_s(S_KW, S_KW + LANES), HEAD_DIM // 2, ck64, sk64)
    for gg in range(NSA_GROUPS):
        kw_ref[0, gg] = g[:, gg * HEAD_DIM:(gg + 1) * HEAD_DIM].astype(BF16)
    kcin_ref[0] = proj_s(S_KC, S_KC + LANES)
    vcin_ref[0] = proj_s(S_VC, S_VC + LANES)


def _projection(x, scale, shift, wt, ws, tabs, bfc):
    bsz, seq, d = x.shape
    tm = PROJ_TOKENS
    nt = seq // tm
    ct64, st64, ct32, st32, ck64, sk64, ck32, sk32 = tabs
    const = lambda shape: pl.BlockSpec(shape, lambda b, i: (0,) * len(shape))
    feat = lambda rows: pl.BlockSpec((1, rows, tm), lambda b, i: (b, 0, i))
    in_specs = [
        pl.BlockSpec((1, tm, d), lambda b, i: (b, i, 0)),
        pl.BlockSpec((1, 1, d), lambda b, i: (b, 0, 0)),
        pl.BlockSpec((1, 1, d), lambda b, i: (b, 0, 0)),
        const((T_ROWS, d)),
        const((d, S_COLS)),
        pl.BlockSpec((HEAD_DIM // 2, tm), lambda b, i: (0, i)),
        pl.BlockSpec((HEAD_DIM // 2, tm), lambda b, i: (0, i)),
        pl.BlockSpec((IDX_DIM // 2, tm), lambda b, i: (0, i)),
        pl.BlockSpec((IDX_DIM // 2, tm), lambda b, i: (0, i)),
        pl.BlockSpec((tm, LANES), lambda b, i: (i, 0)),
        pl.BlockSpec((tm, LANES), lambda b, i: (i, 0)),
        pl.BlockSpec((tm, LANES), lambda b, i: (i, 0)),
        pl.BlockSpec((tm, LANES), lambda b, i: (i, 0)),
        const((8, 1)),
    ]
    sds = jax.ShapeDtypeStruct
    out_shape = [
        sds((bsz, DSA_W, seq), BF16), sds((bsz, IDX_W, seq), BF16), sds((bsz, FOX_W, seq), BF16),
        sds((bsz, NSA_W, seq), BF16), sds((bsz, NSA_W, seq), BF16),
        sds((bsz, N_KEY_CHUNKS, HEAD_DIM, KEY_CHUNK), BF16),
        sds((bsz, FOX_HEADS, N_KEY_CHUNKS, HEAD_DIM, KEY_CHUNK), BF16),
        sds((bsz, NSA_GROUPS, N_KEY_CHUNKS, HEAD_DIM, KEY_CHUNK), BF16),
        sds((bsz, NSA_GROUPS, N_WIN_CHUNKS, HEAD_DIM, WIN_CHUNK), BF16),
        sds((bsz, MIX_W, seq), F32),
        sds((bsz, 8, seq), F32), sds((bsz, 8, seq), F32), sds((bsz, NSA_GROUPS, 16, seq), F32),
        sds((bsz, seq, HEAD_DIM), BF16), sds((bsz, seq, IDX_DIM), BF16),
        sds((bsz, FOX_HEADS, seq, HEAD_DIM), BF16),
        sds((bsz, NSA_GROUPS, seq, HEAD_DIM), BF16), sds((bsz, NSA_GROUPS, seq, HEAD_DIM), BF16),
        sds((bsz, seq, LANES), F32), sds((bsz, seq, LANES), F32),
    ]
    wpc = tm // WIN_CHUNK
    out_specs = [
        feat(DSA_W), feat(IDX_W), feat(FOX_W), feat(NSA_W), feat(NSA_W),
        pl.BlockSpec((1, 1, HEAD_DIM, KEY_CHUNK), lambda b, i: (b, i, 0, 0)),
        pl.BlockSpec((1, FOX_HEADS, 1, HEAD_DIM, KEY_CHUNK), lambda b, i: (b, 0, i, 0, 0)),
        pl.BlockSpec((1, NSA_GROUPS, 1, HEAD_DIM, KEY_CHUNK), lambda b, i: (b, 0, i, 0, 0)),
        pl.BlockSpec((1, NSA_GROUPS, wpc, HEAD_DIM, WIN_CHUNK), lambda b, i: (b, 0, i, 0, 0)),
        feat(MIX_W),
        feat(8), feat(8),
        pl.BlockSpec((1, NSA_GROUPS, 16, tm), lambda b, i: (b, 0, 0, i)),
        pl.BlockSpec((1, tm, HEAD_DIM), lambda b, i: (b, i, 0)),
        pl.BlockSpec((1, tm, IDX_DIM), lambda b, i: (b, i, 0)),
        pl.BlockSpec((1, FOX_HEADS, tm, HEAD_DIM), lambda b, i: (b, 0, i, 0)),
        pl.BlockSpec((1, NSA_GROUPS, tm, HEAD_DIM), lambda b, i: (b, 0, i, 0)),
        pl.BlockSpec((1, NSA_GROUPS, tm, HEAD_DIM), lambda b, i: (b, 0, i, 0)),
        pl.BlockSpec((1, tm, LANES), lambda b, i: (b, i, 0)),
        pl.BlockSpec((1, tm, LANES), lambda b, i: (b, i, 0)),
    ]
    assert tm == KEY_CHUNK
    return pl.pallas_call(
        _proj_kernel,
        grid=(bsz, nt),
        in_specs=in_specs,
        out_specs=out_specs,
        out_shape=out_shape,
        compiler_params=_cparams(("arbitrary", "arbitrary")),
        name="in_proj",
    )(x, scale, shift, wt, ws, ct64, st64, ct32, st32, ck64, sk64, ck32, sk32, bfc)


def _cumsum_kernel(lft_ref, cumt_ref, cums_ref):
    seq = lft_ref.shape[2]
    r = lax.broadcasted_iota(I32, (CUM_CHUNK, CUM_CHUNK), 0)
    c = lax.broadcasted_iota(I32, (CUM_CHUNK, CUM_CHUNK), 1)
    tri_u = (r <= c).astype(F32)
    blocks = [slice(k * CUM_CHUNK, (k + 1) * CUM_CHUNK) for k in range(seq // CUM_CHUNK)]
    local = [jnp.dot(lft_ref[0, :, sl], tri_u, preferred_element_type=F32, precision=lax.Precision.HIGHEST)
             for sl in blocks]
    carry = jnp.zeros((8, 1), F32)
    pad = jnp.zeros((LANES - 8, CUM_CHUNK), F32)
    for k, sl in enumerate(blocks):
        ct = (local[k] + carry) * LOG2E
        carry = local[k][:, CUM_CHUNK - 1:CUM_CHUNK] + carry
        for head in range(8):
            cumt_ref[0, head, :, sl] = ct[head:head + 1, :]
        cums_ref[0, sl, :] = jnp.concatenate([ct, pad], axis=0).T[:, :8]


def _forget_cumsum(lft):
    bsz, _, seq = lft.shape
    return pl.pallas_call(
        _cumsum_kernel,
        grid=(bsz,),
        in_specs=[pl.BlockSpec((1, 8, seq), lambda b: (b, 0, 0))],
        out_specs=[pl.BlockSpec((1, 8, 1, seq), lambda b: (b, 0, 0, 0)),
                   pl.BlockSpec((1, seq, 8), lambda b: (b, 0, 0))],
        out_shape=[jax.ShapeDtypeStruct((bsz, 8, 1, seq), F32),
                   jax.ShapeDtypeStruct((bsz, seq, 8), F32)],
        compiler_params=_cparams(("arbitrary",)),
        name="forget_cumsum",
    )(lft)


def _compress_kernel(kcin_ref, vcin_ref, pelo_ref, pehi_ref, w1lo_ref, w1hi_ref, w2_ref,
                     kc_ref, vct_ref):
    n_blk = kcin_ref.shape[1] // CMP_STRIDE
    for kv, src in enumerate((kcin_ref, vcin_ref)):
        lo, hi = [], []
        for j in range(CMP_STRIDE):
            piece = src[0, pl.ds(j, n_blk, stride=CMP_STRIDE), :]
            lo.append((piece + pelo_ref[kv, j:j + 1, :]).astype(BF16))
            hi.append((piece + pehi_ref[kv, j:j + 1, :]).astype(BF16))
        a = _dot(jnp.concatenate(lo, axis=1), w1lo_ref[kv])
        b = _dot(jnp.concatenate(hi, axis=1), w1hi_ref[kv])
        pre = a + pltpu.roll(b, n_blk - 1, 0)
        act = pre * _sigmoid(pre)
        out = _dot(act.astype(BF16), w2_ref[kv])
        if kv == 0:
            for g in range(NSA_GROUPS):
                kc_ref[0, g] = out[:, g * HEAD_DIM:(g + 1) * HEAD_DIM].astype(BF16)
        else:
            out_t = out.T
            for g in range(NSA_GROUPS):
                vct_ref[0, g] = out_t[g * HEAD_DIM:(g + 1) * HEAD_DIM].astype(BF16)


def _compress(kcin, vcin, pelo, pehi, w1lo, w1hi, w2):
    bsz, seq, _ = kcin.shape
    n_blk = seq // CMP_STRIDE
    full = lambda a: pl.BlockSpec(a.shape, lambda b: (0,) * a.ndim)
    return pl.pallas_call(
        _compress_kernel,
        grid=(bsz,),
        in_specs=[pl.BlockSpec((1, seq, LANES), lambda b: (b, 0, 0)),
                  pl.BlockSpec((1, seq, LANES), lambda b: (b, 0, 0)),
                  full(pelo), full(pehi), full(w1lo), full(w1hi), full(w2)],
        out_specs=[pl.BlockSpec((1, NSA_GROUPS, n_blk, HEAD_DIM), lambda b: (b, 0, 0, 0)),
                   pl.BlockSpec((1, NSA_GROUPS, HEAD_DIM, n_blk), lambda b: (b, 0, 0, 0))],
        out_shape=[jax.ShapeDtypeStruct((bsz, NSA_GROUPS, n_blk, HEAD_DIM), BF16),
                   jax.ShapeDtypeStruct((bsz, NSA_GROUPS, HEAD_DIM, n_blk), BF16)],
        compiler_params=_cparams(("arbitrary",)),
        name="nsa_compress",
    )(kcin, vcin, pelo, pehi, w1lo, w1hi, w2)


def _fold_rows(x, op):
    k, n = x.shape
    return op(x.reshape(k // FOLD_ROWS, FOLD_ROWS, n), axis=0)


def _reduce_rows(x, op):
    if x.shape[0] % FOLD_ROWS == 0 and x.shape[0] > FOLD_ROWS:
        x = _fold_rows(x, op)
    return op(x, axis=0, keepdims=True)


def _with_ones(v_t):
    return jnp.concatenate([v_t, jnp.ones((ONES_ROWS, v_t.shape[1]), v_t.dtype)], axis=0)


def _softmax_step(carry, s, v_aug):
    m, acc = carry
    m_new = jnp.maximum(m, _reduce_rows(s, jnp.max))
    alpha = jnp.exp2(m - m_new)
    p = jnp.exp2(s - m_new)
    acc = alpha * acc + _dot(v_aug, p.astype(BF16))
    return m_new, acc


def _softmax_init(n):
    return (jnp.full((1, n), NEG, F32), jnp.zeros((HEAD_DIM + ONES_ROWS, n), F32))


def _softmax_finish(carry):
    _, acc = carry
    return acc[:HEAD_DIM] * (1.0 / acc[HEAD_DIM:HEAD_DIM + 1])


def _tile_lanes(a, n):
    return jnp.concatenate([a] * n, axis=1)


def _dsa_kernel(iq_ref, ik_ref, iw_ref, dq_ref, dk_ref, dv_ref, g_ref, o_ref, key_ref):
    i = pl.program_id(1)
    tq = DSA_Q_TILE
    n_chunks = i + 1
    tpos = i * tq + lax.broadcasted_iota(I32, (1, tq), 1)
    srow = lax.broadcasted_iota(I32, (KEY_CHUNK, tq), 0)
    iw = iw_ref[0]

    def index_chunk(c, _):
        k0 = pl.multiple_of(c * KEY_CHUNK, KEY_CHUNK)
        ikc = ik_ref[0, pl.ds(k0, KEY_CHUNK), :]
        acc = jnp.zeros((KEY_CHUNK, tq), F32)
        for h in range(IDX_HEADS):
            x = _dot(ikc, iq_ref[0, h * IDX_DIM:(h + 1) * IDX_DIM, :])
            acc = acc + iw[h:h + 1, :] * jnp.maximum(x, 0.0)
        key_ref[pl.ds(k0, KEY_CHUNK), :] = jnp.where(srow + k0 <= tpos, acc, -jnp.inf)
        return 0

    lax.fori_loop(1, n_chunks, index_chunk, index_chunk(0, 0))

    def count(pred):
        def body(c, acc):
            k0 = pl.multiple_of(c * KEY_CHUNK, KEY_CHUNK)
            hit = pred(key_ref[pl.ds(k0, KEY_CHUNK), :], srow + k0)
            return acc + _fold_rows(jnp.where(hit, 1, 0), jnp.sum)
        acc = lax.fori_loop(1, n_chunks, body, body(0, jnp.zeros((FOLD_ROWS, tq), I32)))
        return jnp.sum(acc, axis=0, keepdims=True)

    def decode(code):
        skey = code ^ INT_MIN
        val = lax.bitcast_convert_type(jnp.where(skey < 0, skey ^ 0x7FFFFFFF, skey), F32)
        return jnp.where(jnp.logical_and(code >= 0, code <= CODE_NEG_INF), -jnp.inf, val)

    def value_bit(it, code):
        cand = code | lax.shift_left(jnp.int32(1), 31 - it)
        cand_f = decode(cand)
        cnt = count(lambda key, spos: key >= cand_f)
        return jnp.where(cnt >= DSA_TOPK, cand, code)

    thr = decode(lax.fori_loop(0, 32, value_bit, jnp.zeros((1, tq), I32)))

    need = (DSA_TOPK - count(lambda key, spos: key > thr)).astype(F32)
    r = lax.broadcasted_iota(I32, (KEY_CHUNK, KEY_CHUNK), 0)
    c = lax.broadcasted_iota(I32, (KEY_CHUNK, KEY_CHUNK), 1)
    prefix = jnp.where(c <= r, 1.0, 0.0).astype(BF16)

    def attend(c, carry):
        heads, ties_before = carry
        k0 = pl.multiple_of(c * KEY_CHUNK, KEY_CHUNK)
        key = key_ref[pl.ds(k0, KEY_CHUNK), :]
        tied = key == thr
        rank = _dot(prefix, jnp.where(tied, 1.0, 0.0).astype(BF16)) + ties_before
        keep = jnp.logical_or(key > thr, jnp.logical_and(tied, rank <= need))
        bias = jnp.where(jnp.logical_and(keep, srow + k0 <= tpos), 0.0, NEG)
        kc = dk_ref[0, pl.ds(k0, KEY_CHUNK), :]
        v_aug = _with_ones(dv_ref[0, c])
        heads = tuple(
            _softmax_step(heads[h], _dot(kc, dq_ref[0, h * HEAD_DIM:(h + 1) * HEAD_DIM, :]) + bias, v_aug)
            for h in range(DSA_HEADS))
        return heads, rank[KEY_CHUNK - 1:KEY_CHUNK, :]

    init = tuple(_softmax_init(tq) for _ in range(DSA_HEADS))
    heads, _ = lax.fori_loop(1, n_chunks, attend, attend(0, (init, jnp.zeros((1, tq), F32))))
    for h, carry in enumerate(heads):
        rows = slice(h * HEAD_DIM, (h + 1) * HEAD_DIM)
        o_ref[0, rows, :] = (_softmax_finish(carry) * g_ref[0, rows, :]).astype(BF16)


def _dsa(iq_t, ik, iw_t, dq_t, dk, dv_t, gate_t):
    bsz, _, seq = dq_t.shape
    tq = DSA_Q_TILE
    assert tq == KEY_CHUNK
    return pl.pallas_call(
        _dsa_kernel,
        grid=(bsz, seq // tq),
        in_specs=[
            pl.BlockSpec((1, IDX_W, tq), lambda b, i: (b, 0, i)),
            pl.BlockSpec((1, seq, IDX_DIM), lambda b, i: (b, 0, 0)),
            pl.BlockSpec((1, 8, tq), lambda b, i: (b, 0, i)),
            pl.BlockSpec((1, DSA_W, tq), lambda b, i: (b, 0, i)),
            pl.BlockSpec((1, seq, HEAD_DIM), lambda b, i: (b, 0, 0)),
            pl.BlockSpec((1, N_KEY_CHUNKS, HEAD_DIM, KEY_CHUNK), lambda b, i: (b, 0, 0, 0)),
            pl.BlockSpec((1, DSA_W, tq), lambda b, i: (b, 0, i)),
        ],
        out_specs=pl.BlockSpec((1, DSA_W, tq), lambda b, i: (b, 0, i)),
        out_shape=jax.ShapeDtypeStruct((bsz, DSA_W, seq), BF16),
        scratch_shapes=[pltpu.VMEM((seq, tq), F32)],
        compiler_params=_cparams(("arbitrary", "arbitrary")),
        name="dsa_attn",
    )(iq_t, ik, iw_t, dq_t, dk, dv_t, gate_t)


def _fox_kernel(q_ref, k_ref, v_ref, cumt_ref, cums_ref, g_ref, o_ref):
    h = pl.program_id(1)
    tq = FOX_Q_TILE
    n_tiles = q_ref.shape[2] // tq
    head_lane = lax.broadcasted_iota(I32, (KEY_CHUNK, 8), 1) == h
    srow = lax.broadcasted_iota(I32, (KEY_CHUNK, tq), 0)
    tcol = lax.broadcasted_iota(I32, (KEY_CHUNK, tq), 1)
    ccols = []
    for c in range(n_tiles):
        col = jnp.sum(jnp.where(head_lane, cums_ref[0, c * KEY_CHUNK:(c + 1) * KEY_CHUNK, :], 0.0),
                      axis=1, keepdims=True)
        ccols.append(_tile_lanes(jnp.broadcast_to(col, (KEY_CHUNK, LANES)), tq // LANES))
    v_aug = [_with_ones(v_ref[0, 0, c]) for c in range(n_tiles)]
    for i in range(n_tiles):
        q = q_ref[0, :, i * tq:(i + 1) * tq]
        crow = cumt_ref[0, 0, :, i * tq:(i + 1) * tq]
        carry = _softmax_init(tq)
        for c in range(i + 1):
            s = _dot(k_ref[0, 0, c * KEY_CHUNK:(c + 1) * KEY_CHUNK, :], q) + crow - ccols[c]
            if c == i:
                s = jnp.where(srow <= tcol, s, NEG)
            carry = _softmax_step(carry, s, v_aug[c])
        cols = slice(i * tq, (i + 1) * tq)
        o_ref[0, :, cols] = (_softmax_finish(carry) * g_ref[0, :, cols]).astype(BF16)


def _fox(fq_t, fk, fv_t, cum_t, cum_s, gate_t):
    bsz, _, seq = fq_t.shape
    assert FOX_Q_TILE == KEY_CHUNK
    return pl.pallas_call(
        _fox_kernel,
        grid=(bsz, FOX_HEADS),
        in_specs=[
            pl.BlockSpec((1, HEAD_DIM, seq), lambda b, h: (b, h, 0)),
            pl.BlockSpec((1, 1, seq, HEAD_DIM), lambda b, h: (b, h, 0, 0)),
            pl.BlockSpec((1, 1, N_KEY_CHUNKS, HEAD_DIM, KEY_CHUNK), lambda b, h: (b, h, 0, 0, 0)),
            pl.BlockSpec((1, 1, 1, seq), lambda b, h: (b, h, 0, 0)),
            pl.BlockSpec((1, seq, 8), lambda b, h: (b, 0, 0)),
            pl.BlockSpec((1, HEAD_DIM, seq), lambda b, h: (b, DSA_HEADS + h, 0)),
        ],
        out_specs=pl.BlockSpec((1, HEAD_DIM, seq), lambda b, h: (b, h, 0)),
        out_shape=jax.ShapeDtypeStruct((bsz, FOX_W, seq), BF16),
        compiler_params=_cparams(("arbitrary", "arbitrary")),
        name="fox_attn",
    )(fq_t, fk, fv_t, cum_t, cum_s, gate_t)


def _nsa_kernel(q_ref, qr_ref, kc_ref, vct_ref, ks_ref, vs_ref, kw_ref, vw_ref, g_ref,
                ovl_ref, expand_ref, og0_ref, og1_ref, og2_ref, o_ref, sel_ref):
    i = pl.program_id(2)
    nq = NSA_HPG * Q_TILE
    t0 = i * Q_TILE
    tpos = t0 + lax.broadcasted_iota(I32, (1, Q_TILE), 1)
    q3 = jnp.concatenate([q_ref[0, j * HEAD_DIM:(j + 1) * HEAD_DIM, :] for j in range(NSA_HPG)], axis=1)
    qr3 = jnp.concatenate([qr_ref[0, j * HEAD_DIM:(j + 1) * HEAD_DIM, :] for j in range(NSA_HPG)], axis=1)

    n_cmp = kc_ref.shape[2]
    cend = lax.broadcasted_iota(I32, (n_cmp, Q_TILE), 0) * CMP_STRIDE + (CMP_LEN - 1)
    cbias = _tile_lanes(jnp.where(cend <= tpos, 0.0, NEG), NSA_HPG)
    cvalid = _tile_lanes(jnp.where(cend <= tpos, 1.0, 0.0), NSA_HPG)
    s = _dot(kc_ref[0, 0], q3) + cbias
    m = _reduce_rows(s, jnp.max)
    e = jnp.exp2(s - m) * cvalid
    p_cmp = (e / jnp.maximum(_reduce_rows(e, jnp.sum), 1e-30)).astype(BF16)
    o_cmp = _dot(vct_ref[0, 0], p_cmp)

    p_stack = jnp.concatenate([p_cmp[:, j * Q_TILE:(j + 1) * Q_TILE] for j in range(NSA_HPG)], axis=0)
    score = _dot(ovl_ref[...], p_stack)
    blk = lax.broadcasted_iota(I32, (N_SEL_BLOCKS, Q_TILE), 0)
    cur = lax.shift_right_logical(tpos, 6)
    forced = jnp.logical_or(blk == 0, jnp.logical_or(blk == cur, blk == cur - 1))
    score = jnp.where(forced, jnp.inf, jnp.where(blk > cur, -jnp.inf, score))
    rank = jnp.zeros((N_SEL_BLOCKS, Q_TILE), F32)
    for mth in range(N_SEL_BLOCKS):
        row = score[mth:mth + 1, :]
        ahead = jnp.logical_or(row > score, jnp.logical_and(row == score, blk > mth))
        rank = rank + jnp.where(ahead, 1.0, 0.0)
    sel_ref[...] = jnp.where(rank < SEL_N, 1.0, 0.0)

    n_chunks = (i * Q_TILE + Q_TILE + KEY_CHUNK - 1) // KEY_CHUNK
    srow = lax.broadcasted_iota(I32, (KEY_CHUNK, Q_TILE), 0)
    blocks_per_chunk = KEY_CHUNK // SEL_BLOCK

    def attend_sel(c, carry):
        k0 = pl.multiple_of(c * KEY_CHUNK, KEY_CHUNK)
        b0 = pl.multiple_of(c * blocks_per_chunk, blocks_per_chunk)
        picked = _dot(expand_ref[...], sel_ref[pl.ds(b0, blocks_per_chunk), :])
        ok = jnp.logical_and(picked > 0.5, srow + k0 <= tpos)
        bias = _tile_lanes(jnp.where(ok, 0.0, NEG), NSA_HPG)
        s = _dot(ks_ref[0, 0, pl.ds(k0, KEY_CHUNK), :], qr3) + bias
        return _softmax_step(carry, s, _with_ones(vs_ref[0, 0, c]))

    wc = jnp.maximum(i - WINDOW // Q_TILE, 0) * (Q_TILE // WIN_CHUNK)
    w0 = pl.multiple_of(wc * WIN_CHUNK, WIN_CHUNK)
    kpos = w0 + lax.broadcasted_iota(I32, (WIN_SPAN, Q_TILE), 0)
    wok = jnp.logical_and(kpos <= tpos, kpos > tpos - WINDOW)
    s = _dot(kw_ref[0, 0, pl.ds(w0, WIN_SPAN), :], qr3) + _tile_lanes(jnp.where(wok, 0.0, NEG), NSA_HPG)
    m = _reduce_rows(s, jnp.max)
    e = jnp.exp2(s - m).astype(BF16)
    acc = jnp.zeros((HEAD_DIM + ONES_ROWS, nq), F32)
    for j in range(WIN_SPAN // WIN_CHUNK):
        acc = acc + _dot(_with_ones(vw_ref[0, 0, wc + j]), e[j * WIN_CHUNK:(j + 1) * WIN_CHUNK, :])
    o_win = _softmax_finish((m, acc))

    first = attend_sel(0, _softmax_init(nq))
    o_slc = _softmax_finish(lax.fori_loop(1, n_chunks, attend_sel, first))

    gts = g_ref[0, 0]
    out_gates = (og0_ref, og1_ref, og2_ref)
    for j in range(NSA_HPG):
        sl = slice(j * Q_TILE, (j + 1) * Q_TILE)
        mixed = (gts[j:j + 1, :] * o_cmp[:, sl]
                 + gts[NSA_HPG + j:NSA_HPG + j + 1, :] * o_slc[:, sl]
                 + gts[2 * NSA_HPG + j:2 * NSA_HPG + j + 1, :] * o_win[:, sl])
        o_ref[0, j * HEAD_DIM:(j + 1) * HEAD_DIM, :] = (mixed * out_gates[j][0]).astype(BF16)


def _nsa(nq_t, nqr_t, kc, vc_t, ks, vs_t, kw, vw_t, gts, ovl, expand, gate_t):
    bsz, _, seq = nq_t.shape
    n_cmp = kc.shape[2]
    gw = NSA_HPG * HEAD_DIM
    per_group = lambda shape: pl.BlockSpec((1, 1) + shape, lambda b, g, i: (b, g) + (0,) * len(shape))
    return pl.pallas_call(
        _nsa_kernel,
        grid=(bsz, NSA_GROUPS, seq // Q_TILE),
        in_specs=[
            pl.BlockSpec((1, gw, Q_TILE), lambda b, g, i: (b, g, i)),
            pl.BlockSpec((1, gw, Q_TILE), lambda b, g, i: (b, g, i)),
            per_group((n_cmp, HEAD_DIM)),
            per_group((HEAD_DIM, n_cmp)),
            per_group((seq, HEAD_DIM)),
            per_group((N_KEY_CHUNKS, HEAD_DIM, KEY_CHUNK)),
            per_group((seq, HEAD_DIM)),
            per_group((N_WIN_CHUNKS, HEAD_DIM, WIN_CHUNK)),
            pl.BlockSpec((1, 1, 16, Q_TILE), lambda b, g, i: (b, g, 0, i)),
            pl.BlockSpec(ovl.shape, lambda b, g, i: (0, 0)),
            pl.BlockSpec(expand.shape, lambda b, g, i: (0, 0)),
        ] + [
            pl.BlockSpec((1, HEAD_DIM, Q_TILE),
                         lambda b, g, i, j=j: (b, DSA_HEADS + FOX_HEADS + g * NSA_HPG + j, i))
            for j in range(NSA_HPG)
        ],
        out_specs=pl.BlockSpec((1, gw, Q_TILE), lambda b, g, i: (b, g, i)),
        out_shape=jax.ShapeDtypeStruct((bsz, NSA_W, seq), BF16),
        scratch_shapes=[pltpu.VMEM((N_SEL_BLOCKS, Q_TILE), F32)],
        compiler_params=_cparams(("arbitrary", "arbitrary", "arbitrary")),
        name="nsa_attn",
    )(nq_t, nqr_t, kc, vc_t, ks, vs_t, kw, vw_t, gts, ovl, expand, gate_t, gate_t, gate_t)


def _out_kernel(od_ref, of_ref, on_ref, w_ref, x_ref, gm_ref, lg_ref, lb_ref, o_ref):
    z = jnp.concatenate([od_ref[0], of_ref[0], on_ref[0]], axis=0)
    y = lax.dot_general(z, w_ref[...], (((0,), (0,)), ((), ())),
                        preferred_element_type=F32)
    r = ALPHA * x_ref[0] + (1.0 + gm_ref[0]) * y
    mu = jnp.mean(r, axis=-1, keepdims=True)
    rc = r - mu
    var = jnp.mean(rc * rc, axis=-1, keepdims=True)
    o_ref[0] = rc * lax.rsqrt(var + LN_EPS) * lg_ref[...] + lb_ref[...]


def _output(od_t, of_t, on_t, w_out, x, gmod, ln_g, ln_b):
    bsz, seq, d = x.shape
    tm = PROJ_TOKENS
    feat = lambda rows: pl.BlockSpec((1, rows, tm), lambda b, i: (b, 0, i))
    return pl.pallas_call(
        _out_kernel,
        grid=(bsz, seq // tm),
        in_specs=[
            feat(DSA_W), feat(FOX_W), feat(NSA_W),
            pl.BlockSpec((MIX_W, d), lambda b, i: (0, 0)),
            pl.BlockSpec((1, tm, d), lambda b, i: (b, i, 0)),
            pl.BlockSpec((1, 1, d), lambda b, i: (b, 0, 0)),
            pl.BlockSpec((1, d), lambda b, i: (0, 0)),
            pl.BlockSpec((1, d), lambda b, i: (0, 0)),
        ],
        out_specs=pl.BlockSpec((1, tm, d), lambda b, i: (b, i, 0)),
        out_shape=jax.ShapeDtypeStruct((bsz, seq, d), F32),
        compiler_params=_cparams(("arbitrary", "arbitrary")),
        name="out_proj_ln",
    )(od_t, of_t, on_t, w_out, x, gmod, ln_g, ln_b)


def _split_rows(w):
    out, off = {}, 0
    for name, width in IN_SPLITS:
        out[name] = w[off:off + width]
        off += width
    return out


def _prep_w_in(w_in):
    d = w_in.shape[0]
    p = _split_rows(w_in.T.astype(BF16))
    z = lambda n: jnp.zeros((n, d), BF16)
    grows = p["nsa_g"].reshape(3, NSA_GROUPS, NSA_HPG, d)
    gparts = []
    for g in range(NSA_GROUPS):
        gparts += [grows[:, g].reshape(3 * NSA_HPG, d), z(16 - 3 * NSA_HPG)]
    wt = jnp.concatenate(
        [p["dsa_q"], p["idx_q"], p["fox_q"], p["nsa_q"],
         p["dsa_v"], z(HEAD_DIM), p["fox_v"], p["nsa_vs"], p["nsa_vw"], p["gate"],
         p["idx_w"], p["fox_f"], z(8 - FOX_HEADS)] + gparts, axis=0)
    assert wt.shape[0] == T_ROWS
    ws_t = jnp.concatenate(
        [p["dsa_k"], z(LANES - HEAD_DIM), p["idx_k"], z(LANES - IDX_DIM), p["fox_k"],
         p["nsa_ks"], p["nsa_kw"], p["nsa_kc"], p["nsa_vc"]], axis=0)
    assert ws_t.shape[0] == S_COLS
    return wt, ws_t.T


def _rope_tables(seq):
    pos = jnp.arange(seq, dtype=F32)
    feature_major, token_major = [], []
    lane = np.arange(LANES)
    for hd in (HEAD_DIM, IDX_DIM):
        half = hd // 2
        inv = ROPE_THETA ** (-jnp.arange(half, dtype=F32) / half)
        ang = inv[:, None] * pos[None, :]
        cos, sin = jnp.cos(ang), jnp.sin(ang)
        feature_major += [cos, sin]
        sign = jnp.asarray(np.where((lane % hd) < half, -1.0, 1.0), F32)
        reps = LANES // half
        token_major += [jnp.tile(cos.T, (1, reps)), jnp.tile(sin.T, (1, reps)) * sign[None, :]]
    return tuple(feature_major + token_major)


def _prep_compress(cmp_pe, cmp_w1, cmp_w2):
    assert NSA_GROUPS == 2

    def block_diag(w):
        z = jnp.zeros_like(w)
        return jnp.concatenate([jnp.concatenate([w, z], axis=-1), jnp.concatenate([z, w], axis=-1)], axis=-2)

    big1 = block_diag(cmp_w1.astype(BF16).reshape(2, CMP_LEN, HEAD_DIM, HEAD_DIM))
    w1lo = big1[:, :CMP_STRIDE].reshape(2, CMP_STRIDE * LANES, LANES)
    w1hi = big1[:, CMP_STRIDE:].reshape(2, CMP_STRIDE * LANES, LANES)
    w2 = block_diag(cmp_w2.astype(BF16))
    pe2 = jnp.concatenate([cmp_pe] * NSA_GROUPS, axis=-1)
    return pe2[:, :CMP_STRIDE], pe2[:, CMP_STRIDE:], w1lo, w1hi, w2


def _selection_constants(n_cmp):
    cstart = np.arange(n_cmp) * CMP_STRIDE
    bstart = np.arange(N_SEL_BLOCKS) * SEL_BLOCK
    real = (np.arange(n_cmp) < (SEQ - CMP_LEN) // CMP_STRIDE + 1)[:, None]
    ovl = ((cstart[:, None] < bstart[None, :] + SEL_BLOCK) & (cstart[:, None] + CMP_LEN > bstart[None, :]) & real)
    ovl_t = np.concatenate([ovl.T.astype(np.float32)] * NSA_HPG, axis=1)
    expand = (np.arange(KEY_CHUNK)[:, None] // SEL_BLOCK == np.arange(KEY_CHUNK // SEL_BLOCK)[None, :])
    return jnp.asarray(ovl_t, BF16), jnp.asarray(expand, F32)


def kernel(x, c, w_ada, b_ada, w_in, b_f, cmp_pe, cmp_w1, cmp_w2, w_out, ln_g, ln_b):
    bsz, seq, d = x.shape
    assert (seq, d) == (SEQ, D_MODEL)
    mod = _modulation(c, w_ada, b_ada)
    tabs = _rope_tables(seq)
    ovl, expand = _selection_constants(seq // CMP_STRIDE)
    for l in range(DEPTH):
        shift = mod[l, :, :d].reshape(bsz, 1, d)
        scale = mod[l, :, d:2 * d].reshape(bsz, 1, d)
        gmod = mod[l, :, 2 * d:].reshape(bsz, 1, d)
        wt, ws = _prep_w_in(w_in[l])
        bf8 = jnp.concatenate([b_f[l], jnp.zeros((8 - FOX_HEADS,), F32)])
        (dq, iq, fq, nq, nqr, dv, fv, vs, vw, gate, iw, lft, gts,
         dk, ik, fk, ks, kw, kcin, vcin) = _projection(
            x, scale, shift, wt, ws, tabs, bf8.reshape(8, 1))
        cum_t, cum_s = _forget_cumsum(lft)
        kc, vc_t = _compress(kcin, vcin, *_prep_compress(cmp_pe[l], cmp_w1[l], cmp_w2[l]))
        o_dsa = _dsa(iq, ik, iw, dq, dk, dv, gate)
        o_fox = _fox(fq, fk, fv, cum_t, cum_s, gate)
        o_nsa = _nsa(nq, nqr, kc, vc_t, ks, vs, kw, vw, gts, ovl, expand, gate)
        x = _output(o_dsa, o_fox, o_nsa, w_out[l].astype(BF16), x, gmod,
                    ln_g[l].reshape(1, d), ln_b[l].reshape(1, d))
    return x
```

```python
import jax
import jax.numpy as jnp
import numpy as np
from jax import lax
from jax.experimental import pallas as pl
from jax.experimental.pallas import tpu as pltpu

F32 = jnp.float32
BF16 = jnp.bfloat16
I32 = jnp.int32

D_MODEL = 1024
SEQ = 2048
DEPTH = 2
HEAD_DIM = 64
DSA_HEADS = 4
DSA_TOPK = 256
IDX_HEADS = 8
IDX_DIM = 32
FOX_HEADS = 6
NSA_HEADS = 6
NSA_GROUPS = 2
NSA_HPG = NSA_HEADS // NSA_GROUPS
CMP_LEN = 32
CMP_STRIDE = 16
SEL_BLOCK = 64
SEL_N = 16
N_SEL_BLOCKS = SEQ // SEL_BLOCK
WINDOW = 512
ROPE_THETA = 10000.0
LN_EPS = 1e-5
ALPHA = (2.0 * DEPTH) ** 0.25

DSA_W = DSA_HEADS * HEAD_DIM
FOX_W = FOX_HEADS * HEAD_DIM
NSA_W = NSA_HEADS * HEAD_DIM
NSA_KV_W = NSA_GROUPS * HEAD_DIM
MIX_W = DSA_W + FOX_W + NSA_W
IDX_W = IDX_HEADS * IDX_DIM

IN_SPLITS = (
    ("dsa_q", DSA_W), ("dsa_k", HEAD_DIM), ("dsa_v", HEAD_DIM),
    ("idx_q", IDX_W), ("idx_k", IDX_DIM), ("idx_w", IDX_HEADS),
    ("fox_q", FOX_W), ("fox_k", FOX_W), ("fox_v", FOX_W), ("fox_f", FOX_HEADS),
    ("nsa_q", NSA_W),
    ("nsa_kc", NSA_KV_W), ("nsa_vc", NSA_KV_W),
    ("nsa_ks", NSA_KV_W), ("nsa_vs", NSA_KV_W),
    ("nsa_kw", NSA_KV_W), ("nsa_vw", NSA_KV_W),
    ("nsa_g", 3 * NSA_HEADS),
    ("gate", MIX_W),
)

LANES = 128
KEY_CHUNK = 512
PROJ_TOKENS = 512
Q_TILE = 256
DSA_Q_TILE = 512
FOX_Q_TILE = 512
N_KEY_CHUNKS = SEQ // KEY_CHUNK
WIN_CHUNK = 256
N_WIN_CHUNKS = SEQ // WIN_CHUNK
WIN_SPAN = WINDOW + Q_TILE
CUM_CHUNK = 256
VMEM_LIMIT = 56 * 1024 * 1024

FOLD_ROWS = 32
ONES_ROWS = 16
LOG2E = 1.4426950408889634
NEG = -(2.0 ** 100)
INT_MIN = -(2 ** 31)
CODE_NEG_INF = 0x007FFFFF

T_DQ, T_IQ, T_FQ, T_NQ = 0, 256, 512, 896
T_DV, T_FV, T_VS, T_VW = 1280, 1408, 1792, 1920
T_GATE, T_SMALL, T_ROWS = 2048, 3072, 3120
S_DK, S_IK, S_FK, S_KS, S_KW, S_KC, S_VC, S_COLS = 0, 128, 256, 640, 768, 896, 1024, 1152


def _cparams(sem):
    return pltpu.CompilerParams(dimension_semantics=sem, vmem_limit_bytes=VMEM_LIMIT)


def _dot(a, b):
    return jnp.dot(a, b, preferred_element_type=F32)


def _dot_nt(a, b):
    return lax.dot_general(a, b, (((1,), (1,)), ((), ())), preferred_element_type=F32)


def _log_sigmoid(x):
    return jnp.minimum(x, 0.0) - jnp.log(1.0 + jnp.exp(-jnp.abs(x)))


def _sigmoid(x):
    return 1.0 / (1.0 + jnp.exp(-x))


def _mod_kernel(c_ref, w_ref, b_ref, o_ref):
    o_ref[0] = _dot(c_ref[...].astype(BF16), w_ref[0].astype(BF16)) + b_ref[0]


def _modulation(c, w_ada, b_ada):
    depth, d, d3 = w_ada.shape
    bsz = c.shape[0]
    return pl.pallas_call(
        _mod_kernel,
        grid=(depth, d3 // d),
        in_specs=[
            pl.BlockSpec((bsz, d), lambda l, j: (0, 0)),
            pl.BlockSpec((1, d, d), lambda l, j: (l, 0, j)),
            pl.BlockSpec((1, 1, d), lambda l, j: (l, 0, j)),
        ],
        out_specs=pl.BlockSpec((1, bsz, d), lambda l, j: (l, 0, j)),
        out_shape=jax.ShapeDtypeStruct((depth, bsz, d3), F32),
        compiler_params=_cparams(("arbitrary", "arbitrary")),
        name="adaln_mod",
    )(c, w_ada, b_ada.reshape(depth, 1, d3))


def _proj_kernel(x_ref, sc_ref, sh_ref, wt_ref, ws_ref,
                 ct64_ref, st64_ref, ct32_ref, st32_ref,
                 ck64_ref, sk64_ref, ck32_ref, sk32_ref, bfc_ref,
                 dq_ref, iq_ref, fq_ref, nq_ref, nqr_ref,
                 dv_ref, fv_ref, vs_ref, vw_ref, gate_ref,
                 iw_ref, lft_ref, gts_ref,
                 dk_ref, ik_ref, fk_ref, ks_ref, kw_ref, kcin_ref, vcin_ref):
    tm = x_ref.shape[1]
    u = (x_ref[0] * (1.0 + sc_ref[0]) + sh_ref[0]).astype(BF16)

    ht_all = _dot_nt(wt_ref[...], u)

    def proj_t(r0, r1):
        return ht_all[r0:r1]

    def rope_t(h, n_heads, hd, c, s):
        half = hd // 2
        out = []
        for hh in range(n_heads):
            x1 = h[hh * hd:hh * hd + half]
            x2 = h[hh * hd + half:(hh + 1) * hd]
            out.append((hh * hd, x1 * c - x2 * s))
            out.append((hh * hd + half, x1 * s + x2 * c))
        return out

    c64, s64 = ct64_ref[...], st64_ref[...]
    c32, s32 = ct32_ref[...], st32_ref[...]
    qscale = HEAD_DIM ** -0.5 * LOG2E

    h = proj_t(T_DQ, T_DQ + DSA_W)
    for r, v in rope_t(h, DSA_HEADS, HEAD_DIM, c64, s64):
        dq_ref[0, r:r + HEAD_DIM // 2, :] = (v * qscale).astype(BF16)
    h = proj_t(T_IQ, T_IQ + IDX_W)
    for r, v in rope_t(h, IDX_HEADS, IDX_DIM, c32, s32):
        iq_ref[0, r:r + IDX_DIM // 2, :] = v.astype(BF16)
    fq_ref[0] = (proj_t(T_FQ, T_FQ + FOX_W) * qscale).astype(BF16)
    h = proj_t(T_NQ, T_NQ + NSA_W)
    nq_ref[0] = (h * qscale).astype(BF16)
    for r, v in rope_t(h, NSA_HEADS, HEAD_DIM, c64, s64):
        nqr_ref[0, r:r + HEAD_DIM // 2, :] = (v * qscale).astype(BF16)

    dv_ref[0, 0] = proj_t(T_DV, T_DV + HEAD_DIM).astype(BF16)
    h = proj_t(T_FV, T_FV + FOX_W)
    for hh in range(FOX_HEADS):
        fv_ref[0, hh, 0] = h[hh * HEAD_DIM:(hh + 1) * HEAD_DIM].astype(BF16)
    h = proj_t(T_VS, T_VS + NSA_KV_W)
    for g in range(NSA_GROUPS):
        vs_ref[0, g, 0] = h[g * HEAD_DIM:(g + 1) * HEAD_DIM].astype(BF16)
    h = proj_t(T_VW, T_VW + NSA_KV_W)
    for g in range(NSA_GROUPS):
        for j in range(tm // WIN_CHUNK):
            vw_ref[0, g, j] = h[g * HEAD_DIM:(g + 1) * HEAD_DIM, j * WIN_CHUNK:(j + 1) * WIN_CHUNK].astype(BF16)

    for r0 in range(0, MIX_W, 256):
        h = proj_t(T_GATE + r0, T_GATE + r0 + 256)
        gate_ref[0, r0:r0 + 256, :] = h * _sigmoid(h)

    h = proj_t(T_SMALL, T_ROWS)
    iw_ref[0] = h[0:8] * (IDX_HEADS ** -0.5)
    lft_ref[0] = _log_sigmoid(h[8:16] + bfc_ref[...])
    g_all = _sigmoid(h[16:48])
    gts_ref[0, 0] = g_all[0:16]
    gts_ref[0, 1] = g_all[16:32]

    lane = lax.broadcasted_iota(I32, (tm, LANES), 1)

    def rope_s(g, half, c, s_signed):
        first = (lane & (2 * half - 1)) < half
        sw = jnp.where(first, pltpu.roll(g, LANES - half, 1), pltpu.roll(g, half, 1))
        return g * c + sw * s_signed

    hs_all = _dot_nt(u, ws_ref[...])

    def proj_s(c0, c1):
        return hs_all[:, c0:c1]

    ck64, sk64 = ck64_ref[...], sk64_ref[...]
    g = rope_s(proj_s(S_DK, S_DK + LANES), HEAD_DIM // 2, ck64, sk64)
    dk_ref[0] = g[:, :HEAD_DIM].astype(BF16)
    g = rope_s(proj_s(S_IK, S_IK + LANES), IDX_DIM // 2, ck32_ref[...], sk32_ref[...])
    ik_ref[0] = g[:, :IDX_DIM].astype(BF16)
    g = proj_s(S_FK, S_FK + FOX_W)
    for hh in range(FOX_HEADS):
        fk_ref[0, hh] = g[:, hh * HEAD_DIM:(hh + 1) * HEAD_DIM].astype(BF16)
    g = rope_s(proj_s(S_KS, S_KS + LANES), HEAD_DIM // 2, ck64, sk64)
    for gg in range(NSA_GROUPS):
        ks_ref[0, gg] = g[:, gg * HEAD_DIM:(gg + 1) * HEAD_DIM].astype(BF16)
    g = rope_s(proj_s(S_KW, S_KW + LANES), HEAD_DIM // 2, ck64, sk64)
    for gg in range(NSA_GROUPS):
        kw_ref[0, gg] = g[:, gg * HEAD_DIM:(gg + 1) * HEAD_DIM].astype(BF16)
    kcin_ref[0] = proj_s(S_KC, S_KC + LANES)
    vcin_ref[0] = proj_s(S_VC, S_VC + LANES)


def _projection(x, scale, shift, wt, ws, tabs, bfc):
    bsz, seq, d = x.shape
    tm = PROJ_TOKENS
    nt = seq // tm
    ct64, st64, ct32, st32, ck64, sk64, ck32, sk32 = tabs
    const = lambda shape: pl.BlockSpec(shape, lambda b, i: (0,) * len(shape))
    feat = lambda rows: pl.BlockSpec((1, rows, tm), lambda b, i: (b, 0, i))
    in_specs = [
        pl.BlockSpec((1, tm, d), lambda b, i: (b, i, 0)),
        pl.BlockSpec((1, 1, d), lambda b, i: (b, 0, 0)),
        pl.BlockSpec((1, 1, d), lambda b, i: (b, 0, 0)),
        const((T_ROWS, d)),
        const((S_COLS, d)),
        pl.BlockSpec((HEAD_DIM // 2, tm), lambda b, i: (0, i)),
        pl.BlockSpec((HEAD_DIM // 2, tm), lambda b, i: (0, i)),
        pl.BlockSpec((IDX_DIM // 2, tm), lambda b, i: (0, i)),
        pl.BlockSpec((IDX_DIM // 2, tm), lambda b, i: (0, i)),
        pl.BlockSpec((tm, LANES), lambda b, i: (i, 0)),
        pl.BlockSpec((tm, LANES), lambda b, i: (i, 0)),
        pl.BlockSpec((tm, LANES), lambda b, i: (i, 0)),
        pl.BlockSpec((tm, LANES), lambda b, i: (i, 0)),
        const((8, 1)),
    ]
    sds = jax.ShapeDtypeStruct
    out_shape = [
        sds((bsz, DSA_W, seq), BF16), sds((bsz, IDX_W, seq), BF16), sds((bsz, FOX_W, seq), BF16),
        sds((bsz, NSA_W, seq), BF16), sds((bsz, NSA_W, seq), BF16),
        sds((bsz, N_KEY_CHUNKS, HEAD_DIM, KEY_CHUNK), BF16),
        sds((bsz, FOX_HEADS, N_KEY_CHUNKS, HEAD_DIM, KEY_CHUNK), BF16),
        sds((bsz, NSA_GROUPS, N_KEY_CHUNKS, HEAD_DIM, KEY_CHUNK), BF16),
        sds((bsz, NSA_GROUPS, N_WIN_CHUNKS, HEAD_DIM, WIN_CHUNK), BF16),
        sds((bsz, MIX_W, seq), F32),
        sds((bsz, 8, seq), F32), sds((bsz, 8, seq), F32), sds((bsz, NSA_GROUPS, 16, seq), F32),
        sds((bsz, seq, HEAD_DIM), BF16), sds((bsz, seq, IDX_DIM), BF16),
        sds((bsz, FOX_HEADS, seq, HEAD_DIM), BF16),
        sds((bsz, NSA_GROUPS, seq, HEAD_DIM), BF16), sds((bsz, NSA_GROUPS, seq, HEAD_DIM), BF16),
        sds((bsz, seq, LANES), F32), sds((bsz, seq, LANES), F32),
    ]
    wpc = tm // WIN_CHUNK
    out_specs = [
        feat(DSA_W), feat(IDX_W), feat(FOX_W), feat(NSA_W), feat(NSA_W),
        pl.BlockSpec((1, 1, HEAD_DIM, KEY_CHUNK), lambda b, i: (b, i, 0, 0)),
        pl.BlockSpec((1, FOX_HEADS, 1, HEAD_DIM, KEY_CHUNK), lambda b, i: (b, 0, i, 0, 0)),
        pl.BlockSpec((1, NSA_GROUPS, 1, HEAD_DIM, KEY_CHUNK), lambda b, i: (b, 0, i, 0, 0)),
        pl.BlockSpec((1, NSA_GROUPS, wpc, HEAD_DIM, WIN_CHUNK), lambda b, i: (b, 0, i, 0, 0)),
        feat(MIX_W),
        feat(8), feat(8),
        pl.BlockSpec((1, NSA_GROUPS, 16, tm), lambda b, i: (b, 0, 0, i)),
        pl.BlockSpec((1, tm, HEAD_DIM), lambda b, i: (b, i, 0)),
        pl.BlockSpec((1, tm, IDX_DIM), lambda b, i: (b, i, 0)),
        pl.BlockSpec((1, FOX_HEADS, tm, HEAD_DIM), lambda b, i: (b, 0, i, 0)),
        pl.BlockSpec((1, NSA_GROUPS, tm, HEAD_DIM), lambda b, i: (b, 0, i, 0)),
        pl.BlockSpec((1, NSA_GROUPS, tm, HEAD_DIM), lambda b, i: (b, 0, i, 0)),
        pl.BlockSpec((1, tm, LANES), lambda b, i: (b, i, 0)),
        pl.BlockSpec((1, tm, LANES), lambda b, i: (b, i, 0)),
    ]
    assert tm == KEY_CHUNK
    return pl.pallas_call(
        _proj_kernel,
        grid=(bsz, nt),
        in_specs=in_specs,
        out_specs=out_specs,
        out_shape=out_shape,
        compiler_params=_cparams(("arbitrary", "arbitrary")),
        name="in_proj",
    )(x, scale, shift, wt, ws, ct64, st64, ct32, st32, ck64, sk64, ck32, sk32, bfc)


def _cumsum_kernel(lft_ref, cumt_ref, cums_ref):
    seq = lft_ref.shape[2]
    r = lax.broadcasted_iota(I32, (CUM_CHUNK, CUM_CHUNK), 0)
    c = lax.broadcasted_iota(I32, (CUM_CHUNK, CUM_CHUNK), 1)
    tri_u = (r <= c).astype(F32)
    blocks = [slice(k * CUM_CHUNK, (k + 1) * CUM_CHUNK) for k in range(seq // CUM_CHUNK)]
    local = [jnp.dot(lft_ref[0, :, sl], tri_u, preferred_element_type=F32, precision=lax.Precision.HIGHEST)
             for sl in blocks]
    carry = jnp.zeros((8, 1), F32)
    pad = jnp.zeros((LANES - 8, CUM_CHUNK), F32)
    for k, sl in enumerate(blocks):
        ct = (local[k] + carry) * LOG2E
        carry = local[k][:, CUM_CHUNK - 1:CUM_CHUNK] + carry
        for head in range(8):
            cumt_ref[0, head, :, sl] = ct[head:head + 1, :]
        cums_ref[0, sl, :] = jnp.concatenate([ct, pad], axis=0).T[:, :8]


def _forget_cumsum(lft):
    bsz, _, seq = lft.shape
    return pl.pallas_call(
        _cumsum_kernel,
        grid=(bsz,),
        in_specs=[pl.BlockSpec((1, 8, seq), lambda b: (b, 0, 0))],
        out_specs=[pl.BlockSpec((1, 8, 1, seq), lambda b: (b, 0, 0, 0)),
                   pl.BlockSpec((1, seq, 8), lambda b: (b, 0, 0))],
        out_shape=[jax.ShapeDtypeStruct((bsz, 8, 1, seq), F32),
                   jax.ShapeDtypeStruct((bsz, seq, 8), F32)],
        compiler_params=_cparams(("arbitrary",)),
        name="forget_cumsum",
    )(lft)


def _compress_kernel(kcin_ref, vcin_ref, pelo_ref, pehi_ref, w1lo_ref, w1hi_ref, w2_ref,
                     kc_ref, vct_ref):
    n_blk = kcin_ref.shape[1] // CMP_STRIDE
    for kv, src in enumerate((kcin_ref, vcin_ref)):
        lo, hi = [], []
        for j in range(CMP_STRIDE):
            piece = src[0, pl.ds(j, n_blk, stride=CMP_STRIDE), :]
            lo.append((piece + pelo_ref[kv, j:j + 1, :]).astype(BF16))
            hi.append((piece + pehi_ref[kv, j:j + 1, :]).astype(BF16))
        a = _dot(jnp.concatenate(lo, axis=1), w1lo_ref[kv])
        b = _dot(jnp.concatenate(hi, axis=1), w1hi_ref[kv])
        pre = a + pltpu.roll(b, n_blk - 1, 0)
        act = pre * _sigmoid(pre)
        out = _dot(act.astype(BF16), w2_ref[kv])
        if kv == 0:
            for g in range(NSA_GROUPS):
                kc_ref[0, g] = out[:, g * HEAD_DIM:(g + 1) * HEAD_DIM].astype(BF16)
        else:
            out_t = out.T
            for g in range(NSA_GROUPS):
                vct_ref[0, g] = out_t[g * HEAD_DIM:(g + 1) * HEAD_DIM].astype(BF16)


def _compress(kcin, vcin, pelo, pehi, w1lo, w1hi, w2):
    bsz, seq, _ = kcin.shape
    n_blk = seq // CMP_STRIDE
    full = lambda a: pl.BlockSpec(a.shape, lambda b: (0,) * a.ndim)
    return pl.pallas_call(
        _compress_kernel,
        grid=(bsz,),
        in_specs=[pl.BlockSpec((1, seq, LANES), lambda b: (b, 0, 0)),
                  pl.BlockSpec((1, seq, LANES), lambda b: (b, 0, 0)),
                  full(pelo), full(pehi), full(w1lo), full(w1hi), full(w2)],
        out_specs=[pl.BlockSpec((1, NSA_GROUPS, n_blk, HEAD_DIM), lambda b: (b, 0, 0, 0)),
                   pl.BlockSpec((1, NSA_GROUPS, HEAD_DIM, n_blk), lambda b: (b, 0, 0, 0))],
        out_shape=[jax.ShapeDtypeStruct((bsz, NSA_GROUPS, n_blk, HEAD_DIM), BF16),
                   jax.ShapeDtypeStruct((bsz, NSA_GROUPS, HEAD_DIM, n_blk), BF16)],
        compiler_params=_cparams(("arbitrary",)),
        name="nsa_compress",
    )(kcin, vcin, pelo, pehi, w1lo, w1hi, w2)


def _fold_rows(x, op):
    k, n = x.shape
    return op(x.reshape(k // FOLD_ROWS, FOLD_ROWS, n), axis=0)


def _reduce_rows(x, op):
    if x.shape[0] % FOLD_ROWS == 0 and x.shape[0] > FOLD_ROWS:
        x = _fold_rows(x, op)
    return op(x, axis=0, keepdims=True)


def _with_ones(v_t):
    return jnp.concatenate([v_t, jnp.ones((ONES_ROWS, v_t.shape[1]), v_t.dtype)], axis=0)


def _softmax_step(carry, s, v_aug):
    m, acc = carry
    m_new = jnp.maximum(m, _reduce_rows(s, jnp.max))
    alpha = jnp.exp2(m - m_new)
    p = jnp.exp2(s - m_new)
    acc = alpha * acc + _dot(v_aug, p.astype(BF16))
    return m_new, acc


def _softmax_init(n):
    return (jnp.full((1, n), NEG, F32), jnp.zeros((HEAD_DIM + ONES_ROWS, n), F32))


def _softmax_finish(carry):
    _, acc = carry
    return acc[:HEAD_DIM] * (1.0 / acc[HEAD_DIM:HEAD_DIM + 1])


def _tile_lanes(a, n):
    return jnp.concatenate([a] * n, axis=1)


def _dsa_kernel(iq_ref, ik_ref, iw_ref, dq_ref, dk_ref, dv_ref, g_ref, o_ref, key_ref):
    i = pl.program_id(1)
    tq = DSA_Q_TILE
    n_chunks = i + 1
    tpos = i * tq + lax.broadcasted_iota(I32, (1, tq), 1)
    srow = lax.broadcasted_iota(I32, (KEY_CHUNK, tq), 0)
    iw = iw_ref[0]

    def index_chunk(c, _):
        k0 = pl.multiple_of(c * KEY_CHUNK, KEY_CHUNK)
        ikc = ik_ref[0, pl.ds(k0, KEY_CHUNK), :]
        acc = jnp.zeros((KEY_CHUNK, tq), F32)
        for h in range(IDX_HEADS):
            x = _dot(ikc, iq_ref[0, h * IDX_DIM:(h + 1) * IDX_DIM, :])
            acc = acc + iw[h:h + 1, :] * jnp.maximum(x, 0.0)
        key_ref[pl.ds(k0, KEY_CHUNK), :] = jnp.where(srow + k0 <= tpos, acc, -jnp.inf)
        return 0

    lax.fori_loop(1, n_chunks, index_chunk, index_chunk(0, 0))

    def count(pred):
        def body(c, acc):
            k0 = pl.multiple_of(c * KEY_CHUNK, KEY_CHUNK)
            hit = pred(key_ref[pl.ds(k0, KEY_CHUNK), :], srow + k0)
            return acc + _fold_rows(jnp.where(hit, 1, 0), jnp.sum)
        acc = lax.fori_loop(1, n_chunks, body, body(0, jnp.zeros((FOLD_ROWS, tq), I32)))
        return jnp.sum(acc, axis=0, keepdims=True)

    def decode(code):
        skey = code ^ INT_MIN
        val = lax.bitcast_convert_type(jnp.where(skey < 0, skey ^ 0x7FFFFFFF, skey), F32)
        return jnp.where(jnp.logical_and(code >= 0, code <= CODE_NEG_INF), -jnp.inf, val)

    def value_bit(it, code):
        cand = code | lax.shift_left(jnp.int32(1), 31 - it)
        cand_f = decode(cand)
        cnt = count(lambda key, spos: key >= cand_f)
        return jnp.where(cnt >= DSA_TOPK, cand, code)

    thr = decode(lax.fori_loop(0, 32, value_bit, jnp.zeros((1, tq), I32)))

    need = (DSA_TOPK - count(lambda key, spos: key > thr)).astype(F32)
    r = lax.broadcasted_iota(I32, (KEY_CHUNK, KEY_CHUNK), 0)
    c = lax.broadcasted_iota(I32, (KEY_CHUNK, KEY_CHUNK), 1)
    prefix = jnp.where(c <= r, 1.0, 0.0).astype(BF16)

    def attend(c, carry):
        heads, ties_before = carry
        k0 = pl.multiple_of(c * KEY_CHUNK, KEY_CHUNK)
        key = key_ref[pl.ds(k0, KEY_CHUNK), :]
        tied = key == thr
        rank = _dot(prefix, jnp.where(tied, 1.0, 0.0).astype(BF16)) + ties_before
        keep = jnp.logical_or(key > thr, jnp.logical_and(tied, rank <= need))
        bias = jnp.where(jnp.logical_and(keep, srow + k0 <= tpos), 0.0, NEG)
        kc = dk_ref[0, pl.ds(k0, KEY_CHUNK), :]
        v_aug = _with_ones(dv_ref[0, c])
        heads = tuple(
            _softmax_step(heads[h], _dot(kc, dq_ref[0, h * HEAD_DIM:(h + 1) * HEAD_DIM, :]) + bias, v_aug)
            for h in range(DSA_HEADS))
        return heads, rank[KEY_CHUNK - 1:KEY_CHUNK, :]

    init = tuple(_softmax_init(tq) for _ in range(DSA_HEADS))
    heads, _ = lax.fori_loop(1, n_chunks, attend, attend(0, (init, jnp.zeros((1, tq), F32))))
    for h, carry in enumerate(heads):
        rows = slice(h * HEAD_DIM, (h + 1) * HEAD_DIM)
        o_ref[0, rows, :] = (_softmax_finish(carry) * g_ref[0, rows, :]).astype(BF16)


def _dsa(iq_t, ik, iw_t, dq_t, dk, dv_t, gate_t):
    bsz, _, seq = dq_t.shape
    tq = DSA_Q_TILE
    assert tq == KEY_CHUNK
    return pl.pallas_call(
        _dsa_kernel,
        grid=(bsz, seq // tq),
        in_specs=[
            pl.BlockSpec((1, IDX_W, tq), lambda b, i: (b, 0, i)),
            pl.BlockSpec((1, seq, IDX_DIM), lambda b, i: (b, 0, 0)),
            pl.BlockSpec((1, 8, tq), lambda b, i: (b, 0, i)),
            pl.BlockSpec((1, DSA_W, tq), lambda b, i: (b, 0, i)),
            pl.BlockSpec((1, seq, HEAD_DIM), lambda b, i: (b, 0, 0)),
            pl.BlockSpec((1, N_KEY_CHUNKS, HEAD_DIM, KEY_CHUNK), lambda b, i: (b, 0, 0, 0)),
            pl.BlockSpec((1, DSA_W, tq), lambda b, i: (b, 0, i)),
        ],
        out_specs=pl.BlockSpec((1, DSA_W, tq), lambda b, i: (b, 0, i)),
        out_shape=jax.ShapeDtypeStruct((bsz, DSA_W, seq), BF16),
        scratch_shapes=[pltpu.VMEM((seq, tq), F32)],
        compiler_params=_cparams(("arbitrary", "arbitrary")),
        name="dsa_attn",
    )(iq_t, ik, iw_t, dq_t, dk, dv_t, gate_t)


def _fox_kernel(q_ref, k_ref, v_ref, cumt_ref, cums_ref, g_ref, o_ref):
    h = pl.program_id(1)
    tq = FOX_Q_TILE
    n_tiles = q_ref.shape[2] // tq
    head_lane = lax.broadcasted_iota(I32, (KEY_CHUNK, 8), 1) == h
    srow = lax.broadcasted_iota(I32, (KEY_CHUNK, tq), 0)
    tcol = lax.broadcasted_iota(I32, (KEY_CHUNK, tq), 1)
    ccols = []
    for c in range(n_tiles):
        col = jnp.sum(jnp.where(head_lane, cums_ref[0, c * KEY_CHUNK:(c + 1) * KEY_CHUNK, :], 0.0),
                      axis=1, keepdims=True)
        ccols.append(_tile_lanes(jnp.broadcast_to(col, (KEY_CHUNK, LANES)), tq // LANES))
    v_aug = [_with_ones(v_ref[0, 0, c]) for c in range(n_tiles)]
    for i in range(n_tiles):
        q = q_ref[0, :, i * tq:(i + 1) * tq]
        crow = cumt_ref[0, 0, :, i * tq:(i + 1) * tq]
        carry = _softmax_init(tq)
        for c in range(i + 1):
            s = _dot(k_ref[0, 0, c * KEY_CHUNK:(c + 1) * KEY_CHUNK, :], q) + crow - ccols[c]
            if c == i:
                s = jnp.where(srow <= tcol, s, NEG)
            carry = _softmax_step(carry, s, v_aug[c])
        cols = slice(i * tq, (i + 1) * tq)
        o_ref[0, :, cols] = (_softmax_finish(carry) * g_ref[0, :, cols]).astype(BF16)


def _fox(fq_t, fk, fv_t, cum_t, cum_s, gate_t):
    bsz, _, seq = fq_t.shape
    assert FOX_Q_TILE == KEY_CHUNK
    return pl.pallas_call(
        _fox_kernel,
        grid=(bsz, FOX_HEADS),
        in_specs=[
            pl.BlockSpec((1, HEAD_DIM, seq), lambda b, h: (b, h, 0)),
            pl.BlockSpec((1, 1, seq, HEAD_DIM), lambda b, h: (b, h, 0, 0)),
            pl.BlockSpec((1, 1, N_KEY_CHUNKS, HEAD_DIM, KEY_CHUNK), lambda b, h: (b, h, 0, 0, 0)),
            pl.BlockSpec((1, 1, 1, seq), lambda b, h: (b, h, 0, 0)),
            pl.BlockSpec((1, seq, 8), lambda b, h: (b, 0, 0)),
            pl.BlockSpec((1, HEAD_DIM, seq), lambda b, h: (b, DSA_HEADS + h, 0)),
        ],
        out_specs=pl.BlockSpec((1, HEAD_DIM, seq), lambda b, h: (b, h, 0)),
        out_shape=jax.ShapeDtypeStruct((bsz, FOX_W, seq), BF16),
        compiler_params=_cparams(("arbitrary", "arbitrary")),
        name="fox_attn",
    )(fq_t, fk, fv_t, cum_t, cum_s, gate_t)


def _nsa_kernel(q_ref, qr_ref, kc_ref, vct_ref, ks_ref, vs_ref, kw_ref, vw_ref, g_ref,
                ovl_ref, expand_ref, og0_ref, og1_ref, og2_ref, o_ref, sel_ref):
    i = pl.program_id(2)
    nq = NSA_HPG * Q_TILE
    t0 = i * Q_TILE
    tpos = t0 + lax.broadcasted_iota(I32, (1, Q_TILE), 1)
    q3 = jnp.concatenate([q_ref[0, j * HEAD_DIM:(j + 1) * HEAD_DIM, :] for j in range(NSA_HPG)], axis=1)
    qr3 = jnp.concatenate([qr_ref[0, j * HEAD_DIM:(j + 1) * HEAD_DIM, :] for j in range(NSA_HPG)], axis=1)

    n_cmp = kc_ref.shape[2]
    cend = lax.broadcasted_iota(I32, (n_cmp, Q_TILE), 0) * CMP_STRIDE + (CMP_LEN - 1)
    cbias = _tile_lanes(jnp.where(cend <= tpos, 0.0, NEG), NSA_HPG)
    cvalid = _tile_lanes(jnp.where(cend <= tpos, 1.0, 0.0), NSA_HPG)
    s = _dot(kc_ref[0, 0], q3) + cbias
    m = _reduce_rows(s, jnp.max)
    e = jnp.exp2(s - m) * cvalid
    p_cmp = (e / jnp.maximum(_reduce_rows(e, jnp.sum), 1e-30)).astype(BF16)
    o_cmp = _dot(vct_ref[0, 0], p_cmp)

    p_stack = jnp.concatenate([p_cmp[:, j * Q_TILE:(j + 1) * Q_TILE] for j in range(NSA_HPG)], axis=0)
    score = _dot(ovl_ref[...], p_stack)
    blk = lax.broadcasted_iota(I32, (N_SEL_BLOCKS, Q_TILE), 0)
    cur = lax.shift_right_logical(tpos, 6)
    forced = jnp.logical_or(blk == 0, jnp.logical_or(blk == cur, blk == cur - 1))
    score = jnp.where(forced, jnp.inf, jnp.where(blk > cur, -jnp.inf, score))
    rank = jnp.zeros((N_SEL_BLOCKS, Q_TILE), F32)
    for mth in range(N_SEL_BLOCKS):
        row = score[mth:mth + 1, :]
        ahead = jnp.logical_or(row > score, jnp.logical_and(row == score, blk > mth))
        rank = rank + jnp.where(ahead, 1.0, 0.0)
    sel_ref[...] = jnp.where(rank < SEL_N, 1.0, 0.0)

    n_chunks = (i * Q_TILE + Q_TILE + KEY_CHUNK - 1) // KEY_CHUNK
    srow = lax.broadcasted_iota(I32, (KEY_CHUNK, Q_TILE), 0)
    blocks_per_chunk = KEY_CHUNK // SEL_BLOCK

    def attend_sel(c, carry):
        k0 = pl.multiple_of(c * KEY_CHUNK, KEY_CHUNK)
        b0 = pl.multiple_of(c * blocks_per_chunk, blocks_per_chunk)
        picked = _dot(expand_ref[...], sel_ref[pl.ds(b0, blocks_per_chunk), :])
        ok = jnp.logical_and(picked > 0.5, srow + k0 <= tpos)
        bias = _tile_lanes(jnp.where(ok, 0.0, NEG), NSA_HPG)
        s = _dot(ks_ref[0, 0, pl.ds(k0, KEY_CHUNK), :], qr3) + bias
        return _softmax_step(carry, s, _with_ones(vs_ref[0, 0, c]))

    wc = jnp.maximum(i - WINDOW // Q_TILE, 0) * (Q_TILE // WIN_CHUNK)
    w0 = pl.multiple_of(wc * WIN_CHUNK, WIN_CHUNK)
    kpos = w0 + lax.broadcasted_iota(I32, (WIN_SPAN, Q_TILE), 0)
    wok = jnp.logical_and(kpos <= tpos, kpos > tpos - WINDOW)
    s = _dot(kw_ref[0, 0, pl.ds(w0, WIN_SPAN), :], qr3) + _tile_lanes(jnp.where(wok, 0.0, NEG), NSA_HPG)
    m = _reduce_rows(s, jnp.max)
    e = jnp.exp2(s - m).astype(BF16)
    acc = jnp.zeros((HEAD_DIM + ONES_ROWS, nq), F32)
    for j in range(WIN_SPAN // WIN_CHUNK):
        acc = acc + _dot(_with_ones(vw_ref[0, 0, wc + j]), e[j * WIN_CHUNK:(j + 1) * WIN_CHUNK, :])
    o_win = _softmax_finish((m, acc))

    first = attend_sel(0, _softmax_init(nq))
    o_slc = _softmax_finish(lax.fori_loop(1, n_chunks, attend_sel, first))

    gts = g_ref[0, 0]
    out_gates = (og0_ref, og1_ref, og2_ref)
    for j in range(NSA_HPG):
        sl = slice(j * Q_TILE, (j + 1) * Q_TILE)
        mixed = (gts[j:j + 1, :] * o_cmp[:, sl]
                 + gts[NSA_HPG + j:NSA_HPG + j + 1, :] * o_slc[:, sl]
                 + gts[2 * NSA_HPG + j:2 * NSA_HPG + j + 1, :] * o_win[:, sl])
        o_ref[0, j * HEAD_DIM:(j + 1) * HEAD_DIM, :] = (mixed * out_gates[j][0]).astype(BF16)


def _nsa(nq_t, nqr_t, kc, vc_t, ks, vs_t, kw, vw_t, gts, ovl, expand, gate_t):
    bsz, _, seq = nq_t.shape
    n_cmp = kc.shape[2]
    gw = NSA_HPG * HEAD_DIM
    per_group = lambda shape: pl.BlockSpec((1, 1) + shape, lambda b, g, i: (b, g) + (0,) * len(shape))
    return pl.pallas_call(
        _nsa_kernel,
        grid=(bsz, NSA_GROUPS, seq // Q_TILE),
        in_specs=[
            pl.BlockSpec((1, gw, Q_TILE), lambda b, g, i: (b, g, i)),
            pl.BlockSpec((1, gw, Q_TILE), lambda b, g, i: (b, g, i)),
            per_group((n_cmp, HEAD_DIM)),
            per_group((HEAD_DIM, n_cmp)),
            per_group((seq, HEAD_DIM)),
            per_group((N_KEY_CHUNKS, HEAD_DIM, KEY_CHUNK)),
            per_group((seq, HEAD_DIM)),
            per_group((N_WIN_CHUNKS, HEAD_DIM, WIN_CHUNK)),
            pl.BlockSpec((1, 1, 16, Q_TILE), lambda b, g, i: (b, g, 0, i)),
            pl.BlockSpec(ovl.shape, lambda b, g, i: (0, 0)),
            pl.BlockSpec(expand.shape, lambda b, g, i: (0, 0)),
        ] + [
            pl.BlockSpec((1, HEAD_DIM, Q_TILE),
                         lambda b, g, i, j=j: (b, DSA_HEADS + FOX_HEADS + g * NSA_HPG + j, i))
            for j in range(NSA_HPG)
        ],
        out_specs=pl.BlockSpec((1, gw, Q_TILE), lambda b, g, i: (b, g, i)),
        out_shape=jax.ShapeDtypeStruct((bsz, NSA_W, seq), BF16),
        scratch_shapes=[pltpu.VMEM((N_SEL_BLOCKS, Q_TILE), F32)],
        compiler_params=_cparams(("arbitrary", "arbitrary", "arbitrary")),
        name="nsa_attn",
    )(nq_t, nqr_t, kc, vc_t, ks, vs_t, kw, vw_t, gts, ovl, expand, gate_t, gate_t, gate_t)


def _out_kernel(od_ref, of_ref, on_ref, w_ref, x_ref, gm_ref, lg_ref, lb_ref, o_ref):
    half = x_ref.shape[1] // 2
    for t0 in (0, half):
        cols = slice(t0, t0 + half)
        z = jnp.concatenate([od_ref[0, :, cols], of_ref[0, :, cols], on_ref[0, :, cols]], axis=0)
        y = lax.dot_general(z, w_ref[...], (((0,), (0,)), ((), ())),
                            preferred_element_type=F32)
        r = ALPHA * x_ref[0, cols] + (1.0 + gm_ref[0]) * y
        mu = jnp.mean(r, axis=-1, keepdims=True)
        rc = r - mu
        var = jnp.mean(rc * rc, axis=-1, keepdims=True)
        o_ref[0, cols] = rc * lax.rsqrt(var + LN_EPS) * lg_ref[...] + lb_ref[...]


def _output(od_t, of_t, on_t, w_out, x, gmod, ln_g, ln_b):
    bsz, seq, d = x.shape
    tm = PROJ_TOKENS
    feat = lambda rows: pl.BlockSpec((1, rows, tm), lambda b, i: (b, 0, i))
    return pl.pallas_call(
        _out_kernel,
        grid=(bsz, seq // tm),
        in_specs=[
            feat(DSA_W), feat(FOX_W), feat(NSA_W),
            pl.BlockSpec((MIX_W, d), lambda b, i: (0, 0)),
            pl.BlockSpec((1, tm, d), lambda b, i: (b, i, 0)),
            pl.BlockSpec((1, 1, d), lambda b, i: (b, 0, 0)),
            pl.BlockSpec((1, d), lambda b, i: (0, 0)),
            pl.BlockSpec((1, d), lambda b, i: (0, 0)),
        ],
        out_specs=pl.BlockSpec((1, tm, d), lambda b, i: (b, i, 0)),
        out_shape=jax.ShapeDtypeStruct((bsz, seq, d), F32),
        compiler_params=_cparams(("arbitrary", "arbitrary")),
        name="out_proj_ln",
    )(od_t, of_t, on_t, w_out, x, gmod, ln_g, ln_b)


def _split_rows(w):
    out, off = {}, 0
    for name, width in IN_SPLITS:
        out[name] = w[off:off + width]
        off += width
    return out


def _prep_w_in(w_in):
    d = w_in.shape[0]
    p = _split_rows(w_in.T.astype(BF16))
    z = lambda n: jnp.zeros((n, d), BF16)
    grows = p["nsa_g"].reshape(3, NSA_GROUPS, NSA_HPG, d)
    gparts = []
    for g in range(NSA_GROUPS):
        gparts += [grows[:, g].reshape(3 * NSA_HPG, d), z(16 - 3 * NSA_HPG)]
    wt = jnp.concatenate(
        [p["dsa_q"], p["idx_q"], p["fox_q"], p["nsa_q"],
         p["dsa_v"], z(HEAD_DIM), p["fox_v"], p["nsa_vs"], p["nsa_vw"], p["gate"],
         p["idx_w"], p["fox_f"], z(8 - FOX_HEADS)] + gparts, axis=0)
    assert wt.shape[0] == T_ROWS
    ws_t = jnp.concatenate(
        [p["dsa_k"], z(LANES - HEAD_DIM), p["idx_k"], z(LANES - IDX_DIM), p["fox_k"],
         p["nsa_ks"], p["nsa_kw"], p["nsa_kc"], p["nsa_vc"]], axis=0)
    assert ws_t.shape[0] == S_COLS
    return wt, ws_t


def _rope_tables(seq):
    pos = jnp.arange(seq, dtype=F32)
    feature_major, token_major = [], []
    lane = np.arange(LANES)
    for hd in (HEAD_DIM, IDX_DIM):
        half = hd // 2
        inv = ROPE_THETA ** (-jnp.arange(half, dtype=F32) / half)
        ang = inv[:, None] * pos[None, :]
        cos, sin = jnp.cos(ang), jnp.sin(ang)
        feature_major += [cos, sin]
        sign = jnp.asarray(np.where((lane % hd) < half, -1.0, 1.0), F32)
        reps = LANES // half
        token_major += [jnp.tile(cos.T, (1, reps)), jnp.tile(sin.T, (1, reps)) * sign[None, :]]
    return tuple(feature_major + token_major)


def _prep_compress(cmp_pe, cmp_w1, cmp_w2):
    assert NSA_GROUPS == 2

    def block_diag(w):
        z = jnp.zeros_like(w)
        return jnp.concatenate([jnp.concatenate([w, z], axis=-1), jnp.concatenate([z, w], axis=-1)], axis=-2)

    big1 = block_diag(cmp_w1.astype(BF16).reshape(2, CMP_LEN, HEAD_DIM, HEAD_DIM))
    w1lo = big1[:, :CMP_STRIDE].reshape(2, CMP_STRIDE * LANES, LANES)
    w1hi = big1[:, CMP_STRIDE:].reshape(2, CMP_STRIDE * LANES, LANES)
    w2 = block_diag(cmp_w2.astype(BF16))
    pe2 = jnp.concatenate([cmp_pe] * NSA_GROUPS, axis=-1)
    return pe2[:, :CMP_STRIDE], pe2[:, CMP_STRIDE:], w1lo, w1hi, w2


def _selection_constants(n_cmp):
    cstart = np.arange(n_cmp) * CMP_STRIDE
    bstart = np.arange(N_SEL_BLOCKS) * SEL_BLOCK
    real = (np.arange(n_cmp) < (SEQ - CMP_LEN) // CMP_STRIDE + 1)[:, None]
    ovl = ((cstart[:, None] < bstart[None, :] + SEL_BLOCK) & (cstart[:, None] + CMP_LEN > bstart[None, :]) & real)
    ovl_t = np.concatenate([ovl.T.astype(np.float32)] * NSA_HPG, axis=1)
    expand = (np.arange(KEY_CHUNK)[:, None] // SEL_BLOCK == np.arange(KEY_CHUNK // SEL_BLOCK)[None, :])
    return jnp.asarray(ovl_t, BF16), jnp.asarray(expand, F32)


def kernel(x, c, w_ada, b_ada, w_in, b_f, cmp_pe, cmp_w1, cmp_w2, w_out, ln_g, ln_b):
    bsz, seq, d = x.shape
    assert (seq, d) == (SEQ, D_MODEL)
    mod = _modulation(c, w_ada, b_ada)
    tabs = _rope_tables(seq)
    ovl, expand = _selection_constants(seq // CMP_STRIDE)
    for l in range(DEPTH):
        shift = mod[l, :, :d].reshape(bsz, 1, d)
        scale = mod[l, :, d:2 * d].reshape(bsz, 1, d)
        gmod = mod[l, :, 2 * d:].reshape(bsz, 1, d)
        wt, ws = _prep_w_in(w_in[l])
        bf8 = jnp.concatenate([b_f[l], jnp.zeros((8 - FOX_HEADS,), F32)])
        (dq, iq, fq, nq, nqr, dv, fv, vs, vw, gate, iw, lft, gts,
         dk, ik, fk, ks, kw, kcin, vcin) = _projection(
            x, scale, shift, wt, ws, tabs, bf8.reshape(8, 1))
        cum_t, cum_s = _forget_cumsum(lft)
        kc, vc_t = _compress(kcin, vcin, *_prep_compress(cmp_pe[l], cmp_w1[l], cmp_w2[l]))
        o_dsa = _dsa(iq, ik, iw, dq, dk, dv, gate)
        o_fox = _fox(fq, fk, fv, cum_t, cum_s, gate)
        o_nsa = _nsa(nq, nqr, kc, vc_t, ks, vs, kw, vw, gts, ovl, expand, gate)
        x = _output(o_dsa, o_fox, o_nsa, w_out[l].astype(BF16), x, gmod,
                    ln_g[l].reshape(1, d), ln_b[l].reshape(1, d))
    return x
```

```python
import jax
import jax.numpy as jnp
import numpy as np
from jax import lax
from jax.experimental import pallas as pl
from jax.experimental.pallas import tpu as pltpu

F32 = jnp.float32
BF16 = jnp.bfloat16
I32 = jnp.int32

D_MODEL = 1024
SEQ = 2048
DEPTH = 2
HEAD_DIM = 64
DSA_HEADS = 4
DSA_TOPK = 256
IDX_HEADS = 8
IDX_DIM = 32
FOX_HEADS = 6
NSA_HEADS = 6
NSA_GROUPS = 2
NSA_HPG = NSA_HEADS // NSA_GROUPS
CMP_LEN = 32
CMP_STRIDE = 16
SEL_BLOCK = 64
SEL_N = 16
N_SEL_BLOCKS = SEQ // SEL_BLOCK
WINDOW = 512
ROPE_THETA = 10000.0
LN_EPS = 1e-5
ALPHA = (2.0 * DEPTH) ** 0.25

DSA_W = DSA_HEADS * HEAD_DIM
FOX_W = FOX_HEADS * HEAD_DIM
NSA_W = NSA_HEADS * HEAD_DIM
NSA_KV_W = NSA_GROUPS * HEAD_DIM
MIX_W = DSA_W + FOX_W + NSA_W
IDX_W = IDX_HEADS * IDX_DIM

IN_SPLITS = (
    ("dsa_q", DSA_W), ("dsa_k", HEAD_DIM), ("dsa_v", HEAD_DIM),
    ("idx_q", IDX_W), ("idx_k", IDX_DIM), ("idx_w", IDX_HEADS),
    ("fox_q", FOX_W), ("fox_k", FOX_W), ("fox_v", FOX_W), ("fox_f", FOX_HEADS),
    ("nsa_q", NSA_W),
    ("nsa_kc", NSA_KV_W), ("nsa_vc", NSA_KV_W),
    ("nsa_ks", NSA_KV_W), ("nsa_vs", NSA_KV_W),
    ("nsa_kw", NSA_KV_W), ("nsa_vw", NSA_KV_W),
    ("nsa_g", 3 * NSA_HEADS),
    ("gate", MIX_W),
)

LANES = 128
KEY_CHUNK = 512
PROJ_TOKENS = 512
Q_TILE = 256
DSA_Q_TILE = 512
FOX_Q_TILE = 512
N_KEY_CHUNKS = SEQ // KEY_CHUNK
WIN_CHUNK = 256
N_WIN_CHUNKS = SEQ // WIN_CHUNK
WIN_SPAN = WINDOW + Q_TILE
CUM_CHUNK = 256
VMEM_LIMIT = 56 * 1024 * 1024

FOLD_ROWS = 32
ONES_ROWS = 16
LOG2E = 1.4426950408889634
NEG = -(2.0 ** 100)
INT_MIN = -(2 ** 31)
CODE_NEG_INF = 0x007FFFFF

T_DQ, T_IQ, T_FQ, T_NQ = 0, 256, 512, 896
T_DV, T_FV, T_VS, T_VW = 1280, 1408, 1792, 1920
T_GATE, T_SMALL, T_ROWS = 2048, 3072, 3120
S_DK, S_IK, S_FK, S_KS, S_KW, S_KC, S_VC, S_COLS = 0, 128, 256, 640, 768, 896, 1024, 1152


def _cparams(sem):
    return pltpu.CompilerParams(dimension_semantics=sem, vmem_limit_bytes=VMEM_LIMIT)


def _dot(a, b):
    return jnp.dot(a, b, preferred_element_type=F32)


def _dot_tn_nt(a, b):
    return lax.dot_general(a, b, (((0,), (1,)), ((), ())), preferred_element_type=F32)


def _log_sigmoid(x):
    return jnp.minimum(x, 0.0) - jnp.log(1.0 + jnp.exp(-jnp.abs(x)))


def _sigmoid(x):
    return 1.0 / (1.0 + jnp.exp(-x))


def _mod_kernel(c_ref, w_ref, b_ref, o_ref):
    o_ref[0] = _dot(c_ref[...].astype(BF16), w_ref[0].astype(BF16)) + b_ref[0]


def _modulation(c, w_ada, b_ada):
    depth, d, d3 = w_ada.shape
    bsz = c.shape[0]
    return pl.pallas_call(
        _mod_kernel,
        grid=(depth, d3 // d),
        in_specs=[
            pl.BlockSpec((bsz, d), lambda l, j: (0, 0)),
            pl.BlockSpec((1, d, d), lambda l, j: (l, 0, j)),
            pl.BlockSpec((1, 1, d), lambda l, j: (l, 0, j)),
        ],
        out_specs=pl.BlockSpec((1, bsz, d), lambda l, j: (l, 0, j)),
        out_shape=jax.ShapeDtypeStruct((depth, bsz, d3), F32),
        compiler_params=_cparams(("arbitrary", "arbitrary")),
        name="adaln_mod",
    )(c, w_ada, b_ada.reshape(depth, 1, d3))


def _proj_kernel(x_ref, sc_ref, sh_ref, wt_ref, ws_ref,
                 ct64_ref, st64_ref, ct32_ref, st32_ref,
                 ck64_ref, sk64_ref, ck32_ref, sk32_ref, bfc_ref,
                 dq_ref, iq_ref, fq_ref, nq_ref, nqr_ref,
                 dv_ref, fv_ref, vs_ref, vw_ref, gate_ref,
                 iw_ref, lft_ref, gts_ref,
                 dk_ref, ik_ref, fk_ref, ks_ref, kw_ref, kcin_ref, vcin_ref):
    tm = x_ref.shape[1]
    u = (x_ref[0] * (1.0 + sc_ref[0]) + sh_ref[0]).astype(BF16)

    ht_all = _dot_tn_nt(wt_ref[...], u)

    def proj_t(r0, r1):
        return ht_all[r0:r1]

    def rope_t(h, n_heads, hd, c, s):
        half = hd // 2
        out = []
        for hh in range(n_heads):
            x1 = h[hh * hd:hh * hd + half]
            x2 = h[hh * hd + half:(hh + 1) * hd]
            out.append((hh * hd, x1 * c - x2 * s))
            out.append((hh * hd + half, x1 * s + x2 * c))
        return out

    c64, s64 = ct64_ref[...], st64_ref[...]
    c32, s32 = ct32_ref[...], st32_ref[...]
    qscale = HEAD_DIM ** -0.5 * LOG2E

    h = proj_t(T_DQ, T_DQ + DSA_W)
    for r, v in rope_t(h, DSA_HEADS, HEAD_DIM, c64, s64):
        dq_ref[0, r:r + HEAD_DIM // 2, :] = (v * qscale).astype(BF16)
    h = proj_t(T_IQ, T_IQ + IDX_W)
    for r, v in rope_t(h, IDX_HEADS, IDX_DIM, c32, s32):
        iq_ref[0, r:r + IDX_DIM // 2, :] = v.astype(BF16)
    fq_ref[0] = (proj_t(T_FQ, T_FQ + FOX_W) * qscale).astype(BF16)
    h = proj_t(T_NQ, T_NQ + NSA_W)
    nq_ref[0] = (h * qscale).astype(BF16)
    for r, v in rope_t(h, NSA_HEADS, HEAD_DIM, c64, s64):
        nqr_ref[0, r:r + HEAD_DIM // 2, :] = (v * qscale).astype(BF16)

    dv_ref[0, 0] = proj_t(T_DV, T_DV + HEAD_DIM).astype(BF16)
    h = proj_t(T_FV, T_FV + FOX_W)
    for hh in range(FOX_HEADS):
        fv_ref[0, hh, 0] = h[hh * HEAD_DIM:(hh + 1) * HEAD_DIM].astype(BF16)
    h = proj_t(T_VS, T_VS + NSA_KV_W)
    for g in range(NSA_GROUPS):
        vs_ref[0, g, 0] = h[g * HEAD_DIM:(g + 1) * HEAD_DIM].astype(BF16)
    h = proj_t(T_VW, T_VW + NSA_KV_W)
    for g in range(NSA_GROUPS):
        for j in range(tm // WIN_CHUNK):
            vw_ref[0, g, j] = h[g * HEAD_DIM:(g + 1) * HEAD_DIM, j * WIN_CHUNK:(j + 1) * WIN_CHUNK].astype(BF16)

    for r0 in range(0, MIX_W, 256):
        h = proj_t(T_GATE + r0, T_GATE + r0 + 256)
        gate_ref[0, r0:r0 + 256, :] = h * _sigmoid(h)

    h = proj_t(T_SMALL, T_ROWS)
    iw_ref[0] = h[0:8] * (IDX_HEADS ** -0.5)
    lft_ref[0] = _log_sigmoid(h[8:16] + bfc_ref[...])
    g_all = _sigmoid(h[16:48])
    gts_ref[0, 0] = g_all[0:16]
    gts_ref[0, 1] = g_all[16:32]

    lane = lax.broadcasted_iota(I32, (tm, LANES), 1)

    def rope_s(g, half, c, s_signed):
        first = (lane & (2 * half - 1)) < half
        sw = jnp.where(first, pltpu.roll(g, LANES - half, 1), pltpu.roll(g, half, 1))
        return g * c + sw * s_signed

    hs_all = _dot(u, ws_ref[...])

    def proj_s(c0, c1):
        return hs_all[:, c0:c1]

    ck64, sk64 = ck64_ref[...], sk64_ref[...]
    g = rope_s(proj_s(S_DK, S_DK + LANES), HEAD_DIM // 2, ck64, sk64)
    dk_ref[0] = g[:, :HEAD_DIM].astype(BF16)
    g = rope_s(proj_s(S_IK, S_IK + LANES), IDX_DIM // 2, ck32_ref[...], sk32_ref[...])
    ik_ref[0] = g[:, :IDX_DIM].astype(BF16)
    g = proj_s(S_FK, S_FK + FOX_W)
    for hh in range(FOX_HEADS):
        fk_ref[0, hh] = g[:, hh * HEAD_DIM:(hh + 1) * HEAD_DIM].astype(BF16)
    g = rope_s(proj_s(S_KS, S_KS + LANES), HEAD_DIM // 2, ck64, sk64)
    for gg in range(NSA_GROUPS):
        ks_ref[0, gg] = g[:, gg * HEAD_DIM:(gg + 1) * HEAD_DIM].astype(BF16)
    g = rope_s(proj_s(S_KW, S_KW + LANES), HEAD_DIM // 2, ck64, sk64)
    for gg in range(NSA_GROUPS):
        kw_ref[0, gg] = g[:, gg * HEAD_DIM:(gg + 1) * HEAD_DIM].astype(BF16)
    kcin_ref[0] = proj_s(S_KC, S_KC + LANES)
    vcin_ref[0] = proj_s(S_VC, S_VC + LANES)


def _projection(x, scale, shift, wt, ws, tabs, bfc):
    bsz, seq, d = x.shape
    tm = PROJ_TOKENS
    nt = seq // tm
    ct64, st64, ct32, st32, ck64, sk64, ck32, sk32 = tabs
    const = lambda shape: pl.BlockSpec(shape, lambda b, i: (0,) * len(shape))
    feat = lambda rows: pl.BlockSpec((1, rows, tm), lambda b, i: (b, 0, i))
    in_specs = [
        pl.BlockSpec((1, tm, d), lambda b, i: (b, i, 0)),
        pl.BlockSpec((1, 1, d), lambda b, i: (b, 0, 0)),
        pl.BlockSpec((1, 1, d), lambda b, i: (b, 0, 0)),
        const((d, T_ROWS)),
        const((d, S_COLS)),
        pl.BlockSpec((HEAD_DIM // 2, tm), lambda b, i: (0, i)),
        pl.BlockSpec((HEAD_DIM // 2, tm), lambda b, i: (0, i)),
        pl.BlockSpec((IDX_DIM // 2, tm), lambda b, i: (0, i)),
        pl.BlockSpec((IDX_DIM // 2, tm), lambda b, i: (0, i)),
        pl.BlockSpec((tm, LANES), lambda b, i: (i, 0)),
        pl.BlockSpec((tm, LANES), lambda b, i: (i, 0)),
        pl.BlockSpec((tm, LANES), lambda b, i: (i, 0)),
        pl.BlockSpec((tm, LANES), lambda b, i: (i, 0)),
        const((8, 1)),
    ]
    sds = jax.ShapeDtypeStruct
    out_shape = [
        sds((bsz, DSA_W, seq), BF16), sds((bsz, IDX_W, seq), BF16), sds((bsz, FOX_W, seq), BF16),
        sds((bsz, NSA_W, seq), BF16), sds((bsz, NSA_W, seq), BF16),
        sds((bsz, N_KEY_CHUNKS, HEAD_DIM, KEY_CHUNK), BF16),
        sds((bsz, FOX_HEADS, N_KEY_CHUNKS, HEAD_DIM, KEY_CHUNK), BF16),
        sds((bsz, NSA_GROUPS, N_KEY_CHUNKS, HEAD_DIM, KEY_CHUNK), BF16),
        sds((bsz, NSA_GROUPS, N_WIN_CHUNKS, HEAD_DIM, WIN_CHUNK), BF16),
        sds((bsz, MIX_W, seq), F32),
        sds((bsz, 8, seq), F32), sds((bsz, 8, seq), F32), sds((bsz, NSA_GROUPS, 16, seq), F32),
        sds((bsz, seq, HEAD_DIM), BF16), sds((bsz, seq, IDX_DIM), BF16),
        sds((bsz, FOX_HEADS, seq, HEAD_DIM), BF16),
        sds((bsz, NSA_GROUPS, seq, HEAD_DIM), BF16), sds((bsz, NSA_GROUPS, seq, HEAD_DIM), BF16),
        sds((bsz, seq, LANES), F32), sds((bsz, seq, LANES), F32),
    ]
    wpc = tm // WIN_CHUNK
    out_specs = [
        feat(DSA_W), feat(IDX_W), feat(FOX_W), feat(NSA_W), feat(NSA_W),
        pl.BlockSpec((1, 1, HEAD_DIM, KEY_CHUNK), lambda b, i: (b, i, 0, 0)),
        pl.BlockSpec((1, FOX_HEADS, 1, HEAD_DIM, KEY_CHUNK), lambda b, i: (b, 0, i, 0, 0)),
        pl.BlockSpec((1, NSA_GROUPS, 1, HEAD_DIM, KEY_CHUNK), lambda b, i: (b, 0, i, 0, 0)),
        pl.BlockSpec((1, NSA_GROUPS, wpc, HEAD_DIM, WIN_CHUNK), lambda b, i: (b, 0, i, 0, 0)),
        feat(MIX_W),
        feat(8), feat(8),
        pl.BlockSpec((1, NSA_GROUPS, 16, tm), lambda b, i: (b, 0, 0, i)),
        pl.BlockSpec((1, tm, HEAD_DIM), lambda b, i: (b, i, 0)),
        pl.BlockSpec((1, tm, IDX_DIM), lambda b, i: (b, i, 0)),
        pl.BlockSpec((1, FOX_HEADS, tm, HEAD_DIM), lambda b, i: (b, 0, i, 0)),
        pl.BlockSpec((1, NSA_GROUPS, tm, HEAD_DIM), lambda b, i: (b, 0, i, 0)),
        pl.BlockSpec((1, NSA_GROUPS, tm, HEAD_DIM), lambda b, i: (b, 0, i, 0)),
        pl.BlockSpec((1, tm, LANES), lambda b, i: (b, i, 0)),
        pl.BlockSpec((1, tm, LANES), lambda b, i: (b, i, 0)),
    ]
    assert tm == KEY_CHUNK
    return pl.pallas_call(
        _proj_kernel,
        grid=(bsz, nt),
        in_specs=in_specs,
        out_specs=out_specs,
        out_shape=out_shape,
        compiler_params=_cparams(("arbitrary", "arbitrary")),
        name="in_proj",
    )(x, scale, shift, wt, ws, ct64, st64, ct32, st32, ck64, sk64, ck32, sk32, bfc)


def _cumsum_kernel(lft_ref, cumt_ref, cums_ref):
    seq = lft_ref.shape[2]
    r = lax.broadcasted_iota(I32, (CUM_CHUNK, CUM_CHUNK), 0)
    c = lax.broadcasted_iota(I32, (CUM_CHUNK, CUM_CHUNK), 1)
    tri_u = (r <= c).astype(F32)
    blocks = [slice(k * CUM_CHUNK, (k + 1) * CUM_CHUNK) for k in range(seq // CUM_CHUNK)]
    local = [jnp.dot(lft_ref[0, :, sl], tri_u, preferred_element_type=F32, precision=lax.Precision.HIGHEST)
             for sl in blocks]
    carry = jnp.zeros((8, 1), F32)
    pad = jnp.zeros((LANES - 8, CUM_CHUNK), F32)
    for k, sl in enumerate(blocks):
        ct = (local[k] + carry) * LOG2E
        carry = local[k][:, CUM_CHUNK - 1:CUM_CHUNK] + carry
        for head in range(8):
            cumt_ref[0, head, :, sl] = ct[head:head + 1, :]
        cums_ref[0, sl, :] = jnp.concatenate([ct, pad], axis=0).T[:, :8]


def _forget_cumsum(lft):
    bsz, _, seq = lft.shape
    return pl.pallas_call(
        _cumsum_kernel,
        grid=(bsz,),
        in_specs=[pl.BlockSpec((1, 8, seq), lambda b: (b, 0, 0))],
        out_specs=[pl.BlockSpec((1, 8, 1, seq), lambda b: (b, 0, 0, 0)),
                   pl.BlockSpec((1, seq, 8), lambda b: (b, 0, 0))],
        out_shape=[jax.ShapeDtypeStruct((bsz, 8, 1, seq), F32),
                   jax.ShapeDtypeStruct((bsz, seq, 8), F32)],
        compiler_params=_cparams(("arbitrary",)),
        name="forget_cumsum",
    )(lft)


def _compress_kernel(kcin_ref, vcin_ref, pelo_ref, pehi_ref, w1lo_ref, w1hi_ref, w2_ref,
                     kc_ref, vct_ref):
    n_blk = kcin_ref.shape[1] // CMP_STRIDE
    for kv, src in enumerate((kcin_ref, vcin_ref)):
        lo, hi = [], []
        for j in range(CMP_STRIDE):
            piece = src[0, pl.ds(j, n_blk, stride=CMP_STRIDE), :]
            lo.append((piece + pelo_ref[kv, j:j + 1, :]).astype(BF16))
            hi.append((piece + pehi_ref[kv, j:j + 1, :]).astype(BF16))
        a = _dot(jnp.concatenate(lo, axis=1), w1lo_ref[kv])
        b = _dot(jnp.concatenate(hi, axis=1), w1hi_ref[kv])
        pre = a + pltpu.roll(b, n_blk - 1, 0)
        act = pre * _sigmoid(pre)
        out = _dot(act.astype(BF16), w2_ref[kv])
        if kv == 0:
            for g in range(NSA_GROUPS):
                kc_ref[0, g] = out[:, g * HEAD_DIM:(g + 1) * HEAD_DIM].astype(BF16)
        else:
            out_t = out.T
            for g in range(NSA_GROUPS):
                vct_ref[0, g] = out_t[g * HEAD_DIM:(g + 1) * HEAD_DIM].astype(BF16)


def _compress(kcin, vcin, pelo, pehi, w1lo, w1hi, w2):
    bsz, seq, _ = kcin.shape
    n_blk = seq // CMP_STRIDE
    full = lambda a: pl.BlockSpec(a.shape, lambda b: (0,) * a.ndim)
    return pl.pallas_call(
        _compress_kernel,
        grid=(bsz,),
        in_specs=[pl.BlockSpec((1, seq, LANES), lambda b: (b, 0, 0)),
                  pl.BlockSpec((1, seq, LANES), lambda b: (b, 0, 0)),
                  full(pelo), full(pehi), full(w1lo), full(w1hi), full(w2)],
        out_specs=[pl.BlockSpec((1, NSA_GROUPS, n_blk, HEAD_DIM), lambda b: (b, 0, 0, 0)),
                   pl.BlockSpec((1, NSA_GROUPS, HEAD_DIM, n_blk), lambda b: (b, 0, 0, 0))],
        out_shape=[jax.ShapeDtypeStruct((bsz, NSA_GROUPS, n_blk, HEAD_DIM), BF16),
                   jax.ShapeDtypeStruct((bsz, NSA_GROUPS, HEAD_DIM, n_blk), BF16)],
        compiler_params=_cparams(("arbitrary",)),
        name="nsa_compress",
    )(kcin, vcin, pelo, pehi, w1lo, w1hi, w2)


def _fold_rows(x, op):
    k, n = x.shape
    return op(x.reshape(k // FOLD_ROWS, FOLD_ROWS, n), axis=0)


def _reduce_rows(x, op):
    if x.shape[0] % FOLD_ROWS == 0 and x.shape[0] > FOLD_ROWS:
        x = _fold_rows(x, op)
    return op(x, axis=0, keepdims=True)


def _with_ones(v_t):
    return jnp.concatenate([v_t, jnp.ones((ONES_ROWS, v_t.shape[1]), v_t.dtype)], axis=0)


def _softmax_step(carry, s, v_aug):
    m, acc = carry
    m_new = jnp.maximum(m, _reduce_rows(s, jnp.max))
    alpha = jnp.exp2(m - m_new)
    p = jnp.exp2(s - m_new)
    acc = alpha * acc + _dot(v_aug, p.astype(BF16))
    return m_new, acc


def _softmax_init(n):
    return (jnp.full((1, n), NEG, F32), jnp.zeros((HEAD_DIM + ONES_ROWS, n), F32))


def _softmax_finish(carry):
    _, acc = carry
    return acc[:HEAD_DIM] * (1.0 / acc[HEAD_DIM:HEAD_DIM + 1])


def _tile_lanes(a, n):
    return jnp.concatenate([a] * n, axis=1)


def _dsa_kernel(iq_ref, ik_ref, iw_ref, dq_ref, dk_ref, dv_ref, g_ref, o_ref, key_ref):
    i = pl.program_id(1)
    tq = DSA_Q_TILE
    n_chunks = i + 1
    tpos = i * tq + lax.broadcasted_iota(I32, (1, tq), 1)
    srow = lax.broadcasted_iota(I32, (KEY_CHUNK, tq), 0)
    iw = iw_ref[0]

    def index_chunk(c, _):
        k0 = pl.multiple_of(c * KEY_CHUNK, KEY_CHUNK)
        ikc = ik_ref[0, pl.ds(k0, KEY_CHUNK), :]
        acc = jnp.zeros((KEY_CHUNK, tq), F32)
        for h in range(IDX_HEADS):
            x = _dot(ikc, iq_ref[0, h * IDX_DIM:(h + 1) * IDX_DIM, :])
            acc = acc + iw[h:h + 1, :] * jnp.maximum(x, 0.0)
        key_ref[pl.ds(k0, KEY_CHUNK), :] = jnp.where(srow + k0 <= tpos, acc, -jnp.inf)
        return 0

    lax.fori_loop(1, n_chunks, index_chunk, index_chunk(0, 0))

    def count(pred):
        def body(c, acc):
            k0 = pl.multiple_of(c * KEY_CHUNK, KEY_CHUNK)
            hit = pred(key_ref[pl.ds(k0, KEY_CHUNK), :], srow + k0)
            return acc + _fold_rows(jnp.where(hit, 1, 0), jnp.sum)
        acc = lax.fori_loop(1, n_chunks, body, body(0, jnp.zeros((FOLD_ROWS, tq), I32)))
        return jnp.sum(acc, axis=0, keepdims=True)

    def decode(code):
        skey = code ^ INT_MIN
        val = lax.bitcast_convert_type(jnp.where(skey < 0, skey ^ 0x7FFFFFFF, skey), F32)
        return jnp.where(jnp.logical_and(code >= 0, code <= CODE_NEG_INF), -jnp.inf, val)

    def value_bit(it, code):
        cand = code | lax.shift_left(jnp.int32(1), 31 - it)
        cand_f = decode(cand)
        cnt = count(lambda key, spos: key >= cand_f)
        return jnp.where(cnt >= DSA_TOPK, cand, code)

    thr = decode(lax.fori_loop(0, 32, value_bit, jnp.zeros((1, tq), I32)))

    need = (DSA_TOPK - count(lambda key, spos: key > thr)).astype(F32)
    r = lax.broadcasted_iota(I32, (KEY_CHUNK, KEY_CHUNK), 0)
    c = lax.broadcasted_iota(I32, (KEY_CHUNK, KEY_CHUNK), 1)
    prefix = jnp.where(c <= r, 1.0, 0.0).astype(BF16)

    def attend(c, carry):
        heads, ties_before = carry
        k0 = pl.multiple_of(c * KEY_CHUNK, KEY_CHUNK)
        key = key_ref[pl.ds(k0, KEY_CHUNK), :]
        tied = key == thr
        rank = _dot(prefix, jnp.where(tied, 1.0, 0.0).astype(BF16)) + ties_before
        keep = jnp.logical_or(key > thr, jnp.logical_and(tied, rank <= need))
        bias = jnp.where(jnp.logical_and(keep, srow + k0 <= tpos), 0.0, NEG)
        kc = dk_ref[0, pl.ds(k0, KEY_CHUNK), :]
        v_aug = _with_ones(dv_ref[0, c])
        heads = tuple(
            _softmax_step(heads[h], _dot(kc, dq_ref[0, h * HEAD_DIM:(h + 1) * HEAD_DIM, :]) + bias, v_aug)
            for h in range(DSA_HEADS))
        return heads, rank[KEY_CHUNK - 1:KEY_CHUNK, :]

    init = tuple(_softmax_init(tq) for _ in range(DSA_HEADS))
    heads, _ = lax.fori_loop(1, n_chunks, attend, attend(0, (init, jnp.zeros((1, tq), F32))))
    for h, carry in enumerate(heads):
        rows = slice(h * HEAD_DIM, (h + 1) * HEAD_DIM)
        o_ref[0, rows, :] = (_softmax_finish(carry) * g_ref[0, rows, :]).astype(BF16)


def _dsa(iq_t, ik, iw_t, dq_t, dk, dv_t, gate_t):
    bsz, _, seq = dq_t.shape
    tq = DSA_Q_TILE
    assert tq == KEY_CHUNK
    return pl.pallas_call(
        _dsa_kernel,
        grid=(bsz, seq // tq),
        in_specs=[
            pl.BlockSpec((1, IDX_W, tq), lambda b, i: (b, 0, i)),
            pl.BlockSpec((1, seq, IDX_DIM), lambda b, i: (b, 0, 0)),
            pl.BlockSpec((1, 8, tq), lambda b, i: (b, 0, i)),
            pl.BlockSpec((1, DSA_W, tq), lambda b, i: (b, 0, i)),
            pl.BlockSpec((1, seq, HEAD_DIM), lambda b, i: (b, 0, 0)),
            pl.BlockSpec((1, N_KEY_CHUNKS, HEAD_DIM, KEY_CHUNK), lambda b, i: (b, 0, 0, 0)),
            pl.BlockSpec((1, DSA_W, tq), lambda b, i: (b, 0, i)),
        ],
        out_specs=pl.BlockSpec((1, DSA_W, tq), lambda b, i: (b, 0, i)),
        out_shape=jax.ShapeDtypeStruct((bsz, DSA_W, seq), BF16),
        scratch_shapes=[pltpu.VMEM((seq, tq), F32)],
        compiler_params=_cparams(("arbitrary", "arbitrary")),
        name="dsa_attn",
    )(iq_t, ik, iw_t, dq_t, dk, dv_t, gate_t)


def _fox_kernel(q_ref, k_ref, v_ref, cumt_ref, cums_ref, g_ref, o_ref):
    h = pl.program_id(1)
    tq = FOX_Q_TILE
    n_tiles = q_ref.shape[2] // tq
    head_lane = lax.broadcasted_iota(I32, (KEY_CHUNK, 8), 1) == h
    srow = lax.broadcasted_iota(I32, (KEY_CHUNK, tq), 0)
    tcol = lax.broadcasted_iota(I32, (KEY_CHUNK, tq), 1)
    ccols = []
    for c in range(n_tiles):
        col = jnp.sum(jnp.where(head_lane, cums_ref[0, c * KEY_CHUNK:(c + 1) * KEY_CHUNK, :], 0.0),
                      axis=1, keepdims=True)
        ccols.append(_tile_lanes(jnp.broadcast_to(col, (KEY_CHUNK, LANES)), tq // LANES))
    v_aug = [_with_ones(v_ref[0, 0, c]) for c in range(n_tiles)]
    for i in range(n_tiles):
        q = q_ref[0, :, i * tq:(i + 1) * tq]
        crow = cumt_ref[0, 0, :, i * tq:(i + 1) * tq]
        carry = _softmax_init(tq)
        for c in range(i + 1):
            s = _dot(k_ref[0, 0, c * KEY_CHUNK:(c + 1) * KEY_CHUNK, :], q) + crow - ccols[c]
            if c == i:
                s = jnp.where(srow <= tcol, s, NEG)
            carry = _softmax_step(carry, s, v_aug[c])
        cols = slice(i * tq, (i + 1) * tq)
        o_ref[0, :, cols] = (_softmax_finish(carry) * g_ref[0, :, cols]).astype(BF16)


def _fox(fq_t, fk, fv_t, cum_t, cum_s, gate_t):
    bsz, _, seq = fq_t.shape
    assert FOX_Q_TILE == KEY_CHUNK
    return pl.pallas_call(
        _fox_kernel,
        grid=(bsz, FOX_HEADS),
        in_specs=[
            pl.BlockSpec((1, HEAD_DIM, seq), lambda b, h: (b, h, 0)),
            pl.BlockSpec((1, 1, seq, HEAD_DIM), lambda b, h: (b, h, 0, 0)),
            pl.BlockSpec((1, 1, N_KEY_CHUNKS, HEAD_DIM, KEY_CHUNK), lambda b, h: (b, h, 0, 0, 0)),
            pl.BlockSpec((1, 1, 1, seq), lambda b, h: (b, h, 0, 0)),
            pl.BlockSpec((1, seq, 8), lambda b, h: (b, 0, 0)),
            pl.BlockSpec((1, HEAD_DIM, seq), lambda b, h: (b, DSA_HEADS + h, 0)),
        ],
        out_specs=pl.BlockSpec((1, HEAD_DIM, seq), lambda b, h: (b, h, 0)),
        out_shape=jax.ShapeDtypeStruct((bsz, FOX_W, seq), BF16),
        compiler_params=_cparams(("arbitrary", "arbitrary")),
        name="fox_attn",
    )(fq_t, fk, fv_t, cum_t, cum_s, gate_t)


def _nsa_kernel(q_ref, qr_ref, kc_ref, vct_ref, ks_ref, vs_ref, kw_ref, vw_ref, g_ref,
                ovl_ref, expand_ref, og0_ref, og1_ref, og2_ref, o_ref, sel_ref):
    i = pl.program_id(2)
    nq = NSA_HPG * Q_TILE
    t0 = i * Q_TILE
    tpos = t0 + lax.broadcasted_iota(I32, (1, Q_TILE), 1)
    q3 = jnp.concatenate([q_ref[0, j * HEAD_DIM:(j + 1) * HEAD_DIM, :] for j in range(NSA_HPG)], axis=1)
    qr3 = jnp.concatenate([qr_ref[0, j * HEAD_DIM:(j + 1) * HEAD_DIM, :] for j in range(NSA_HPG)], axis=1)

    n_cmp = kc_ref.shape[2]
    cend = lax.broadcasted_iota(I32, (n_cmp, Q_TILE), 0) * CMP_STRIDE + (CMP_LEN - 1)
    cbias = _tile_lanes(jnp.where(cend <= tpos, 0.0, NEG), NSA_HPG)
    cvalid = _tile_lanes(jnp.where(cend <= tpos, 1.0, 0.0), NSA_HPG)
    s = _dot(kc_ref[0, 0], q3) + cbias
    m = _reduce_rows(s, jnp.max)
    e = jnp.exp2(s - m) * cvalid
    p_cmp = (e / jnp.maximum(_reduce_rows(e, jnp.sum), 1e-30)).astype(BF16)
    o_cmp = _dot(vct_ref[0, 0], p_cmp)

    p_stack = jnp.concatenate([p_cmp[:, j * Q_TILE:(j + 1) * Q_TILE] for j in range(NSA_HPG)], axis=0)
    score = _dot(ovl_ref[...], p_stack)
    blk = lax.broadcasted_iota(I32, (N_SEL_BLOCKS, Q_TILE), 0)
    cur = lax.shift_right_logical(tpos, 6)
    forced = jnp.logical_or(blk == 0, jnp.logical_or(blk == cur, blk == cur - 1))
    score = jnp.where(forced, jnp.inf, jnp.where(blk > cur, -jnp.inf, score))
    rank = jnp.zeros((N_SEL_BLOCKS, Q_TILE), F32)
    for mth in range(N_SEL_BLOCKS):
        row = score[mth:mth + 1, :]
        ahead = jnp.logical_or(row > score, jnp.logical_and(row == score, blk > mth))
        rank = rank + jnp.where(ahead, 1.0, 0.0)
    sel_ref[...] = jnp.where(rank < SEL_N, 1.0, 0.0)

    n_chunks = (i * Q_TILE + Q_TILE + KEY_CHUNK - 1) // KEY_CHUNK
    srow = lax.broadcasted_iota(I32, (KEY_CHUNK, Q_TILE), 0)
    blocks_per_chunk = KEY_CHUNK // SEL_BLOCK

    def attend_sel(c, carry):
        k0 = pl.multiple_of(c * KEY_CHUNK, KEY_CHUNK)
        b0 = pl.multiple_of(c * blocks_per_chunk, blocks_per_chunk)
        picked = _dot(expand_ref[...], sel_ref[pl.ds(b0, blocks_per_chunk), :])
        ok = jnp.logical_and(picked > 0.5, srow + k0 <= tpos)
        bias = _tile_lanes(jnp.where(ok, 0.0, NEG), NSA_HPG)
        s = _dot(ks_ref[0, 0, pl.ds(k0, KEY_CHUNK), :], qr3) + bias
        return _softmax_step(carry, s, _with_ones(vs_ref[0, 0, c]))

    wc = jnp.maximum(i - WINDOW // Q_TILE, 0) * (Q_TILE // WIN_CHUNK)
    w0 = pl.multiple_of(wc * WIN_CHUNK, WIN_CHUNK)
    kpos = w0 + lax.broadcasted_iota(I32, (WIN_SPAN, Q_TILE), 0)
    wok = jnp.logical_and(kpos <= tpos, kpos > tpos - WINDOW)
    s = _dot(kw_ref[0, 0, pl.ds(w0, WIN_SPAN), :], qr3) + _tile_lanes(jnp.where(wok, 0.0, NEG), NSA_HPG)
    m = _reduce_rows(s, jnp.max)
    e = jnp.exp2(s - m).astype(BF16)
    acc = jnp.zeros((HEAD_DIM + ONES_ROWS, nq), F32)
    for j in range(WIN_SPAN // WIN_CHUNK):
        acc = acc + _dot(_with_ones(vw_ref[0, 0, wc + j]), e[j * WIN_CHUNK:(j + 1) * WIN_CHUNK, :])
    o_win = _softmax_finish((m, acc))

    first = attend_sel(0, _softmax_init(nq))
    o_slc = _softmax_finish(lax.fori_loop(1, n_chunks, attend_sel, first))

    gts = g_ref[0, 0]
    out_gates = (og0_ref, og1_ref, og2_ref)
    for j in range(NSA_HPG):
        sl = slice(j * Q_TILE, (j + 1) * Q_TILE)
        mixed = (gts[j:j + 1, :] * o_cmp[:, sl]
                 + gts[NSA_HPG + j:NSA_HPG + j + 1, :] * o_slc[:, sl]
                 + gts[2 * NSA_HPG + j:2 * NSA_HPG + j + 1, :] * o_win[:, sl])
        o_ref[0, j * HEAD_DIM:(j + 1) * HEAD_DIM, :] = (mixed * out_gates[j][0]).astype(BF16)


def _nsa(nq_t, nqr_t, kc, vc_t, ks, vs_t, kw, vw_t, gts, ovl, expand, gate_t):
    bsz, _, seq = nq_t.shape
    n_cmp = kc.shape[2]
    gw = NSA_HPG * HEAD_DIM
    per_group = lambda shape: pl.BlockSpec((1, 1) + shape, lambda b, g, i: (b, g) + (0,) * len(shape))
    return pl.pallas_call(
        _nsa_kernel,
        grid=(bsz, NSA_GROUPS, seq // Q_TILE),
        in_specs=[
            pl.BlockSpec((1, gw, Q_TILE), lambda b, g, i: (b, g, i)),
            pl.BlockSpec((1, gw, Q_TILE), lambda b, g, i: (b, g, i)),
            per_group((n_cmp, HEAD_DIM)),
            per_group((HEAD_DIM, n_cmp)),
            per_group((seq, HEAD_DIM)),
            per_group((N_KEY_CHUNKS, HEAD_DIM, KEY_CHUNK)),
            per_group((seq, HEAD_DIM)),
            per_group((N_WIN_CHUNKS, HEAD_DIM, WIN_CHUNK)),
            pl.BlockSpec((1, 1, 16, Q_TILE), lambda b, g, i: (b, g, 0, i)),
            pl.BlockSpec(ovl.shape, lambda b, g, i: (0, 0)),
            pl.BlockSpec(expand.shape, lambda b, g, i: (0, 0)),
        ] + [
            pl.BlockSpec((1, HEAD_DIM, Q_TILE),
                         lambda b, g, i, j=j: (b, DSA_HEADS + FOX_HEADS + g * NSA_HPG + j, i))
            for j in range(NSA_HPG)
        ],
        out_specs=pl.BlockSpec((1, gw, Q_TILE), lambda b, g, i: (b, g, i)),
        out_shape=jax.ShapeDtypeStruct((bsz, NSA_W, seq), BF16),
        scratch_shapes=[pltpu.VMEM((N_SEL_BLOCKS, Q_TILE), F32)],
        compiler_params=_cparams(("arbitrary", "arbitrary", "arbitrary")),
        name="nsa_attn",
    )(nq_t, nqr_t, kc, vc_t, ks, vs_t, kw, vw_t, gts, ovl, expand, gate_t, gate_t, gate_t)


def _out_kernel(od_ref, of_ref, on_ref, w_ref, x_ref, gm_ref, lg_ref, lb_ref, o_ref):
    z = jnp.concatenate([od_ref[0], of_ref[0], on_ref[0]], axis=0)
    y = lax.dot_general(z, w_ref[...], (((0,), (0,)), ((), ())),
                        preferred_element_type=F32)
    r = ALPHA * x_ref[0] + (1.0 + gm_ref[0]) * y
    mu = jnp.mean(r, axis=-1, keepdims=True)
    rc = r - mu
    var = jnp.mean(rc * rc, axis=-1, keepdims=True)
    o_ref[0] = rc * lax.rsqrt(var + LN_EPS) * lg_ref[...] + lb_ref[...]


def _output(od_t, of_t, on_t, w_out, x, gmod, ln_g, ln_b):
    bsz, seq, d = x.shape
    tm = PROJ_TOKENS
    feat = lambda rows: pl.BlockSpec((1, rows, tm), lambda b, i: (b, 0, i))
    return pl.pallas_call(
        _out_kernel,
        grid=(bsz, seq // tm),
        in_specs=[
            feat(DSA_W), feat(FOX_W), feat(NSA_W),
            pl.BlockSpec((MIX_W, d), lambda b, i: (0, 0)),
            pl.BlockSpec((1, tm, d), lambda b, i: (b, i, 0)),
            pl.BlockSpec((1, 1, d), lambda b, i: (b, 0, 0)),
            pl.BlockSpec((1, d), lambda b, i: (0, 0)),
            pl.BlockSpec((1, d), lambda b, i: (0, 0)),
        ],
        out_specs=pl.BlockSpec((1, tm, d), lambda b, i: (b, i, 0)),
        out_shape=jax.ShapeDtypeStruct((bsz, seq, d), F32),
        compiler_params=_cparams(("arbitrary", "arbitrary")),
        name="out_proj_ln",
    )(od_t, of_t, on_t, w_out, x, gmod, ln_g, ln_b)


def _split_cols(w):
    out, off = {}, 0
    for name, width in IN_SPLITS:
        out[name] = w[:, off:off + width]
        off += width
    return out


def _prep_w_in(w_in):
    d = w_in.shape[0]
    p = _split_cols(w_in.astype(BF16))
    z = lambda n: jnp.zeros((d, n), BF16)
    gcols = p["nsa_g"].reshape(d, 3, NSA_GROUPS, NSA_HPG)
    gparts = []
    for g in range(NSA_GROUPS):
        gparts += [gcols[:, :, g].reshape(d, 3 * NSA_HPG), z(16 - 3 * NSA_HPG)]
    wt = jnp.concatenate(
        [p["dsa_q"], p["idx_q"], p["fox_q"], p["nsa_q"],
         p["dsa_v"], z(HEAD_DIM), p["fox_v"], p["nsa_vs"], p["nsa_vw"], p["gate"],
         p["idx_w"], p["fox_f"], z(8 - FOX_HEADS)] + gparts, axis=1)
    assert wt.shape[1] == T_ROWS
    ws = jnp.concatenate(
        [p["dsa_k"], z(LANES - HEAD_DIM), p["idx_k"], z(LANES - IDX_DIM), p["fox_k"],
         p["nsa_ks"], p["nsa_kw"], p["nsa_kc"], p["nsa_vc"]], axis=1)
    assert ws.shape[1] == S_COLS
    return wt, ws


def _rope_tables(seq):
    pos = jnp.arange(seq, dtype=F32)
    feature_major, token_major = [], []
    lane = np.arange(LANES)
    for hd in (HEAD_DIM, IDX_DIM):
        half = hd // 2
        inv = ROPE_THETA ** (-jnp.arange(half, dtype=F32) / half)
        ang = inv[:, None] * pos[None, :]
        cos, sin = jnp.cos(ang), jnp.sin(ang)
        feature_major += [cos, sin]
        sign = jnp.asarray(np.where((lane % hd) < half, -1.0, 1.0), F32)
        reps = LANES // half
        token_major += [jnp.tile(cos.T, (1, reps)), jnp.tile(sin.T, (1, reps)) * sign[None, :]]
    return tuple(feature_major + token_major)


def _prep_compress(cmp_pe, cmp_w1, cmp_w2):
    assert NSA_GROUPS == 2

    def block_diag(w):
        z = jnp.zeros_like(w)
        return jnp.concatenate([jnp.concatenate([w, z], axis=-1), jnp.concatenate([z, w], axis=-1)], axis=-2)

    big1 = block_diag(cmp_w1.astype(BF16).reshape(2, CMP_LEN, HEAD_DIM, HEAD_DIM))
    w1lo = big1[:, :CMP_STRIDE].reshape(2, CMP_STRIDE * LANES, LANES)
    w1hi = big1[:, CMP_STRIDE:].reshape(2, CMP_STRIDE * LANES, LANES)
    w2 = block_diag(cmp_w2.astype(BF16))
    pe2 = jnp.concatenate([cmp_pe] * NSA_GROUPS, axis=-1)
    return pe2[:, :CMP_STRIDE], pe2[:, CMP_STRIDE:], w1lo, w1hi, w2


def _selection_constants(n_cmp):
    cstart = np.arange(n_cmp) * CMP_STRIDE
    bstart = np.arange(N_SEL_BLOCKS) * SEL_BLOCK
    real = (np.arange(n_cmp) < (SEQ - CMP_LEN) // CMP_STRIDE + 1)[:, None]
    ovl = ((cstart[:, None] < bstart[None, :] + SEL_BLOCK) & (cstart[:, None] + CMP_LEN > bstart[None, :]) & real)
    ovl_t = np.concatenate([ovl.T.astype(np.float32)] * NSA_HPG, axis=1)
    expand = (np.arange(KEY_CHUNK)[:, None] // SEL_BLOCK == np.arange(KEY_CHUNK // SEL_BLOCK)[None, :])
    return jnp.asarray(ovl_t, BF16), jnp.asarray(expand, F32)


def kernel(x, c, w_ada, b_ada, w_in, b_f, cmp_pe, cmp_w1, cmp_w2, w_out, ln_g, ln_b):
    bsz, seq, d = x.shape
    assert (seq, d) == (SEQ, D_MODEL)
    mod = _modulation(c, w_ada, b_ada)
    tabs = _rope_tables(seq)
    ovl, expand = _selection_constants(seq // CMP_STRIDE)
    for l in range(DEPTH):
        shift = mod[l, :, :d].reshape(bsz, 1, d)
        scale = mod[l, :, d:2 * d].reshape(bsz, 1, d)
        gmod = mod[l, :, 2 * d:].reshape(bsz, 1, d)
        wt, ws = _prep_w_in(w_in[l])
        bf8 = jnp.concatenate([b_f[l], jnp.zeros((8 - FOX_HEADS,), F32)])
        (dq, iq, fq, nq, nqr, dv, fv, vs, vw, gate, iw, lft, gts,
         dk, ik, fk, ks, kw, kcin, vcin) = _projection(
            x, scale, shift, wt, ws, tabs, bf8.reshape(8, 1))
        cum_t, cum_s = _forget_cumsum(lft)
        kc, vc_t = _compress(kcin, vcin, *_prep_compress(cmp_pe[l], cmp_w1[l], cmp_w2[l]))
        o_dsa = _dsa(iq, ik, iw, dq, dk, dv, gate)
        o_fox = _fox(fq, fk, fv, cum_t, cum_s, gate)
        o_nsa = _nsa(nq, nqr, kc, vc_t, ks, vs, kw, vw, gts, ovl, expand, gate)
        x = _output(o_dsa, o_fox, o_nsa, w_out[l].astype(BF16), x, gmod,
                    ln_g[l].reshape(1, d), ln_b[l].reshape(1, d))
    return x
```

```python
import jax
import jax.numpy as jnp
import numpy as np
from jax import lax
from jax.experimental import pallas as pl
from jax.experimental.pallas import tpu as pltpu

F32 = jnp.float32
BF16 = jnp.bfloat16
I32 = jnp.int32

D_MODEL = 1024
SEQ = 2048
DEPTH = 2
HEAD_DIM = 64
DSA_HEADS = 4
DSA_TOPK = 256
IDX_HEADS = 8
IDX_DIM = 32
FOX_HEADS = 6
NSA_HEADS = 6
NSA_GROUPS = 2
NSA_HPG = NSA_HEADS // NSA_GROUPS
CMP_LEN = 32
CMP_STRIDE = 16
SEL_BLOCK = 64
SEL_N = 16
N_SEL_BLOCKS = SEQ // SEL_BLOCK
WINDOW = 512
ROPE_THETA = 10000.0
LN_EPS = 1e-5
ALPHA = (2.0 * DEPTH) ** 0.25

DSA_W = DSA_HEADS * HEAD_DIM
FOX_W = FOX_HEADS * HEAD_DIM
NSA_W = NSA_HEADS * HEAD_DIM
NSA_KV_W = NSA_GROUPS * HEAD_DIM
MIX_W = DSA_W + FOX_W + NSA_W
IDX_W = IDX_HEADS * IDX_DIM

IN_SPLITS = (
    ("dsa_q", DSA_W), ("dsa_k", HEAD_DIM), ("dsa_v", HEAD_DIM),
    ("idx_q", IDX_W), ("idx_k", IDX_DIM), ("idx_w", IDX_HEADS),
    ("fox_q", FOX_W), ("fox_k", FOX_W), ("fox_v", FOX_W), ("fox_f", FOX_HEADS),
    ("nsa_q", NSA_W),
    ("nsa_kc", NSA_KV_W), ("nsa_vc", NSA_KV_W),
    ("nsa_ks", NSA_KV_W), ("nsa_vs", NSA_KV_W),
    ("nsa_kw", NSA_KV_W), ("nsa_vw", NSA_KV_W),
    ("nsa_g", 3 * NSA_HEADS),
    ("gate", MIX_W),
)

LANES = 128
KEY_CHUNK = 512
PROJ_TOKENS = 512
Q_TILE = 256
DSA_Q_TILE = 512
FOX_Q_TILE = 512
N_KEY_CHUNKS = SEQ // KEY_CHUNK
WIN_CHUNK = 256
N_WIN_CHUNKS = SEQ // WIN_CHUNK
WIN_SPAN = WINDOW + Q_TILE
CUM_CHUNK = 256
VMEM_LIMIT = 56 * 1024 * 1024

FOLD_ROWS = 32
ONES_ROWS = 16
LOG2E = 1.4426950408889634
NEG = -(2.0 ** 100)
INT_MIN = -(2 ** 31)
CODE_NEG_INF = 0x007FFFFF

T_DQ, T_IQ, T_FQ, T_NQ = 0, 256, 512, 896
T_DV, T_FV, T_VS, T_VW = 1280, 1408, 1792, 1920
T_GATE, T_SMALL, T_ROWS = 2048, 3072, 3120
S_DK, S_IK, S_FK, S_KS, S_KW, S_KC, S_VC, S_COLS = 0, 128, 256, 640, 768, 896, 1024, 1152


def _cparams(sem):
    return pltpu.CompilerParams(dimension_semantics=sem, vmem_limit_bytes=VMEM_LIMIT)


def _dot(a, b):
    return jnp.dot(a, b, preferred_element_type=F32)


def _dot_tn_nt(a, b):
    return lax.dot_general(a, b, (((0,), (1,)), ((), ())), preferred_element_type=F32)


def _log_sigmoid(x):
    return jnp.minimum(x, 0.0) - jnp.log(1.0 + jnp.exp(-jnp.abs(x)))


def _sigmoid(x):
    return 1.0 / (1.0 + jnp.exp(-x))


def _mod_kernel(c_ref, w_ref, b_ref, o_ref):
    o_ref[0] = _dot(c_ref[...].astype(BF16), w_ref[0].astype(BF16)) + b_ref[0]


def _modulation(c, w_ada, b_ada):
    depth, d, d3 = w_ada.shape
    bsz = c.shape[0]
    return pl.pallas_call(
        _mod_kernel,
        grid=(depth, d3 // d),
        in_specs=[
            pl.BlockSpec((bsz, d), lambda l, j: (0, 0)),
            pl.BlockSpec((1, d, d), lambda l, j: (l, 0, j)),
            pl.BlockSpec((1, 1, d), lambda l, j: (l, 0, j)),
        ],
        out_specs=pl.BlockSpec((1, bsz, d), lambda l, j: (l, 0, j)),
        out_shape=jax.ShapeDtypeStruct((depth, bsz, d3), F32),
        compiler_params=_cparams(("arbitrary", "arbitrary")),
        name="adaln_mod",
    )(c, w_ada, b_ada.reshape(depth, 1, d3))


def _proj_kernel(x_ref, sc_ref, sh_ref, wt_ref, ws_ref,
                 ct64_ref, st64_ref, ct32_ref, st32_ref,
                 ck64_ref, sk64_ref, ck32_ref, sk32_ref, bfc_ref,
                 dq_ref, iq_ref, fq_ref, nq_ref, nqr_ref,
                 dv_ref, fv_ref, vs_ref, vw_ref, gate_ref,
                 iw_ref, lft_ref, gts_ref,
                 dk_ref, ik_ref, fk_ref, ks_ref, kw_ref, kcin_ref, vcin_ref):
    tm = x_ref.shape[1]
    u = (x_ref[0] * (1.0 + sc_ref[0]) + sh_ref[0]).astype(BF16)

    ht_all = _dot_tn_nt(wt_ref[...], u)

    def proj_t(r0, r1):
        return ht_all[r0:r1]

    def rope_t(h, n_heads, hd, c, s):
        half = hd // 2
        out = []
        for hh in range(n_heads):
            x1 = h[hh * hd:hh * hd + half]
            x2 = h[hh * hd + half:(hh + 1) * hd]
            out.append((hh * hd, x1 * c - x2 * s))
            out.append((hh * hd + half, x1 * s + x2 * c))
        return out

    c64, s64 = ct64_ref[...], st64_ref[...]
    c32, s32 = ct32_ref[...], st32_ref[...]
    qscale = HEAD_DIM ** -0.5 * LOG2E

    h = proj_t(T_DQ, T_DQ + DSA_W)
    for r, v in rope_t(h, DSA_HEADS, HEAD_DIM, c64, s64):
        dq_ref[0, r:r + HEAD_DIM // 2, :] = (v * qscale).astype(BF16)
    h = proj_t(T_IQ, T_IQ + IDX_W)
    for r, v in rope_t(h, IDX_HEADS, IDX_DIM, c32, s32):
        iq_ref[0, r:r + IDX_DIM // 2, :] = v.astype(BF16)
    fq_ref[0] = (proj_t(T_FQ, T_FQ + FOX_W) * qscale).astype(BF16)
    h = proj_t(T_NQ, T_NQ + NSA_W)
    nq_ref[0] = (h * qscale).astype(BF16)
    for r, v in rope_t(h, NSA_HEADS, HEAD_DIM, c64, s64):
        nqr_ref[0, r:r + HEAD_DIM // 2, :] = (v * qscale).astype(BF16)

    dv_ref[0, 0] = proj_t(T_DV, T_DV + HEAD_DIM).astype(BF16)
    h = proj_t(T_FV, T_FV + FOX_W)
    for hh in range(FOX_HEADS):
        fv_ref[0, hh, 0] = h[hh * HEAD_DIM:(hh + 1) * HEAD_DIM].astype(BF16)
    h = proj_t(T_VS, T_VS + NSA_KV_W)
    for g in range(NSA_GROUPS):
        vs_ref[0, g, 0] = h[g * HEAD_DIM:(g + 1) * HEAD_DIM].astype(BF16)
    h = proj_t(T_VW, T_VW + NSA_KV_W)
    for g in range(NSA_GROUPS):
        for j in range(tm // WIN_CHUNK):
            vw_ref[0, g, j] = h[g * HEAD_DIM:(g + 1) * HEAD_DIM, j * WIN_CHUNK:(j + 1) * WIN_CHUNK].astype(BF16)

    for r0 in range(0, MIX_W, 256):
        h = proj_t(T_GATE + r0, T_GATE + r0 + 256)
        gate_ref[0, r0:r0 + 256, :] = h * _sigmoid(h)

    h = proj_t(T_SMALL, T_ROWS)
    iw_ref[0] = h[0:8] * (IDX_HEADS ** -0.5)
    lft_ref[0] = _log_sigmoid(h[8:16] + bfc_ref[...])
    g_all = _sigmoid(h[16:48])
    gts_ref[0, 0] = g_all[0:16]
    gts_ref[0, 1] = g_all[16:32]

    lane = lax.broadcasted_iota(I32, (tm, LANES), 1)

    def rope_s(g, half, c, s_signed):
        first = (lane & (2 * half - 1)) < half
        sw = jnp.where(first, pltpu.roll(g, LANES - half, 1), pltpu.roll(g, half, 1))
        return g * c + sw * s_signed

    hs_all = _dot(u, ws_ref[...])

    def proj_s(c0, c1):
        return hs_all[:, c0:c1]

    ck64, sk64 = ck64_ref[...], sk64_ref[...]
    g = rope_s(proj_s(S_DK, S_DK + LANES), HEAD_DIM // 2, ck64, sk64)
    dk_ref[0] = g[:, :HEAD_DIM].astype(BF16)
    g = rope_s(proj_s(S_IK, S_IK + LANES), IDX_DIM // 2, ck32_ref[...], sk32_ref[...])
    ik_ref[0] = g[:, :IDX_DIM].astype(BF16)
    g = proj_s(S_FK, S_FK + FOX_W)
    for hh in range(FOX_HEADS):
        fk_ref[0, hh] = g[:, hh * HEAD_DIM:(hh + 1) * HEAD_DIM].astype(BF16)
    g = rope_s(proj_s(S_KS, S_KS + LANES), HEAD_DIM // 2, ck64, sk64)
    for gg in range(NSA_GROUPS):
        ks_ref[0, gg] = g[:, gg * HEAD_DIM:(gg + 1) * HEAD_DIM].astype(BF16)
    g = rope_s(proj_s(S_KW, S_KW + LANES), HEAD_DIM // 2, ck64, sk64)
    for gg in range(NSA_GROUPS):
        kw_ref[0, gg] = g[:, gg * HEAD_DIM:(gg + 1) * HEAD_DIM].astype(BF16)
    kcin_ref[0] = proj_s(S_KC, S_KC + LANES)
    vcin_ref[0] = proj_s(S_VC, S_VC + LANES)


def _projection(x, scale, shift, wt, ws, tabs, bfc):
    bsz, seq, d = x.shape
    tm = PROJ_TOKENS
    nt = seq // tm
    ct64, st64, ct32, st32, ck64, sk64, ck32, sk32 = tabs
    const = lambda shape: pl.BlockSpec(shape, lambda b, i: (0,) * len(shape))
    feat = lambda rows: pl.BlockSpec((1, rows, tm), lambda b, i: (b, 0, i))
    in_specs = [
        pl.BlockSpec((1, tm, d), lambda b, i: (b, i, 0)),
        pl.BlockSpec((1, 1, d), lambda b, i: (b, 0, 0)),
        pl.BlockSpec((1, 1, d), lambda b, i: (b, 0, 0)),
        const((d, T_ROWS)),
        const((d, S_COLS)),
        pl.BlockSpec((HEAD_DIM // 2, tm), lambda b, i: (0, i)),
        pl.BlockSpec((HEAD_DIM // 2, tm), lambda b, i: (0, i)),
        pl.BlockSpec((IDX_DIM // 2, tm), lambda b, i: (0, i)),
        pl.BlockSpec((IDX_DIM // 2, tm), lambda b, i: (0, i)),
        pl.BlockSpec((tm, LANES), lambda b, i: (i, 0)),
        pl.BlockSpec((tm, LANES), lambda b, i: (i, 0)),
        pl.BlockSpec((tm, LANES), lambda b, i: (i, 0)),
        pl.BlockSpec((tm, LANES), lambda b, i: (i, 0)),
        const((8, 1)),
    ]
    sds = jax.ShapeDtypeStruct
    out_shape = [
        sds((bsz, DSA_W, seq), BF16), sds((bsz, IDX_W, seq), BF16), sds((bsz, FOX_W, seq), BF16),
        sds((bsz, NSA_W, seq), BF16), sds((bsz, NSA_W, seq), BF16),
        sds((bsz, N_KEY_CHUNKS, HEAD_DIM, KEY_CHUNK), BF16),
        sds((bsz, FOX_HEADS, N_KEY_CHUNKS, HEAD_DIM, KEY_CHUNK), BF16),
        sds((bsz, NSA_GROUPS, N_KEY_CHUNKS, HEAD_DIM, KEY_CHUNK), BF16),
        sds((bsz, NSA_GROUPS, N_WIN_CHUNKS, HEAD_DIM, WIN_CHUNK), BF16),
        sds((bsz, MIX_W, seq), F32),
        sds((bsz, 8, seq), F32), sds((bsz, 8, seq), F32), sds((bsz, NSA_GROUPS, 16, seq), F32),
        sds((bsz, seq, HEAD_DIM), BF16), sds((bsz, seq, IDX_DIM), BF16),
        sds((bsz, FOX_HEADS, seq, HEAD_DIM), BF16),
        sds((bsz, NSA_GROUPS, seq, HEAD_DIM), BF16), sds((bsz, NSA_GROUPS, seq, HEAD_DIM), BF16),
        sds((bsz, seq, LANES), F32), sds((bsz, seq, LANES), F32),
    ]
    wpc = tm // WIN_CHUNK
    out_specs = [
        feat(DSA_W), feat(IDX_W), feat(FOX_W), feat(NSA_W), feat(NSA_W),
        pl.BlockSpec((1, 1, HEAD_DIM, KEY_CHUNK), lambda b, i: (b, i, 0, 0)),
        pl.BlockSpec((1, FOX_HEADS, 1, HEAD_DIM, KEY_CHUNK), lambda b, i: (b, 0, i, 0, 0)),
        pl.BlockSpec((1, NSA_GROUPS, 1, HEAD_DIM, KEY_CHUNK), lambda b, i: (b, 0, i, 0, 0)),
        pl.BlockSpec((1, NSA_GROUPS, wpc, HEAD_DIM, WIN_CHUNK), lambda b, i: (b, 0, i, 0, 0)),
        feat(MIX_W),
        feat(8), feat(8),
        pl.BlockSpec((1, NSA_GROUPS, 16, tm), lambda b, i: (b, 0, 0, i)),
        pl.BlockSpec((1, tm, HEAD_DIM), lambda b, i: (b, i, 0)),
        pl.BlockSpec((1, tm, IDX_DIM), lambda b, i: (b, i, 0)),
        pl.BlockSpec((1, FOX_HEADS, tm, HEAD_DIM), lambda b, i: (b, 0, i, 0)),
        pl.BlockSpec((1, NSA_GROUPS, tm, HEAD_DIM), lambda b, i: (b, 0, i, 0)),
        pl.BlockSpec((1, NSA_GROUPS, tm, HEAD_DIM), lambda b, i: (b, 0, i, 0)),
        pl.BlockSpec((1, tm, LANES), lambda b, i: (b, i, 0)),
        pl.BlockSpec((1, tm, LANES), lambda b, i: (b, i, 0)),
    ]
    assert tm == KEY_CHUNK
    return pl.pallas_call(
        _proj_kernel,
        grid=(bsz, nt),
        in_specs=in_specs,
        out_specs=out_specs,
        out_shape=out_shape,
        compiler_params=_cparams(("arbitrary", "arbitrary")),
        name="in_proj",
    )(x, scale, shift, wt, ws, ct64, st64, ct32, st32, ck64, sk64, ck32, sk32, bfc)


def _cumsum_kernel(lft_ref, cumt_ref, cums_ref):
    seq = lft_ref.shape[2]
    r = lax.broadcasted_iota(I32, (CUM_CHUNK, CUM_CHUNK), 0)
    c = lax.broadcasted_iota(I32, (CUM_CHUNK, CUM_CHUNK), 1)
    tri_u = (r <= c).astype(F32)
    blocks = [slice(k * CUM_CHUNK, (k + 1) * CUM_CHUNK) for k in range(seq // CUM_CHUNK)]
    local = [jnp.dot(lft_ref[0, :, sl], tri_u, preferred_element_type=F32, precision=lax.Precision.HIGHEST)
             for sl in blocks]
    carry = jnp.zeros((8, 1), F32)
    pad = jnp.zeros((LANES - 8, CUM_CHUNK), F32)
    for k, sl in enumerate(blocks):
        ct = (local[k] + carry) * LOG2E
        carry = local[k][:, CUM_CHUNK - 1:CUM_CHUNK] + carry
        for head in range(8):
            cumt_ref[0, head, :, sl] = ct[head:head + 1, :]
        cums_ref[0, sl, :] = jnp.concatenate([ct, pad], axis=0).T[:, :8]


def _forget_cumsum(lft):
    bsz, _, seq = lft.shape
    return pl.pallas_call(
        _cumsum_kernel,
        grid=(bsz,),
        in_specs=[pl.BlockSpec((1, 8, seq), lambda b: (b, 0, 0))],
        out_specs=[pl.BlockSpec((1, 8, 1, seq), lambda b: (b, 0, 0, 0)),
                   pl.BlockSpec((1, seq, 8), lambda b: (b, 0, 0))],
        out_shape=[jax.ShapeDtypeStruct((bsz, 8, 1, seq), F32),
                   jax.ShapeDtypeStruct((bsz, seq, 8), F32)],
        compiler_params=_cparams(("arbitrary",)),
        name="forget_cumsum",
    )(lft)


def _compress_kernel(kcin_ref, vcin_ref, pelo_ref, pehi_ref, w1lo_ref, w1hi_ref, w2_ref,
                     kc_ref, vct_ref):
    n_blk = kcin_ref.shape[1] // CMP_STRIDE
    for kv, src in enumerate((kcin_ref, vcin_ref)):
        lo, hi = [], []
        for j in range(CMP_STRIDE):
            piece = src[0, pl.ds(j, n_blk, stride=CMP_STRIDE), :]
            lo.append((piece + pelo_ref[kv, j:j + 1, :]).astype(BF16))
            hi.append((piece + pehi_ref[kv, j:j + 1, :]).astype(BF16))
        a = _dot(jnp.concatenate(lo, axis=1), w1lo_ref[kv])
        b = _dot(jnp.concatenate(hi, axis=1), w1hi_ref[kv])
        pre = a + pltpu.roll(b, n_blk - 1, 0)
        act = pre * _sigmoid(pre)
        out = _dot(act.astype(BF16), w2_ref[kv])
        if kv == 0:
            for g in range(NSA_GROUPS):
                kc_ref[0, g] = out[:, g * HEAD_DIM:(g + 1) * HEAD_DIM].astype(BF16)
        else:
            out_t = out.T
            for g in range(NSA_GROUPS):
                vct_ref[0, g] = out_t[g * HEAD_DIM:(g + 1) * HEAD_DIM].astype(BF16)


def _compress(kcin, vcin, pelo, pehi, w1lo, w1hi, w2):
    bsz, seq, _ = kcin.shape
    n_blk = seq // CMP_STRIDE
    full = lambda a: pl.BlockSpec(a.shape, lambda b: (0,) * a.ndim)
    return pl.pallas_call(
        _compress_kernel,
        grid=(bsz,),
        in_specs=[pl.BlockSpec((1, seq, LANES), lambda b: (b, 0, 0)),
                  pl.BlockSpec((1, seq, LANES), lambda b: (b, 0, 0)),
                  full(pelo), full(pehi), full(w1lo), full(w1hi), full(w2)],
        out_specs=[pl.BlockSpec((1, NSA_GROUPS, n_blk, HEAD_DIM), lambda b: (b, 0, 0, 0)),
                   pl.BlockSpec((1, NSA_GROUPS, HEAD_DIM, n_blk), lambda b: (b, 0, 0, 0))],
        out_shape=[jax.ShapeDtypeStruct((bsz, NSA_GROUPS, n_blk, HEAD_DIM), BF16),
                   jax.ShapeDtypeStruct((bsz, NSA_GROUPS, HEAD_DIM, n_blk), BF16)],
        compiler_params=_cparams(("arbitrary",)),
        name="nsa_compress",
    )(kcin, vcin, pelo, pehi, w1lo, w1hi, w2)


def _fold_rows(x, op):
    k, n = x.shape
    return op(x.reshape(k // FOLD_ROWS, FOLD_ROWS, n), axis=0)


def _reduce_rows(x, op):
    if x.shape[0] % FOLD_ROWS == 0 and x.shape[0] > FOLD_ROWS:
        x = _fold_rows(x, op)
    return op(x, axis=0, keepdims=True)


def _with_ones(v_t):
    return jnp.concatenate([v_t, jnp.ones((ONES_ROWS, v_t.shape[1]), v_t.dtype)], axis=0)


def _softmax_step(carry, s, v_aug):
    m, acc = carry
    m_new = jnp.maximum(m, _reduce_rows(s, jnp.max))
    alpha = jnp.exp2(m - m_new)
    p = jnp.exp2(s - m_new)
    acc = alpha * acc + _dot(v_aug, p.astype(BF16))
    return m_new, acc


def _softmax_init(n):
    return (jnp.full((1, n), NEG, F32), jnp.zeros((HEAD_DIM + ONES_ROWS, n), F32))


def _softmax_finish(carry):
    _, acc = carry
    return acc[:HEAD_DIM] * (1.0 / acc[HEAD_DIM:HEAD_DIM + 1])


def _tile_lanes(a, n):
    return jnp.concatenate([a] * n, axis=1)


def _dsa_kernel(iq_ref, ik_ref, iw_ref, dq_ref, dk_ref, dv_ref, g_ref, o_ref, key_ref):
    i = pl.program_id(1)
    tq = DSA_Q_TILE
    n_chunks = i + 1
    tpos = i * tq + lax.broadcasted_iota(I32, (1, tq), 1)
    srow = lax.broadcasted_iota(I32, (KEY_CHUNK, tq), 0)
    iw = iw_ref[0]

    def index_chunk(c, _):
        k0 = pl.multiple_of(c * KEY_CHUNK, KEY_CHUNK)
        ikc = ik_ref[0, pl.ds(k0, KEY_CHUNK), :]
        acc = jnp.zeros((KEY_CHUNK, tq), F32)
        for h in range(IDX_HEADS):
            x = _dot(ikc, iq_ref[0, h * IDX_DIM:(h + 1) * IDX_DIM, :])
            acc = acc + iw[h:h + 1, :] * jnp.maximum(x, 0.0)
        key_ref[pl.ds(k0, KEY_CHUNK), :] = jnp.where(srow + k0 <= tpos, acc, -jnp.inf)
        return 0

    lax.fori_loop(1, n_chunks, index_chunk, index_chunk(0, 0))

    def count(pred):
        def body(c, acc):
            k0 = pl.multiple_of(c * KEY_CHUNK, KEY_CHUNK)
            hit = pred(key_ref[pl.ds(k0, KEY_CHUNK), :], srow + k0)
            return acc + _fold_rows(jnp.where(hit, 1, 0), jnp.sum)
        acc = lax.fori_loop(1, n_chunks, body, body(0, jnp.zeros((FOLD_ROWS, tq), I32)))
        return jnp.sum(acc, axis=0, keepdims=True)

    def decode(code):
        skey = code ^ INT_MIN
        val = lax.bitcast_convert_type(jnp.where(skey < 0, skey ^ 0x7FFFFFFF, skey), F32)
        return jnp.where(jnp.logical_and(code >= 0, code <= CODE_NEG_INF), -jnp.inf, val)

    def value_bit(it, code):
        cand = code | lax.shift_left(jnp.int32(1), 31 - it)
        cand_f = decode(cand)
        cnt = count(lambda key, spos: key >= cand_f)
        return jnp.where(cnt >= DSA_TOPK, cand, code)

    thr = decode(lax.fori_loop(0, 32, value_bit, jnp.zeros((1, tq), I32)))

    need = (DSA_TOPK - count(lambda key, spos: key > thr)).astype(F32)
    r = lax.broadcasted_iota(I32, (KEY_CHUNK, KEY_CHUNK), 0)
    c = lax.broadcasted_iota(I32, (KEY_CHUNK, KEY_CHUNK), 1)
    prefix = jnp.where(c <= r, 1.0, 0.0).astype(BF16)

    def attend(c, carry):
        heads, ties_before = carry
        k0 = pl.multiple_of(c * KEY_CHUNK, KEY_CHUNK)
        key = key_ref[pl.ds(k0, KEY_CHUNK), :]
        tied = key == thr
        rank = _dot(prefix, jnp.where(tied, 1.0, 0.0).astype(BF16)) + ties_before
        keep = jnp.logical_or(key > thr, jnp.logical_and(tied, rank <= need))
        bias = jnp.where(jnp.logical_and(keep, srow + k0 <= tpos), 0.0, NEG)
        kc = dk_ref[0, pl.ds(k0, KEY_CHUNK), :]
        v_aug = _with_ones(dv_ref[0, c])
        s = _dot(kc, q_all) + _tile_lanes(bias, DSA_HEADS)
        return _softmax_step(heads, s, v_aug), rank[KEY_CHUNK - 1:KEY_CHUNK, :]

    q_all = jnp.concatenate([dq_ref[0, h * HEAD_DIM:(h + 1) * HEAD_DIM, :] for h in range(DSA_HEADS)], axis=1)
    init = _softmax_init(DSA_HEADS * tq)
    heads, _ = lax.fori_loop(1, n_chunks, attend, attend(0, (init, jnp.zeros((1, tq), F32))))
    o_all = _softmax_finish(heads)
    for h in range(DSA_HEADS):
        rows = slice(h * HEAD_DIM, (h + 1) * HEAD_DIM)
        o_ref[0, rows, :] = (o_all[:, h * tq:(h + 1) * tq] * g_ref[0, rows, :]).astype(BF16)


def _dsa(iq_t, ik, iw_t, dq_t, dk, dv_t, gate_t):
    bsz, _, seq = dq_t.shape
    tq = DSA_Q_TILE
    assert tq == KEY_CHUNK
    return pl.pallas_call(
        _dsa_kernel,
        grid=(bsz, seq // tq),
        in_specs=[
            pl.BlockSpec((1, IDX_W, tq), lambda b, i: (b, 0, i)),
            pl.BlockSpec((1, seq, IDX_DIM), lambda b, i: (b, 0, 0)),
            pl.BlockSpec((1, 8, tq), lambda b, i: (b, 0, i)),
            pl.BlockSpec((1, DSA_W, tq), lambda b, i: (b, 0, i)),
            pl.BlockSpec((1, seq, HEAD_DIM), lambda b, i: (b, 0, 0)),
            pl.BlockSpec((1, N_KEY_CHUNKS, HEAD_DIM, KEY_CHUNK), lambda b, i: (b, 0, 0, 0)),
            pl.BlockSpec((1, DSA_W, tq), lambda b, i: (b, 0, i)),
        ],
        out_specs=pl.BlockSpec((1, DSA_W, tq), lambda b, i: (b, 0, i)),
        out_shape=jax.ShapeDtypeStruct((bsz, DSA_W, seq), BF16),
        scratch_shapes=[pltpu.VMEM((seq, tq), F32)],
        compiler_params=_cparams(("arbitrary", "arbitrary")),
        name="dsa_attn",
    )(iq_t, ik, iw_t, dq_t, dk, dv_t, gate_t)


def _fox_kernel(q_ref, k_ref, v_ref, cumt_ref, cums_ref, g_ref, o_ref):
    h = pl.program_id(1)
    tq = FOX_Q_TILE
    n_tiles = q_ref.shape[2] // tq
    head_lane = lax.broadcasted_iota(I32, (KEY_CHUNK, 8), 1) == h
    srow = lax.broadcasted_iota(I32, (KEY_CHUNK, tq), 0)
    tcol = lax.broadcasted_iota(I32, (KEY_CHUNK, tq), 1)
    ccols = []
    for c in range(n_tiles):
        col = jnp.sum(jnp.where(head_lane, cums_ref[0, c * KEY_CHUNK:(c + 1) * KEY_CHUNK, :], 0.0),
                      axis=1, keepdims=True)
        ccols.append(_tile_lanes(jnp.broadcast_to(col, (KEY_CHUNK, LANES)), tq // LANES))
    v_aug = [_with_ones(v_ref[0, 0, c]) for c in range(n_tiles)]
    for i in range(n_tiles):
        q = q_ref[0, :, i * tq:(i + 1) * tq]
        crow = cumt_ref[0, 0, :, i * tq:(i + 1) * tq]
        carry = _softmax_init(tq)
        for c in range(i + 1):
            s = _dot(k_ref[0, 0, c * KEY_CHUNK:(c + 1) * KEY_CHUNK, :], q) + crow - ccols[c]
            if c == i:
                s = jnp.where(srow <= tcol, s, NEG)
            carry = _softmax_step(carry, s, v_aug[c])
        cols = slice(i * tq, (i + 1) * tq)
        o_ref[0, :, cols] = (_softmax_finish(carry) * g_ref[0, :, cols]).astype(BF16)


def _fox(fq_t, fk, fv_t, cum_t, cum_s, gate_t):
    bsz, _, seq = fq_t.shape
    assert FOX_Q_TILE == KEY_CHUNK
    return pl.pallas_call(
        _fox_kernel,
        grid=(bsz, FOX_HEADS),
        in_specs=[
            pl.BlockSpec((1, HEAD_DIM, seq), lambda b, h: (b, h, 0)),
            pl.BlockSpec((1, 1, seq, HEAD_DIM), lambda b, h: (b, h, 0, 0)),
            pl.BlockSpec((1, 1, N_KEY_CHUNKS, HEAD_DIM, KEY_CHUNK), lambda b, h: (b, h, 0, 0, 0)),
            pl.BlockSpec((1, 1, 1, seq), lambda b, h: (b, h, 0, 0)),
            pl.BlockSpec((1, seq, 8), lambda b, h: (b, 0, 0)),
            pl.BlockSpec((1, HEAD_DIM, seq), lambda b, h: (b, DSA_HEADS + h, 0)),
        ],
        out_specs=pl.BlockSpec((1, HEAD_DIM, seq), lambda b, h: (b, h, 0)),
        out_shape=jax.ShapeDtypeStruct((bsz, FOX_W, seq), BF16),
        compiler_params=_cparams(("arbitrary", "arbitrary")),
        name="fox_attn",
    )(fq_t, fk, fv_t, cum_t, cum_s, gate_t)


def _nsa_kernel(q_ref, qr_ref, kc_ref, vct_ref, ks_ref, vs_ref, kw_ref, vw_ref, g_ref,
                ovl_ref, expand_ref, og0_ref, og1_ref, og2_ref, o_ref, sel_ref):
    i = pl.program_id(2)
    nq = NSA_HPG * Q_TILE
    t0 = i * Q_TILE
    tpos = t0 + lax.broadcasted_iota(I32, (1, Q_TILE), 1)
    q3 = jnp.concatenate([q_ref[0, j * HEAD_DIM:(j + 1) * HEAD_DIM, :] for j in range(NSA_HPG)], axis=1)
    qr3 = jnp.concatenate([qr_ref[0, j * HEAD_DIM:(j + 1) * HEAD_DIM, :] for j in range(NSA_HPG)], axis=1)

    n_cmp = kc_ref.shape[2]
    cend = lax.broadcasted_iota(I32, (n_cmp, Q_TILE), 0) * CMP_STRIDE + (CMP_LEN - 1)
    cbias = _tile_lanes(jnp.where(cend <= tpos, 0.0, NEG), NSA_HPG)
    cvalid = _tile_lanes(jnp.where(cend <= tpos, 1.0, 0.0), NSA_HPG)
    s = _dot(kc_ref[0, 0], q3) + cbias
    m = _reduce_rows(s, jnp.max)
    e = jnp.exp2(s - m) * cvalid
    p_cmp = (e / jnp.maximum(_reduce_rows(e, jnp.sum), 1e-30)).astype(BF16)
    o_cmp = _dot(vct_ref[0, 0], p_cmp)

    p_stack = jnp.concatenate([p_cmp[:, j * Q_TILE:(j + 1) * Q_TILE] for j in range(NSA_HPG)], axis=0)
    score = _dot(ovl_ref[...], p_stack)
    blk = lax.broadcasted_iota(I32, (N_SEL_BLOCKS, Q_TILE), 0)
    cur = lax.shift_right_logical(tpos, 6)
    forced = jnp.logical_or(blk == 0, jnp.logical_or(blk == cur, blk == cur - 1))
    score = jnp.where(forced, jnp.inf, jnp.where(blk > cur, -jnp.inf, score))
    rank = jnp.zeros((N_SEL_BLOCKS, Q_TILE), F32)
    for mth in range(N_SEL_BLOCKS):
        row = score[mth:mth + 1, :]
        ahead = jnp.logical_or(row > score, jnp.logical_and(row == score, blk > mth))
        rank = rank + jnp.where(ahead, 1.0, 0.0)
    sel_ref[...] = jnp.where(rank < SEL_N, 1.0, 0.0)

    n_chunks = (i * Q_TILE + Q_TILE + KEY_CHUNK - 1) // KEY_CHUNK
    srow = lax.broadcasted_iota(I32, (KEY_CHUNK, Q_TILE), 0)
    blocks_per_chunk = KEY_CHUNK // SEL_BLOCK

    def attend_sel(c, carry):
        k0 = pl.multiple_of(c * KEY_CHUNK, KEY_CHUNK)
        b0 = pl.multiple_of(c * blocks_per_chunk, blocks_per_chunk)
        picked = _dot(expand_ref[...], sel_ref[pl.ds(b0, blocks_per_chunk), :])
        ok = jnp.logical_and(picked > 0.5, srow + k0 <= tpos)
        bias = _tile_lanes(jnp.where(ok, 0.0, NEG), NSA_HPG)
        s = _dot(ks_ref[0, 0, pl.ds(k0, KEY_CHUNK), :], qr3) + bias
        return _softmax_step(carry, s, _with_ones(vs_ref[0, 0, c]))

    wc = jnp.maximum(i - WINDOW // Q_TILE, 0) * (Q_TILE // WIN_CHUNK)
    w0 = pl.multiple_of(wc * WIN_CHUNK, WIN_CHUNK)
    kpos = w0 + lax.broadcasted_iota(I32, (WIN_SPAN, Q_TILE), 0)
    wok = jnp.logical_and(kpos <= tpos, kpos > tpos - WINDOW)
    s = _dot(kw_ref[0, 0, pl.ds(w0, WIN_SPAN), :], qr3) + _tile_lanes(jnp.where(wok, 0.0, NEG), NSA_HPG)
    m = _reduce_rows(s, jnp.max)
    e = jnp.exp2(s - m).astype(BF16)
    acc = jnp.zeros((HEAD_DIM + ONES_ROWS, nq), F32)
    for j in range(WIN_SPAN // WIN_CHUNK):
        acc = acc + _dot(_with_ones(vw_ref[0, 0, wc + j]), e[j * WIN_CHUNK:(j + 1) * WIN_CHUNK, :])
    o_win = _softmax_finish((m, acc))

    first = attend_sel(0, _softmax_init(nq))
    o_slc = _softmax_finish(lax.fori_loop(1, n_chunks, attend_sel, first))

    gts = g_ref[0, 0]
    out_gates = (og0_ref, og1_ref, og2_ref)
    for j in range(NSA_HPG):
        sl = slice(j * Q_TILE, (j + 1) * Q_TILE)
        mixed = (gts[j:j + 1, :] * o_cmp[:, sl]
                 + gts[NSA_HPG + j:NSA_HPG + j + 1, :] * o_slc[:, sl]
                 + gts[2 * NSA_HPG + j:2 * NSA_HPG + j + 1, :] * o_win[:, sl])
        o_ref[0, j * HEAD_DIM:(j + 1) * HEAD_DIM, :] = (mixed * out_gates[j][0]).astype(BF16)


def _nsa(nq_t, nqr_t, kc, vc_t, ks, vs_t, kw, vw_t, gts, ovl, expand, gate_t):
    bsz, _, seq = nq_t.shape
    n_cmp = kc.shape[2]
    gw = NSA_HPG * HEAD_DIM
    per_group = lambda shape: pl.BlockSpec((1, 1) + shape, lambda b, g, i: (b, g) + (0,) * len(shape))
    return pl.pallas_call(
        _nsa_kernel,
        grid=(bsz, NSA_GROUPS, seq // Q_TILE),
        in_specs=[
            pl.BlockSpec((1, gw, Q_TILE), lambda b, g, i: (b, g, i)),
            pl.BlockSpec((1, gw, Q_TILE), lambda b, g, i: (b, g, i)),
            per_group((n_cmp, HEAD_DIM)),
            per_group((HEAD_DIM, n_cmp)),
            per_group((seq, HEAD_DIM)),
            per_group((N_KEY_CHUNKS, HEAD_DIM, KEY_CHUNK)),
            per_group((seq, HEAD_DIM)),
            per_group((N_WIN_CHUNKS, HEAD_DIM, WIN_CHUNK)),
            pl.BlockSpec((1, 1, 16, Q_TILE), lambda b, g, i: (b, g, 0, i)),
            pl.BlockSpec(ovl.shape, lambda b, g, i: (0, 0)),
            pl.BlockSpec(expand.shape, lambda b, g, i: (0, 0)),
        ] + [
            pl.BlockSpec((1, HEAD_DIM, Q_TILE),
                         lambda b, g, i, j=j: (b, DSA_HEADS + FOX_HEADS + g * NSA_HPG + j, i))
            for j in range(NSA_HPG)
        ],
        out_specs=pl.BlockSpec((1, gw, Q_TILE), lambda b, g, i: (b, g, i)),
        out_shape=jax.ShapeDtypeStruct((bsz, NSA_W, seq), BF16),
        scratch_shapes=[pltpu.VMEM((N_SEL_BLOCKS, Q_TILE), F32)],
        compiler_params=_cparams(("arbitrary", "arbitrary", "arbitrary")),
        name="nsa_attn",
    )(nq_t, nqr_t, kc, vc_t, ks, vs_t, kw, vw_t, gts, ovl, expand, gate_t, gate_t, gate_t)


def _out_kernel(od_ref, of_ref, on_ref, w_ref, x_ref, gm_ref, lg_ref, lb_ref, o_ref):
    z = jnp.concatenate([od_ref[0], of_ref[0], on_ref[0]], axis=0)
    y = lax.dot_general(z, w_ref[...], (((0,), (0,)), ((), ())),
                        preferred_element_type=F32)
    r = ALPHA * x_ref[0] + (1.0 + gm_ref[0]) * y
    mu = jnp.mean(r, axis=-1, keepdims=True)
    rc = r - mu
    var = jnp.mean(rc * rc, axis=-1, keepdims=True)
    o_ref[0] = rc * lax.rsqrt(var + LN_EPS) * lg_ref[...] + lb_ref[...]


def _output(od_t, of_t, on_t, w_out, x, gmod, ln_g, ln_b):
    bsz, seq, d = x.shape
    tm = PROJ_TOKENS
    feat = lambda rows: pl.BlockSpec((1, rows, tm), lambda b, i: (b, 0, i))
    return pl.pallas_call(
        _out_kernel,
        grid=(bsz, seq // tm),
        in_specs=[
            feat(DSA_W), feat(FOX_W), feat(NSA_W),
            pl.BlockSpec((MIX_W, d), lambda b, i: (0, 0)),
            pl.BlockSpec((1, tm, d), lambda b, i: (b, i, 0)),
            pl.BlockSpec((1, 1, d), lambda b, i: (b, 0, 0)),
            pl.BlockSpec((1, d), lambda b, i: (0, 0)),
            pl.BlockSpec((1, d), lambda b, i: (0, 0)),
        ],
        out_specs=pl.BlockSpec((1, tm, d), lambda b, i: (b, i, 0)),
        out_shape=jax.ShapeDtypeStruct((bsz, seq, d), F32),
        compiler_params=_cparams(("arbitrary", "arbitrary")),
        name="out_proj_ln",
    )(od_t, of_t, on_t, w_out, x, gmod, ln_g, ln_b)


def _split_cols(w):
    out, off = {}, 0
    for name, width in IN_SPLITS:
        out[name] = w[:, off:off + width]
        off += width
    return out


def _prep_w_in(w_in):
    d = w_in.shape[0]
    p = _split_cols(w_in.astype(BF16))
    z = lambda n: jnp.zeros((d, n), BF16)
    gcols = p["nsa_g"].reshape(d, 3, NSA_GROUPS, NSA_HPG)
    gparts = []
    for g in range(NSA_GROUPS):
        gparts += [gcols[:, :, g].reshape(d, 3 * NSA_HPG), z(16 - 3 * NSA_HPG)]
    wt = jnp.concatenate(
        [p["dsa_q"], p["idx_q"], p["fox_q"], p["nsa_q"],
         p["dsa_v"], z(HEAD_DIM), p["fox_v"], p["nsa_vs"], p["nsa_vw"], p["gate"],
         p["idx_w"], p["fox_f"], z(8 - FOX_HEADS)] + gparts, axis=1)
    assert wt.shape[1] == T_ROWS
    ws = jnp.concatenate(
        [p["dsa_k"], z(LANES - HEAD_DIM), p["idx_k"], z(LANES - IDX_DIM), p["fox_k"],
         p["nsa_ks"], p["nsa_kw"], p["nsa_kc"], p["nsa_vc"]], axis=1)
    assert ws.shape[1] == S_COLS
    return wt, ws


def _rope_tables(seq):
    pos = jnp.arange(seq, dtype=F32)
    feature_major, token_major = [], []
    lane = np.arange(LANES)
    for hd in (HEAD_DIM, IDX_DIM):
        half = hd // 2
        inv = ROPE_THETA ** (-jnp.arange(half, dtype=F32) / half)
        ang = inv[:, None] * pos[None, :]
        cos, sin = jnp.cos(ang), jnp.sin(ang)
        feature_major += [cos, sin]
        sign = jnp.asarray(np.where((lane % hd) < half, -1.0, 1.0), F32)
        reps = LANES // half
        token_major += [jnp.tile(cos.T, (1, reps)), jnp.tile(sin.T, (1, reps)) * sign[None, :]]
    return tuple(feature_major + token_major)


def _prep_compress(cmp_pe, cmp_w1, cmp_w2):
    assert NSA_GROUPS == 2

    def block_diag(w):
        z = jnp.zeros_like(w)
        return jnp.concatenate([jnp.concatenate([w, z], axis=-1), jnp.concatenate([z, w], axis=-1)], axis=-2)

    big1 = block_diag(cmp_w1.astype(BF16).reshape(2, CMP_LEN, HEAD_DIM, HEAD_DIM))
    w1lo = big1[:, :CMP_STRIDE].reshape(2, CMP_STRIDE * LANES, LANES)
    w1hi = big1[:, CMP_STRIDE:].reshape(2, CMP_STRIDE * LANES, LANES)
    w2 = block_diag(cmp_w2.astype(BF16))
    pe2 = jnp.concatenate([cmp_pe] * NSA_GROUPS, axis=-1)
    return pe2[:, :CMP_STRIDE], pe2[:, CMP_STRIDE:], w1lo, w1hi, w2


def _selection_constants(n_cmp):
    cstart = np.arange(n_cmp) * CMP_STRIDE
    bstart = np.arange(N_SEL_BLOCKS) * SEL_BLOCK
    real = (np.arange(n_cmp) < (SEQ - CMP_LEN) // CMP_STRIDE + 1)[:, None]
    ovl = ((cstart[:, None] < bstart[None, :] + SEL_BLOCK) & (cstart[:, None] + CMP_LEN > bstart[None, :]) & real)
    ovl_t = np.concatenate([ovl.T.astype(np.float32)] * NSA_HPG, axis=1)
    expand = (np.arange(KEY_CHUNK)[:, None] // SEL_BLOCK == np.arange(KEY_CHUNK // SEL_BLOCK)[None, :])
    return jnp.asarray(ovl_t, BF16), jnp.asarray(expand, F32)


def kernel(x, c, w_ada, b_ada, w_in, b_f, cmp_pe, cmp_w1, cmp_w2, w_out, ln_g, ln_b):
    bsz, seq, d = x.shape
    assert (seq, d) == (SEQ, D_MODEL)
    mod = _modulation(c, w_ada, b_ada)
    tabs = _rope_tables(seq)
    ovl, expand = _selection_constants(seq // CMP_STRIDE)
    for l in range(DEPTH):
        shift = mod[l, :, :d].reshape(bsz, 1, d)
        scale = mod[l, :, d:2 * d].reshape(bsz, 1, d)
        gmod = mod[l, :, 2 * d:].reshape(bsz, 1, d)
        wt, ws = _prep_w_in(w_in[l])
        bf8 = jnp.concatenate([b_f[l], jnp.zeros((8 - FOX_HEADS,), F32)])
        (dq, iq, fq, nq, nqr, dv, fv, vs, vw, gate, iw, lft, gts,
         dk, ik, fk, ks, kw, kcin, vcin) = _projection(
            x, scale, shift, wt, ws, tabs, bf8.reshape(8, 1))
        cum_t, cum_s = _forget_cumsum(lft)
        kc, vc_t = _compress(kcin, vcin, *_prep_compress(cmp_pe[l], cmp_w1[l], cmp_w2[l]))
        o_dsa = _dsa(iq, ik, iw, dq, dk, dv, gate)
        o_fox = _fox(fq, fk, fv, cum_t, cum_s, gate)
        o_nsa = _nsa(nq, nqr, kc, vc_t, ks, vs, kw, vw, gts, ovl, expand, gate)
        x = _output(o_dsa, o_fox, o_nsa, w_out[l].astype(BF16), x, gmod,
                    ln_g[l].reshape(1, d), ln_b[l].reshape(1, d))
    return x
```
